```python
import math
import jax, jax.numpy as jnp
from jax import lax
import numpy as np

D_MODEL = 2048
BATCH = 16
SEQ = 256
DEPTH = 4
DEC_BATCH = 8
DEC_SEQ = 1024
PAST_LEN = 512

GRID_W = 64
N_MIXERS = 3
EPS = 1e-6
D_RNN = 2560
RG_HEADS = 16
RG_HW = D_RNN // RG_HEADS
RG_CONV = 4
RG_C = 8.0
SG_WIDTH = 2 * D_MODEL
SG_GROUPS = 16
SG_GW = SG_WIDTH // SG_GROUPS
CHUNK = 128
SC_CONV = 3
D_FF = 5632
N_EXPERTS = 8
TOP_K = 2
D_FF_EXPERT = 5632
N_A = (DEPTH + 2) // 3
N_B = (DEPTH + 1) // 3
N_C = DEPTH // 3
N_DENSE = (DEPTH + 1) // 2
N_MOE = DEPTH // 2

kernel_name = "bidir_hybrid_diffusion_step"


def rmsnorm(x, g):
    xf = x.astype(jnp.float32)
    y = xf * lax.rsqrt(jnp.mean(xf * xf, axis=-1, keepdims=True) + EPS)
    return (y * g.astype(jnp.float32)).astype(x.dtype)


def adaln(cond, w, b):
    m = jax.nn.silu(cond) @ w + b
    return [t[:, None, :] for t in jnp.split(m, 6, axis=-1)]


def modulate(h, shift, scale):
    return h * (1 + scale) + shift


def dwconv(x, w, pad_l, pad_r):
    L = x.shape[1]
    xp = jnp.pad(x, ((0, 0), (pad_l, pad_r), (0, 0)))
    return sum(xp[:, k:k + L] * w[k] for k in range(w.shape[0]))


def grid_pos_embed(L, D, dtype):
    rows = L // GRID_W
    r = jnp.repeat(jnp.arange(rows), GRID_W)
    col = jnp.tile(jnp.arange(GRID_W), rows)
    quarter = D // 4
    omega = 1.0 / (10000.0 ** (jnp.arange(quarter, dtype=jnp.float32) / quarter))

    def emb(p):
        ang = p[:, None].astype(jnp.float32) * omega[None, :]
        return jnp.concatenate([jnp.sin(ang), jnp.cos(ang)], axis=-1)

    return jnp.concatenate([emb(r), emb(col)], axis=-1).astype(dtype)


def _lin_combine(e1, e2):
    a1, b1 = e1
    a2, b2 = e2
    return a1 * a2, a2 * b1 + b2


def rglru_scan(xc, w_a, b_a, w_x, b_x, lam, h0, reverse):
    B, L, R = xc.shape
    xh = xc.reshape(B, L, RG_HEADS, RG_HW)
    r = jax.nn.sigmoid(jnp.einsum('blhi,hij->blhj', xh, w_a.astype(jnp.float32)).reshape(B, L, R) + b_a)
    ig = jax.nn.sigmoid(jnp.einsum('blhi,hij->blhj', xh, w_x.astype(jnp.float32)).reshape(B, L, R) + b_x)
    log_a = -RG_C * r * jax.nn.softplus(-lam.astype(jnp.float32))
    a = jnp.exp(log_a)
    bterm = jnp.sqrt(-jnp.expm1(2.0 * log_a)) * (ig * xc)
    h0 = h0.astype(jnp.float32)
    if reverse:
        bterm = bterm.at[:, -1].add(a[:, -1] * h0)
    else:
        bterm = bterm.at[:, 0].add(a[:, 0] * h0)
    _, h = lax.associative_scan(_lin_combine, (a, bterm), axis=1, reverse=reverse)
    final = h[:, 0] if reverse else h[:, -1]
    return h, final


def rglru_mixer(h, w_in, conv_w, conv_b, w_a, b_a, w_x, b_x, lam, w_out, h0_f, h0_b):
    z = h @ w_in
    gate_br, x_br = jnp.split(z, 2, axis=-1)
    xc = (dwconv(x_br, conv_w, 2, 1) + conv_b).astype(jnp.float32)
    hf, sf = rglru_scan(xc, w_a[0], b_a[0], w_x[0], b_x[0], lam[0], h0_f, False)
    hb, sb = rglru_scan(xc, w_a[1], b_a[1], w_x[1], b_x[1], lam[1], h0_b, True)
    y = (hf + hb).astype(h.dtype) * jax.nn.gelu(gate_br)
    return y @ w_out, jnp.stack([sf, sb], axis=1)


def spatial_gating_mixer(h, w_in, norm_g, w_s, b_s, w_out):
    z = jax.nn.gelu(h @ w_in)
    u, v = jnp.split(z, 2, axis=-1)
    v = rmsnorm(v, norm_g)
    B, L, _ = v.shape
    vh = v.reshape(B, L // CHUNK, CHUNK, SG_GROUPS, SG_GW)
    vm = jnp.einsum('bnpgc,gqp->bnqgc', vh, w_s) + b_s.T[None, None, :, :, None]
    y = u * vm.reshape(B, L, SG_WIDTH)
    return y @ w_out


def short_conv_mixer(h, w_in, conv_w, w_out):
    bg, cg, xv = jnp.split(h @ w_in, 3, axis=-1)
    y = bg * dwconv(cg * xv, conv_w, 1, 1)
    return y @ w_out


def swiglu(h, w1, w3, w2):
    return (jax.nn.silu(h @ w1) * (h @ w3)) @ w2


def moe_swiglu(h, router, router_b, w1, w3, w2):
    logits = (h @ router).astype(jnp.float32) + router_b.astype(jnp.float32)
    top_v, top_i = lax.top_k(logits, TOP_K)
    gates = jax.nn.softmax(top_v, axis=-1)
    comb = jnp.einsum('blk,blke->ble', gates, jax.nn.one_hot(top_i, N_EXPERTS, dtype=jnp.float32))
    out = jnp.zeros_like(h)
    for e in range(N_EXPERTS):
        out = out + comb[..., e:e + 1].astype(h.dtype) * swiglu(h, w1[e], w3[e], w2[e])
    return out


def setup_inputs(seed: int = 0) -> dict:
    key = jax.random.key(seed)
    ks = iter(jax.random.split(key, 48))

    def nrm(shape, s):
        return jax.random.normal(next(ks), shape, jnp.float32) * s

    D = D_MODEL
    u = jax.random.uniform(next(ks), (N_A, 2, D_RNN), jnp.float32, 0.9, 0.999)
    a0 = u ** (1.0 / RG_C)
    rg_lam = jnp.log(a0) - jnp.log1p(-a0)
    return {
        "x_prompt": nrm((BATCH, SEQ, D), 1.0),
        "x_sample": nrm((DEC_BATCH, DEC_SEQ, D), 1.0),
        "state_rglru": nrm((DEC_BATCH, N_A, 2, D_RNN), 0.5),
        "c": nrm((DEC_BATCH, D), 1.0),
        "c_ctx": nrm((D,), 1.0),
        "norm_mix_g": 1.0 + nrm((DEPTH, D), 0.02),
        "norm_ffn_g": 1.0 + nrm((DEPTH, D), 0.02),
        "w_mod": nrm((DEPTH, D, 6 * D), 0.5 * D ** -0.5),
        "b_mod": nrm((DEPTH, 6 * D), 0.02),
        "final_norm_g": 1.0 + nrm((D,), 0.02),
        "rg_w_in": nrm((N_A, D, 2 * D_RNN), D ** -0.5),
        "rg_conv_w": nrm((N_A, RG_CONV, D_RNN), RG_CONV ** -0.5),
        "rg_conv_b": nrm((N_A, D_RNN), 0.02),
        "rg_w_a": nrm((N_A, 2, RG_HEADS, RG_HW, RG_HW), RG_HW ** -0.5),
        "rg_b_a": nrm((N_A, 2, D_RNN), 0.02),
        "rg_w_x": nrm((N_A, 2, RG_HEADS, RG_HW, RG_HW), RG_HW ** -0.5),
        "rg_b_x": nrm((N_A, 2, D_RNN), 0.02),
        "rg_lam": rg_lam,
        "rg_w_out": nrm((N_A, D_RNN, D), D_RNN ** -0.5),
        "sg_w_in": nrm((N_B, D, 2 * SG_WIDTH), D ** -0.5),
        "sg_norm_g": 1.0 + nrm((N_B, SG_WIDTH), 0.02),
        "sg_w_s": nrm((N_B, SG_GROUPS, CHUNK, CHUNK), CHUNK ** -0.5),
        "sg_b_s": 1.0 + nrm((N_B, SG_GROUPS, CHUNK), 0.02),
        "sg_w_out": nrm((N_B, SG_WIDTH, D), SG_WIDTH ** -0.5),
        "sc_w_in": nrm((N_C, D, 3 * D), D ** -0.5),
        "sc_conv_w": nrm((N_C, SC_CONV, D), SC_CONV ** -0.5),
        "sc_w_out": nrm((N_C, D, D), D ** -0.5),
        "ff_w1": nrm((N_DENSE, D, D_FF), D ** -0.5),
        "ff_w3": nrm((N_DENSE, D, D_FF), D ** -0.5),
        "ff_w2": nrm((N_DENSE, D_FF, D), D_FF ** -0.5),
        "moe_router": nrm((N_MOE, D, N_EXPERTS), D ** -0.5),
        "moe_router_b": nrm((N_MOE, N_EXPERTS), 0.01),
        "moe_w1": nrm((N_MOE, N_EXPERTS, D, D_FF_EXPERT), D ** -0.5),
        "moe_w3": nrm((N_MOE, N_EXPERTS, D, D_FF_EXPERT), D ** -0.5),
        "moe_w2": nrm((N_MOE, N_EXPERTS, D_FF_EXPERT, D), D_FF_EXPERT ** -0.5),
    }


def reference(x_prompt, x_sample, state_rglru, c, c_ctx, norm_mix_g, norm_ffn_g, w_mod, b_mod,
              final_norm_g, rg_w_in, rg_conv_w, rg_conv_b, rg_w_a, rg_b_a, rg_w_x, rg_b_x, rg_lam,
              rg_w_out, sg_w_in, sg_norm_g, sg_w_s, sg_b_s, sg_w_out, sc_w_in, sc_conv_w, sc_w_out,
              ff_w1, ff_w3, ff_w2, moe_router, moe_router_b, moe_w1, moe_w3, moe_w2):
    y_p = x_prompt
    y_s = x_sample + grid_pos_embed(x_sample.shape[1], D_MODEL, x_sample.dtype)[None]
    cond_ctx = c_ctx[None, :]
    new_states = []
    for i in range(DEPTH):
        mp = adaln(cond_ctx, w_mod[i], b_mod[i])
        ms = adaln(c, w_mod[i], b_mod[i])
        hp = modulate(rmsnorm(y_p, norm_mix_g[i]), mp[0], mp[1])
        hs = modulate(rmsnorm(y_s, norm_mix_g[i]), ms[0], ms[1])
        kind = i % N_MIXERS
        if kind == 0:
            j = i // N_MIXERS
            prm = (rg_w_in[j], rg_conv_w[j], rg_conv_b[j], rg_w_a[j], rg_b_a[j], rg_w_x[j],
                   rg_b_x[j], rg_lam[j], rg_w_out[j])
            zeros = jnp.zeros((y_p.shape[0], D_RNN), jnp.float32)
            op, st = rglru_mixer(hp, *prm, zeros, zeros)
            os_, _ = rglru_mixer(hs, *prm, state_rglru[:, j, 0], state_rglru[:, j, 1])
            new_states.append(st.astype(y_p.dtype))
        elif kind == 1:
            j = i // N_MIXERS
            prm = (sg_w_in[j], sg_norm_g[j], sg_w_s[j], sg_b_s[j], sg_w_out[j])
            op = spatial_gating_mixer(hp, *prm)
            os_ = spatial_gating_mixer(hs, *prm)
        else:
            j = i // N_MIXERS
            prm = (sc_w_in[j], sc_conv_w[j], sc_w_out[j])
            op = short_conv_mixer(hp, *prm)
            os_ = short_conv_mixer(hs, *prm)
        y_p = y_p + mp[2] * op
        y_s = y_s + ms[2] * os_

        hp = modulate(rmsnorm(y_p, norm_ffn_g[i]), mp[3], mp[4])
        hs = modulate(rmsnorm(y_s, norm_ffn_g[i]), ms[3], ms[4])
        f = i // 2
        if i % 2 == 0:
            fp = swiglu(hp, ff_w1[f], ff_w3[f], ff_w2[f])
            fs = swiglu(hs, ff_w1[f], ff_w3[f], ff_w2[f])
        else:
            prm = (moe_router[f], moe_router_b[f], moe_w1[f], moe_w3[f], moe_w2[f])
            fp = moe_swiglu(hp, *prm)
            fs = moe_swiglu(hs, *prm)
        y_p = y_p + mp[5] * fp
        y_s = y_s + ms[5] * fs

    y_prompt = rmsnorm(y_p, final_norm_g)
    y_sample = rmsnorm(y_s, final_norm_g)
    new_state_rglru = jnp.stack(new_states, axis=1)
    return (y_prompt, y_sample, new_state_rglru)
```

```python
import functools

import jax
import jax.numpy as jnp
from jax import lax
from jax.experimental import pallas as pl
from jax.experimental.pallas import tpu as pltpu

F32 = jnp.float32
BF16 = jnp.bfloat16

GRID_W = 64
EPS = 1e-6
RG_C = 8.0
TOP_K = 2

LANES = 128
SUBLANES = 8
VMEM_LIMIT_BYTES = 58 * 1024 * 1024
NEG_BIG = -1e30


def _cparams(n_axes):
    return pltpu.CompilerParams(dimension_semantics=("arbitrary",) * n_axes,
                                vmem_limit_bytes=VMEM_LIMIT_BYTES)


def _pick_tile(n, pref):
    if n <= pref:
        return n
    t = (pref // LANES) * LANES
    while t > LANES and n % t:
        t -= LANES
    assert n % t == 0, (n, pref)
    return t


def _sigmoid(x):
    return 1.0 / (1.0 + jnp.exp(-x))


def _gelu_tanh(x):
    c = 0.7978845608028654
    return 0.5 * x * (1.0 + jnp.tanh(c * (x + 0.044715 * (x * x * x))))


def _mod_row(i, tm, n_ctx, dec_seq):
    start = i * tm
    return jnp.where(start < n_ctx, 0, 1 + (start - n_ctx) // dec_seq)


def _adaln_kernel(c_ref, w_ref, b_ref, o_ref):
    c = c_ref[...]
    s = (c * _sigmoid(c)).astype(BF16)
    o_ref[...] = jnp.dot(s, w_ref[...].astype(BF16), preferred_element_type=F32) + b_ref[...]


def _adaln(cond, w_mod, b_mod):
    depth, d, n = w_mod.shape
    mc = cond.shape[0]
    tn = _pick_tile(n, 1024)
    return pl.pallas_call(
        _adaln_kernel,
        grid=(depth, n // tn),
        in_specs=[pl.BlockSpec((mc, d), lambda l, j: (0, 0)),
                  pl.BlockSpec((None, d, tn), lambda l, j: (l, 0, j)),
                  pl.BlockSpec((None, 1, tn), lambda l, j: (l, 0, j))],
        out_specs=pl.BlockSpec((None, mc, tn), lambda l, j: (l, 0, j)),
        out_shape=jax.ShapeDtypeStruct((depth, mc, n), F32),
        compiler_params=_cparams(2), name="adaln",
    )(cond, w_mod, b_mod.reshape(depth, 1, n))


def _embed_kernel(xp_ref, xs_ref, pos_ref, o_ref, *, n_ctx_tiles):
    i = pl.program_id(0)

    @pl.when(i < n_ctx_tiles)
    def _():
        o_ref[...] = xp_ref[...]

    @pl.when(i >= n_ctx_tiles)
    def _():
        o_ref[...] = xs_ref[...] + pos_ref[...]


def _embed(xp, xs, pos, tm):
    n_ctx, d = xp.shape
    n_dec = xs.shape[0]
    dec_seq = pos.shape[0]
    nct = n_ctx // tm
    ppt = dec_seq // tm
    return pl.pallas_call(
        functools.partial(_embed_kernel, n_ctx_tiles=nct),
        grid=((n_ctx + n_dec) // tm,),
        in_specs=[pl.BlockSpec((tm, d), lambda i: (jnp.minimum(i, nct - 1), 0)),
                  pl.BlockSpec((tm, d), lambda i: (jnp.maximum(i - nct, 0), 0)),
                  pl.BlockSpec((tm, d), lambda i: (jnp.maximum(i - nct, 0) % ppt, 0))],
        out_specs=pl.BlockSpec((tm, d), lambda i: (i, 0)),
        out_shape=jax.ShapeDtypeStruct((n_ctx + n_dec, d), F32),
        compiler_params=_cparams(1), name="embed",
    )(xp, xs, pos)


def _grid_pos_embed(length, d):
    rows = length // GRID_W
    r = jnp.repeat(jnp.arange(rows), GRID_W)
    col = jnp.tile(jnp.arange(GRID_W), rows)
    quarter = d // 4
    omega = 1.0 / (10000.0 ** (jnp.arange(quarter, dtype=F32) / quarter))

    def emb(p):
        ang = p[:, None].astype(F32) * omega[None, :]
        return jnp.concatenate([jnp.sin(ang), jnp.cos(ang)], axis=-1)

    return jnp.concatenate([emb(r), emb(col)], axis=-1).astype(F32)


def _norm_mod_value(y, g, mod_ref, shift_row):
    ms = jnp.mean(y * y, axis=-1, keepdims=True)
    n = (y * lax.rsqrt(ms + EPS)) * g
    return n * (1.0 + mod_ref[shift_row + 1:shift_row + 2, :]) + mod_ref[shift_row:shift_row + 1, :]


def _norm_mod_kernel(y_ref, g_ref, mod_ref, h_ref, *, shift_row):
    h_ref[...] = _norm_mod_value(y_ref[...], g_ref[...], mod_ref, shift_row).astype(h_ref.dtype)


def _norm_mod(y, g, mod_l, shift_row, n_ctx, dec_seq, tm):
    t, d = y.shape
    return pl.pallas_call(
        functools.partial(_norm_mod_kernel, shift_row=shift_row),
        grid=(t // tm,),
        in_specs=[pl.BlockSpec((tm, d), lambda i: (i, 0)),
                  pl.BlockSpec((1, d), lambda i: (0, 0)),
                  pl.BlockSpec((None, 6, d), lambda i: (_mod_row(i, tm, n_ctx, dec_seq), 0, 0))],
        out_specs=pl.BlockSpec((tm, d), lambda i: (i, 0)),
        out_shape=jax.ShapeDtypeStruct((t, d), BF16),
        compiler_params=_cparams(1), name="norm_mod",
    )(y, g.reshape(1, d), mod_l)


def _norm_router_kernel(y_ref, g_ref, mod_ref, r_ref, rb_ref, h_ref, ids_ref, gates_ref, *, shift_row):
    h = _norm_mod_value(y_ref[...], g_ref[...], mod_ref, shift_row)
    h_ref[...] = h
    logits = jnp.dot(h, r_ref[...], preferred_element_type=F32,
                     precision=lax.Precision.HIGHEST) + rb_ref[...]
    lane = lax.broadcasted_iota(jnp.int32, logits.shape, 1).astype(F32)
    big = float(LANES)
    m1 = jnp.max(logits, axis=-1, keepdims=True)
    i1 = jnp.min(jnp.where(logits == m1, lane, big), axis=-1, keepdims=True)
    l2 = jnp.where(lane == i1, 2.0 * NEG_BIG, logits)
    m2 = jnp.max(l2, axis=-1, keepdims=True)
    i2 = jnp.min(jnp.where(l2 == m2, lane, big), axis=-1, keepdims=True)
    e = jnp.exp(m2 - m1)
    g1 = 1.0 / (1.0 + e)
    g2 = e / (1.0 + e)
    ids_ref[...] = jnp.where(lane == 0.0, i1, jnp.where(lane == 1.0, i2, 0.0)).astype(jnp.int32)
    gates_ref[...] = jnp.where(lane == 0.0, g1, jnp.where(lane == 1.0, g2, 0.0))


def _norm_router(y, g, mod_l, shift_row, router, router_b, n_ctx, dec_seq, tm):
    t, d = y.shape
    n_exp = router.shape[1]
    rp = jnp.zeros((d, LANES), F32).at[:, :n_exp].set(router)
    rbp = jnp.full((1, LANES), NEG_BIG, F32).at[0, :n_exp].set(router_b)
    return pl.pallas_call(
        functools.partial(_norm_router_kernel, shift_row=shift_row),
        grid=(t // tm,),
        in_specs=[pl.BlockSpec((tm, d), lambda i: (i, 0)),
                  pl.BlockSpec((1, d), lambda i: (0, 0)),
                  pl.BlockSpec((None, 6, d), lambda i: (_mod_row(i, tm, n_ctx, dec_seq), 0, 0)),
                  pl.BlockSpec((d, LANES), lambda i: (0, 0)),
                  pl.BlockSpec((1, LANES), lambda i: (0, 0))],
        out_specs=[pl.BlockSpec((tm, d), lambda i: (i, 0)),
                   pl.BlockSpec((tm, LANES), lambda i: (i, 0)),
                   pl.BlockSpec((tm, LANES), lambda i: (i, 0))],
        out_shape=[jax.ShapeDtypeStruct((t, d), F32),
                   jax.ShapeDtypeStruct((t, LANES), jnp.int32),
                   jax.ShapeDtypeStruct((t, LANES), F32)],
        compiler_params=_cparams(1), name="norm_router",
    )(y, g.reshape(1, d), mod_l, rp, rbp)


def _final_norm_kernel(y_ref, g_ref, op_ref, os_ref, *, n_ctx_tiles):
    i = pl.program_id(0)
    y = y_ref[...]
    ms = jnp.mean(y * y, axis=-1, keepdims=True)
    n = (y * lax.rsqrt(ms + EPS)) * g_ref[...]

    @pl.when(i < n_ctx_tiles)
    def _():
        op_ref[...] = n

    @pl.when(i >= n_ctx_tiles)
    def _():
        os_ref[...] = n


def _final_norm(y, g, n_ctx, tm):
    t, d = y.shape
    nct = n_ctx // tm
    return pl.pallas_call(
        functools.partial(_final_norm_kernel, n_ctx_tiles=nct),
        grid=(t // tm,),
        in_specs=[pl.BlockSpec((tm, d), lambda i: (i, 0)),
                  pl.BlockSpec((1, d), lambda i: (0, 0))],
        out_specs=[pl.BlockSpec((tm, d), lambda i: (jnp.minimum(i, nct - 1), 0)),
                   pl.BlockSpec((tm, d), lambda i: (jnp.maximum(i - nct, 0), 0))],
        out_shape=[jax.ShapeDtypeStruct((n_ctx, d), F32),
                   jax.ShapeDtypeStruct((t - n_ctx, d), F32)],
        compiler_params=_cparams(1), name="final_norm",
    )(y, g.reshape(1, d))


def _cast_at_first_row_tile(w_refs, wb_refs):
    @pl.when(pl.program_id(1) == 0)
    def _():
        for w, wb in zip(w_refs, wb_refs):
            wb[...] = w[...].astype(BF16)


def _mm_act_kernel(x_ref, w_ref, o_ref, wb, *, act):
    _cast_at_first_row_tile((w_ref,), (wb,))
    acc = jnp.dot(x_ref[...], wb[...], preferred_element_type=F32)
    if act == "gelu":
        acc = _gelu_tanh(acc)
    o_ref[...] = acc.astype(o_ref.dtype)


def _mm_act(x, w, act, out_dtype, tm, tn_pref=512):
    t, k = x.shape
    n = w.shape[1]
    tn = _pick_tile(n, tn_pref)
    return pl.pallas_call(
        functools.partial(_mm_act_kernel, act=act),
        grid=(n // tn, t // tm),
        in_specs=[pl.BlockSpec((tm, k), lambda j, i: (i, 0)),
                  pl.BlockSpec((k, tn), lambda j, i: (0, j))],
        out_specs=pl.BlockSpec((tm, tn), lambda j, i: (i, j)),
        out_shape=jax.ShapeDtypeStruct((t, n), out_dtype),
        scratch_shapes=[pltpu.VMEM((k, tn), BF16)],
        compiler_params=_cparams(2), name="mm_" + str(act),
    )(x, w)


def _mm_swiglu_kernel(x_ref, w1_ref, w3_ref, o_ref, wb1, wb3):
    _cast_at_first_row_tile((w1_ref, w3_ref), (wb1, wb3))
    x = x_ref[...]
    a = jnp.dot(x, wb1[...], preferred_element_type=F32)
    b = jnp.dot(x, wb3[...], preferred_element_type=F32)
    o_ref[...] = ((a * _sigmoid(a)) * b).astype(o_ref.dtype)


def _mm_swiglu(x, w1, w3, tm, tn_pref=512):
    t, k = x.shape
    n = w1.shape[1]
    tn = _pick_tile(n, tn_pref)
    return pl.pallas_call(
        _mm_swiglu_kernel,
        grid=(n // tn, t // tm),
        in_specs=[pl.BlockSpec((tm, k), lambda j, i: (i, 0)),
                  pl.BlockSpec((k, tn), lambda j, i: (0, j)),
                  pl.BlockSpec((k, tn), lambda j, i: (0, j))],
        out_specs=pl.BlockSpec((tm, tn), lambda j, i: (i, j)),
        out_shape=jax.ShapeDtypeStruct((t, n), BF16),
        scratch_shapes=[pltpu.VMEM((k, tn), BF16), pltpu.VMEM((k, tn), BF16)],
        compiler_params=_cparams(2), name="mm_swiglu",
    )(x, w1, w3)


def _mm_sconv_kernel(x_ref, wb_ref, wc_ref, wx_ref, cw_ref, o_ref, sb, sc, sx, *, n_ctx_tiles, seq_ctx):
    _cast_at_first_row_tile((wb_ref, wc_ref, wx_ref), (sb, sc, sx))
    i = pl.program_id(1)
    is_ctx = i < n_ctx_tiles
    x = x_ref[...]
    bg = jnp.dot(x, sb[...], preferred_element_type=F32)
    p = jnp.dot(x, sc[...], preferred_element_type=F32) * jnp.dot(x, sx[...], preferred_element_type=F32)
    tm = p.shape[0]
    row = lax.broadcasted_iota(jnp.int32, p.shape, 0)
    pos = jnp.where(is_ctx, row % seq_ctx, row)
    last_pos = jnp.where(is_ctx, seq_ctx - 1, tm - 1)
    p_prev = jnp.where(pos == 0, 0.0, pltpu.roll(p, 1, axis=0))
    p_next = jnp.where(pos == last_pos, 0.0, pltpu.roll(p, tm - 1, axis=0))
    cw = cw_ref[...]
    conv = (cw[0:1] * p_prev + cw[1:2] * p) + cw[2:3] * p_next
    o_ref[...] = (bg * conv).astype(o_ref.dtype)


def _mm_sconv(x, w_in, conv_w, n_ctx, seq_ctx, tm, tn_pref=512):
    t, k = x.shape
    d = conv_w.shape[1]
    tn = _pick_tile(d, tn_pref)
    nj = d // tn
    return pl.pallas_call(
        functools.partial(_mm_sconv_kernel, n_ctx_tiles=n_ctx // tm, seq_ctx=seq_ctx),
        grid=(nj, t // tm),
        in_specs=[pl.BlockSpec((tm, k), lambda j, i: (i, 0)),
                  pl.BlockSpec((k, tn), lambda j, i: (0, j)),
                  pl.BlockSpec((k, tn), lambda j, i: (0, nj + j)),
                  pl.BlockSpec((k, tn), lambda j, i: (0, 2 * nj + j)),
                  pl.BlockSpec((conv_w.shape[0], tn), lambda j, i: (0, j))],
        out_specs=pl.BlockSpec((tm, tn), lambda j, i: (i, j)),
        out_shape=jax.ShapeDtypeStruct((t, d), BF16),
        scratch_shapes=[pltpu.VMEM((k, tn), BF16)] * 3,
        compiler_params=_cparams(2), name="mm_sconv",
    )(x, w_in, w_in, w_in, conv_w)


def _mm_residual_kernel(x_ref, w_ref, y_ref, mod_ref, o_ref, wb, *, gate_row):
    _cast_at_first_row_tile((w_ref,), (wb,))
    acc = jnp.dot(x_ref[...], wb[...], preferred_element_type=F32)
    o_ref[...] = y_ref[...] + mod_ref[gate_row:gate_row + 1, :] * acc


def _mm_residual(x, w, y, mod_l, gate_row, n_ctx, dec_seq, tm, tn_pref=512):
    t, k = x.shape
    n = w.shape[1]
    tn = _pick_tile(n, tn_pref)
    return pl.pallas_call(
        functools.partial(_mm_residual_kernel, gate_row=gate_row),
        grid=(n // tn, t // tm),
        in_specs=[pl.BlockSpec((tm, k), lambda j, i: (i, 0)),
                  pl.BlockSpec((k, tn), lambda j, i: (0, j)),
                  pl.BlockSpec((tm, tn), lambda j, i: (i, j)),
                  pl.BlockSpec((None, 6, tn), lambda j, i: (_mod_row(i, tm, n_ctx, dec_seq), 0, j))],
        out_specs=pl.BlockSpec((tm, tn), lambda j, i: (i, j)),
        out_shape=jax.ShapeDtypeStruct((t, n), F32),
        scratch_shapes=[pltpu.VMEM((k, tn), BF16)],
        compiler_params=_cparams(2), name="mm_residual",
    )(x, w, y, mod_l)


def _rg_core_kernel(gate_ref, x_ref, cw_ref, cb_ref, wg_ref, bg_ref, lam_ref, h0_ref,
                    y_ref, st_ref, af_s, bf_s, ab_s, bb_s, *, n_ctx_tiles, seg, tm, cg):
    i = pl.program_id(1)
    is_ctx = i < n_ctx_tiles
    nseg = tm // seg
    cw = cw_ref[...]
    cb = cb_ref[...]
    lam = lam_ref[...]
    softplus_neg_lam = jnp.maximum(-lam, 0.0) + jnp.log1p(jnp.exp(-jnp.abs(lam)))
    neg_c_sp = (-RG_C) * softplus_neg_lam
    zeros8 = jnp.zeros((SUBLANES, cg), F32)

    for s in range(nseg):
        c0 = s * seg
        cur = x_ref[pl.ds(c0, seg), :]
        prev8 = zeros8 if s == 0 else jnp.where(is_ctx, 0.0, x_ref[pl.ds(c0 - SUBLANES, SUBLANES), :])
        next8 = zeros8 if s == nseg - 1 else jnp.where(is_ctx, 0.0, x_ref[pl.ds(c0 + seg, SUBLANES), :])
        xe = jnp.concatenate([prev8, cur, next8], axis=0)
        xc = (((cw[0:1] * xe[6:6 + seg] + cw[1:2] * xe[7:7 + seg]) + cw[2:3] * cur)
              + cw[3:4] * xe[9:9 + seg]) + cb
        g = jnp.dot(xc.astype(BF16), wg_ref[...], preferred_element_type=F32) + bg_ref[...]
        for d, (a_s, b_s) in enumerate(((af_s, bf_s), (ab_s, bb_s))):
            r = _sigmoid(g[:, (2 * d) * cg:(2 * d + 1) * cg])
            ig = _sigmoid(g[:, (2 * d + 1) * cg:(2 * d + 2) * cg])
            log_a = neg_c_sp[d:d + 1] * r
            a = jnp.exp(log_a)
            one_minus_a2 = -jnp.tanh(log_a) * (a * a + 1.0)
            a_s[pl.ds(c0, seg), :] = a
            b_s[pl.ds(c0, seg), :] = jnp.sqrt(one_minus_a2) * (ig * xc)

    row8 = lax.broadcasted_iota(jnp.int32, (SUBLANES, cg), 0)
    h0 = h0_ref[...]
    hf = jnp.broadcast_to(h0[0:1], (SUBLANES, cg))
    hb = jnp.broadcast_to(h0[1:2], (SUBLANES, cg))
    ntile = seg // SUBLANES

    def step8(a_s, b_s, t0, h, rows):
        a = a_s[pl.ds(t0, SUBLANES), :]
        b = b_s[pl.ds(t0, SUBLANES), :]
        out = b
        for r in rows:
            hn = a * h + b
            out = jnp.where(row8 == r, hn, out)
            h = jnp.broadcast_to(hn[r:r + 1, :], (SUBLANES, cg))
        b_s[pl.ds(t0, SUBLANES), :] = out
        return h

    for q in range(nseg):
        sf, sb = q, nseg - 1 - q
        if q > 0:
            hf = jnp.where(is_ctx, 0.0, hf)
            hb = jnp.where(is_ctx, 0.0, hb)

        def body(k, carry, sf=sf, sb=sb):
            hf, hb = carry
            tf = pl.multiple_of(sf * seg + k * SUBLANES, SUBLANES)
            tb = pl.multiple_of(sb * seg + (ntile - 1 - k) * SUBLANES, SUBLANES)
            hf = step8(af_s, bf_s, tf, hf, range(SUBLANES))
            hb = step8(ab_s, bb_s, tb, hb, reversed(range(SUBLANES)))
            return hf, hb

        hf, hb = lax.fori_loop(0, ntile, body, (hf, hb))
        st_ref[sf:sf + 1, :] = hf[0:1]
        st_ref[nseg + sb:nseg + sb + 1, :] = hb[0:1]

    for s in range(nseg):
        c0 = s * seg
        hsum = bf_s[pl.ds(c0, seg), :] + bb_s[pl.ds(c0, seg), :]
        y_ref[pl.ds(c0, seg), :] = (hsum * _gelu_tanh(gate_ref[pl.ds(c0, seg), :])).astype(y_ref.dtype)


def _rg_core(z, conv_w, conv_b, w_a, b_a, w_x, b_x, lam, h0, n_ctx, seq_ctx, tm):
    t = z.shape[0]
    r = conv_w.shape[1]
    heads, hw = w_a.shape[1], w_a.shape[2]
    hpg = 4
    while (hpg * hw) % LANES:
        hpg *= 2
    ng = heads // hpg
    cg = hpg * hw
    nseg = tm // seq_ctx
    ntiles = t // tm

    def blockdiag(w):
        w5 = w.reshape(2, ng, hpg, hw, hw)
        eye = jnp.eye(hpg, dtype=w.dtype)
        return jnp.einsum("dghio,hk->dghiko", w5, eye).reshape(2, ng, cg, cg)

    wa, wx = blockdiag(w_a), blockdiag(w_x)
    wg = jnp.concatenate([wa[0], wx[0], wa[1], wx[1]], axis=-1).astype(BF16)
    ba, bx = b_a.reshape(2, ng, 1, cg), b_x.reshape(2, ng, 1, cg)
    bg = jnp.concatenate([ba[0], bx[0], ba[1], bx[1]], axis=-1)

    return pl.pallas_call(
        functools.partial(_rg_core_kernel, n_ctx_tiles=n_ctx // tm, seg=seq_ctx, tm=tm, cg=cg),
        grid=(ng, ntiles),
        in_specs=[pl.BlockSpec((tm, cg), lambda g, i: (i, g)),
                  pl.BlockSpec((tm, cg), lambda g, i: (i, ng + g)),
                  pl.BlockSpec((conv_w.shape[0], cg), lambda g, i: (0, g)),
                  pl.BlockSpec((1, cg), lambda g, i: (0, g)),
                  pl.BlockSpec((None, cg, 4 * cg), lambda g, i: (g, 0, 0)),
                  pl.BlockSpec((None, 1, 4 * cg), lambda g, i: (g, 0, 0)),
                  pl.BlockSpec((2, cg), lambda g, i: (0, g)),
                  pl.BlockSpec((None, 2, cg), lambda g, i: (i, 0, g))],
        out_specs=[pl.BlockSpec((tm, cg), lambda g, i: (i, g)),
                   pl.BlockSpec((None, 2 * nseg, cg), lambda g, i: (i, 0, g))],
        out_shape=[jax.ShapeDtypeStruct((t, r), BF16),
                   jax.ShapeDtypeStruct((ntiles, 2 * nseg, r), F32)],
        scratch_shapes=[pltpu.VMEM((tm, cg), F32)] * 4,
        compiler_params=_cparams(2), name="rg_core",
    )(z, z, conv_w, conv_b.reshape(1, r), wg, bg, lam, h0)


def _sgu_core_kernel(u_ref, v_ref, g_ref, ws_ref, bs_ref, y_ref, vn_s, *, chunk, gw, ngroups, tm):
    v = v_ref[...]
    ms = jnp.mean(v * v, axis=-1, keepdims=True)
    vn_s[...] = ((v * lax.rsqrt(ms + EPS)) * g_ref[...]).astype(BF16)
    for c in range(tm // chunk):
        rows = pl.ds(c * chunk, chunk)
        for g in range(ngroups):
            cols = pl.ds(g * gw, gw)
            vm = jnp.dot(ws_ref[g], vn_s[rows, cols], preferred_element_type=F32) + bs_ref[:, g:g + 1]
            y_ref[rows, cols] = (u_ref[rows, cols] * vm).astype(y_ref.dtype)


def _sgu_core(z, norm_g, w_s, b_s, tm):
    t = z.shape[0]
    w = norm_g.shape[0]
    ngroups, chunk = w_s.shape[0], w_s.shape[1]
    gw = w // ngroups
    return pl.pallas_call(
        functools.partial(_sgu_core_kernel, chunk=chunk, gw=gw, ngroups=ngroups, tm=tm),
        grid=(t // tm,),
        in_specs=[pl.BlockSpec((tm, w), lambda i: (i, 0)),
                  pl.BlockSpec((tm, w), lambda i: (i, 1)),
                  pl.BlockSpec((1, w), lambda i: (0, 0)),
                  pl.BlockSpec((ngroups, chunk, chunk), lambda i: (0, 0, 0)),
                  pl.BlockSpec((chunk, ngroups), lambda i: (0, 0))],
        out_specs=pl.BlockSpec((tm, w), lambda i: (i, 0)),
        out_shape=jax.ShapeDtypeStruct((t, w), BF16),
        scratch_shapes=[pltpu.VMEM((tm, w), BF16)],
        compiler_params=_cparams(1), name="sgu_core",
    )(z, z, norm_g.reshape(1, w), w_s.astype(BF16), b_s.T)


def _moe_plan(ids, n_exp, bm):
    t = ids.shape[0]
    e = ids[:, :TOP_K].reshape(-1)
    oh = (e[:, None] == jnp.arange(n_exp, dtype=jnp.int32)[None, :]).astype(jnp.int32)
    csum = jnp.cumsum(oh, axis=0)
    rank = jnp.sum((csum - oh) * oh, axis=1)
    counts = csum[-1]
    padded = ((counts + bm - 1) // bm) * bm
    ends = jnp.cumsum(padded)
    starts = ends - padded
    dest = (jnp.sum(starts[None, :] * oh, axis=1) + rank).astype(jnp.int32)
    p = TOP_K * t + n_exp * bm
    src = jnp.zeros((p,), jnp.int32).at[dest].set(jnp.arange(TOP_K * t, dtype=jnp.int32) // TOP_K)
    nb = p // bm
    blk_start = jnp.arange(nb, dtype=jnp.int32) * bm
    blk_exp = jnp.minimum(jnp.sum((blk_start[:, None] >= ends[None, :]).astype(jnp.int32), axis=1),
                          n_exp - 1).astype(jnp.int32)
    n_used = (ends[-1] // bm).astype(jnp.int32).reshape(1)
    return dest, src, blk_exp, n_used


def _gather_rows_kernel(src_ref, nused_ref, h_hbm, o_ref, buf, sem, *, rows):
    b = pl.program_id(0)

    @pl.when(b < nused_ref[0])
    def _():
        base = b * rows

        def issue(r, carry):
            tok = src_ref[base + r]
            pltpu.make_async_copy(h_hbm.at[pl.ds(tok, 1), :], buf.at[pl.ds(r, 1), :], sem).start()
            return carry

        lax.fori_loop(0, rows, issue, 0)
        pltpu.make_async_copy(h_hbm.at[pl.ds(0, rows), :], buf, sem).wait()
        o_ref[...] = buf[...].astype(o_ref.dtype)

    @pl.when(b >= nused_ref[0])
    def _():
        o_ref[...] = jnp.zeros_like(o_ref)


def _gather_rows(h, src, n_used, bm):
    d = h.shape[1]
    p = src.shape[0]
    return pl.pallas_call(
        functools.partial(_gather_rows_kernel, rows=bm),
        grid_spec=pltpu.PrefetchScalarGridSpec(
            num_scalar_prefetch=2,
            grid=(p // bm,),
            in_specs=[pl.BlockSpec(memory_space=pl.ANY)],
            out_specs=pl.BlockSpec((bm, d), lambda b, src, nu: (b, 0)),
            scratch_shapes=[pltpu.VMEM((bm, d), F32), pltpu.SemaphoreType.DMA(())]),
        out_shape=jax.ShapeDtypeStruct((p, d), BF16),
        compiler_params=_cparams(1), name="moe_gather",
    )(src, n_used, h)


def _expert_changed(be_ref, b):
    return jnp.logical_or(b == 0, be_ref[b] != be_ref[jnp.maximum(b - 1, 0)])


def _grouped_swiglu_kernel(be_ref, nused_ref, x_ref, w1_ref, w3_ref, o_ref, wb1, wb3):
    b = pl.program_id(1)

    @pl.when(b < nused_ref[0])
    def _():
        @pl.when(_expert_changed(be_ref, b))
        def _():
            wb1[...] = w1_ref[...].astype(BF16)
            wb3[...] = w3_ref[...].astype(BF16)

        x = x_ref[...]
        a = jnp.dot(x, wb1[...], preferred_element_type=F32)
        c = jnp.dot(x, wb3[...], preferred_element_type=F32)
        o_ref[...] = ((a * _sigmoid(a)) * c).astype(o_ref.dtype)

    @pl.when(b >= nused_ref[0])
    def _():
        o_ref[...] = jnp.zeros_like(o_ref)


def _grouped_swiglu(xs, w1, w3, blk_exp, n_used, bm, tn_pref=512):
    p, k = xs.shape
    n = w1.shape[2]
    tn = _pick_tile(n, tn_pref)

    def blk(b, nu):
        return jnp.minimum(b, nu[0] - 1)

    return pl.pallas_call(
        _grouped_swiglu_kernel,
        grid_spec=pltpu.PrefetchScalarGridSpec(
            num_scalar_prefetch=2,
            grid=(n // tn, p // bm),
            in_specs=[pl.BlockSpec((bm, k), lambda j, b, be, nu: (blk(b, nu), 0)),
                      pl.BlockSpec((None, k, tn), lambda j, b, be, nu: (be[blk(b, nu)], 0, j)),
                      pl.BlockSpec((None, k, tn), lambda j, b, be, nu: (be[blk(b, nu)], 0, j))],
            out_specs=pl.BlockSpec((bm, tn), lambda j, b, be, nu: (b, j)),
            scratch_shapes=[pltpu.VMEM((k, tn), BF16), pltpu.VMEM((k, tn), BF16)]),
        out_shape=jax.ShapeDtypeStruct((p, n), BF16),
        compiler_params=_cparams(2), name="moe_swiglu",
    )(blk_exp, n_used, xs, w1, w3)


def _grouped_down_kernel(be_ref, nused_ref, x_ref, w_ref, o_ref, wb):
    b = pl.program_id(1)

    @pl.when(b < nused_ref[0])
    def _():
        @pl.when(_expert_changed(be_ref, b))
        def _():
            wb[...] = w_ref[...].astype(BF16)

        o_ref[...] = jnp.dot(x_ref[...], wb[...], preferred_element_type=F32)

    @pl.when(b >= nused_ref[0])
    def _():
        o_ref[...] = jnp.zeros_like(o_ref)


def _grouped_down(gs, w2, blk_exp, n_used, bm, tn_pref=512):
    p, k = gs.shape
    n = w2.shape[2]
    tn = _pick_tile(n, tn_pref)

    def blk(b, nu):
        return jnp.minimum(b, nu[0] - 1)

    return pl.pallas_call(
        _grouped_down_kernel,
        grid_spec=pltpu.PrefetchScalarGridSpec(
            num_scalar_prefetch=2,
            grid=(n // tn, p // bm),
            in_specs=[pl.BlockSpec((bm, k), lambda j, b, be, nu: (blk(b, nu), 0)),
                      pl.BlockSpec((None, k, tn), lambda j, b, be, nu: (be[blk(b, nu)], 0, j))],
            out_specs=pl.BlockSpec((bm, tn), lambda j, b, be, nu: (b, j)),
            scratch_shapes=[pltpu.VMEM((k, tn), BF16)]),
        out_shape=jax.ShapeDtypeStruct((p, n), F32),
        compiler_params=_cparams(2), name="moe_down",
    )(blk_exp, n_used, gs, w2)


def _combine_kernel(dest_ref, os_hbm, y_ref, gates_ref, mod_ref, o_ref, buf, sem, *, rows, gate_row):
    i = pl.program_id(0)
    base = i * rows

    def issue(r, carry):
        for k in range(TOP_K):
            slot = dest_ref[TOP_K * (base + r) + k]
            pltpu.make_async_copy(os_hbm.at[pl.ds(slot, 1), :], buf.at[k, pl.ds(r, 1), :], sem).start()
        return carry

    lax.fori_loop(0, rows, issue, 0)
    for k in range(TOP_K):
        pltpu.make_async_copy(os_hbm.at[pl.ds(0, rows), :], buf.at[k], sem).wait()
    gates = gates_ref[...]
    f = gates[:, 0:1] * buf[0] + gates[:, 1:2] * buf[1]
    o_ref[...] = y_ref[...] + mod_ref[gate_row:gate_row + 1, :] * f


def _combine(os_, dest, gates, y, mod_l, gate_row, n_ctx, dec_seq, tm):
    t, d = y.shape
    return pl.pallas_call(
        functools.partial(_combine_kernel, rows=tm, gate_row=gate_row),
        grid_spec=pltpu.PrefetchScalarGridSpec(
            num_scalar_prefetch=1,
            grid=(t // tm,),
            in_specs=[pl.BlockSpec(memory_space=pl.ANY),
                      pl.BlockSpec((tm, d), lambda i, dst: (i, 0)),
                      pl.BlockSpec((tm, LANES), lambda i, dst: (i, 0)),
                      pl.BlockSpec((None, 6, d), lambda i, dst: (_mod_row(i, tm, n_ctx, dec_seq), 0, 0))],
            out_specs=pl.BlockSpec((tm, d), lambda i, dst: (i, 0)),
            scratch_shapes=[pltpu.VMEM((TOP_K, tm, d), F32), pltpu.SemaphoreType.DMA(())]),
        out_shape=jax.ShapeDtypeStruct((t, d), F32),
        compiler_params=_cparams(1), name="moe_combine",
    )(dest, os_, y, gates, mod_l)


def kernel(x_prompt, x_sample, state_rglru, c, c_ctx, norm_mix_g, norm_ffn_g, w_mod, b_mod, final_norm_g, rg_w_in, rg_conv_w, rg_conv_b, rg_w_a, rg_b_a, rg_w_x, rg_b_x, rg_lam, rg_w_out, sg_w_in, sg_norm_g, sg_w_s, sg_b_s, sg_w_out, sc_w_in, sc_conv_w, sc_w_out, ff_w1, ff_w3, ff_w2, moe_router, moe_router_b, moe_w1, moe_w3, moe_w2):
    batch, seq, d = x_prompt.shape
    dec_batch, dec_seq, _ = x_sample.shape
    depth = w_mod.shape[0]
    n_ctx = batch * seq
    d_rnn = rg_w_out.shape[1]
    n_exp = moe_router.shape[2]
    chunk = sg_w_s.shape[2]

    tm = dec_seq
    assert dec_seq % seq == 0 and n_ctx % tm == 0 and seq % SUBLANES == 0
    tm_half = max(tm // 2, chunk)
    tm_small = max(tm // 4, chunk)
    assert tm % tm_half == 0 and tm % tm_small == 0 and tm_small % chunk == 0
    moe_bm = tm_half

    n_cond = 1 + dec_batch
    cond = jnp.zeros((-(-n_cond // SUBLANES) * SUBLANES, d), F32)
    cond = cond.at[0].set(c_ctx).at[1:n_cond].set(c)
    mod = _adaln(cond, w_mod, b_mod)[:, :n_cond].reshape(depth, n_cond, 6, d)

    y = _embed(x_prompt.reshape(n_ctx, d), x_sample.reshape(dec_batch * dec_seq, d),
               _grid_pos_embed(dec_seq, d), tm_small)
    n_tiles = y.shape[0] // tm

    states = []
    for i in range(depth):
        mod_l = mod[i]
        kind, j = i % 3, i // 3
        if kind == 0:
            h = _norm_mod(y, norm_mix_g[i], mod_l, 0, n_ctx, dec_seq, tm_half)
            z = _mm_act(h, rg_w_in[j], None, F32, tm)
            h0 = jnp.concatenate([jnp.zeros((n_ctx // tm, 2, d_rnn), F32),
                                  state_rglru[:, j].astype(F32)], axis=0)
            mix, st = _rg_core(z, rg_conv_w[j], rg_conv_b[j], rg_w_a[j], rg_b_a[j], rg_w_x[j], rg_b_x[j],
                               rg_lam[j], h0, n_ctx, seq, tm)
            nseg = tm // seq
            st = st[:n_ctx // tm].reshape(n_ctx // tm, 2, nseg, d_rnn)
            states.append(jnp.transpose(st, (0, 2, 1, 3)).reshape(batch, 2, d_rnn))
        elif kind == 1:
            h = _norm_mod(y, norm_mix_g[i], mod_l, 0, n_ctx, dec_seq, tm_half)
            z = _mm_act(h, sg_w_in[j], "gelu", F32, tm)
            mix = _sgu_core(z, sg_norm_g[j], sg_w_s[j], sg_b_s[j], tm_small)
        else:
            h = _norm_mod(y, norm_mix_g[i], mod_l, 0, n_ctx, dec_seq, tm_half)
            mix = _mm_sconv(h, sc_w_in[j], sc_conv_w[j], n_ctx, seq, tm)
        w_out = (rg_w_out, sg_w_out, sc_w_out)[kind][j]
        y = _mm_residual(mix, w_out, y, mod_l, 2, n_ctx, dec_seq, tm_half)

        f = i // 2
        if i % 2 == 0:
            h = _norm_mod(y, norm_ffn_g[i], mod_l, 3, n_ctx, dec_seq, tm_half)
            g = _mm_swiglu(h, ff_w1[f], ff_w3[f], tm)
            y = _mm_residual(g, ff_w2[f], y, mod_l, 5, n_ctx, dec_seq, tm_half)
        else:
            h32, ids, gates = _norm_router(y, norm_ffn_g[i], mod_l, 3, moe_router[f], moe_router_b[f],
                                           n_ctx, dec_seq, tm_half)
            dest, src, blk_exp, n_used = _moe_plan(ids, n_exp, moe_bm)
            xs = _gather_rows(h32, src, n_used, moe_bm)
            gs = _grouped_swiglu(xs, moe_w1[f], moe_w3[f], blk_exp, n_used, moe_bm)
            os_ = _grouped_down(gs, moe_w2[f], blk_exp, n_used, moe_bm)
            y = _combine(os_, dest, gates, y, mod_l, 5, n_ctx, dec_seq, tm_small)

    y_p, y_s = _final_norm(y, final_norm_g, n_ctx, tm_half)
    new_state = jnp.stack(states, axis=1).astype(x_prompt.dtype)
    return (y_p.reshape(batch, seq, d), y_s.reshape(dec_batch, dec_seq, d), new_state)
```

```python
import functools

import jax
import jax.numpy as jnp
from jax import lax
from jax.experimental import pallas as pl
from jax.experimental.pallas import tpu as pltpu

F32 = jnp.float32
BF16 = jnp.bfloat16

GRID_W = 64
EPS = 1e-6
RG_C = 8.0
TOP_K = 2

LANES = 128
SUBLANES = 8
VMEM_LIMIT_BYTES = 58 * 1024 * 1024
NEG_BIG = -1e30
MXU_ROWS_PER_PASS = 256


def _cparams(n_axes):
    return pltpu.CompilerParams(dimension_semantics=("arbitrary",) * n_axes,
                                vmem_limit_bytes=VMEM_LIMIT_BYTES)


def _pick_tile(n, pref):
    if n <= pref:
        return n
    t = (pref // LANES) * LANES
    while t > LANES and n % t:
        t -= LANES
    assert n % t == 0, (n, pref)
    return t


def _row_passes(rows):
    step = MXU_ROWS_PER_PASS if rows % MXU_ROWS_PER_PASS == 0 else rows
    return [pl.ds(r, step) for r in range(0, rows, step)]


def _sigmoid(x):
    return 1.0 / (1.0 + jnp.exp(-x))


def _gelu_tanh(x):
    c = 0.7978845608028654
    return 0.5 * x * (1.0 + jnp.tanh(c * (x + 0.044715 * (x * x * x))))


def _mod_row(i, tm, n_ctx, dec_seq):
    start = i * tm
    return jnp.where(start < n_ctx, 0, 1 + (start - n_ctx) // dec_seq)


def _adaln_kernel(c_ref, w_ref, b_ref, o_ref):
    c = c_ref[...]
    s = (c * _sigmoid(c)).astype(BF16)
    o_ref[...] = jnp.dot(s, w_ref[...].astype(BF16), preferred_element_type=F32) + b_ref[...]


def _adaln(cond, w_mod, b_mod):
    depth, d, n = w_mod.shape
    mc = cond.shape[0]
    tn = _pick_tile(n, 1024)
    return pl.pallas_call(
        _adaln_kernel,
        grid=(depth, n // tn),
        in_specs=[pl.BlockSpec((mc, d), lambda l, j: (0, 0)),
                  pl.BlockSpec((None, d, tn), lambda l, j: (l, 0, j)),
                  pl.BlockSpec((None, 1, tn), lambda l, j: (l, 0, j))],
        out_specs=pl.BlockSpec((None, mc, tn), lambda l, j: (l, 0, j)),
        out_shape=jax.ShapeDtypeStruct((depth, mc, n), F32),
        compiler_params=_cparams(2), name="adaln",
    )(cond, w_mod, b_mod.reshape(depth, 1, n))


def _embed_kernel(xp_ref, xs_ref, pos_ref, o_ref, *, n_ctx_tiles):
    i = pl.program_id(0)

    @pl.when(i < n_ctx_tiles)
    def _():
        o_ref[...] = xp_ref[...]

    @pl.when(i >= n_ctx_tiles)
    def _():
        o_ref[...] = xs_ref[...] + pos_ref[...]


def _embed(xp, xs, pos, tm):
    n_ctx, d = xp.shape
    n_dec = xs.shape[0]
    dec_seq = pos.shape[0]
    nct = n_ctx // tm
    ppt = dec_seq // tm
    return pl.pallas_call(
        functools.partial(_embed_kernel, n_ctx_tiles=nct),
        grid=((n_ctx + n_dec) // tm,),
        in_specs=[pl.BlockSpec((tm, d), lambda i: (jnp.minimum(i, nct - 1), 0)),
                  pl.BlockSpec((tm, d), lambda i: (jnp.maximum(i - nct, 0), 0)),
                  pl.BlockSpec((tm, d), lambda i: (jnp.maximum(i - nct, 0) % ppt, 0))],
        out_specs=pl.BlockSpec((tm, d), lambda i: (i, 0)),
        out_shape=jax.ShapeDtypeStruct((n_ctx + n_dec, d), F32),
        compiler_params=_cparams(1), name="embed",
    )(xp, xs, pos)


def _grid_pos_embed(length, d):
    rows = length // GRID_W
    r = jnp.repeat(jnp.arange(rows), GRID_W)
    col = jnp.tile(jnp.arange(GRID_W), rows)
    quarter = d // 4
    omega = 1.0 / (10000.0 ** (jnp.arange(quarter, dtype=F32) / quarter))

    def emb(p):
        ang = p[:, None].astype(F32) * omega[None, :]
        return jnp.concatenate([jnp.sin(ang), jnp.cos(ang)], axis=-1)

    return jnp.concatenate([emb(r), emb(col)], axis=-1).astype(F32)


def _norm_mod_value(y, g, mod_ref, shift_row):
    ms = jnp.mean(y * y, axis=-1, keepdims=True)
    n = (y * lax.rsqrt(ms + EPS)) * g
    return n * (1.0 + mod_ref[shift_row + 1:shift_row + 2, :]) + mod_ref[shift_row:shift_row + 1, :]


def _norm_mod_kernel(y_ref, g_ref, mod_ref, h_ref, *, shift_row):
    h_ref[...] = _norm_mod_value(y_ref[...], g_ref[...], mod_ref, shift_row).astype(h_ref.dtype)


def _norm_mod(y, g, mod, layer, shift_row, n_ctx, dec_seq, tm):
    t, d = y.shape
    return pl.pallas_call(
        functools.partial(_norm_mod_kernel, shift_row=shift_row),
        grid=(t // tm,),
        in_specs=[pl.BlockSpec((tm, d), lambda i: (i, 0)),
                  pl.BlockSpec((1, d), lambda i: (0, 0)),
                  pl.BlockSpec((None, None, 6, d),
                               lambda i: (layer, _mod_row(i, tm, n_ctx, dec_seq), 0, 0))],
        out_specs=pl.BlockSpec((tm, d), lambda i: (i, 0)),
        out_shape=jax.ShapeDtypeStruct((t, d), BF16),
        compiler_params=_cparams(1), name="norm_mod",
    )(y, g.reshape(1, d), mod)


def _norm_router_kernel(y_ref, g_ref, mod_ref, r_ref, rb_ref, h_ref, ids_ref, gates_ref, *,
                        shift_row, n_slab):
    h = _norm_mod_value(y_ref[...], g_ref[...], mod_ref, shift_row)
    tm = h.shape[0]
    for s in range(n_slab):
        h_ref[pl.ds(s, tm, stride=n_slab), :] = h[:, s * LANES:(s + 1) * LANES]
    logits = jnp.dot(h, r_ref[...], preferred_element_type=F32,
                     precision=lax.Precision.HIGHEST) + rb_ref[...]
    lane = lax.broadcasted_iota(jnp.int32, logits.shape, 1).astype(F32)
    big = float(LANES)
    m1 = jnp.max(logits, axis=-1, keepdims=True)
    i1 = jnp.min(jnp.where(logits == m1, lane, big), axis=-1, keepdims=True)
    l2 = jnp.where(lane == i1, 2.0 * NEG_BIG, logits)
    m2 = jnp.max(l2, axis=-1, keepdims=True)
    i2 = jnp.min(jnp.where(l2 == m2, lane, big), axis=-1, keepdims=True)
    e = jnp.exp(m2 - m1)
    g1 = 1.0 / (1.0 + e)
    g2 = e / (1.0 + e)
    ids_ref[...] = jnp.where(lane == 0.0, i1, jnp.where(lane == 1.0, i2, 0.0)).astype(jnp.int32)
    gates_ref[...] = jnp.where(lane == 0.0, g1, jnp.where(lane == 1.0, g2, 0.0))


def _norm_router(y, g, mod, layer, shift_row, router, router_b, n_ctx, dec_seq, tm):
    t, d = y.shape
    n_exp = router.shape[1]
    n_slab = d // LANES
    rp = jnp.zeros((d, LANES), F32).at[:, :n_exp].set(router)
    rbp = jnp.full((1, LANES), NEG_BIG, F32).at[0, :n_exp].set(router_b)
    return pl.pallas_call(
        functools.partial(_norm_router_kernel, shift_row=shift_row, n_slab=n_slab),
        grid=(t // tm,),
        in_specs=[pl.BlockSpec((tm, d), lambda i: (i, 0)),
                  pl.BlockSpec((1, d), lambda i: (0, 0)),
                  pl.BlockSpec((None, None, 6, d),
                               lambda i: (layer, _mod_row(i, tm, n_ctx, dec_seq), 0, 0)),
                  pl.BlockSpec((d, LANES), lambda i: (0, 0)),
                  pl.BlockSpec((1, LANES), lambda i: (0, 0))],
        out_specs=[pl.BlockSpec((tm * n_slab, LANES), lambda i: (i, 0)),
                   pl.BlockSpec((tm, LANES), lambda i: (i, 0)),
                   pl.BlockSpec((tm, LANES), lambda i: (i, 0))],
        out_shape=[jax.ShapeDtypeStruct((t * n_slab, LANES), F32),
                   jax.ShapeDtypeStruct((t, LANES), jnp.int32),
                   jax.ShapeDtypeStruct((t, LANES), F32)],
        compiler_params=_cparams(1), name="norm_router",
    )(y, g.reshape(1, d), mod, rp, rbp)


def _final_norm_kernel(y_ref, g_ref, op_ref, os_ref, *, n_ctx_tiles):
    i = pl.program_id(0)
    y = y_ref[...]
    ms = jnp.mean(y * y, axis=-1, keepdims=True)
    n = (y * lax.rsqrt(ms + EPS)) * g_ref[...]

    @pl.when(i < n_ctx_tiles)
    def _():
        op_ref[...] = n

    @pl.when(i >= n_ctx_tiles)
    def _():
        os_ref[...] = n


def _final_norm(y, g, n_ctx, tm):
    t, d = y.shape
    nct = n_ctx // tm
    return pl.pallas_call(
        functools.partial(_final_norm_kernel, n_ctx_tiles=nct),
        grid=(t // tm,),
        in_specs=[pl.BlockSpec((tm, d), lambda i: (i, 0)),
                  pl.BlockSpec((1, d), lambda i: (0, 0))],
        out_specs=[pl.BlockSpec((tm, d), lambda i: (jnp.minimum(i, nct - 1), 0)),
                   pl.BlockSpec((tm, d), lambda i: (jnp.maximum(i - nct, 0), 0))],
        out_shape=[jax.ShapeDtypeStruct((n_ctx, d), F32),
                   jax.ShapeDtypeStruct((t - n_ctx, d), F32)],
        compiler_params=_cparams(1), name="final_norm",
    )(y, g.reshape(1, d))


def _cast_at_first_row_tile(w_refs, wb_refs):
    @pl.when(pl.program_id(1) == 0)
    def _():
        for w, wb in zip(w_refs, wb_refs):
            wb[...] = w[...].astype(BF16)


def _mm_act_kernel(x_ref, w_ref, o_ref, wb, *, act):
    _cast_at_first_row_tile((w_ref,), (wb,))
    for rows in _row_passes(x_ref.shape[0]):
        acc = jnp.dot(x_ref[rows, :], wb[...], preferred_element_type=F32)
        if act == "gelu":
            acc = _gelu_tanh(acc)
        o_ref[rows, :] = acc.astype(o_ref.dtype)


def _mm_act(x, w, layer, act, out_dtype, tm, tn_pref=512):
    t, k = x.shape
    n = w.shape[2]
    tn = _pick_tile(n, tn_pref)
    return pl.pallas_call(
        functools.partial(_mm_act_kernel, act=act),
        grid=(n // tn, t // tm),
        in_specs=[pl.BlockSpec((tm, k), lambda j, i: (i, 0)),
                  pl.BlockSpec((None, k, tn), lambda j, i: (layer, 0, j))],
        out_specs=pl.BlockSpec((tm, tn), lambda j, i: (i, j)),
        out_shape=jax.ShapeDtypeStruct((t, n), out_dtype),
        scratch_shapes=[pltpu.VMEM((k, tn), BF16)],
        compiler_params=_cparams(2), name="mm_" + str(act),
    )(x, w)


def _swiglu_passes(x_ref, wb1, wb3, o_ref):
    for rows in _row_passes(x_ref.shape[0]):
        x = x_ref[rows, :]
        a = jnp.dot(x, wb1[...], preferred_element_type=F32)
        b = jnp.dot(x, wb3[...], preferred_element_type=F32)
        o_ref[rows, :] = ((a * _sigmoid(a)) * b).astype(o_ref.dtype)


def _mm_swiglu_kernel(x_ref, w1_ref, w3_ref, o_ref, wb1, wb3):
    _cast_at_first_row_tile((w1_ref, w3_ref), (wb1, wb3))
    _swiglu_passes(x_ref, wb1, wb3, o_ref)


def _mm_swiglu(x, w1, w3, layer, tm, tn_pref=512):
    t, k = x.shape
    n = w1.shape[2]
    tn = _pick_tile(n, tn_pref)
    return pl.pallas_call(
        _mm_swiglu_kernel,
        grid=(n // tn, t // tm),
        in_specs=[pl.BlockSpec((tm, k), lambda j, i: (i, 0)),
                  pl.BlockSpec((None, k, tn), lambda j, i: (layer, 0, j)),
                  pl.BlockSpec((None, k, tn), lambda j, i: (layer, 0, j))],
        out_specs=pl.BlockSpec((tm, tn), lambda j, i: (i, j)),
        out_shape=jax.ShapeDtypeStruct((t, n), BF16),
        scratch_shapes=[pltpu.VMEM((k, tn), BF16), pltpu.VMEM((k, tn), BF16)],
        compiler_params=_cparams(2), name="mm_swiglu",
    )(x, w1, w3)


def _mm_sconv_kernel(x_ref, wb_ref, wc_ref, wx_ref, cw_ref, o_ref, sb, sc, sx, *, n_ctx_tiles, seq_ctx):
    _cast_at_first_row_tile((wb_ref, wc_ref, wx_ref), (sb, sc, sx))
    i = pl.program_id(1)
    is_ctx = i < n_ctx_tiles
    x = x_ref[...]
    bg = jnp.dot(x, sb[...], preferred_element_type=F32)
    p = jnp.dot(x, sc[...], preferred_element_type=F32) * jnp.dot(x, sx[...], preferred_element_type=F32)
    tm = p.shape[0]
    row = lax.broadcasted_iota(jnp.int32, p.shape, 0)
    pos = jnp.where(is_ctx, row % seq_ctx, row)
    last_pos = jnp.where(is_ctx, seq_ctx - 1, tm - 1)
    p_prev = jnp.where(pos == 0, 0.0, pltpu.roll(p, 1, axis=0))
    p_next = jnp.where(pos == last_pos, 0.0, pltpu.roll(p, tm - 1, axis=0))
    cw = cw_ref[...]
    conv = (cw[0:1] * p_prev + cw[1:2] * p) + cw[2:3] * p_next
    o_ref[...] = (bg * conv).astype(o_ref.dtype)


def _mm_sconv(x, w_in, conv_w, layer, n_ctx, seq_ctx, tm, tn_pref=512):
    t, k = x.shape
    d = conv_w.shape[2]
    tn = _pick_tile(d, tn_pref)
    nj = d // tn
    return pl.pallas_call(
        functools.partial(_mm_sconv_kernel, n_ctx_tiles=n_ctx // tm, seq_ctx=seq_ctx),
        grid=(nj, t // tm),
        in_specs=[pl.BlockSpec((tm, k), lambda j, i: (i, 0)),
                  pl.BlockSpec((None, k, tn), lambda j, i: (layer, 0, j)),
                  pl.BlockSpec((None, k, tn), lambda j, i: (layer, 0, nj + j)),
                  pl.BlockSpec((None, k, tn), lambda j, i: (layer, 0, 2 * nj + j)),
                  pl.BlockSpec((None, conv_w.shape[1], tn), lambda j, i: (layer, 0, j))],
        out_specs=pl.BlockSpec((tm, tn), lambda j, i: (i, j)),
        out_shape=jax.ShapeDtypeStruct((t, d), BF16),
        scratch_shapes=[pltpu.VMEM((k, tn), BF16)] * 3,
        compiler_params=_cparams(2), name="mm_sconv",
    )(x, w_in, w_in, w_in, conv_w)


def _mm_residual_kernel(x_ref, w_ref, y_ref, mod_ref, o_ref, wb, *, gate_row):
    _cast_at_first_row_tile((w_ref,), (wb,))
    gate = mod_ref[gate_row:gate_row + 1, :]
    for rows in _row_passes(x_ref.shape[0]):
        acc = jnp.dot(x_ref[rows, :], wb[...], preferred_element_type=F32)
        o_ref[rows, :] = y_ref[rows, :] + gate * acc


def _mm_residual(x, w, w_layer, y, mod, layer, gate_row, n_ctx, dec_seq, tm, tn_pref=512):
    t, k = x.shape
    n = w.shape[2]
    tn = _pick_tile(n, tn_pref)
    return pl.pallas_call(
        functools.partial(_mm_residual_kernel, gate_row=gate_row),
        grid=(n // tn, t // tm),
        in_specs=[pl.BlockSpec((tm, k), lambda j, i: (i, 0)),
                  pl.BlockSpec((None, k, tn), lambda j, i: (w_layer, 0, j)),
                  pl.BlockSpec((tm, tn), lambda j, i: (i, j)),
                  pl.BlockSpec((None, None, 6, tn),
                               lambda j, i: (layer, _mod_row(i, tm, n_ctx, dec_seq), 0, j))],
        out_specs=pl.BlockSpec((tm, tn), lambda j, i: (i, j)),
        out_shape=jax.ShapeDtypeStruct((t, n), F32),
        scratch_shapes=[pltpu.VMEM((k, tn), BF16)],
        compiler_params=_cparams(2), name="mm_residual",
    )(x, w, y, mod)


def _rg_core_kernel(gate_ref, x_ref, cw_ref, cb_ref, wg_ref, bg_ref, lam_ref, h0_ref,
                    y_ref, st_ref, af_s, bf_s, ab_s, bb_s, *, n_ctx_tiles, seg, tm, cg):
    i = pl.program_id(1)
    is_ctx = i < n_ctx_tiles
    nseg = tm // seg
    nslab = cg // LANES
    lsub = tm // SUBLANES
    sub_per_seq = seg // lsub
    cw = cw_ref[...]
    cb = cb_ref[...]
    lam = lam_ref[...]
    softplus_neg_lam = jnp.maximum(-lam, 0.0) + jnp.log1p(jnp.exp(-jnp.abs(lam)))
    neg_c_sp = (-RG_C) * softplus_neg_lam
    zeros8 = jnp.zeros((SUBLANES, cg), F32)

    for s in range(nseg):
        c0 = s * seg
        cur = x_ref[pl.ds(c0, seg), :]
        prev8 = zeros8 if s == 0 else jnp.where(is_ctx, 0.0, x_ref[pl.ds(c0 - SUBLANES, SUBLANES), :])
        next8 = zeros8 if s == nseg - 1 else jnp.where(is_ctx, 0.0, x_ref[pl.ds(c0 + seg, SUBLANES), :])
        xe = jnp.concatenate([prev8, cur, next8], axis=0)
        xc = (((cw[0:1] * xe[6:6 + seg] + cw[1:2] * xe[7:7 + seg]) + cw[2:3] * cur)
              + cw[3:4] * xe[9:9 + seg]) + cb
        g = jnp.dot(xc.astype(BF16), wg_ref[...], preferred_element_type=F32) + bg_ref[...]
        for d, (a_s, b_s) in enumerate(((af_s, bf_s), (ab_s, bb_s))):
            r = _sigmoid(g[:, (2 * d) * cg:(2 * d + 1) * cg])
            ig = _sigmoid(g[:, (2 * d + 1) * cg:(2 * d + 2) * cg])
            log_a = neg_c_sp[d:d + 1] * r
            a = jnp.exp(log_a)
            one_minus_a2 = -jnp.tanh(log_a) * (a * a + 1.0)
            bt = jnp.sqrt(one_minus_a2) * (ig * xc)
            for kk in range(sub_per_seq):
                k = s * sub_per_seq + kk
                dst = pl.ds(k, lsub, stride=SUBLANES)
                for l in range(nslab):
                    a_s[l, dst, :] = a[kk * lsub:(kk + 1) * lsub, l * LANES:(l + 1) * LANES]
                    b_s[l, dst, :] = bt[kk * lsub:(kk + 1) * lsub, l * LANES:(l + 1) * LANES]

    def local_scan(j, carry):
        hf, pf, hb, pb = carry
        rf = pl.multiple_of(j * SUBLANES, SUBLANES)
        rb = pl.multiple_of((lsub - 1 - j) * SUBLANES, SUBLANES)
        nhf, npf, nhb, npb = [], [], [], []
        for l in range(nslab):
            a = af_s[l, pl.ds(rf, SUBLANES), :]
            h = a * hf[l] + bf_s[l, pl.ds(rf, SUBLANES), :]
            p = a * pf[l]
            bf_s[l, pl.ds(rf, SUBLANES), :] = h
            af_s[l, pl.ds(rf, SUBLANES), :] = p
            nhf.append(h)
            npf.append(p)
            a = ab_s[l, pl.ds(rb, SUBLANES), :]
            h = a * hb[l] + bb_s[l, pl.ds(rb, SUBLANES), :]
            p = a * pb[l]
            bb_s[l, pl.ds(rb, SUBLANES), :] = h
            ab_s[l, pl.ds(rb, SUBLANES), :] = p
            nhb.append(h)
            npb.append(p)
        return tuple(nhf), tuple(npf), tuple(nhb), tuple(npb)

    zero = tuple(jnp.zeros((SUBLANES, LANES), F32) for _ in range(nslab))
    one = tuple(jnp.ones((SUBLANES, LANES), F32) for _ in range(nslab))
    hf_end, pf_end, hb_end, pb_end = lax.fori_loop(0, lsub, local_scan, (zero, one, zero, one))

    row8 = lax.broadcasted_iota(jnp.int32, (SUBLANES, LANES), 0)
    h0 = h0_ref[...]
    for l in range(nslab):
        lanes = slice(l * LANES, (l + 1) * LANES)
        ent_f = jnp.zeros((SUBLANES, LANES), F32)
        h_in = h0[0:1, lanes]
        for k in range(SUBLANES):
            if k > 0 and k % sub_per_seq == 0:
                h_in = jnp.where(is_ctx, 0.0, h_in)
            ent_f = jnp.where(row8 == k, h_in, ent_f)
            h_in = hf_end[l][k:k + 1] + pf_end[l][k:k + 1] * h_in
            if (k + 1) % sub_per_seq == 0:
                q = k // sub_per_seq
                st_ref[q:q + 1, lanes] = h_in
        ent_b = jnp.zeros((SUBLANES, LANES), F32)
        h_in = h0[1:2, lanes]
        for k in reversed(range(SUBLANES)):
            if k < SUBLANES - 1 and (k + 1) % sub_per_seq == 0:
                h_in = jnp.where(is_ctx, 0.0, h_in)
            ent_b = jnp.where(row8 == k, h_in, ent_b)
            h_in = hb_end[l][k:k + 1] + pb_end[l][k:k + 1] * h_in
            if k % sub_per_seq == 0:
                q = k // sub_per_seq
                st_ref[nseg + q:nseg + q + 1, lanes] = h_in
        ch = min(tm, 256)
        ef = jnp.concatenate([ent_f] * (ch // SUBLANES), axis=0)
        eb = jnp.concatenate([ent_b] * (ch // SUBLANES), axis=0)
        for r0 in range(0, tm, ch):
            rows = pl.ds(r0, ch)
            bf_s[l, rows, :] = ((bf_s[l, rows, :] + af_s[l, rows, :] * ef)
                                + (bb_s[l, rows, :] + ab_s[l, rows, :] * eb))

    for k in range(SUBLANES):
        rows = pl.ds(k * lsub, lsub)
        hsum = jnp.concatenate([bf_s[l, pl.ds(k, lsub, stride=SUBLANES), :] for l in range(nslab)], axis=1)
        y_ref[rows, :] = (hsum * _gelu_tanh(gate_ref[rows, :])).astype(y_ref.dtype)


def _rg_core(z, conv_w, conv_b, w_a, b_a, w_x, b_x, lam, h0, n_ctx, seq_ctx, tm):
    t = z.shape[0]
    r = conv_w.shape[1]
    heads, hw = w_a.shape[1], w_a.shape[2]
    hpg = 4
    while (hpg * hw) % LANES:
        hpg *= 2
    ng = heads // hpg
    cg = hpg * hw
    nseg = tm // seq_ctx
    ntiles = t // tm
    assert tm % SUBLANES == 0 and seq_ctx % (tm // SUBLANES) == 0

    def blockdiag(w):
        w5 = w.reshape(2, ng, hpg, hw, hw)
        eye = jnp.eye(hpg, dtype=w.dtype)
        return jnp.einsum("dghio,hk->dghiko", w5, eye).reshape(2, ng, cg, cg)

    wa, wx = blockdiag(w_a), blockdiag(w_x)
    wg = jnp.concatenate([wa[0], wx[0], wa[1], wx[1]], axis=-1).astype(BF16)
    ba, bx = b_a.reshape(2, ng, 1, cg), b_x.reshape(2, ng, 1, cg)
    bg = jnp.concatenate([ba[0], bx[0], ba[1], bx[1]], axis=-1)

    return pl.pallas_call(
        functools.partial(_rg_core_kernel, n_ctx_tiles=n_ctx // tm, seg=seq_ctx, tm=tm, cg=cg),
        grid=(ng, ntiles),
        in_specs=[pl.BlockSpec((tm, cg), lambda g, i: (i, g)),
                  pl.BlockSpec((tm, cg), lambda g, i: (i, ng + g)),
                  pl.BlockSpec((conv_w.shape[0], cg), lambda g, i: (0, g)),
                  pl.BlockSpec((1, cg), lambda g, i: (0, g)),
                  pl.BlockSpec((None, cg, 4 * cg), lambda g, i: (g, 0, 0)),
                  pl.BlockSpec((None, 1, 4 * cg), lambda g, i: (g, 0, 0)),
                  pl.BlockSpec((2, cg), lambda g, i: (0, g)),
                  pl.BlockSpec((None, 2, cg), lambda g, i: (i, 0, g))],
        out_specs=[pl.BlockSpec((tm, cg), lambda g, i: (i, g)),
                   pl.BlockSpec((None, 2 * nseg, cg), lambda g, i: (i, 0, g))],
        out_shape=[jax.ShapeDtypeStruct((t, r), BF16),
                   jax.ShapeDtypeStruct((ntiles, 2 * nseg, r), F32)],
        scratch_shapes=[pltpu.VMEM((cg // LANES, tm, LANES), F32)] * 4,
        compiler_params=_cparams(2), name="rg_core",
    )(z, z, conv_w, conv_b.reshape(1, r), wg, bg, lam, h0)


def _sgu_core_kernel(u_ref, v_ref, g_ref, ws_ref, bs_ref, y_ref, vn_s, *, chunk, gw, ngroups, tm):
    v = v_ref[...].astype(F32)
    ms = jnp.mean(v * v, axis=-1, keepdims=True)
    vn_s[...] = ((v * lax.rsqrt(ms + EPS)) * g_ref[...]).astype(BF16)
    for c in range(tm // chunk):
        rows = pl.ds(c * chunk, chunk)
        for g in range(ngroups):
            cols = pl.ds(g * gw, gw)
            vm = jnp.dot(ws_ref[g], vn_s[rows, cols], preferred_element_type=F32) + bs_ref[:, g:g + 1]
            y_ref[rows, cols] = (u_ref[rows, cols].astype(F32) * vm).astype(y_ref.dtype)


def _sgu_core(z, norm_g, w_s, b_s, tm):
    t = z.shape[0]
    w = norm_g.shape[0]
    ngroups, chunk = w_s.shape[0], w_s.shape[1]
    gw = w // ngroups
    return pl.pallas_call(
        functools.partial(_sgu_core_kernel, chunk=chunk, gw=gw, ngroups=ngroups, tm=tm),
        grid=(t // tm,),
        in_specs=[pl.BlockSpec((tm, w), lambda i: (i, 0)),
                  pl.BlockSpec((tm, w), lambda i: (i, 1)),
                  pl.BlockSpec((1, w), lambda i: (0, 0)),
                  pl.BlockSpec((ngroups, chunk, chunk), lambda i: (0, 0, 0)),
                  pl.BlockSpec((chunk, ngroups), lambda i: (0, 0))],
        out_specs=pl.BlockSpec((tm, w), lambda i: (i, 0)),
        out_shape=jax.ShapeDtypeStruct((t, w), BF16),
        scratch_shapes=[pltpu.VMEM((tm, w), BF16)],
        compiler_params=_cparams(1), name="sgu_core",
    )(z, z, norm_g.reshape(1, w), w_s.astype(BF16), b_s.T)


def _moe_plan(ids, n_exp, bm):
    t = ids.shape[0]
    e = ids[:, :TOP_K].reshape(-1)
    oh = (e[:, None] == jnp.arange(n_exp, dtype=jnp.int32)[None, :]).astype(jnp.int32)
    csum = jnp.cumsum(oh, axis=0)
    rank = jnp.sum((csum - oh) * oh, axis=1)
    counts = csum[-1]
    padded = ((counts + bm - 1) // bm) * bm
    ends = jnp.cumsum(padded)
    starts = ends - padded
    dest = (jnp.sum(starts[None, :] * oh, axis=1) + rank).astype(jnp.int32)
    p = TOP_K * t + n_exp * bm
    src = jnp.zeros((p,), jnp.int32).at[dest].set(jnp.arange(TOP_K * t, dtype=jnp.int32) // TOP_K)
    nb = p // bm
    blk_start = jnp.arange(nb, dtype=jnp.int32) * bm
    blk_exp = jnp.minimum(jnp.sum((blk_start[:, None] >= ends[None, :]).astype(jnp.int32), axis=1),
                          n_exp - 1).astype(jnp.int32)
    n_used = (ends[-1] // bm).astype(jnp.int32).reshape(1)
    return dest, src, blk_exp, n_used


def _gather_rows_kernel(src_ref, nused_ref, h_hbm, o_ref, buf, sem, *, rows, n_slab):
    b = pl.program_id(0)
    n_used = nused_ref[0]
    slot = b % 2

    def issue(blk, to_slot):
        base = blk * rows

        def body(r, carry):
            tok = pl.multiple_of(src_ref[base + r] * n_slab, n_slab)
            pltpu.make_async_copy(h_hbm.at[pl.ds(tok, n_slab), :],
                                  buf.at[to_slot, pl.ds(pl.multiple_of(r * n_slab, n_slab), n_slab), :],
                                  sem.at[to_slot]).start()
            return carry

        lax.fori_loop(0, rows, body, 0)

    @pl.when(b == 0)
    def _():
        issue(0, 0)

    @pl.when(b + 1 < n_used)
    def _():
        issue(b + 1, 1 - slot)

    @pl.when(b < n_used)
    def _():
        pltpu.make_async_copy(h_hbm.at[pl.ds(0, rows * n_slab), :], buf.at[slot], sem.at[slot]).wait()
        for s in range(n_slab):
            o_ref[:, s * LANES:(s + 1) * LANES] = buf[slot, pl.ds(s, rows, stride=n_slab), :].astype(o_ref.dtype)

    @pl.when(b >= n_used)
    def _():
        o_ref[...] = jnp.zeros_like(o_ref)


def _gather_rows(h_tok, src, n_used, bm, d):
    n_slab = d // LANES
    p = src.shape[0]
    return pl.pallas_call(
        functools.partial(_gather_rows_kernel, rows=bm, n_slab=n_slab),
        grid_spec=pltpu.PrefetchScalarGridSpec(
            num_scalar_prefetch=2,
            grid=(p // bm,),
            in_specs=[pl.BlockSpec(memory_space=pl.ANY)],
            out_specs=pl.BlockSpec((bm, d), lambda b, src, nu: (b, 0)),
            scratch_shapes=[pltpu.VMEM((2, bm * n_slab, LANES), F32), pltpu.SemaphoreType.DMA((2,))]),
        out_shape=jax.ShapeDtypeStruct((p, d), BF16),
        compiler_params=_cparams(1), name="moe_gather",
    )(src, n_used, h_tok)


def _expert_changed(be_ref, b):
    return jnp.logical_or(b == 0, be_ref[b] != be_ref[jnp.maximum(b - 1, 0)])


def _grouped_swiglu_kernel(be_ref, nused_ref, x_ref, w1_ref, w3_ref, o_ref, wb1, wb3):
    b = pl.program_id(1)

    @pl.when(b < nused_ref[0])
    def _():
        @pl.when(_expert_changed(be_ref, b))
        def _():
            wb1[...] = w1_ref[...].astype(BF16)
            wb3[...] = w3_ref[...].astype(BF16)

        _swiglu_passes(x_ref, wb1, wb3, o_ref)

    @pl.when(b >= nused_ref[0])
    def _():
        o_ref[...] = jnp.zeros_like(o_ref)


def _grouped_swiglu(xs, w1, w3, layer, blk_exp, n_used, bm, tn_pref=512):
    p, k = xs.shape
    n = w1.shape[3]
    tn = _pick_tile(n, tn_pref)

    def blk(b, nu):
        return jnp.minimum(b, nu[0] - 1)

    return pl.pallas_call(
        _grouped_swiglu_kernel,
        grid_spec=pltpu.PrefetchScalarGridSpec(
            num_scalar_prefetch=2,
            grid=(n // tn, p // bm),
            in_specs=[pl.BlockSpec((bm, k), lambda j, b, be, nu: (blk(b, nu), 0)),
                      pl.BlockSpec((None, None, k, tn), lambda j, b, be, nu: (layer, be[blk(b, nu)], 0, j)),
                      pl.BlockSpec((None, None, k, tn), lambda j, b, be, nu: (layer, be[blk(b, nu)], 0, j))],
            out_specs=pl.BlockSpec((bm, tn), lambda j, b, be, nu: (b, j)),
            scratch_shapes=[pltpu.VMEM((k, tn), BF16), pltpu.VMEM((k, tn), BF16)]),
        out_shape=jax.ShapeDtypeStruct((p, n), BF16),
        compiler_params=_cparams(2), name="moe_swiglu",
    )(blk_exp, n_used, xs, w1, w3)


def _grouped_down_kernel(be_ref, nused_ref, x_ref, w_ref, o_ref, wb):
    b = pl.program_id(1)

    @pl.when(b < nused_ref[0])
    def _():
        @pl.when(_expert_changed(be_ref, b))
        def _():
            wb[...] = w_ref[...].astype(BF16)

        for rows in _row_passes(x_ref.shape[0]):
            o_ref[rows, :] = jnp.dot(x_ref[rows, :], wb[...], preferred_element_type=F32)

    @pl.when(b >= nused_ref[0])
    def _():
        o_ref[...] = jnp.zeros_like(o_ref)


def _grouped_down(gs, w2, layer, blk_exp, n_used, bm, tn_pref=512):
    p, k = gs.shape
    n = w2.shape[3]
    tn = _pick_tile(n, tn_pref)

    def blk(b, nu):
        return jnp.minimum(b, nu[0] - 1)

    return pl.pallas_call(
        _grouped_down_kernel,
        grid_spec=pltpu.PrefetchScalarGridSpec(
            num_scalar_prefetch=2,
            grid=(n // tn, p // bm),
            in_specs=[pl.BlockSpec((bm, k), lambda j, b, be, nu: (blk(b, nu), 0)),
                      pl.BlockSpec((None, None, k, tn), lambda j, b, be, nu: (layer, be[blk(b, nu)], 0, j))],
            out_specs=pl.BlockSpec((bm, tn), lambda j, b, be, nu: (b, j)),
            scratch_shapes=[pltpu.VMEM((k, tn), BF16)]),
        out_shape=jax.ShapeDtypeStruct((p, n), F32),
        compiler_params=_cparams(2), name="moe_down",
    )(blk_exp, n_used, gs, w2)


def _combine_kernel(dest_ref, os_hbm, y_ref, gates_ref, mod_ref, o_ref, buf, sem, *, rows, gate_row):
    i = pl.program_id(0)
    base = i * rows

    def issue(r, carry):
        for k in range(TOP_K):
            slot = dest_ref[TOP_K * (base + r) + k]
            pltpu.make_async_copy(os_hbm.at[pl.ds(slot, 1), :], buf.at[k, pl.ds(r, 1), :], sem).start()
        return carry

    lax.fori_loop(0, rows, issue, 0)
    for k in range(TOP_K):
        pltpu.make_async_copy(os_hbm.at[pl.ds(0, rows), :], buf.at[k], sem).wait()
    gates = gates_ref[...]
    f = gates[:, 0:1] * buf[0] + gates[:, 1:2] * buf[1]
    o_ref[...] = y_ref[...] + mod_ref[gate_row:gate_row + 1, :] * f


def _combine(os_, dest, gates, y, mod, layer, gate_row, n_ctx, dec_seq, tm):
    t, d = y.shape
    return pl.pallas_call(
        functools.partial(_combine_kernel, rows=tm, gate_row=gate_row),
        grid_spec=pltpu.PrefetchScalarGridSpec(
            num_scalar_prefetch=1,
            grid=(t // tm,),
            in_specs=[pl.BlockSpec(memory_space=pl.ANY),
                      pl.BlockSpec((tm, d), lambda i, dst: (i, 0)),
                      pl.BlockSpec((tm, LANES), lambda i, dst: (i, 0)),
                      pl.BlockSpec((None, None, 6, d),
                                   lambda i, dst: (layer, _mod_row(i, tm, n_ctx, dec_seq), 0, 0))],
            out_specs=pl.BlockSpec((tm, d), lambda i, dst: (i, 0)),
            scratch_shapes=[pltpu.VMEM((TOP_K, tm, d), F32), pltpu.SemaphoreType.DMA(())]),
        out_shape=jax.ShapeDtypeStruct((t, d), F32),
        compiler_params=_cparams(1), name="moe_combine",
    )(dest, os_, y, gates, mod)


def kernel(x_prompt, x_sample, state_rglru, c, c_ctx, norm_mix_g, norm_ffn_g, w_mod, b_mod, final_norm_g, rg_w_in, rg_conv_w, rg_conv_b, rg_w_a, rg_b_a, rg_w_x, rg_b_x, rg_lam, rg_w_out, sg_w_in, sg_norm_g, sg_w_s, sg_b_s, sg_w_out, sc_w_in, sc_conv_w, sc_w_out, ff_w1, ff_w3, ff_w2, moe_router, moe_router_b, moe_w1, moe_w3, moe_w2):
    batch, seq, d = x_prompt.shape
    dec_batch, dec_seq, _ = x_sample.shape
    depth = w_mod.shape[0]
    n_ctx = batch * seq
    d_rnn = rg_w_out.shape[1]
    n_exp = moe_router.shape[2]
    chunk = sg_w_s.shape[2]

    tm = dec_seq
    assert dec_seq % seq == 0 and n_ctx % tm == 0 and seq % SUBLANES == 0
    tm_half = max(tm // 2, chunk)
    tm_small = max(tm // 4, chunk)
    assert tm % tm_half == 0 and tm % tm_small == 0 and tm_small % chunk == 0
    moe_bm = tm_half

    n_cond = 1 + dec_batch
    cond = jnp.zeros((-(-n_cond // SUBLANES) * SUBLANES, d), F32)
    cond = cond.at[0].set(c_ctx).at[1:n_cond].set(c)
    mod = _adaln(cond, w_mod, b_mod)[:, :n_cond].reshape(depth, n_cond, 6, d)

    y = _embed(x_prompt.reshape(n_ctx, d), x_sample.reshape(dec_batch * dec_seq, d),
               _grid_pos_embed(dec_seq, d), tm_small)

    states = []
    for i in range(depth):
        kind, j = i % 3, i // 3
        h = _norm_mod(y, norm_mix_g[i], mod, i, 0, n_ctx, dec_seq, tm_half)
        if kind == 0:
            z = _mm_act(h, rg_w_in, j, None, F32, tm)
            h0 = jnp.concatenate([jnp.zeros((n_ctx // tm, 2, d_rnn), F32),
                                  state_rglru[:, j].astype(F32)], axis=0)
            mix, st = _rg_core(z, rg_conv_w[j], rg_conv_b[j], rg_w_a[j], rg_b_a[j], rg_w_x[j], rg_b_x[j],
                               rg_lam[j], h0, n_ctx, seq, tm)
            nseg = tm // seq
            st = st[:n_ctx // tm].reshape(n_ctx // tm, 2, nseg, d_rnn)
            states.append(jnp.transpose(st, (0, 2, 1, 3)).reshape(batch, 2, d_rnn))
        elif kind == 1:
            z = _mm_act(h, sg_w_in, j, "gelu", BF16, tm)
            mix = _sgu_core(z, sg_norm_g[j], sg_w_s[j], sg_b_s[j], tm_small)
        else:
            mix = _mm_sconv(h, sc_w_in, sc_conv_w, j, n_ctx, seq, tm)
        w_out = (rg_w_out, sg_w_out, sc_w_out)[kind]
        y = _mm_residual(mix, w_out, j, y, mod, i, 2, n_ctx, dec_seq, tm_half)

        f = i // 2
        if i % 2 == 0:
            h = _norm_mod(y, norm_ffn_g[i], mod, i, 3, n_ctx, dec_seq, tm_half)
            g = _mm_swiglu(h, ff_w1, ff_w3, f, tm)
            y = _mm_residual(g, ff_w2, f, y, mod, i, 5, n_ctx, dec_seq, tm_half)
        else:
            h_tok, ids, gates = _norm_router(y, norm_ffn_g[i], mod, i, 3, moe_router[f], moe_router_b[f],
                                             n_ctx, dec_seq, tm_half)
            dest, src, blk_exp, n_used = _moe_plan(ids, n_exp, moe_bm)
            xs = _gather_rows(h_tok, src, n_used, moe_bm, d)
            gs = _grouped_swiglu(xs, moe_w1, moe_w3, f, blk_exp, n_used, moe_bm)
            os_ = _grouped_down(gs, moe_w2, f, blk_exp, n_used, moe_bm)
            y = _combine(os_, dest, gates, y, mod, i, 5, n_ctx, dec_seq, tm_small)

    y_p, y_s = _final_norm(y, final_norm_g, n_ctx, tm_half)
    new_state = jnp.stack(states, axis=1).astype(x_prompt.dtype)
    return (y_p.reshape(batch, seq, d), y_s.reshape(dec_batch, dec_seq, d), new_state)
```

```python
import functools

import jax
import jax.numpy as jnp
from jax import lax
from jax.experimental import pallas as pl
from jax.experimental.pallas import tpu as pltpu

F32 = jnp.float32
BF16 = jnp.bfloat16

GRID_W = 64
EPS = 1e-6
RG_C = 8.0
TOP_K = 2

LANES = 128
SUBLANES = 8
VMEM_LIMIT_BYTES = 58 * 1024 * 1024
NEG_BIG = -1e30
MXU_ROWS_PER_PASS = 256


def _cparams(n_axes):
    return pltpu.CompilerParams(dimension_semantics=("arbitrary",) * n_axes,
                                vmem_limit_bytes=VMEM_LIMIT_BYTES)


def _pick_tile(n, pref):
    if n <= pref:
        return n
    t = (pref // LANES) * LANES
    while t > LANES and n % t:
        t -= LANES
    assert n % t == 0, (n, pref)
    return t


def _row_passes(rows):
    step = MXU_ROWS_PER_PASS if rows % MXU_ROWS_PER_PASS == 0 else rows
    return [pl.ds(r, step) for r in range(0, rows, step)]


def _sigmoid(x):
    return 1.0 / (1.0 + jnp.exp(-x))


def _gelu_tanh(x):
    c = 0.7978845608028654
    return 0.5 * x * (1.0 + jnp.tanh(c * (x + 0.044715 * (x * x * x))))


def _mod_row(i, tm, n_ctx, dec_seq):
    start = i * tm
    return jnp.where(start < n_ctx, 0, 1 + (start - n_ctx) // dec_seq)


def _adaln_kernel(c_ref, w_ref, b_ref, o_ref):
    c = c_ref[...]
    s = (c * _sigmoid(c)).astype(BF16)
    o_ref[...] = jnp.dot(s, w_ref[...].astype(BF16), preferred_element_type=F32) + b_ref[...]


def _adaln(cond, w_mod, b_mod):
    depth, d, n = w_mod.shape
    mc = cond.shape[0]
    tn = _pick_tile(n, 1024)
    return pl.pallas_call(
        _adaln_kernel,
        grid=(depth, n // tn),
        in_specs=[pl.BlockSpec((mc, d), lambda l, j: (0, 0)),
                  pl.BlockSpec((None, d, tn), lambda l, j: (l, 0, j)),
                  pl.BlockSpec((None, 1, tn), lambda l, j: (l, 0, j))],
        out_specs=pl.BlockSpec((None, mc, tn), lambda l, j: (l, 0, j)),
        out_shape=jax.ShapeDtypeStruct((depth, mc, n), F32),
        compiler_params=_cparams(2), name="adaln",
    )(cond, w_mod, b_mod.reshape(depth, 1, n))


def _embed_norm_kernel(xp_ref, xs_ref, pos_ref, g_ref, mod_ref, y_ref, h_ref, *, n_ctx_tiles):
    i = pl.program_id(0)

    @pl.when(i < n_ctx_tiles)
    def _():
        y_ref[...] = xp_ref[...]

    @pl.when(i >= n_ctx_tiles)
    def _():
        y_ref[...] = xs_ref[...] + pos_ref[...]

    h_ref[...] = _norm_mod_value(y_ref[...], g_ref[...], mod_ref, 0).astype(h_ref.dtype)


def _embed_norm(xp, xs, pos, g, mod, tm):
    n_ctx, d = xp.shape
    n_dec = xs.shape[0]
    dec_seq = pos.shape[0]
    nct = n_ctx // tm
    ppt = dec_seq // tm
    t = n_ctx + n_dec
    return pl.pallas_call(
        functools.partial(_embed_norm_kernel, n_ctx_tiles=nct),
        grid=(t // tm,),
        in_specs=[pl.BlockSpec((tm, d), lambda i: (jnp.minimum(i, nct - 1), 0)),
                  pl.BlockSpec((tm, d), lambda i: (jnp.maximum(i - nct, 0), 0)),
                  pl.BlockSpec((tm, d), lambda i: (jnp.maximum(i - nct, 0) % ppt, 0)),
                  pl.BlockSpec((1, d), lambda i: (0, 0)),
                  pl.BlockSpec((None, None, 6, d), lambda i: (0, _mod_row(i, tm, n_ctx, dec_seq), 0, 0))],
        out_specs=[pl.BlockSpec((tm, d), lambda i: (i, 0)),
                   pl.BlockSpec((tm, d), lambda i: (i, 0))],
        out_shape=[jax.ShapeDtypeStruct((t, d), F32), jax.ShapeDtypeStruct((t, d), BF16)],
        compiler_params=_cparams(1), name="embed_norm",
    )(xp, xs, pos, g.reshape(1, d), mod)


def _grid_pos_embed(length, d):
    rows = length // GRID_W
    r = jnp.repeat(jnp.arange(rows), GRID_W)
    col = jnp.tile(jnp.arange(GRID_W), rows)
    quarter = d // 4
    omega = 1.0 / (10000.0 ** (jnp.arange(quarter, dtype=F32) / quarter))

    def emb(p):
        ang = p[:, None].astype(F32) * omega[None, :]
        return jnp.concatenate([jnp.sin(ang), jnp.cos(ang)], axis=-1)

    return jnp.concatenate([emb(r), emb(col)], axis=-1).astype(F32)


def _norm_mod_value(y, g, mod_ref, shift_row):
    ms = jnp.mean(y * y, axis=-1, keepdims=True)
    n = (y * lax.rsqrt(ms + EPS)) * g
    return n * (1.0 + mod_ref[shift_row + 1:shift_row + 2, :]) + mod_ref[shift_row:shift_row + 1, :]


def _norm_mod_kernel(y_ref, g_ref, mod_ref, h_ref, *, shift_row):
    h_ref[...] = _norm_mod_value(y_ref[...], g_ref[...], mod_ref, shift_row).astype(h_ref.dtype)


def _norm_mod(y, g, mod, layer, shift_row, n_ctx, dec_seq, tm):
    t, d = y.shape
    return pl.pallas_call(
        functools.partial(_norm_mod_kernel, shift_row=shift_row),
        grid=(t // tm,),
        in_specs=[pl.BlockSpec((tm, d), lambda i: (i, 0)),
                  pl.BlockSpec((1, d), lambda i: (0, 0)),
                  pl.BlockSpec((None, None, 6, d),
                               lambda i: (layer, _mod_row(i, tm, n_ctx, dec_seq), 0, 0))],
        out_specs=pl.BlockSpec((tm, d), lambda i: (i, 0)),
        out_shape=jax.ShapeDtypeStruct((t, d), BF16),
        compiler_params=_cparams(1), name="norm_mod",
    )(y, g.reshape(1, d), mod)


def _norm_router_kernel(y_ref, g_ref, mod_ref, r_ref, rb_ref, h_ref, ids_ref, gates_ref, *,
                        shift_row, n_slab):
    h = _norm_mod_value(y_ref[...], g_ref[...], mod_ref, shift_row)
    tm = h.shape[0]
    for s in range(n_slab):
        h_ref[pl.ds(s, tm, stride=n_slab), :] = h[:, s * LANES:(s + 1) * LANES]
    logits = jnp.dot(h, r_ref[...], preferred_element_type=F32,
                     precision=lax.Precision.HIGHEST) + rb_ref[...]
    lane = lax.broadcasted_iota(jnp.int32, logits.shape, 1).astype(F32)
    big = float(LANES)
    m1 = jnp.max(logits, axis=-1, keepdims=True)
    i1 = jnp.min(jnp.where(logits == m1, lane, big), axis=-1, keepdims=True)
    l2 = jnp.where(lane == i1, 2.0 * NEG_BIG, logits)
    m2 = jnp.max(l2, axis=-1, keepdims=True)
    i2 = jnp.min(jnp.where(l2 == m2, lane, big), axis=-1, keepdims=True)
    e = jnp.exp(m2 - m1)
    g1 = 1.0 / (1.0 + e)
    g2 = e / (1.0 + e)
    ids_ref[...] = jnp.where(lane == 0.0, i1, jnp.where(lane == 1.0, i2, 0.0)).astype(jnp.int32)
    gates_ref[...] = jnp.where(lane == 0.0, g1, jnp.where(lane == 1.0, g2, 0.0))


def _norm_router(y, g, mod, layer, shift_row, router, router_b, n_ctx, dec_seq, tm):
    t, d = y.shape
    n_exp = router.shape[1]
    n_slab = d // LANES
    rp = jnp.zeros((d, LANES), F32).at[:, :n_exp].set(router)
    rbp = jnp.full((1, LANES), NEG_BIG, F32).at[0, :n_exp].set(router_b)
    return pl.pallas_call(
        functools.partial(_norm_router_kernel, shift_row=shift_row, n_slab=n_slab),
        grid=(t // tm,),
        in_specs=[pl.BlockSpec((tm, d), lambda i: (i, 0)),
                  pl.BlockSpec((1, d), lambda i: (0, 0)),
                  pl.BlockSpec((None, None, 6, d),
                               lambda i: (layer, _mod_row(i, tm, n_ctx, dec_seq), 0, 0)),
                  pl.BlockSpec((d, LANES), lambda i: (0, 0)),
                  pl.BlockSpec((1, LANES), lambda i: (0, 0))],
        out_specs=[pl.BlockSpec((tm * n_slab, LANES), lambda i: (i, 0)),
                   pl.BlockSpec((tm, LANES), lambda i: (i, 0)),
                   pl.BlockSpec((tm, LANES), lambda i: (i, 0))],
        out_shape=[jax.ShapeDtypeStruct((t * n_slab, LANES), F32),
                   jax.ShapeDtypeStruct((t, LANES), jnp.int32),
                   jax.ShapeDtypeStruct((t, LANES), F32)],
        compiler_params=_cparams(1), name="norm_router",
    )(y, g.reshape(1, d), mod, rp, rbp)


def _final_norm_kernel(y_ref, g_ref, op_ref, os_ref, *, n_ctx_tiles):
    i = pl.program_id(0)
    y = y_ref[...]
    ms = jnp.mean(y * y, axis=-1, keepdims=True)
    n = (y * lax.rsqrt(ms + EPS)) * g_ref[...]

    @pl.when(i < n_ctx_tiles)
    def _():
        op_ref[...] = n

    @pl.when(i >= n_ctx_tiles)
    def _():
        os_ref[...] = n


def _final_norm(y, g, n_ctx, tm):
    t, d = y.shape
    nct = n_ctx // tm
    return pl.pallas_call(
        functools.partial(_final_norm_kernel, n_ctx_tiles=nct),
        grid=(t // tm,),
        in_specs=[pl.BlockSpec((tm, d), lambda i: (i, 0)),
                  pl.BlockSpec((1, d), lambda i: (0, 0))],
        out_specs=[pl.BlockSpec((tm, d), lambda i: (jnp.minimum(i, nct - 1), 0)),
                   pl.BlockSpec((tm, d), lambda i: (jnp.maximum(i - nct, 0), 0))],
        out_shape=[jax.ShapeDtypeStruct((n_ctx, d), F32),
                   jax.ShapeDtypeStruct((t - n_ctx, d), F32)],
        compiler_params=_cparams(1), name="final_norm",
    )(y, g.reshape(1, d))


def _cast_at_first_row_tile(w_refs, wb_refs):
    @pl.when(pl.program_id(1) == 0)
    def _():
        for w, wb in zip(w_refs, wb_refs):
            wb[...] = w[...].astype(BF16)


def _mm_act_kernel(x_ref, w_ref, o_ref, wb, *, act):
    _cast_at_first_row_tile((w_ref,), (wb,))
    for rows in _row_passes(x_ref.shape[0]):
        acc = jnp.dot(x_ref[rows, :], wb[...], preferred_element_type=F32)
        if act == "gelu":
            acc = _gelu_tanh(acc)
        o_ref[rows, :] = acc.astype(o_ref.dtype)


def _mm_act(x, w, layer, act, out_dtype, tm, tn_pref=512):
    t, k = x.shape
    n = w.shape[2]
    tn = _pick_tile(n, tn_pref)
    return pl.pallas_call(
        functools.partial(_mm_act_kernel, act=act),
        grid=(n // tn, t // tm),
        in_specs=[pl.BlockSpec((tm, k), lambda j, i: (i, 0)),
                  pl.BlockSpec((None, k, tn), lambda j, i: (layer, 0, j))],
        out_specs=pl.BlockSpec((tm, tn), lambda j, i: (i, j)),
        out_shape=jax.ShapeDtypeStruct((t, n), out_dtype),
        scratch_shapes=[pltpu.VMEM((k, tn), BF16)],
        compiler_params=_cparams(2), name="mm_" + str(act),
    )(x, w)


def _swiglu_passes(x_ref, wb1, wb3, o_ref):
    for rows in _row_passes(x_ref.shape[0]):
        x = x_ref[rows, :]
        a = jnp.dot(x, wb1[...], preferred_element_type=F32)
        b = jnp.dot(x, wb3[...], preferred_element_type=F32)
        o_ref[rows, :] = ((a * _sigmoid(a)) * b).astype(o_ref.dtype)


def _mm_swiglu_kernel(x_ref, w1_ref, w3_ref, o_ref, wb1, wb3):
    _cast_at_first_row_tile((w1_ref, w3_ref), (wb1, wb3))
    _swiglu_passes(x_ref, wb1, wb3, o_ref)


def _mm_swiglu(x, w1, w3, layer, tm, tn_pref=512):
    t, k = x.shape
    n = w1.shape[2]
    tn = _pick_tile(n, tn_pref)
    return pl.pallas_call(
        _mm_swiglu_kernel,
        grid=(n // tn, t // tm),
        in_specs=[pl.BlockSpec((tm, k), lambda j, i: (i, 0)),
                  pl.BlockSpec((None, k, tn), lambda j, i: (layer, 0, j)),
                  pl.BlockSpec((None, k, tn), lambda j, i: (layer, 0, j))],
        out_specs=pl.BlockSpec((tm, tn), lambda j, i: (i, j)),
        out_shape=jax.ShapeDtypeStruct((t, n), BF16),
        scratch_shapes=[pltpu.VMEM((k, tn), BF16), pltpu.VMEM((k, tn), BF16)],
        compiler_params=_cparams(2), name="mm_swiglu",
    )(x, w1, w3)


def _mm_sconv_kernel(x_ref, wb_ref, wc_ref, wx_ref, cw_ref, o_ref, sb, sc, sx, *, n_ctx_tiles, seq_ctx):
    _cast_at_first_row_tile((wb_ref, wc_ref, wx_ref), (sb, sc, sx))
    i = pl.program_id(1)
    is_ctx = i < n_ctx_tiles
    x = x_ref[...]
    bg = jnp.dot(x, sb[...], preferred_element_type=F32)
    p = jnp.dot(x, sc[...], preferred_element_type=F32) * jnp.dot(x, sx[...], preferred_element_type=F32)
    tm = p.shape[0]
    row = lax.broadcasted_iota(jnp.int32, p.shape, 0)
    pos = jnp.where(is_ctx, row % seq_ctx, row)
    last_pos = jnp.where(is_ctx, seq_ctx - 1, tm - 1)
    p_prev = jnp.where(pos == 0, 0.0, pltpu.roll(p, 1, axis=0))
    p_next = jnp.where(pos == last_pos, 0.0, pltpu.roll(p, tm - 1, axis=0))
    cw = cw_ref[...]
    conv = (cw[0:1] * p_prev + cw[1:2] * p) + cw[2:3] * p_next
    o_ref[...] = (bg * conv).astype(o_ref.dtype)


def _mm_sconv(x, w_in, conv_w, layer, n_ctx, seq_ctx, tm, tn_pref=512):
    t, k = x.shape
    d = conv_w.shape[2]
    tn = _pick_tile(d, tn_pref)
    nj = d // tn
    return pl.pallas_call(
        functools.partial(_mm_sconv_kernel, n_ctx_tiles=n_ctx // tm, seq_ctx=seq_ctx),
        grid=(nj, t // tm),
        in_specs=[pl.BlockSpec((tm, k), lambda j, i: (i, 0)),
                  pl.BlockSpec((None, k, tn), lambda j, i: (layer, 0, j)),
                  pl.BlockSpec((None, k, tn), lambda j, i: (layer, 0, nj + j)),
                  pl.BlockSpec((None, k, tn), lambda j, i: (layer, 0, 2 * nj + j)),
                  pl.BlockSpec((None, conv_w.shape[1], tn), lambda j, i: (layer, 0, j))],
        out_specs=pl.BlockSpec((tm, tn), lambda j, i: (i, j)),
        out_shape=jax.ShapeDtypeStruct((t, d), BF16),
        scratch_shapes=[pltpu.VMEM((k, tn), BF16)] * 3,
        compiler_params=_cparams(2), name="mm_sconv",
    )(x, w_in, w_in, w_in, conv_w)


def _mm_residual_kernel(x_ref, w_ref, y_ref, mod_ref, o_ref, wb, *, gate_row):
    _cast_at_first_row_tile((w_ref,), (wb,))
    gate = mod_ref[gate_row:gate_row + 1, :]
    for rows in _row_passes(x_ref.shape[0]):
        acc = jnp.dot(x_ref[rows, :], wb[...], preferred_element_type=F32)
        o_ref[rows, :] = y_ref[rows, :] + gate * acc


def _mm_residual(x, w, w_layer, y, mod, layer, gate_row, n_ctx, dec_seq, tm, tn_pref=512):
    t, k = x.shape
    n = w.shape[2]
    tn = _pick_tile(n, tn_pref)
    return pl.pallas_call(
        functools.partial(_mm_residual_kernel, gate_row=gate_row),
        grid=(n // tn, t // tm),
        in_specs=[pl.BlockSpec((tm, k), lambda j, i: (i, 0)),
                  pl.BlockSpec((None, k, tn), lambda j, i: (w_layer, 0, j)),
                  pl.BlockSpec((tm, tn), lambda j, i: (i, j)),
                  pl.BlockSpec((None, None, 6, tn),
                               lambda j, i: (layer, _mod_row(i, tm, n_ctx, dec_seq), 0, j))],
        out_specs=pl.BlockSpec((tm, tn), lambda j, i: (i, j)),
        out_shape=jax.ShapeDtypeStruct((t, n), F32),
        scratch_shapes=[pltpu.VMEM((k, tn), BF16)],
        compiler_params=_cparams(2), name="mm_residual",
    )(x, w, y, mod)


def _rg_core_kernel(gate_ref, x_ref, cw_ref, cb_ref, wg_ref, bg_ref, lam_ref, h0_ref,
                    y_ref, st_ref, af_s, bf_s, ab_s, bb_s, *, n_ctx_tiles, seg, tm, cg):
    i = pl.program_id(1)
    is_ctx = i < n_ctx_tiles
    nseg = tm // seg
    nslab = cg // LANES
    lsub = tm // SUBLANES
    sub_per_seq = seg // lsub
    cw = cw_ref[...]
    cb = cb_ref[...]
    lam = lam_ref[...]
    softplus_neg_lam = jnp.maximum(-lam, 0.0) + jnp.log1p(jnp.exp(-jnp.abs(lam)))
    neg_c_sp = (-RG_C) * softplus_neg_lam
    zeros8 = jnp.zeros((SUBLANES, cg), F32)

    for s in range(nseg):
        c0 = s * seg
        cur = x_ref[pl.ds(c0, seg), :]
        prev8 = zeros8 if s == 0 else jnp.where(is_ctx, 0.0, x_ref[pl.ds(c0 - SUBLANES, SUBLANES), :])
        next8 = zeros8 if s == nseg - 1 else jnp.where(is_ctx, 0.0, x_ref[pl.ds(c0 + seg, SUBLANES), :])
        xe = jnp.concatenate([prev8, cur, next8], axis=0)
        xc = (((cw[0:1] * xe[6:6 + seg] + cw[1:2] * xe[7:7 + seg]) + cw[2:3] * cur)
              + cw[3:4] * xe[9:9 + seg]) + cb
        g = jnp.dot(xc.astype(BF16), wg_ref[...], preferred_element_type=F32) + bg_ref[...]
        for d, (a_s, b_s) in enumerate(((af_s, bf_s), (ab_s, bb_s))):
            r = _sigmoid(g[:, (2 * d) * cg:(2 * d + 1) * cg])
            ig = _sigmoid(g[:, (2 * d + 1) * cg:(2 * d + 2) * cg])
            log_a = neg_c_sp[d:d + 1] * r
            a = jnp.exp(log_a)
            one_minus_a2 = -jnp.tanh(log_a) * (a * a + 1.0)
            bt = jnp.sqrt(one_minus_a2) * (ig * xc)
            for kk in range(sub_per_seq):
                k = s * sub_per_seq + kk
                dst = pl.ds(k, lsub, stride=SUBLANES)
                for l in range(nslab):
                    a_s[l, dst, :] = a[kk * lsub:(kk + 1) * lsub, l * LANES:(l + 1) * LANES]
                    b_s[l, dst, :] = bt[kk * lsub:(kk + 1) * lsub, l * LANES:(l + 1) * LANES]

    def local_scan(j, carry):
        hf, pf, hb, pb = carry
        rf = pl.multiple_of(j * SUBLANES, SUBLANES)
        rb = pl.multiple_of((lsub - 1 - j) * SUBLANES, SUBLANES)
        nhf, npf, nhb, npb = [], [], [], []
        for l in range(nslab):
            a = af_s[l, pl.ds(rf, SUBLANES), :]
            h = a * hf[l] + bf_s[l, pl.ds(rf, SUBLANES), :]
            p = a * pf[l]
            bf_s[l, pl.ds(rf, SUBLANES), :] = h
            af_s[l, pl.ds(rf, SUBLANES), :] = p
            nhf.append(h)
            npf.append(p)
            a = ab_s[l, pl.ds(rb, SUBLANES), :]
            h = a * hb[l] + bb_s[l, pl.ds(rb, SUBLANES), :]
            p = a * pb[l]
            bb_s[l, pl.ds(rb, SUBLANES), :] = h
            ab_s[l, pl.ds(rb, SUBLANES), :] = p
            nhb.append(h)
            npb.append(p)
        return tuple(nhf), tuple(npf), tuple(nhb), tuple(npb)

    zero = tuple(jnp.zeros((SUBLANES, LANES), F32) for _ in range(nslab))
    one = tuple(jnp.ones((SUBLANES, LANES), F32) for _ in range(nslab))
    hf_end, pf_end, hb_end, pb_end = lax.fori_loop(0, lsub, local_scan, (zero, one, zero, one))

    row8 = lax.broadcasted_iota(jnp.int32, (SUBLANES, LANES), 0)
    h0 = h0_ref[...]
    for l in range(nslab):
        lanes = slice(l * LANES, (l + 1) * LANES)
        ent_f = jnp.zeros((SUBLANES, LANES), F32)
        h_in = h0[0:1, lanes]
        for k in range(SUBLANES):
            if k > 0 and k % sub_per_seq == 0:
                h_in = jnp.where(is_ctx, 0.0, h_in)
            ent_f = jnp.where(row8 == k, h_in, ent_f)
            h_in = hf_end[l][k:k + 1] + pf_end[l][k:k + 1] * h_in
            if (k + 1) % sub_per_seq == 0:
                q = k // sub_per_seq
                st_ref[q:q + 1, lanes] = h_in
        ent_b = jnp.zeros((SUBLANES, LANES), F32)
        h_in = h0[1:2, lanes]
        for k in reversed(range(SUBLANES)):
            if k < SUBLANES - 1 and (k + 1) % sub_per_seq == 0:
                h_in = jnp.where(is_ctx, 0.0, h_in)
            ent_b = jnp.where(row8 == k, h_in, ent_b)
            h_in = hb_end[l][k:k + 1] + pb_end[l][k:k + 1] * h_in
            if k % sub_per_seq == 0:
                q = k // sub_per_seq
                st_ref[nseg + q:nseg + q + 1, lanes] = h_in
        ch = min(tm, 256)
        ef = jnp.concatenate([ent_f] * (ch // SUBLANES), axis=0)
        eb = jnp.concatenate([ent_b] * (ch // SUBLANES), axis=0)
        for r0 in range(0, tm, ch):
            rows = pl.ds(r0, ch)
            bf_s[l, rows, :] = ((bf_s[l, rows, :] + af_s[l, rows, :] * ef)
                                + (bb_s[l, rows, :] + ab_s[l, rows, :] * eb))

    for k in range(SUBLANES):
        rows = pl.ds(k * lsub, lsub)
        hsum = jnp.concatenate([bf_s[l, pl.ds(k, lsub, stride=SUBLANES), :] for l in range(nslab)], axis=1)
        y_ref[rows, :] = (hsum * _gelu_tanh(gate_ref[rows, :])).astype(y_ref.dtype)


def _rg_core(z, conv_w, conv_b, w_a, b_a, w_x, b_x, lam, h0, n_ctx, seq_ctx, tm):
    t = z.shape[0]
    r = conv_w.shape[1]
    heads, hw = w_a.shape[1], w_a.shape[2]
    hpg = 4
    while (hpg * hw) % LANES:
        hpg *= 2
    ng = heads // hpg
    cg = hpg * hw
    nseg = tm // seq_ctx
    ntiles = t // tm
    assert tm % SUBLANES == 0 and seq_ctx % (tm // SUBLANES) == 0

    def blockdiag(w):
        w5 = w.reshape(2, ng, hpg, hw, hw)
        eye = jnp.eye(hpg, dtype=w.dtype)
        return jnp.einsum("dghio,hk->dghiko", w5, eye).reshape(2, ng, cg, cg)

    wa, wx = blockdiag(w_a), blockdiag(w_x)
    wg = jnp.concatenate([wa[0], wx[0], wa[1], wx[1]], axis=-1).astype(BF16)
    ba, bx = b_a.reshape(2, ng, 1, cg), b_x.reshape(2, ng, 1, cg)
    bg = jnp.concatenate([ba[0], bx[0], ba[1], bx[1]], axis=-1)

    return pl.pallas_call(
        functools.partial(_rg_core_kernel, n_ctx_tiles=n_ctx // tm, seg=seq_ctx, tm=tm, cg=cg),
        grid=(ng, ntiles),
        in_specs=[pl.BlockSpec((tm, cg), lambda g, i: (i, g)),
                  pl.BlockSpec((tm, cg), lambda g, i: (i, ng + g)),
                  pl.BlockSpec((conv_w.shape[0], cg), lambda g, i: (0, g)),
                  pl.BlockSpec((1, cg), lambda g, i: (0, g)),
                  pl.BlockSpec((None, cg, 4 * cg), lambda g, i: (g, 0, 0)),
                  pl.BlockSpec((None, 1, 4 * cg), lambda g, i: (g, 0, 0)),
                  pl.BlockSpec((2, cg), lambda g, i: (0, g)),
                  pl.BlockSpec((None, 2, cg), lambda g, i: (i, 0, g))],
        out_specs=[pl.BlockSpec((tm, cg), lambda g, i: (i, g)),
                   pl.BlockSpec((None, 2 * nseg, cg), lambda g, i: (i, 0, g))],
        out_shape=[jax.ShapeDtypeStruct((t, r), BF16),
                   jax.ShapeDtypeStruct((ntiles, 2 * nseg, r), F32)],
        scratch_shapes=[pltpu.VMEM((cg // LANES, tm, LANES), F32)] * 4,
        compiler_params=_cparams(2), name="rg_core",
    )(z, z, conv_w, conv_b.reshape(1, r), wg, bg, lam, h0)


def _sgu_core_kernel(u_ref, v_ref, g_ref, ws_ref, bs_ref, y_ref, vn_s, *, chunk, gw, ngroups, tm):
    v = v_ref[...].astype(F32)
    ms = jnp.mean(v * v, axis=-1, keepdims=True)
    vn_s[...] = ((v * lax.rsqrt(ms + EPS)) * g_ref[...]).astype(BF16)
    for c in range(tm // chunk):
        rows = pl.ds(c * chunk, chunk)
        for g in range(ngroups):
            cols = pl.ds(g * gw, gw)
            vm = jnp.dot(ws_ref[g], vn_s[rows, cols], preferred_element_type=F32) + bs_ref[:, g:g + 1]
            y_ref[rows, cols] = (u_ref[rows, cols].astype(F32) * vm).astype(y_ref.dtype)


def _sgu_core(z, norm_g, w_s, b_s, tm):
    t = z.shape[0]
    w = norm_g.shape[0]
    ngroups, chunk = w_s.shape[0], w_s.shape[1]
    gw = w // ngroups
    return pl.pallas_call(
        functools.partial(_sgu_core_kernel, chunk=chunk, gw=gw, ngroups=ngroups, tm=tm),
        grid=(t // tm,),
        in_specs=[pl.BlockSpec((tm, w), lambda i: (i, 0)),
                  pl.BlockSpec((tm, w), lambda i: (i, 1)),
                  pl.BlockSpec((1, w), lambda i: (0, 0)),
                  pl.BlockSpec((ngroups, chunk, chunk), lambda i: (0, 0, 0)),
                  pl.BlockSpec((chunk, ngroups), lambda i: (0, 0))],
        out_specs=pl.BlockSpec((tm, w), lambda i: (i, 0)),
        out_shape=jax.ShapeDtypeStruct((t, w), BF16),
        scratch_shapes=[pltpu.VMEM((tm, w), BF16)],
        compiler_params=_cparams(1), name="sgu_core",
    )(z, z, norm_g.reshape(1, w), w_s.astype(BF16), b_s.T)


def _moe_plan(ids, n_exp, bm):
    t = ids.shape[0]
    e = ids[:, :TOP_K].reshape(-1)
    oh = (e[:, None] == jnp.arange(n_exp, dtype=jnp.int32)[None, :]).astype(jnp.int32)
    csum = jnp.cumsum(oh, axis=0)
    rank = jnp.sum((csum - oh) * oh, axis=1)
    counts = csum[-1]
    padded = ((counts + bm - 1) // bm) * bm
    ends = jnp.cumsum(padded)
    starts = ends - padded
    dest = (jnp.sum(starts[None, :] * oh, axis=1) + rank).astype(jnp.int32)
    p = TOP_K * t + n_exp * bm
    src = jnp.zeros((p,), jnp.int32).at[dest].set(jnp.arange(TOP_K * t, dtype=jnp.int32) // TOP_K)
    nb = p // bm
    blk_start = jnp.arange(nb, dtype=jnp.int32) * bm
    blk_exp = jnp.minimum(jnp.sum((blk_start[:, None] >= ends[None, :]).astype(jnp.int32), axis=1),
                          n_exp - 1).astype(jnp.int32)
    n_used = (ends[-1] // bm).astype(jnp.int32).reshape(1)
    return dest, src, blk_exp, n_used


def _gather_rows_kernel(src_ref, nused_ref, h_hbm, o_ref, buf, sem, *, rows, n_slab):
    b = pl.program_id(0)
    n_used = nused_ref[0]
    slot = b % 2

    def issue(blk, to_slot):
        base = blk * rows

        def body(q, carry):
            for prio in range(2):
                r = 2 * q + prio
                tok = pl.multiple_of(src_ref[base + r] * n_slab, n_slab)
                pltpu.make_async_copy(h_hbm.at[pl.ds(tok, n_slab), :],
                                      buf.at[to_slot, pl.ds(pl.multiple_of(r * n_slab, n_slab), n_slab), :],
                                      sem.at[to_slot]).start(priority=prio)
            return carry

        lax.fori_loop(0, rows // 2, body, 0)

    @pl.when(b == 0)
    def _():
        issue(0, 0)

    @pl.when(b + 1 < n_used)
    def _():
        issue(b + 1, 1 - slot)

    @pl.when(b < n_used)
    def _():
        pltpu.make_async_copy(h_hbm.at[pl.ds(0, rows * n_slab), :], buf.at[slot], sem.at[slot]).wait()
        for s in range(n_slab):
            o_ref[:, s * LANES:(s + 1) * LANES] = buf[slot, pl.ds(s, rows, stride=n_slab), :].astype(o_ref.dtype)

    @pl.when(b >= n_used)
    def _():
        o_ref[...] = jnp.zeros_like(o_ref)


def _gather_rows(h_tok, src, n_used, bm, d):
    n_slab = d // LANES
    p = src.shape[0]
    return pl.pallas_call(
        functools.partial(_gather_rows_kernel, rows=bm, n_slab=n_slab),
        grid_spec=pltpu.PrefetchScalarGridSpec(
            num_scalar_prefetch=2,
            grid=(p // bm,),
            in_specs=[pl.BlockSpec(memory_space=pl.ANY)],
            out_specs=pl.BlockSpec((bm, d), lambda b, src, nu: (b, 0)),
            scratch_shapes=[pltpu.VMEM((2, bm * n_slab, LANES), F32), pltpu.SemaphoreType.DMA((2,))]),
        out_shape=jax.ShapeDtypeStruct((p, d), BF16),
        compiler_params=_cparams(1), name="moe_gather",
    )(src, n_used, h_tok)


def _expert_changed(be_ref, b):
    return jnp.logical_or(b == 0, be_ref[b] != be_ref[jnp.maximum(b - 1, 0)])


def _grouped_swiglu_kernel(be_ref, nused_ref, x_ref, w1_ref, w3_ref, o_ref, wb1, wb3):
    b = pl.program_id(1)

    @pl.when(b < nused_ref[0])
    def _():
        @pl.when(_expert_changed(be_ref, b))
        def _():
            wb1[...] = w1_ref[...].astype(BF16)
            wb3[...] = w3_ref[...].astype(BF16)

        _swiglu_passes(x_ref, wb1, wb3, o_ref)

    @pl.when(b >= nused_ref[0])
    def _():
        o_ref[...] = jnp.zeros_like(o_ref)


def _grouped_swiglu(xs, w1, w3, layer, blk_exp, n_used, bm, tn_pref=512):
    p, k = xs.shape
    n = w1.shape[3]
    tn = _pick_tile(n, tn_pref)

    def blk(b, nu):
        return jnp.minimum(b, nu[0] - 1)

    return pl.pallas_call(
        _grouped_swiglu_kernel,
        grid_spec=pltpu.PrefetchScalarGridSpec(
            num_scalar_prefetch=2,
            grid=(n // tn, p // bm),
            in_specs=[pl.BlockSpec((bm, k), lambda j, b, be, nu: (blk(b, nu), 0)),
                      pl.BlockSpec((None, None, k, tn), lambda j, b, be, nu: (layer, be[blk(b, nu)], 0, j)),
                      pl.BlockSpec((None, None, k, tn), lambda j, b, be, nu: (layer, be[blk(b, nu)], 0, j))],
            out_specs=pl.BlockSpec((bm, tn), lambda j, b, be, nu: (b, j)),
            scratch_shapes=[pltpu.VMEM((k, tn), BF16), pltpu.VMEM((k, tn), BF16)]),
        out_shape=jax.ShapeDtypeStruct((p, n), BF16),
        compiler_params=_cparams(2), name="moe_swiglu",
    )(blk_exp, n_used, xs, w1, w3)


def _grouped_down_kernel(be_ref, nused_ref, x_ref, w_ref, o_ref, wb):
    b = pl.program_id(1)

    @pl.when(b < nused_ref[0])
    def _():
        @pl.when(_expert_changed(be_ref, b))
        def _():
            wb[...] = w_ref[...].astype(BF16)

        for rows in _row_passes(x_ref.shape[0]):
            o_ref[rows, :] = jnp.dot(x_ref[rows, :], wb[...], preferred_element_type=F32)

    @pl.when(b >= nused_ref[0])
    def _():
        o_ref[...] = jnp.zeros_like(o_ref)


def _grouped_down(gs, w2, layer, blk_exp, n_used, bm, tn_pref=512):
    p, k = gs.shape
    n = w2.shape[3]
    tn = _pick_tile(n, tn_pref)

    def blk(b, nu):
        return jnp.minimum(b, nu[0] - 1)

    return pl.pallas_call(
        _grouped_down_kernel,
        grid_spec=pltpu.PrefetchScalarGridSpec(
            num_scalar_prefetch=2,
            grid=(n // tn, p // bm),
            in_specs=[pl.BlockSpec((bm, k), lambda j, b, be, nu: (blk(b, nu), 0)),
                      pl.BlockSpec((None, None, k, tn), lambda j, b, be, nu: (layer, be[blk(b, nu)], 0, j))],
            out_specs=pl.BlockSpec((bm, tn), lambda j, b, be, nu: (b, j)),
            scratch_shapes=[pltpu.VMEM((k, tn), BF16)]),
        out_shape=jax.ShapeDtypeStruct((p, n), F32),
        compiler_params=_cparams(2), name="moe_down",
    )(blk_exp, n_used, gs, w2)


def _combine_kernel(dest_ref, os_hbm, y_ref, gates_ref, mod_ref, g_ref, *rest,
                    rows, gate_row, n_ctx_tiles, final):
    if final:
        out_a, out_b, buf, sem = rest
    else:
        nmod_ref, out_a, out_b, buf, sem = rest
    i = pl.program_id(0)
    slot = i % 2

    def issue(tile, to_slot):
        base = tile * rows

        def body(r, carry):
            for k in range(TOP_K):
                row = dest_ref[TOP_K * (base + r) + k]
                pltpu.make_async_copy(os_hbm.at[pl.ds(row, 1), :], buf.at[to_slot, k, pl.ds(r, 1), :],
                                      sem.at[to_slot]).start(priority=k)
            return carry

        lax.fori_loop(0, rows, body, 0)

    @pl.when(i == 0)
    def _():
        issue(0, 0)

    @pl.when(i + 1 < pl.num_programs(0))
    def _():
        issue(i + 1, 1 - slot)

    for k in range(TOP_K):
        pltpu.make_async_copy(os_hbm.at[pl.ds(0, rows), :], buf.at[slot, k], sem.at[slot]).wait()
    gates = gates_ref[...]
    f = gates[:, 0:1] * buf[slot, 0] + gates[:, 1:2] * buf[slot, 1]
    y = y_ref[...] + mod_ref[gate_row:gate_row + 1, :] * f
    if final:
        ms = jnp.mean(y * y, axis=-1, keepdims=True)
        n = (y * lax.rsqrt(ms + EPS)) * g_ref[...]

        @pl.when(i < n_ctx_tiles)
        def _():
            out_a[...] = n

        @pl.when(i >= n_ctx_tiles)
        def _():
            out_b[...] = n
    else:
        out_a[...] = y
        out_b[...] = _norm_mod_value(y, g_ref[...], nmod_ref, 0).astype(out_b.dtype)


def _combine(os_, dest, gates, y, mod, layer, gate_row, next_g, final, n_ctx, dec_seq, tm):
    t, d = y.shape
    nct = n_ctx // tm

    def mod_spec(which):
        return pl.BlockSpec((None, None, 6, d),
                            lambda i, dst: (which, _mod_row(i, tm, n_ctx, dec_seq), 0, 0))

    in_specs = [pl.BlockSpec(memory_space=pl.ANY),
                pl.BlockSpec((tm, d), lambda i, dst: (i, 0)),
                pl.BlockSpec((tm, LANES), lambda i, dst: (i, 0)),
                mod_spec(layer),
                pl.BlockSpec((1, d), lambda i, dst: (0, 0))]
    args = [dest, os_, y, gates, mod, next_g.reshape(1, d)]
    if final:
        out_specs = [pl.BlockSpec((tm, d), lambda i, dst: (jnp.minimum(i, nct - 1), 0)),
                     pl.BlockSpec((tm, d), lambda i, dst: (jnp.maximum(i - nct, 0), 0))]
        out_shape = [jax.ShapeDtypeStruct((n_ctx, d), F32), jax.ShapeDtypeStruct((t - n_ctx, d), F32)]
    else:
        in_specs.append(mod_spec(layer + 1))
        args.append(mod)
        out_specs = [pl.BlockSpec((tm, d), lambda i, dst: (i, 0)),
                     pl.BlockSpec((tm, d), lambda i, dst: (i, 0))]
        out_shape = [jax.ShapeDtypeStruct((t, d), F32), jax.ShapeDtypeStruct((t, d), BF16)]
    return pl.pallas_call(
        functools.partial(_combine_kernel, rows=tm, gate_row=gate_row, n_ctx_tiles=nct, final=final),
        grid_spec=pltpu.PrefetchScalarGridSpec(
            num_scalar_prefetch=1,
            grid=(t // tm,),
            in_specs=in_specs,
            out_specs=out_specs,
            scratch_shapes=[pltpu.VMEM((2, TOP_K, tm, d), F32), pltpu.SemaphoreType.DMA((2,))]),
        out_shape=out_shape,
        compiler_params=_cparams(1), name="moe_combine_final" if final else "moe_combine",
    )(*args)


def kernel(x_prompt, x_sample, state_rglru, c, c_ctx, norm_mix_g, norm_ffn_g, w_mod, b_mod, final_norm_g, rg_w_in, rg_conv_w, rg_conv_b, rg_w_a, rg_b_a, rg_w_x, rg_b_x, rg_lam, rg_w_out, sg_w_in, sg_norm_g, sg_w_s, sg_b_s, sg_w_out, sc_w_in, sc_conv_w, sc_w_out, ff_w1, ff_w3, ff_w2, moe_router, moe_router_b, moe_w1, moe_w3, moe_w2):
    batch, seq, d = x_prompt.shape
    dec_batch, dec_seq, _ = x_sample.shape
    depth = w_mod.shape[0]
    n_ctx = batch * seq
    d_rnn = rg_w_out.shape[1]
    n_exp = moe_router.shape[2]
    chunk = sg_w_s.shape[2]

    tm = dec_seq
    assert dec_seq % seq == 0 and n_ctx % tm == 0 and seq % SUBLANES == 0
    tm_half = max(tm // 2, chunk)
    tm_small = max(tm // 4, chunk)
    assert tm % tm_half == 0 and tm % tm_small == 0 and tm_small % chunk == 0
    moe_bm = tm_half

    n_cond = 1 + dec_batch
    cond = jnp.zeros((-(-n_cond // SUBLANES) * SUBLANES, d), F32)
    cond = cond.at[0].set(c_ctx).at[1:n_cond].set(c)
    mod = _adaln(cond, w_mod, b_mod)[:, :n_cond].reshape(depth, n_cond, 6, d)

    y, h = _embed_norm(x_prompt.reshape(n_ctx, d), x_sample.reshape(dec_batch * dec_seq, d),
                       _grid_pos_embed(dec_seq, d), norm_mix_g[0], mod, tm_small)

    states = []
    outs = None
    for i in range(depth):
        kind, j = i % 3, i // 3
        if h is None:
            h = _norm_mod(y, norm_mix_g[i], mod, i, 0, n_ctx, dec_seq, tm_half)
        if kind == 0:
            z = _mm_act(h, rg_w_in, j, None, F32, tm, tn_pref=1024)
            h0 = jnp.concatenate([jnp.zeros((n_ctx // tm, 2, d_rnn), F32),
                                  state_rglru[:, j].astype(F32)], axis=0)
            mix, st = _rg_core(z, rg_conv_w[j], rg_conv_b[j], rg_w_a[j], rg_b_a[j], rg_w_x[j], rg_b_x[j],
                               rg_lam[j], h0, n_ctx, seq, tm)
            nseg = tm // seq
            st = st[:n_ctx // tm].reshape(n_ctx // tm, 2, nseg, d_rnn)
            states.append(jnp.transpose(st, (0, 2, 1, 3)).reshape(batch, 2, d_rnn))
        elif kind == 1:
            z = _mm_act(h, sg_w_in, j, "gelu", BF16, tm, tn_pref=1024)
            mix = _sgu_core(z, sg_norm_g[j], sg_w_s[j], sg_b_s[j], tm_small)
        else:
            mix = _mm_sconv(h, sc_w_in, sc_conv_w, j, n_ctx, seq, tm)
        w_out = (rg_w_out, sg_w_out, sc_w_out)[kind]
        y = _mm_residual(mix, w_out, j, y, mod, i, 2, n_ctx, dec_seq, tm if mix.shape[1] <= 4096 else tm_half)

        f = i // 2
        last = i == depth - 1
        if i % 2 == 0:
            h = _norm_mod(y, norm_ffn_g[i], mod, i, 3, n_ctx, dec_seq, tm_half)
            g = _mm_swiglu(h, ff_w1, ff_w3, f, tm)
            y = _mm_residual(g, ff_w2, f, y, mod, i, 5, n_ctx, dec_seq, tm_half)
            h = None
        else:
            h_tok, ids, gates = _norm_router(y, norm_ffn_g[i], mod, i, 3, moe_router[f], moe_router_b[f],
                                             n_ctx, dec_seq, tm_half)
            dest, src, blk_exp, n_used = _moe_plan(ids, n_exp, moe_bm)
            xs = _gather_rows(h_tok, src, n_used, moe_bm, d)
            gs = _grouped_swiglu(xs, moe_w1, moe_w3, f, blk_exp, n_used, moe_bm)
            os_ = _grouped_down(gs, moe_w2, f, blk_exp, n_used, moe_bm)
            if last:
                outs = _combine(os_, dest, gates, y, mod, i, 5, final_norm_g, True, n_ctx, dec_seq, tm_small)
            else:
                y, h = _combine(os_, dest, gates, y, mod, i, 5, norm_mix_g[i + 1], False,
                                n_ctx, dec_seq, tm_small)

    y_p, y_s = outs if outs is not None else _final_norm(y, final_norm_g, n_ctx, tm_half)
    new_state = jnp.stack(states, axis=1).astype(x_prompt.dtype)
    return (y_p.reshape(batch, seq, d), y_s.reshape(dec_batch, dec_seq, d), new_state)
```

```python
import functools

import jax
import jax.numpy as jnp
from jax import lax
from jax.experimental import pallas as pl
from jax.experimental.pallas import tpu as pltpu

F32 = jnp.float32
BF16 = jnp.bfloat16

GRID_W = 64
EPS = 1e-6
RG_C = 8.0
TOP_K = 2

LANES = 128
SUBLANES = 8
VMEM_LIMIT_BYTES = 58 * 1024 * 1024
NEG_BIG = -1e30
MXU_ROWS_PER_PASS = 256


def _cparams(n_axes):
    return pltpu.CompilerParams(dimension_semantics=("arbitrary",) * n_axes,
                                vmem_limit_bytes=VMEM_LIMIT_BYTES)


def _pick_tile(n, pref):
    if n <= pref:
        return n
    t = (pref // LANES) * LANES
    while t > LANES and n % t:
        t -= LANES
    assert n % t == 0, (n, pref)
    return t


def _row_passes(rows):
    step = MXU_ROWS_PER_PASS if rows % MXU_ROWS_PER_PASS == 0 else rows
    return [pl.ds(r, step) for r in range(0, rows, step)]


def _sigmoid(x):
    return 1.0 / (1.0 + jnp.exp(-x))


def _gelu_tanh(x):
    c = 0.7978845608028654
    return 0.5 * x * (1.0 + jnp.tanh(c * (x + 0.044715 * (x * x * x))))


def _mod_row(i, tm, n_ctx, dec_seq):
    start = i * tm
    return jnp.where(start < n_ctx, 0, 1 + (start - n_ctx) // dec_seq)


def _adaln_kernel(c_ref, w_ref, b_ref, o_ref):
    c = c_ref[...]
    s = (c * _sigmoid(c)).astype(BF16)
    o_ref[...] = jnp.dot(s, w_ref[...].astype(BF16), preferred_element_type=F32) + b_ref[...]


def _adaln(cond, w_mod, b_mod):
    depth, d, n = w_mod.shape
    mc = cond.shape[0]
    tn = _pick_tile(n, 1024)
    return pl.pallas_call(
        _adaln_kernel,
        grid=(depth, n // tn),
        in_specs=[pl.BlockSpec((mc, d), lambda l, j: (0, 0)),
                  pl.BlockSpec((None, d, tn), lambda l, j: (l, 0, j)),
                  pl.BlockSpec((None, 1, tn), lambda l, j: (l, 0, j))],
        out_specs=pl.BlockSpec((None, mc, tn), lambda l, j: (l, 0, j)),
        out_shape=jax.ShapeDtypeStruct((depth, mc, n), F32),
        compiler_params=_cparams(2), name="adaln",
    )(cond, w_mod, b_mod.reshape(depth, 1, n))


def _embed_norm_kernel(xp_ref, xs_ref, pos_ref, g_ref, mod_ref, y_ref, h_ref, *, n_ctx_tiles):
    i = pl.program_id(0)

    @pl.when(i < n_ctx_tiles)
    def _():
        y_ref[...] = xp_ref[...]

    @pl.when(i >= n_ctx_tiles)
    def _():
        y_ref[...] = xs_ref[...] + pos_ref[...]

    h_ref[...] = _norm_mod_value(y_ref[...], g_ref[...], mod_ref, 0).astype(h_ref.dtype)


def _embed_norm(xp, xs, pos, g, mod, tm):
    n_ctx, d = xp.shape
    n_dec = xs.shape[0]
    dec_seq = pos.shape[0]
    nct = n_ctx // tm
    ppt = dec_seq // tm
    t = n_ctx + n_dec
    return pl.pallas_call(
        functools.partial(_embed_norm_kernel, n_ctx_tiles=nct),
        grid=(t // tm,),
        in_specs=[pl.BlockSpec((tm, d), lambda i: (jnp.minimum(i, nct - 1), 0)),
                  pl.BlockSpec((tm, d), lambda i: (jnp.maximum(i - nct, 0), 0)),
                  pl.BlockSpec((tm, d), lambda i: (jnp.maximum(i - nct, 0) % ppt, 0)),
                  pl.BlockSpec((1, d), lambda i: (0, 0)),
                  pl.BlockSpec((None, None, 6, d), lambda i: (0, _mod_row(i, tm, n_ctx, dec_seq), 0, 0))],
        out_specs=[pl.BlockSpec((tm, d), lambda i: (i, 0)),
                   pl.BlockSpec((tm, d), lambda i: (i, 0))],
        out_shape=[jax.ShapeDtypeStruct((t, d), F32), jax.ShapeDtypeStruct((t, d), BF16)],
        compiler_params=_cparams(1), name="embed_norm",
    )(xp, xs, pos, g.reshape(1, d), mod)


def _grid_pos_embed(length, d):
    rows = length // GRID_W
    r = jnp.repeat(jnp.arange(rows), GRID_W)
    col = jnp.tile(jnp.arange(GRID_W), rows)
    quarter = d // 4
    omega = 1.0 / (10000.0 ** (jnp.arange(quarter, dtype=F32) / quarter))

    def emb(p):
        ang = p[:, None].astype(F32) * omega[None, :]
        return jnp.concatenate([jnp.sin(ang), jnp.cos(ang)], axis=-1)

    return jnp.concatenate([emb(r), emb(col)], axis=-1).astype(F32)


def _norm_mod_value(y, g, mod_ref, shift_row):
    ms = jnp.mean(y * y, axis=-1, keepdims=True)
    n = (y * lax.rsqrt(ms + EPS)) * g
    return n * (1.0 + mod_ref[shift_row + 1:shift_row + 2, :]) + mod_ref[shift_row:shift_row + 1, :]


def _norm_mod_kernel(y_ref, g_ref, mod_ref, h_ref, *, shift_row):
    h_ref[...] = _norm_mod_value(y_ref[...], g_ref[...], mod_ref, shift_row).astype(h_ref.dtype)


def _norm_mod(y, g, mod, layer, shift_row, n_ctx, dec_seq, tm):
    t, d = y.shape
    return pl.pallas_call(
        functools.partial(_norm_mod_kernel, shift_row=shift_row),
        grid=(t // tm,),
        in_specs=[pl.BlockSpec((tm, d), lambda i: (i, 0)),
                  pl.BlockSpec((1, d), lambda i: (0, 0)),
                  pl.BlockSpec((None, None, 6, d),
                               lambda i: (layer, _mod_row(i, tm, n_ctx, dec_seq), 0, 0))],
        out_specs=pl.BlockSpec((tm, d), lambda i: (i, 0)),
        out_shape=jax.ShapeDtypeStruct((t, d), BF16),
        compiler_params=_cparams(1), name="norm_mod",
    )(y, g.reshape(1, d), mod)


def _norm_router_kernel(y_ref, g_ref, mod_ref, r_ref, rb_ref, h_ref, ids_ref, gates_ref, *,
                        shift_row, n_slab):
    h = _norm_mod_value(y_ref[...], g_ref[...], mod_ref, shift_row)
    tm = h.shape[0]
    for s in range(n_slab):
        h_ref[pl.ds(s, tm, stride=n_slab), :] = h[:, s * LANES:(s + 1) * LANES]
    h_hi = h.astype(BF16)
    h_lo = (h - h_hi.astype(F32)).astype(BF16)
    p_hi = jnp.dot(h_hi, r_ref[...], preferred_element_type=F32)
    p_lo = jnp.dot(h_lo, r_ref[:, :LANES], preferred_element_type=F32)
    logits = ((p_hi[:, :LANES] + p_hi[:, LANES:]) + p_lo) + rb_ref[...]
    lane = lax.broadcasted_iota(jnp.int32, logits.shape, 1).astype(F32)
    big = float(LANES)
    m1 = jnp.max(logits, axis=-1, keepdims=True)
    i1 = jnp.min(jnp.where(logits == m1, lane, big), axis=-1, keepdims=True)
    l2 = jnp.where(lane == i1, 2.0 * NEG_BIG, logits)
    m2 = jnp.max(l2, axis=-1, keepdims=True)
    i2 = jnp.min(jnp.where(l2 == m2, lane, big), axis=-1, keepdims=True)
    e = jnp.exp(m2 - m1)
    g1 = 1.0 / (1.0 + e)
    g2 = e / (1.0 + e)
    ids_ref[...] = jnp.where(lane == 0.0, i1, jnp.where(lane == 1.0, i2, 0.0)).astype(jnp.int32)
    gates_ref[...] = jnp.where(lane == 0.0, g1, jnp.where(lane == 1.0, g2, 0.0))


def _norm_router(y, g, mod, layer, shift_row, router, router_b, n_ctx, dec_seq, tm):
    t, d = y.shape
    n_exp = router.shape[1]
    n_slab = d // LANES
    rp = jnp.zeros((d, LANES), F32).at[:, :n_exp].set(router)
    r_hi = rp.astype(BF16)
    r_lo = (rp - r_hi.astype(F32)).astype(BF16)
    rp = jnp.concatenate([r_hi, r_lo], axis=1)
    rbp = jnp.full((1, LANES), NEG_BIG, F32).at[0, :n_exp].set(router_b)
    return pl.pallas_call(
        functools.partial(_norm_router_kernel, shift_row=shift_row, n_slab=n_slab),
        grid=(t // tm,),
        in_specs=[pl.BlockSpec((tm, d), lambda i: (i, 0)),
                  pl.BlockSpec((1, d), lambda i: (0, 0)),
                  pl.BlockSpec((None, None, 6, d),
                               lambda i: (layer, _mod_row(i, tm, n_ctx, dec_seq), 0, 0)),
                  pl.BlockSpec((d, 2 * LANES), lambda i: (0, 0)),
                  pl.BlockSpec((1, LANES), lambda i: (0, 0))],
        out_specs=[pl.BlockSpec((tm * n_slab, LANES), lambda i: (i, 0)),
                   pl.BlockSpec((tm, LANES), lambda i: (i, 0)),
                   pl.BlockSpec((tm, LANES), lambda i: (i, 0))],
        out_shape=[jax.ShapeDtypeStruct((t * n_slab, LANES), F32),
                   jax.ShapeDtypeStruct((t, LANES), jnp.int32),
                   jax.ShapeDtypeStruct((t, LANES), F32)],
        compiler_params=_cparams(1), name="norm_router",
    )(y, g.reshape(1, d), mod, rp, rbp)


def _final_norm_kernel(y_ref, g_ref, op_ref, os_ref, *, n_ctx_tiles):
    i = pl.program_id(0)
    y = y_ref[...]
    ms = jnp.mean(y * y, axis=-1, keepdims=True)
    n = (y * lax.rsqrt(ms + EPS)) * g_ref[...]

    @pl.when(i < n_ctx_tiles)
    def _():
        op_ref[...] = n

    @pl.when(i >= n_ctx_tiles)
    def _():
        os_ref[...] = n


def _final_norm(y, g, n_ctx, tm):
    t, d = y.shape
    nct = n_ctx // tm
    return pl.pallas_call(
        functools.partial(_final_norm_kernel, n_ctx_tiles=nct),
        grid=(t // tm,),
        in_specs=[pl.BlockSpec((tm, d), lambda i: (i, 0)),
                  pl.BlockSpec((1, d), lambda i: (0, 0))],
        out_specs=[pl.BlockSpec((tm, d), lambda i: (jnp.minimum(i, nct - 1), 0)),
                   pl.BlockSpec((tm, d), lambda i: (jnp.maximum(i - nct, 0), 0))],
        out_shape=[jax.ShapeDtypeStruct((n_ctx, d), F32),
                   jax.ShapeDtypeStruct((t - n_ctx, d), F32)],
        compiler_params=_cparams(1), name="final_norm",
    )(y, g.reshape(1, d))


def _cast_at_first_row_tile(w_refs, wb_refs):
    @pl.when(pl.program_id(1) == 0)
    def _():
        for w, wb in zip(w_refs, wb_refs):
            wb[...] = w[...].astype(BF16)


def _mm_act_kernel(x_ref, w_ref, o_ref, wb, *, act):
    _cast_at_first_row_tile((w_ref,), (wb,))
    for rows in _row_passes(x_ref.shape[0]):
        acc = jnp.dot(x_ref[rows, :], wb[...], preferred_element_type=F32)
        if act == "gelu":
            acc = _gelu_tanh(acc)
        o_ref[rows, :] = acc.astype(o_ref.dtype)


def _mm_act(x, w, layer, act, out_dtype, tm, tn_pref=512):
    t, k = x.shape
    n = w.shape[2]
    tn = _pick_tile(n, tn_pref)
    return pl.pallas_call(
        functools.partial(_mm_act_kernel, act=act),
        grid=(n // tn, t // tm),
        in_specs=[pl.BlockSpec((tm, k), lambda j, i: (i, 0)),
                  pl.BlockSpec((None, k, tn), lambda j, i: (layer, 0, j))],
        out_specs=pl.BlockSpec((tm, tn), lambda j, i: (i, j)),
        out_shape=jax.ShapeDtypeStruct((t, n), out_dtype),
        scratch_shapes=[pltpu.VMEM((k, tn), BF16)],
        compiler_params=_cparams(2), name="mm_" + str(act),
    )(x, w)


def _swiglu_passes(x_ref, wb1, wb3, o_ref):
    for rows in _row_passes(x_ref.shape[0]):
        x = x_ref[rows, :]
        a = jnp.dot(x, wb1[...], preferred_element_type=F32)
        b = jnp.dot(x, wb3[...], preferred_element_type=F32)
        o_ref[rows, :] = ((a * _sigmoid(a)) * b).astype(o_ref.dtype)


def _mm_swiglu_kernel(x_ref, w1_ref, w3_ref, o_ref, wb1, wb3):
    _cast_at_first_row_tile((w1_ref, w3_ref), (wb1, wb3))
    _swiglu_passes(x_ref, wb1, wb3, o_ref)


def _mm_swiglu(x, w1, w3, layer, tm, tn_pref=512):
    t, k = x.shape
    n = w1.shape[2]
    tn = _pick_tile(n, tn_pref)
    return pl.pallas_call(
        _mm_swiglu_kernel,
        grid=(n // tn, t // tm),
        in_specs=[pl.BlockSpec((tm, k), lambda j, i: (i, 0)),
                  pl.BlockSpec((None, k, tn), lambda j, i: (layer, 0, j)),
                  pl.BlockSpec((None, k, tn), lambda j, i: (layer, 0, j))],
        out_specs=pl.BlockSpec((tm, tn), lambda j, i: (i, j)),
        out_shape=jax.ShapeDtypeStruct((t, n), BF16),
        scratch_shapes=[pltpu.VMEM((k, tn), BF16), pltpu.VMEM((k, tn), BF16)],
        compiler_params=_cparams(2), name="mm_swiglu",
    )(x, w1, w3)


def _mm_sconv_kernel(x_ref, wb_ref, wc_ref, wx_ref, cw_ref, o_ref, sb, sc, sx, *, n_ctx_tiles, seq_ctx):
    _cast_at_first_row_tile((wb_ref, wc_ref, wx_ref), (sb, sc, sx))
    i = pl.program_id(1)
    is_ctx = i < n_ctx_tiles
    x = x_ref[...]
    bg = jnp.dot(x, sb[...], preferred_element_type=F32)
    p = jnp.dot(x, sc[...], preferred_element_type=F32) * jnp.dot(x, sx[...], preferred_element_type=F32)
    tm = p.shape[0]
    row = lax.broadcasted_iota(jnp.int32, p.shape, 0)
    pos = jnp.where(is_ctx, row % seq_ctx, row)
    last_pos = jnp.where(is_ctx, seq_ctx - 1, tm - 1)
    p_prev = jnp.where(pos == 0, 0.0, pltpu.roll(p, 1, axis=0))
    p_next = jnp.where(pos == last_pos, 0.0, pltpu.roll(p, tm - 1, axis=0))
    cw = cw_ref[...]
    conv = (cw[0:1] * p_prev + cw[1:2] * p) + cw[2:3] * p_next
    o_ref[...] = (bg * conv).astype(o_ref.dtype)


def _mm_sconv(x, w_in, conv_w, layer, n_ctx, seq_ctx, tm, tn_pref=512):
    t, k = x.shape
    d = conv_w.shape[2]
    tn = _pick_tile(d, tn_pref)
    nj = d // tn
    return pl.pallas_call(
        functools.partial(_mm_sconv_kernel, n_ctx_tiles=n_ctx // tm, seq_ctx=seq_ctx),
        grid=(nj, t // tm),
        in_specs=[pl.BlockSpec((tm, k), lambda j, i: (i, 0)),
                  pl.BlockSpec((None, k, tn), lambda j, i: (layer, 0, j)),
                  pl.BlockSpec((None, k, tn), lambda j, i: (layer, 0, nj + j)),
                  pl.BlockSpec((None, k, tn), lambda j, i: (layer, 0, 2 * nj + j)),
                  pl.BlockSpec((None, conv_w.shape[1], tn), lambda j, i: (layer, 0, j))],
        out_specs=pl.BlockSpec((tm, tn), lambda j, i: (i, j)),
        out_shape=jax.ShapeDtypeStruct((t, d), BF16),
        scratch_shapes=[pltpu.VMEM((k, tn), BF16)] * 3,
        compiler_params=_cparams(2), name="mm_sconv",
    )(x, w_in, w_in, w_in, conv_w)


def _mm_residual_kernel(x_ref, w_ref, y_ref, mod_ref, o_ref, wb, *, gate_row):
    _cast_at_first_row_tile((w_ref,), (wb,))
    gate = mod_ref[gate_row:gate_row + 1, :]
    for rows in _row_passes(x_ref.shape[0]):
        acc = jnp.dot(x_ref[rows, :], wb[...], preferred_element_type=F32)
        o_ref[rows, :] = y_ref[rows, :] + gate * acc


def _mm_residual(x, w, w_layer, y, mod, layer, gate_row, n_ctx, dec_seq, tm, tn_pref=512):
    t, k = x.shape
    n = w.shape[2]
    tn = _pick_tile(n, tn_pref)
    return pl.pallas_call(
        functools.partial(_mm_residual_kernel, gate_row=gate_row),
        grid=(n // tn, t // tm),
        in_specs=[pl.BlockSpec((tm, k), lambda j, i: (i, 0)),
                  pl.BlockSpec((None, k, tn), lambda j, i: (w_layer, 0, j)),
                  pl.BlockSpec((tm, tn), lambda j, i: (i, j)),
                  pl.BlockSpec((None, None, 6, tn),
                               lambda j, i: (layer, _mod_row(i, tm, n_ctx, dec_seq), 0, j))],
        out_specs=pl.BlockSpec((tm, tn), lambda j, i: (i, j)),
        out_shape=jax.ShapeDtypeStruct((t, n), F32),
        scratch_shapes=[pltpu.VMEM((k, tn), BF16)],
        compiler_params=_cparams(2), name="mm_residual",
    )(x, w, y, mod)


def _rg_core_kernel(gate_ref, x_ref, cw_ref, cb_ref, wg_ref, bg_ref, lam_ref, h0_ref,
                    y_ref, st_ref, af_s, bf_s, ab_s, bb_s, *, n_ctx_tiles, seg, tm, cg):
    i = pl.program_id(1)
    is_ctx = i < n_ctx_tiles
    nseg = tm // seg
    nslab = cg // LANES
    lsub = tm // SUBLANES
    sub_per_seq = seg // lsub
    cw = cw_ref[...]
    cb = cb_ref[...]
    lam = lam_ref[...]
    softplus_neg_lam = jnp.maximum(-lam, 0.0) + jnp.log1p(jnp.exp(-jnp.abs(lam)))
    neg_c_sp = (-RG_C) * softplus_neg_lam
    zeros8 = jnp.zeros((SUBLANES, cg), F32)

    for s in range(nseg):
        c0 = s * seg
        cur = x_ref[pl.ds(c0, seg), :]
        prev8 = zeros8 if s == 0 else jnp.where(is_ctx, 0.0, x_ref[pl.ds(c0 - SUBLANES, SUBLANES), :])
        next8 = zeros8 if s == nseg - 1 else jnp.where(is_ctx, 0.0, x_ref[pl.ds(c0 + seg, SUBLANES), :])
        head = jnp.concatenate([prev8, cur[:2 * SUBLANES]], axis=0)
        tail = jnp.concatenate([cur[seg - 2 * SUBLANES:], next8], axis=0)

        def shifted(delta):
            lo = SUBLANES + delta
            mid = x_ref[pl.ds(c0 + lo, seg - 2 * SUBLANES), :]
            return jnp.concatenate([head[lo:lo + SUBLANES], mid, tail[lo:lo + SUBLANES]], axis=0)

        xc = (((cw[0:1] * shifted(-2) + cw[1:2] * shifted(-1)) + cw[2:3] * cur)
              + cw[3:4] * shifted(1)) + cb
        g = jnp.dot(xc.astype(BF16), wg_ref[...], preferred_element_type=F32) + bg_ref[...]
        for d, (a_s, b_s) in enumerate(((af_s, bf_s), (ab_s, bb_s))):
            r = _sigmoid(g[:, (2 * d) * cg:(2 * d + 1) * cg])
            ig = _sigmoid(g[:, (2 * d + 1) * cg:(2 * d + 2) * cg])
            log_a = neg_c_sp[d:d + 1] * r
            a = jnp.exp(log_a)
            one_minus_a2 = -jnp.tanh(log_a) * (a * a + 1.0)
            root = jnp.where(one_minus_a2 > 0.0, one_minus_a2 * lax.rsqrt(one_minus_a2), 0.0)
            bt = root * (ig * xc)
            for kk in range(sub_per_seq):
                k = s * sub_per_seq + kk
                dst = pl.ds(k, lsub, stride=SUBLANES)
                for l in range(nslab):
                    a_s[l, dst, :] = a[kk * lsub:(kk + 1) * lsub, l * LANES:(l + 1) * LANES]
                    b_s[l, dst, :] = bt[kk * lsub:(kk + 1) * lsub, l * LANES:(l + 1) * LANES]

    def local_scan(j, carry):
        hf, pf, hb, pb = carry
        rf = pl.multiple_of(j * SUBLANES, SUBLANES)
        rb = pl.multiple_of((lsub - 1 - j) * SUBLANES, SUBLANES)
        nhf, npf, nhb, npb = [], [], [], []
        for l in range(nslab):
            a = af_s[l, pl.ds(rf, SUBLANES), :]
            h = a * hf[l] + bf_s[l, pl.ds(rf, SUBLANES), :]
            p = a * pf[l]
            bf_s[l, pl.ds(rf, SUBLANES), :] = h
            af_s[l, pl.ds(rf, SUBLANES), :] = p
            nhf.append(h)
            npf.append(p)
            a = ab_s[l, pl.ds(rb, SUBLANES), :]
            h = a * hb[l] + bb_s[l, pl.ds(rb, SUBLANES), :]
            p = a * pb[l]
            bb_s[l, pl.ds(rb, SUBLANES), :] = h
            ab_s[l, pl.ds(rb, SUBLANES), :] = p
            nhb.append(h)
            npb.append(p)
        return tuple(nhf), tuple(npf), tuple(nhb), tuple(npb)

    zero = tuple(jnp.zeros((SUBLANES, LANES), F32) for _ in range(nslab))
    one = tuple(jnp.ones((SUBLANES, LANES), F32) for _ in range(nslab))
    hf_end, pf_end, hb_end, pb_end = lax.fori_loop(0, lsub, local_scan, (zero, one, zero, one))

    row8 = lax.broadcasted_iota(jnp.int32, (SUBLANES, LANES), 0)
    h0 = h0_ref[...]
    for l in range(nslab):
        lanes = slice(l * LANES, (l + 1) * LANES)
        ent_f = jnp.zeros((SUBLANES, LANES), F32)
        h_in = h0[0:1, lanes]
        for k in range(SUBLANES):
            if k > 0 and k % sub_per_seq == 0:
                h_in = jnp.where(is_ctx, 0.0, h_in)
            ent_f = jnp.where(row8 == k, h_in, ent_f)
            h_in = hf_end[l][k:k + 1] + pf_end[l][k:k + 1] * h_in
            if (k + 1) % sub_per_seq == 0:
                q = k // sub_per_seq
                st_ref[q:q + 1, lanes] = h_in
        ent_b = jnp.zeros((SUBLANES, LANES), F32)
        h_in = h0[1:2, lanes]
        for k in reversed(range(SUBLANES)):
            if k < SUBLANES - 1 and (k + 1) % sub_per_seq == 0:
                h_in = jnp.where(is_ctx, 0.0, h_in)
            ent_b = jnp.where(row8 == k, h_in, ent_b)
            h_in = hb_end[l][k:k + 1] + pb_end[l][k:k + 1] * h_in
            if k % sub_per_seq == 0:
                q = k // sub_per_seq
                st_ref[nseg + q:nseg + q + 1, lanes] = h_in
        ch = min(tm, 256)
        ef = jnp.concatenate([ent_f] * (ch // SUBLANES), axis=0)
        eb = jnp.concatenate([ent_b] * (ch // SUBLANES), axis=0)
        for r0 in range(0, tm, ch):
            rows = pl.ds(r0, ch)
            bf_s[l, rows, :] = ((bf_s[l, rows, :] + af_s[l, rows, :] * ef)
                                + (bb_s[l, rows, :] + ab_s[l, rows, :] * eb))

    for k in range(SUBLANES):
        rows = pl.ds(k * lsub, lsub)
        hsum = jnp.concatenate([bf_s[l, pl.ds(k, lsub, stride=SUBLANES), :] for l in range(nslab)], axis=1)
        y_ref[rows, :] = (hsum * _gelu_tanh(gate_ref[rows, :])).astype(y_ref.dtype)


def _rg_core(z, conv_w, conv_b, w_a, b_a, w_x, b_x, lam, h0, n_ctx, seq_ctx, tm):
    t = z.shape[0]
    r = conv_w.shape[1]
    heads, hw = w_a.shape[1], w_a.shape[2]
    hpg = 4
    while (hpg * hw) % LANES:
        hpg *= 2
    ng = heads // hpg
    cg = hpg * hw
    nseg = tm // seq_ctx
    ntiles = t // tm
    assert tm % SUBLANES == 0 and seq_ctx % (tm // SUBLANES) == 0

    def blockdiag(w):
        w5 = w.reshape(2, ng, hpg, hw, hw)
        eye = jnp.eye(hpg, dtype=w.dtype)
        return jnp.einsum("dghio,hk->dghiko", w5, eye).reshape(2, ng, cg, cg)

    wa, wx = blockdiag(w_a), blockdiag(w_x)
    wg = jnp.concatenate([wa[0], wx[0], wa[1], wx[1]], axis=-1).astype(BF16)
    ba, bx = b_a.reshape(2, ng, 1, cg), b_x.reshape(2, ng, 1, cg)
    bg = jnp.concatenate([ba[0], bx[0], ba[1], bx[1]], axis=-1)

    return pl.pallas_call(
        functools.partial(_rg_core_kernel, n_ctx_tiles=n_ctx // tm, seg=seq_ctx, tm=tm, cg=cg),
        grid=(ng, ntiles),
        in_specs=[pl.BlockSpec((tm, cg), lambda g, i: (i, g)),
                  pl.BlockSpec((tm, cg), lambda g, i: (i, ng + g)),
                  pl.BlockSpec((conv_w.shape[0], cg), lambda g, i: (0, g)),
                  pl.BlockSpec((1, cg), lambda g, i: (0, g)),
                  pl.BlockSpec((None, cg, 4 * cg), lambda g, i: (g, 0, 0)),
                  pl.BlockSpec((None, 1, 4 * cg), lambda g, i: (g, 0, 0)),
                  pl.BlockSpec((2, cg), lambda g, i: (0, g)),
                  pl.BlockSpec((None, 2, cg), lambda g, i: (i, 0, g))],
        out_specs=[pl.BlockSpec((tm, cg), lambda g, i: (i, g)),
                   pl.BlockSpec((None, 2 * nseg, cg), lambda g, i: (i, 0, g))],
        out_shape=[jax.ShapeDtypeStruct((t, r), BF16),
                   jax.ShapeDtypeStruct((ntiles, 2 * nseg, r), F32)],
        scratch_shapes=[pltpu.VMEM((cg // LANES, tm, LANES), F32)] * 4,
        compiler_params=_cparams(2), name="rg_core",
    )(z, z, conv_w, conv_b.reshape(1, r), wg, bg, lam, h0)


def _sgu_core_kernel(u_ref, v_ref, g_ref, ws_ref, bs_ref, y_ref, vn_s, *, chunk, gw, ngroups, tm):
    v = v_ref[...].astype(F32)
    ms = jnp.mean(v * v, axis=-1, keepdims=True)
    vn_s[...] = ((v * lax.rsqrt(ms + EPS)) * g_ref[...]).astype(BF16)
    for c in range(tm // chunk):
        rows = pl.ds(c * chunk, chunk)
        for g in range(ngroups):
            cols = pl.ds(g * gw, gw)
            vm = jnp.dot(ws_ref[g], vn_s[rows, cols], preferred_element_type=F32) + bs_ref[:, g:g + 1]
            y_ref[rows, cols] = (u_ref[rows, cols].astype(F32) * vm).astype(y_ref.dtype)


def _sgu_core(z, norm_g, w_s, b_s, tm):
    t = z.shape[0]
    w = norm_g.shape[0]
    ngroups, chunk = w_s.shape[0], w_s.shape[1]
    gw = w // ngroups
    return pl.pallas_call(
        functools.partial(_sgu_core_kernel, chunk=chunk, gw=gw, ngroups=ngroups, tm=tm),
        grid=(t // tm,),
        in_specs=[pl.BlockSpec((tm, w), lambda i: (i, 0)),
                  pl.BlockSpec((tm, w), lambda i: (i, 1)),
                  pl.BlockSpec((1, w), lambda i: (0, 0)),
                  pl.BlockSpec((ngroups, chunk, chunk), lambda i: (0, 0, 0)),
                  pl.BlockSpec((chunk, ngroups), lambda i: (0, 0))],
        out_specs=pl.BlockSpec((tm, w), lambda i: (i, 0)),
        out_shape=jax.ShapeDtypeStruct((t, w), BF16),
        scratch_shapes=[pltpu.VMEM((tm, w), BF16)],
        compiler_params=_cparams(1), name="sgu_core",
    )(z, z, norm_g.reshape(1, w), w_s.astype(BF16), b_s.T)


def _moe_plan(ids, n_exp, bm):
    t = ids.shape[0]
    e = ids[:, :TOP_K].reshape(-1)
    oh = (e[:, None] == jnp.arange(n_exp, dtype=jnp.int32)[None, :]).astype(jnp.int32)
    csum = jnp.cumsum(oh, axis=0)
    rank = jnp.sum((csum - oh) * oh, axis=1)
    counts = csum[-1]
    padded = ((counts + bm - 1) // bm) * bm
    ends = jnp.cumsum(padded)
    starts = ends - padded
    dest = (jnp.sum(starts[None, :] * oh, axis=1) + rank).astype(jnp.int32)
    p = TOP_K * t + n_exp * bm
    src = jnp.zeros((p,), jnp.int32).at[dest].set(jnp.arange(TOP_K * t, dtype=jnp.int32) // TOP_K)
    nb = p // bm
    blk_start = jnp.arange(nb, dtype=jnp.int32) * bm
    blk_exp = jnp.minimum(jnp.sum((blk_start[:, None] >= ends[None, :]).astype(jnp.int32), axis=1),
                          n_exp - 1).astype(jnp.int32)
    n_used = (ends[-1] // bm).astype(jnp.int32).reshape(1)
    return dest, src, blk_exp, n_used


def _gather_rows_kernel(src_ref, nused_ref, h_hbm, o_ref, buf, sem, *, rows, n_slab):
    b = pl.program_id(0)
    n_used = nused_ref[0]
    slot = b % 2

    def issue(blk, to_slot):
        base = blk * rows

        def body(q, carry):
            for prio in range(2):
                r = 2 * q + prio
                tok = pl.multiple_of(src_ref[base + r] * n_slab, n_slab)
                pltpu.make_async_copy(h_hbm.at[pl.ds(tok, n_slab), :],
                                      buf.at[to_slot, pl.ds(pl.multiple_of(r * n_slab, n_slab), n_slab), :],
                                      sem.at[to_slot]).start(priority=prio)
            return carry

        lax.fori_loop(0, rows // 2, body, 0, unroll=4)

    @pl.when(b == 0)
    def _():
        issue(0, 0)

    @pl.when(b + 1 < n_used)
    def _():
        issue(b + 1, 1 - slot)

    @pl.when(b < n_used)
    def _():
        pltpu.make_async_copy(h_hbm.at[pl.ds(0, rows * n_slab), :], buf.at[slot], sem.at[slot]).wait()
        for s in range(n_slab):
            o_ref[:, s * LANES:(s + 1) * LANES] = buf[slot, pl.ds(s, rows, stride=n_slab), :].astype(o_ref.dtype)

    @pl.when(b >= n_used)
    def _():
        o_ref[...] = jnp.zeros_like(o_ref)


def _gather_rows(h_tok, src, n_used, bm, d):
    n_slab = d // LANES
    p = src.shape[0]
    return pl.pallas_call(
        functools.partial(_gather_rows_kernel, rows=bm, n_slab=n_slab),
        grid_spec=pltpu.PrefetchScalarGridSpec(
            num_scalar_prefetch=2,
            grid=(p // bm,),
            in_specs=[pl.BlockSpec(memory_space=pl.ANY)],
            out_specs=pl.BlockSpec((bm, d), lambda b, src, nu: (b, 0)),
            scratch_shapes=[pltpu.VMEM((2, bm * n_slab, LANES), F32), pltpu.SemaphoreType.DMA((2,))]),
        out_shape=jax.ShapeDtypeStruct((p, d), BF16),
        compiler_params=_cparams(1), name="moe_gather",
    )(src, n_used, h_tok)


def _expert_changed(be_ref, b):
    return jnp.logical_or(b == 0, be_ref[b] != be_ref[jnp.maximum(b - 1, 0)])


def _grouped_swiglu_kernel(be_ref, nused_ref, x_ref, w1_ref, w3_ref, o_ref, wb1, wb3):
    b = pl.program_id(1)

    @pl.when(b < nused_ref[0])
    def _():
        @pl.when(_expert_changed(be_ref, b))
        def _():
            wb1[...] = w1_ref[...].astype(BF16)
            wb3[...] = w3_ref[...].astype(BF16)

        _swiglu_passes(x_ref, wb1, wb3, o_ref)

    @pl.when(b >= nused_ref[0])
    def _():
        o_ref[...] = jnp.zeros_like(o_ref)


def _grouped_swiglu(xs, w1, w3, layer, blk_exp, n_used, bm, tn_pref=512):
    p, k = xs.shape
    n = w1.shape[3]
    tn = _pick_tile(n, tn_pref)

    def blk(b, nu):
        return jnp.minimum(b, nu[0] - 1)

    return pl.pallas_call(
        _grouped_swiglu_kernel,
        grid_spec=pltpu.PrefetchScalarGridSpec(
            num_scalar_prefetch=2,
            grid=(n // tn, p // bm),
            in_specs=[pl.BlockSpec((bm, k), lambda j, b, be, nu: (blk(b, nu), 0)),
                      pl.BlockSpec((None, None, k, tn), lambda j, b, be, nu: (layer, be[blk(b, nu)], 0, j)),
                      pl.BlockSpec((None, None, k, tn), lambda j, b, be, nu: (layer, be[blk(b, nu)], 0, j))],
            out_specs=pl.BlockSpec((bm, tn), lambda j, b, be, nu: (b, j)),
            scratch_shapes=[pltpu.VMEM((k, tn), BF16), pltpu.VMEM((k, tn), BF16)]),
        out_shape=jax.ShapeDtypeStruct((p, n), BF16),
        compiler_params=_cparams(2), name="moe_swiglu",
    )(blk_exp, n_used, xs, w1, w3)


def _grouped_down_kernel(be_ref, nused_ref, x_ref, w_ref, o_ref, wb):
    b = pl.program_id(1)

    @pl.when(b < nused_ref[0])
    def _():
        @pl.when(_expert_changed(be_ref, b))
        def _():
            wb[...] = w_ref[...].astype(BF16)

        for rows in _row_passes(x_ref.shape[0]):
            o_ref[rows, :] = jnp.dot(x_ref[rows, :], wb[...], preferred_element_type=F32)

    @pl.when(b >= nused_ref[0])
    def _():
        o_ref[...] = jnp.zeros_like(o_ref)


def _grouped_down(gs, w2, layer, blk_exp, n_used, bm, tn_pref=512):
    p, k = gs.shape
    n = w2.shape[3]
    tn = _pick_tile(n, tn_pref)

    def blk(b, nu):
        return jnp.minimum(b, nu[0] - 1)

    return pl.pallas_call(
        _grouped_down_kernel,
        grid_spec=pltpu.PrefetchScalarGridSpec(
            num_scalar_prefetch=2,
            grid=(n // tn, p // bm),
            in_specs=[pl.BlockSpec((bm, k), lambda j, b, be, nu: (blk(b, nu), 0)),
                      pl.BlockSpec((None, None, k, tn), lambda j, b, be, nu: (layer, be[blk(b, nu)], 0, j))],
            out_specs=pl.BlockSpec((bm, tn), lambda j, b, be, nu: (b, j)),
            scratch_shapes=[pltpu.VMEM((k, tn), BF16)]),
        out_shape=jax.ShapeDtypeStruct((p, n), F32),
        compiler_params=_cparams(2), name="moe_down",
    )(blk_exp, n_used, gs, w2)


def _combine_kernel(dest_ref, os_hbm, y_ref, gates_ref, mod_ref, g_ref, *rest,
                    rows, gate_row, n_ctx_tiles, final):
    if final:
        out_a, out_b, buf, sem = rest
    else:
        nmod_ref, out_a, out_b, buf, sem = rest
    i = pl.program_id(0)
    slot = i % 2

    def issue(tile, to_slot):
        base = tile * rows

        def body(r, carry):
            for k in range(TOP_K):
                row = dest_ref[TOP_K * (base + r) + k]
                pltpu.make_async_copy(os_hbm.at[pl.ds(row, 1), :], buf.at[to_slot, k, pl.ds(r, 1), :],
                                      sem.at[to_slot]).start(priority=k)
            return carry

        lax.fori_loop(0, rows, body, 0, unroll=4)

    @pl.when(i == 0)
    def _():
        issue(0, 0)

    @pl.when(i + 1 < pl.num_programs(0))
    def _():
        issue(i + 1, 1 - slot)

    for k in range(TOP_K):
        pltpu.make_async_copy(os_hbm.at[pl.ds(0, rows), :], buf.at[slot, k], sem.at[slot]).wait()
    gates = gates_ref[...]
    f = gates[:, 0:1] * buf[slot, 0] + gates[:, 1:2] * buf[slot, 1]
    y = y_ref[...] + mod_ref[gate_row:gate_row + 1, :] * f
    if final:
        ms = jnp.mean(y * y, axis=-1, keepdims=True)
        n = (y * lax.rsqrt(ms + EPS)) * g_ref[...]

        @pl.when(i < n_ctx_tiles)
        def _():
            out_a[...] = n

        @pl.when(i >= n_ctx_tiles)
        def _():
            out_b[...] = n
    else:
        out_a[...] = y
        out_b[...] = _norm_mod_value(y, g_ref[...], nmod_ref, 0).astype(out_b.dtype)


def _combine(os_, dest, gates, y, mod, layer, gate_row, next_g, final, n_ctx, dec_seq, tm):
    t, d = y.shape
    nct = n_ctx // tm

    def mod_spec(which):
        return pl.BlockSpec((None, None, 6, d),
                            lambda i, dst: (which, _mod_row(i, tm, n_ctx, dec_seq), 0, 0))

    in_specs = [pl.BlockSpec(memory_space=pl.ANY),
                pl.BlockSpec((tm, d), lambda i, dst: (i, 0)),
                pl.BlockSpec((tm, LANES), lambda i, dst: (i, 0)),
                mod_spec(layer),
                pl.BlockSpec((1, d), lambda i, dst: (0, 0))]
    args = [dest, os_, y, gates, mod, next_g.reshape(1, d)]
    if final:
        out_specs = [pl.BlockSpec((tm, d), lambda i, dst: (jnp.minimum(i, nct - 1), 0)),
                     pl.BlockSpec((tm, d), lambda i, dst: (jnp.maximum(i - nct, 0), 0))]
        out_shape = [jax.ShapeDtypeStruct((n_ctx, d), F32), jax.ShapeDtypeStruct((t - n_ctx, d), F32)]
    else:
        in_specs.append(mod_spec(layer + 1))
        args.append(mod)
        out_specs = [pl.BlockSpec((tm, d), lambda i, dst: (i, 0)),
                     pl.BlockSpec((tm, d), lambda i, dst: (i, 0))]
        out_shape = [jax.ShapeDtypeStruct((t, d), F32), jax.ShapeDtypeStruct((t, d), BF16)]
    return pl.pallas_call(
        functools.partial(_combine_kernel, rows=tm, gate_row=gate_row, n_ctx_tiles=nct, final=final),
        grid_spec=pltpu.PrefetchScalarGridSpec(
            num_scalar_prefetch=1,
            grid=(t // tm,),
            in_specs=in_specs,
            out_specs=out_specs,
            scratch_shapes=[pltpu.VMEM((2, TOP_K, tm, d), F32), pltpu.SemaphoreType.DMA((2,))]),
        out_shape=out_shape,
        compiler_params=_cparams(1), name="moe_combine_final" if final else "moe_combine",
    )(*args)


def kernel(x_prompt, x_sample, state_rglru, c, c_ctx, norm_mix_g, norm_ffn_g, w_mod, b_mod, final_norm_g, rg_w_in, rg_conv_w, rg_conv_b, rg_w_a, rg_b_a, rg_w_x, rg_b_x, rg_lam, rg_w_out, sg_w_in, sg_norm_g, sg_w_s, sg_b_s, sg_w_out, sc_w_in, sc_conv_w, sc_w_out, ff_w1, ff_w3, ff_w2, moe_router, moe_router_b, moe_w1, moe_w3, moe_w2):
    batch, seq, d = x_prompt.shape
    dec_batch, dec_seq, _ = x_sample.shape
    depth = w_mod.shape[0]
    n_ctx = batch * seq
    d_rnn = rg_w_out.shape[1]
    n_exp = moe_router.shape[2]
    chunk = sg_w_s.shape[2]

    tm = dec_seq
    assert dec_seq % seq == 0 and n_ctx % tm == 0 and seq % SUBLANES == 0
    tm_half = max(tm // 2, chunk)
    tm_small = max(tm // 4, chunk)
    assert tm % tm_half == 0 and tm % tm_small == 0 and tm_small % chunk == 0
    moe_bm = tm_half

    n_cond = 1 + dec_batch
    cond = jnp.zeros((-(-n_cond // SUBLANES) * SUBLANES, d), F32)
    cond = cond.at[0].set(c_ctx).at[1:n_cond].set(c)
    mod = _adaln(cond, w_mod, b_mod)[:, :n_cond].reshape(depth, n_cond, 6, d)

    y, h = _embed_norm(x_prompt.reshape(n_ctx, d), x_sample.reshape(dec_batch * dec_seq, d),
                       _grid_pos_embed(dec_seq, d), norm_mix_g[0], mod, tm_small)

    states = []
    outs = None
    for i in range(depth):
        kind, j = i % 3, i // 3
        if h is None:
            h = _norm_mod(y, norm_mix_g[i], mod, i, 0, n_ctx, dec_seq, tm_half)
        if kind == 0:
            z = _mm_act(h, rg_w_in, j, None, F32, tm, tn_pref=1024)
            h0 = jnp.concatenate([jnp.zeros((n_ctx // tm, 2, d_rnn), F32),
                                  state_rglru[:, j].astype(F32)], axis=0)
            mix, st = _rg_core(z, rg_conv_w[j], rg_conv_b[j], rg_w_a[j], rg_b_a[j], rg_w_x[j], rg_b_x[j],
                               rg_lam[j], h0, n_ctx, seq, tm)
            nseg = tm // seq
            st = st[:n_ctx // tm].reshape(n_ctx // tm, 2, nseg, d_rnn)
            states.append(jnp.transpose(st, (0, 2, 1, 3)).reshape(batch, 2, d_rnn))
        elif kind == 1:
            z = _mm_act(h, sg_w_in, j, "gelu", BF16, tm, tn_pref=1024)
            mix = _sgu_core(z, sg_norm_g[j], sg_w_s[j], sg_b_s[j], tm_small)
        else:
            mix = _mm_sconv(h, sc_w_in, sc_conv_w, j, n_ctx, seq, tm)
        w_out = (rg_w_out, sg_w_out, sc_w_out)[kind]
        y = _mm_residual(mix, w_out, j, y, mod, i, 2, n_ctx, dec_seq, tm if mix.shape[1] <= 4096 else tm_half)

        f = i // 2
        last = i == depth - 1
        if i % 2 == 0:
            h = _norm_mod(y, norm_ffn_g[i], mod, i, 3, n_ctx, dec_seq, tm_half)
            g = _mm_swiglu(h, ff_w1, ff_w3, f, tm)
            y = _mm_residual(g, ff_w2, f, y, mod, i, 5, n_ctx, dec_seq, tm_half)
            h = None
        else:
            h_tok, ids, gates = _norm_router(y, norm_ffn_g[i], mod, i, 3, moe_router[f], moe_router_b[f],
                                             n_ctx, dec_seq, tm_half)
            dest, src, blk_exp, n_used = _moe_plan(ids, n_exp, moe_bm)
            xs = _gather_rows(h_tok, src, n_used, moe_bm, d)
            gs = _grouped_swiglu(xs, moe_w1, moe_w3, f, blk_exp, n_used, moe_bm)
            os_ = _grouped_down(gs, moe_w2, f, blk_exp, n_used, moe_bm)
            if last:
                outs = _combine(os_, dest, gates, y, mod, i, 5, final_norm_g, True, n_ctx, dec_seq, tm_small)
            else:
                y, h = _combine(os_, dest, gates, y, mod, i, 5, norm_mix_g[i + 1], False,
                                n_ctx, dec_seq, tm_small)

    y_p, y_s = outs if outs is not None else _final_norm(y, final_norm_g, n_ctx, tm_half)
    new_state = jnp.stack(states, axis=1).astype(x_prompt.dtype)
    return (y_p.reshape(batch, seq, d), y_s.reshape(dec_batch, dec_seq, d), new_state)
```

```python
import functools

import jax
import jax.numpy as jnp
from jax import lax
from jax.experimental import pallas as pl
from jax.experimental.pallas import tpu as pltpu

F32 = jnp.float32
BF16 = jnp.bfloat16

GRID_W = 64
EPS = 1e-6
RG_C = 8.0
TOP_K = 2

LANES = 128
SUBLANES = 8
VMEM_LIMIT_BYTES = 58 * 1024 * 1024
NEG_BIG = -1e30
MXU_ROWS_PER_PASS = 256


def _cparams(n_axes):
    return pltpu.CompilerParams(dimension_semantics=("arbitrary",) * n_axes,
                                vmem_limit_bytes=VMEM_LIMIT_BYTES)


def _pick_tile(n, pref):
    if n <= pref:
        return n
    t = (pref // LANES) * LANES
    while t > LANES and n % t:
        t -= LANES
    assert n % t == 0, (n, pref)
    return t


def _row_passes(rows):
    step = MXU_ROWS_PER_PASS if rows % MXU_ROWS_PER_PASS == 0 else rows
    return [pl.ds(r, step) for r in range(0, rows, step)]


def _sigmoid(x):
    return 1.0 / (1.0 + jnp.exp(-x))


def _gelu_tanh(x):
    c = 0.7978845608028654
    return 0.5 * x * (1.0 + jnp.tanh(c * (x + 0.044715 * (x * x * x))))


def _mod_row(i, tm, n_ctx, dec_seq):
    start = i * tm
    return jnp.where(start < n_ctx, 0, 1 + (start - n_ctx) // dec_seq)


def _adaln_kernel(c_ref, w_ref, b_ref, o_ref):
    c = c_ref[...]
    s = (c * _sigmoid(c)).astype(BF16)
    o_ref[...] = jnp.dot(s, w_ref[...].astype(BF16), preferred_element_type=F32) + b_ref[...]


def _adaln(cond, w_mod, b_mod):
    depth, d, n = w_mod.shape
    mc = cond.shape[0]
    tn = _pick_tile(n, 1024)
    return pl.pallas_call(
        _adaln_kernel,
        grid=(depth, n // tn),
        in_specs=[pl.BlockSpec((mc, d), lambda l, j: (0, 0)),
                  pl.BlockSpec((None, d, tn), lambda l, j: (l, 0, j)),
                  pl.BlockSpec((None, 1, tn), lambda l, j: (l, 0, j))],
        out_specs=pl.BlockSpec((None, mc, tn), lambda l, j: (l, 0, j)),
        out_shape=jax.ShapeDtypeStruct((depth, mc, n), F32),
        compiler_params=_cparams(2), name="adaln",
    )(cond, w_mod, b_mod.reshape(depth, 1, n))


def _embed_norm_kernel(xp_ref, xs_ref, pos_ref, g_ref, mod_ref, y_ref, h_ref, *, n_ctx_tiles):
    i = pl.program_id(0)

    @pl.when(i < n_ctx_tiles)
    def _():
        y_ref[...] = xp_ref[...]

    @pl.when(i >= n_ctx_tiles)
    def _():
        y_ref[...] = xs_ref[...] + pos_ref[...]

    h_ref[...] = _norm_mod_value(y_ref[...], g_ref[...], mod_ref, 0).astype(h_ref.dtype)


def _embed_norm(xp, xs, pos, g, mod, tm):
    n_ctx, d = xp.shape
    n_dec = xs.shape[0]
    dec_seq = pos.shape[0]
    nct = n_ctx // tm
    ppt = dec_seq // tm
    t = n_ctx + n_dec
    return pl.pallas_call(
        functools.partial(_embed_norm_kernel, n_ctx_tiles=nct),
        grid=(t // tm,),
        in_specs=[pl.BlockSpec((tm, d), lambda i: (jnp.minimum(i, nct - 1), 0)),
                  pl.BlockSpec((tm, d), lambda i: (jnp.maximum(i - nct, 0), 0)),
                  pl.BlockSpec((tm, d), lambda i: (jnp.maximum(i - nct, 0) % ppt, 0)),
                  pl.BlockSpec((1, d), lambda i: (0, 0)),
                  pl.BlockSpec((None, None, 6, d), lambda i: (0, _mod_row(i, tm, n_ctx, dec_seq), 0, 0))],
        out_specs=[pl.BlockSpec((tm, d), lambda i: (i, 0)),
                   pl.BlockSpec((tm, d), lambda i: (i, 0))],
        out_shape=[jax.ShapeDtypeStruct((t, d), F32), jax.ShapeDtypeStruct((t, d), BF16)],
        compiler_params=_cparams(1), name="embed_norm",
    )(xp, xs, pos, g.reshape(1, d), mod)


def _grid_pos_embed(length, d):
    rows = length // GRID_W
    r = jnp.repeat(jnp.arange(rows), GRID_W)
    col = jnp.tile(jnp.arange(GRID_W), rows)
    quarter = d // 4
    omega = 1.0 / (10000.0 ** (jnp.arange(quarter, dtype=F32) / quarter))

    def emb(p):
        ang = p[:, None].astype(F32) * omega[None, :]
        return jnp.concatenate([jnp.sin(ang), jnp.cos(ang)], axis=-1)

    return jnp.concatenate([emb(r), emb(col)], axis=-1).astype(F32)


def _norm_mod_value(y, g, mod_ref, shift_row):
    ms = jnp.mean(y * y, axis=-1, keepdims=True)
    n = (y * lax.rsqrt(ms + EPS)) * g
    return n * (1.0 + mod_ref[shift_row + 1:shift_row + 2, :]) + mod_ref[shift_row:shift_row + 1, :]


def _norm_mod_kernel(y_ref, g_ref, mod_ref, h_ref, *, shift_row):
    h_ref[...] = _norm_mod_value(y_ref[...], g_ref[...], mod_ref, shift_row).astype(h_ref.dtype)


def _norm_mod(y, g, mod, layer, shift_row, n_ctx, dec_seq, tm):
    t, d = y.shape
    return pl.pallas_call(
        functools.partial(_norm_mod_kernel, shift_row=shift_row),
        grid=(t // tm,),
        in_specs=[pl.BlockSpec((tm, d), lambda i: (i, 0)),
                  pl.BlockSpec((1, d), lambda i: (0, 0)),
                  pl.BlockSpec((None, None, 6, d),
                               lambda i: (layer, _mod_row(i, tm, n_ctx, dec_seq), 0, 0))],
        out_specs=pl.BlockSpec((tm, d), lambda i: (i, 0)),
        out_shape=jax.ShapeDtypeStruct((t, d), BF16),
        compiler_params=_cparams(1), name="norm_mod",
    )(y, g.reshape(1, d), mod)


def _norm_router_kernel(y_ref, g_ref, mod_ref, r_ref, rb_ref, h_ref, ids_ref, gates_ref, *,
                        shift_row, n_slab):
    h = _norm_mod_value(y_ref[...], g_ref[...], mod_ref, shift_row)
    tm = h.shape[0]
    for s in range(n_slab):
        h_ref[pl.ds(s, tm, stride=n_slab), :] = h[:, s * LANES:(s + 1) * LANES]
    h_hi = h.astype(BF16)
    h_lo = (h - h_hi.astype(F32)).astype(BF16)
    p_hi = jnp.dot(h_hi, r_ref[...], preferred_element_type=F32)
    p_lo = jnp.dot(h_lo, r_ref[:, :LANES], preferred_element_type=F32)
    logits = ((p_hi[:, :LANES] + p_hi[:, LANES:]) + p_lo) + rb_ref[...]
    lane = lax.broadcasted_iota(jnp.int32, logits.shape, 1).astype(F32)
    big = float(LANES)
    m1 = jnp.max(logits, axis=-1, keepdims=True)
    i1 = jnp.min(jnp.where(logits == m1, lane, big), axis=-1, keepdims=True)
    l2 = jnp.where(lane == i1, 2.0 * NEG_BIG, logits)
    m2 = jnp.max(l2, axis=-1, keepdims=True)
    i2 = jnp.min(jnp.where(l2 == m2, lane, big), axis=-1, keepdims=True)
    e = jnp.exp(m2 - m1)
    g1 = 1.0 / (1.0 + e)
    g2 = e / (1.0 + e)
    ids_ref[...] = jnp.where(lane == 0.0, i1, jnp.where(lane == 1.0, i2, 0.0)).astype(jnp.int32)
    gates_ref[...] = jnp.where(lane == 0.0, g1, jnp.where(lane == 1.0, g2, 0.0))


def _norm_router(y, g, mod, layer, shift_row, router, router_b, n_ctx, dec_seq, tm):
    t, d = y.shape
    n_exp = router.shape[1]
    n_slab = d // LANES
    rp = jnp.zeros((d, LANES), F32).at[:, :n_exp].set(router)
    r_hi = rp.astype(BF16)
    r_lo = (rp - r_hi.astype(F32)).astype(BF16)
    rp = jnp.concatenate([r_hi, r_lo], axis=1)
    rbp = jnp.full((1, LANES), NEG_BIG, F32).at[0, :n_exp].set(router_b)
    return pl.pallas_call(
        functools.partial(_norm_router_kernel, shift_row=shift_row, n_slab=n_slab),
        grid=(t // tm,),
        in_specs=[pl.BlockSpec((tm, d), lambda i: (i, 0)),
                  pl.BlockSpec((1, d), lambda i: (0, 0)),
                  pl.BlockSpec((None, None, 6, d),
                               lambda i: (layer, _mod_row(i, tm, n_ctx, dec_seq), 0, 0)),
                  pl.BlockSpec((d, 2 * LANES), lambda i: (0, 0)),
                  pl.BlockSpec((1, LANES), lambda i: (0, 0))],
        out_specs=[pl.BlockSpec((tm * n_slab, LANES), lambda i: (i, 0)),
                   pl.BlockSpec((tm, LANES), lambda i: (i, 0)),
                   pl.BlockSpec((tm, LANES), lambda i: (i, 0))],
        out_shape=[jax.ShapeDtypeStruct((t * n_slab, LANES), F32),
                   jax.ShapeDtypeStruct((t, LANES), jnp.int32),
                   jax.ShapeDtypeStruct((t, LANES), F32)],
        compiler_params=_cparams(1), name="norm_router",
    )(y, g.reshape(1, d), mod, rp, rbp)


def _final_norm_kernel(y_ref, g_ref, op_ref, os_ref, *, n_ctx_tiles):
    i = pl.program_id(0)
    y = y_ref[...]
    ms = jnp.mean(y * y, axis=-1, keepdims=True)
    n = (y * lax.rsqrt(ms + EPS)) * g_ref[...]

    @pl.when(i < n_ctx_tiles)
    def _():
        op_ref[...] = n

    @pl.when(i >= n_ctx_tiles)
    def _():
        os_ref[...] = n


def _final_norm(y, g, n_ctx, tm):
    t, d = y.shape
    nct = n_ctx // tm
    return pl.pallas_call(
        functools.partial(_final_norm_kernel, n_ctx_tiles=nct),
        grid=(t // tm,),
        in_specs=[pl.BlockSpec((tm, d), lambda i: (i, 0)),
                  pl.BlockSpec((1, d), lambda i: (0, 0))],
        out_specs=[pl.BlockSpec((tm, d), lambda i: (jnp.minimum(i, nct - 1), 0)),
                   pl.BlockSpec((tm, d), lambda i: (jnp.maximum(i - nct, 0), 0))],
        out_shape=[jax.ShapeDtypeStruct((n_ctx, d), F32),
                   jax.ShapeDtypeStruct((t - n_ctx, d), F32)],
        compiler_params=_cparams(1), name="final_norm",
    )(y, g.reshape(1, d))


def _cast_at_first_row_tile(w_refs, wb_refs):
    @pl.when(pl.program_id(1) == 0)
    def _():
        for w, wb in zip(w_refs, wb_refs):
            wb[...] = w[...].astype(BF16)


def _mm_act_kernel(x_ref, w_ref, o_ref, wb, *, act):
    _cast_at_first_row_tile((w_ref,), (wb,))
    for rows in _row_passes(x_ref.shape[0]):
        acc = jnp.dot(x_ref[rows, :], wb[...], preferred_element_type=F32)
        if act == "gelu":
            acc = _gelu_tanh(acc)
        o_ref[rows, :] = acc.astype(o_ref.dtype)


def _mm_act(x, w, layer, act, out_dtype, tm, tn_pref=512):
    t, k = x.shape
    n = w.shape[2]
    tn = _pick_tile(n, tn_pref)
    return pl.pallas_call(
        functools.partial(_mm_act_kernel, act=act),
        grid=(n // tn, t // tm),
        in_specs=[pl.BlockSpec((tm, k), lambda j, i: (i, 0)),
                  pl.BlockSpec((None, k, tn), lambda j, i: (layer, 0, j))],
        out_specs=pl.BlockSpec((tm, tn), lambda j, i: (i, j)),
        out_shape=jax.ShapeDtypeStruct((t, n), out_dtype),
        scratch_shapes=[pltpu.VMEM((k, tn), BF16)],
        compiler_params=_cparams(2), name="mm_" + str(act),
    )(x, w)


def _swiglu_passes(x_ref, wb1, wb3, o_ref):
    for rows in _row_passes(x_ref.shape[0]):
        x = x_ref[rows, :]
        a = jnp.dot(x, wb1[...], preferred_element_type=F32)
        b = jnp.dot(x, wb3[...], preferred_element_type=F32)
        o_ref[rows, :] = ((a * _sigmoid(a)) * b).astype(o_ref.dtype)


def _mm_swiglu_kernel(x_ref, w1_ref, w3_ref, o_ref, wb1, wb3):
    _cast_at_first_row_tile((w1_ref, w3_ref), (wb1, wb3))
    _swiglu_passes(x_ref, wb1, wb3, o_ref)


def _mm_swiglu(x, w1, w3, layer, tm, tn_pref=512):
    t, k = x.shape
    n = w1.shape[2]
    tn = _pick_tile(n, tn_pref)
    return pl.pallas_call(
        _mm_swiglu_kernel,
        grid=(n // tn, t // tm),
        in_specs=[pl.BlockSpec((tm, k), lambda j, i: (i, 0)),
                  pl.BlockSpec((None, k, tn), lambda j, i: (layer, 0, j)),
                  pl.BlockSpec((None, k, tn), lambda j, i: (layer, 0, j))],
        out_specs=pl.BlockSpec((tm, tn), lambda j, i: (i, j)),
        out_shape=jax.ShapeDtypeStruct((t, n), BF16),
        scratch_shapes=[pltpu.VMEM((k, tn), BF16), pltpu.VMEM((k, tn), BF16)],
        compiler_params=_cparams(2), name="mm_swiglu",
    )(x, w1, w3)


def _mm_sconv_kernel(x_ref, wb_ref, wc_ref, wx_ref, cw_ref, o_ref, sb, sc, sx, *, n_ctx_tiles, seq_ctx):
    _cast_at_first_row_tile((wb_ref, wc_ref, wx_ref), (sb, sc, sx))
    i = pl.program_id(1)
    is_ctx = i < n_ctx_tiles
    x = x_ref[...]
    bg = jnp.dot(x, sb[...], preferred_element_type=F32)
    p = jnp.dot(x, sc[...], preferred_element_type=F32) * jnp.dot(x, sx[...], preferred_element_type=F32)
    tm = p.shape[0]
    row = lax.broadcasted_iota(jnp.int32, p.shape, 0)
    pos = jnp.where(is_ctx, row % seq_ctx, row)
    last_pos = jnp.where(is_ctx, seq_ctx - 1, tm - 1)
    p_prev = jnp.where(pos == 0, 0.0, pltpu.roll(p, 1, axis=0))
    p_next = jnp.where(pos == last_pos, 0.0, pltpu.roll(p, tm - 1, axis=0))
    cw = cw_ref[...]
    conv = (cw[0:1] * p_prev + cw[1:2] * p) + cw[2:3] * p_next
    o_ref[...] = (bg * conv).astype(o_ref.dtype)


def _mm_sconv(x, w_in, conv_w, layer, n_ctx, seq_ctx, tm, tn_pref=512):
    t, k = x.shape
    d = conv_w.shape[2]
    tn = _pick_tile(d, tn_pref)
    nj = d // tn
    return pl.pallas_call(
        functools.partial(_mm_sconv_kernel, n_ctx_tiles=n_ctx // tm, seq_ctx=seq_ctx),
        grid=(nj, t // tm),
        in_specs=[pl.BlockSpec((tm, k), lambda j, i: (i, 0)),
                  pl.BlockSpec((None, k, tn), lambda j, i: (layer, 0, j)),
                  pl.BlockSpec((None, k, tn), lambda j, i: (layer, 0, nj + j)),
                  pl.BlockSpec((None, k, tn), lambda j, i: (layer, 0, 2 * nj + j)),
                  pl.BlockSpec((None, conv_w.shape[1], tn), lambda j, i: (layer, 0, j))],
        out_specs=pl.BlockSpec((tm, tn), lambda j, i: (i, j)),
        out_shape=jax.ShapeDtypeStruct((t, d), BF16),
        scratch_shapes=[pltpu.VMEM((k, tn), BF16)] * 3,
        compiler_params=_cparams(2), name="mm_sconv",
    )(x, w_in, w_in, w_in, conv_w)


def _mm_residual_kernel(x_ref, w_ref, y_ref, mod_ref, o_ref, wb, *, gate_row):
    _cast_at_first_row_tile((w_ref,), (wb,))
    gate = mod_ref[gate_row:gate_row + 1, :]
    for rows in _row_passes(x_ref.shape[0]):
        acc = jnp.dot(x_ref[rows, :], wb[...], preferred_element_type=F32)
        o_ref[rows, :] = y_ref[rows, :] + gate * acc


def _mm_residual(x, w, w_layer, y, mod, layer, gate_row, n_ctx, dec_seq, tm, tn_pref=512):
    t, k = x.shape
    n = w.shape[2]
    tn = _pick_tile(n, tn_pref)
    return pl.pallas_call(
        functools.partial(_mm_residual_kernel, gate_row=gate_row),
        grid=(n // tn, t // tm),
        in_specs=[pl.BlockSpec((tm, k), lambda j, i: (i, 0)),
                  pl.BlockSpec((None, k, tn), lambda j, i: (w_layer, 0, j)),
                  pl.BlockSpec((tm, tn), lambda j, i: (i, j)),
                  pl.BlockSpec((None, None, 6, tn),
                               lambda j, i: (layer, _mod_row(i, tm, n_ctx, dec_seq), 0, j))],
        out_specs=pl.BlockSpec((tm, tn), lambda j, i: (i, j)),
        out_shape=jax.ShapeDtypeStruct((t, n), F32),
        scratch_shapes=[pltpu.VMEM((k, tn), BF16)],
        compiler_params=_cparams(2), name="mm_residual",
    )(x, w, y, mod)


def _rg_core_kernel(gate_ref, x_ref, cw_ref, cb_ref, wg_ref, bg_ref, lam_ref, h0_ref,
                    y_ref, st_ref, af_s, bf_s, ab_s, bb_s, *, n_ctx_tiles, seg, tm, cg):
    i = pl.program_id(1)
    is_ctx = i < n_ctx_tiles
    nseg = tm // seg
    nslab = cg // LANES
    lsub = tm // SUBLANES
    sub_per_seq = seg // lsub
    cw = cw_ref[...]
    cb = cb_ref[...]
    lam = lam_ref[...]
    softplus_neg_lam = jnp.maximum(-lam, 0.0) + jnp.log1p(jnp.exp(-jnp.abs(lam)))
    neg_c_sp = (-RG_C) * softplus_neg_lam
    zeros8 = jnp.zeros((SUBLANES, cg), F32)

    for s in range(nseg):
        c0 = s * seg
        cur = x_ref[pl.ds(c0, seg), :]
        prev8 = zeros8 if s == 0 else jnp.where(is_ctx, 0.0, x_ref[pl.ds(c0 - SUBLANES, SUBLANES), :])
        next8 = zeros8 if s == nseg - 1 else jnp.where(is_ctx, 0.0, x_ref[pl.ds(c0 + seg, SUBLANES), :])
        head = jnp.concatenate([prev8, cur[:2 * SUBLANES]], axis=0)
        tail = jnp.concatenate([cur[seg - 2 * SUBLANES:], next8], axis=0)

        def shifted(delta):
            lo = SUBLANES + delta
            mid = x_ref[pl.ds(c0 + lo, seg - 2 * SUBLANES), :]
            return jnp.concatenate([head[lo:lo + SUBLANES], mid, tail[lo:lo + SUBLANES]], axis=0)

        xc = (((cw[0:1] * shifted(-2) + cw[1:2] * shifted(-1)) + cw[2:3] * cur)
              + cw[3:4] * shifted(1)) + cb
        g = jnp.dot(xc.astype(BF16), wg_ref[...], preferred_element_type=F32) + bg_ref[...]
        for d, (a_s, b_s) in enumerate(((af_s, bf_s), (ab_s, bb_s))):
            r = _sigmoid(g[:, (2 * d) * cg:(2 * d + 1) * cg])
            ig = _sigmoid(g[:, (2 * d + 1) * cg:(2 * d + 2) * cg])
            log_a = neg_c_sp[d:d + 1] * r
            a = jnp.exp(log_a)
            one_minus_a2 = -jnp.tanh(log_a) * (a * a + 1.0)
            root = jnp.where(one_minus_a2 > 0.0, one_minus_a2 * lax.rsqrt(one_minus_a2), 0.0)
            bt = root * (ig * xc)
            for kk in range(sub_per_seq):
                k = s * sub_per_seq + kk
                dst = pl.ds(k, lsub, stride=SUBLANES)
                for l in range(nslab):
                    a_s[l, dst, :] = a[kk * lsub:(kk + 1) * lsub, l * LANES:(l + 1) * LANES]
                    b_s[l, dst, :] = bt[kk * lsub:(kk + 1) * lsub, l * LANES:(l + 1) * LANES]

    def local_scan(j, carry):
        hf, pf, hb, pb = carry
        rf = pl.multiple_of(j * SUBLANES, SUBLANES)
        rb = pl.multiple_of((lsub - 1 - j) * SUBLANES, SUBLANES)
        nhf, npf, nhb, npb = [], [], [], []
        for l in range(nslab):
            a = af_s[l, pl.ds(rf, SUBLANES), :]
            h = a * hf[l] + bf_s[l, pl.ds(rf, SUBLANES), :]
            p = a * pf[l]
            bf_s[l, pl.ds(rf, SUBLANES), :] = h
            af_s[l, pl.ds(rf, SUBLANES), :] = p
            nhf.append(h)
            npf.append(p)
            a = ab_s[l, pl.ds(rb, SUBLANES), :]
            h = a * hb[l] + bb_s[l, pl.ds(rb, SUBLANES), :]
            p = a * pb[l]
            bb_s[l, pl.ds(rb, SUBLANES), :] = h
            ab_s[l, pl.ds(rb, SUBLANES), :] = p
            nhb.append(h)
            npb.append(p)
        return tuple(nhf), tuple(npf), tuple(nhb), tuple(npb)

    zero = tuple(jnp.zeros((SUBLANES, LANES), F32) for _ in range(nslab))
    one = tuple(jnp.ones((SUBLANES, LANES), F32) for _ in range(nslab))
    hf_end, pf_end, hb_end, pb_end = lax.fori_loop(0, lsub, local_scan, (zero, one, zero, one))

    row8 = lax.broadcasted_iota(jnp.int32, (SUBLANES, LANES), 0)
    h0 = h0_ref[...]
    for l in range(nslab):
        lanes = slice(l * LANES, (l + 1) * LANES)
        ent_f = jnp.zeros((SUBLANES, LANES), F32)
        h_in = h0[0:1, lanes]
        for k in range(SUBLANES):
            if k > 0 and k % sub_per_seq == 0:
                h_in = jnp.where(is_ctx, 0.0, h_in)
            ent_f = jnp.where(row8 == k, h_in, ent_f)
            h_in = hf_end[l][k:k + 1] + pf_end[l][k:k + 1] * h_in
            if (k + 1) % sub_per_seq == 0:
                q = k // sub_per_seq
                st_ref[q:q + 1, lanes] = h_in
        ent_b = jnp.zeros((SUBLANES, LANES), F32)
        h_in = h0[1:2, lanes]
        for k in reversed(range(SUBLANES)):
            if k < SUBLANES - 1 and (k + 1) % sub_per_seq == 0:
                h_in = jnp.where(is_ctx, 0.0, h_in)
            ent_b = jnp.where(row8 == k, h_in, ent_b)
            h_in = hb_end[l][k:k + 1] + pb_end[l][k:k + 1] * h_in
            if k % sub_per_seq == 0:
                q = k // sub_per_seq
                st_ref[nseg + q:nseg + q + 1, lanes] = h_in
        ch = min(tm, 256)
        ef = jnp.concatenate([ent_f] * (ch // SUBLANES), axis=0)
        eb = jnp.concatenate([ent_b] * (ch // SUBLANES), axis=0)
        for r0 in range(0, tm, ch):
            rows = pl.ds(r0, ch)
            bf_s[l, rows, :] = ((bf_s[l, rows, :] + af_s[l, rows, :] * ef)
                                + (bb_s[l, rows, :] + ab_s[l, rows, :] * eb))

    for k in range(SUBLANES):
        rows = pl.ds(k * lsub, lsub)
        hsum = jnp.concatenate([bf_s[l, pl.ds(k, lsub, stride=SUBLANES), :] for l in range(nslab)], axis=1)
        y_ref[rows, :] = (hsum * _gelu_tanh(gate_ref[rows, :])).astype(y_ref.dtype)


def _rg_core(z, conv_w, conv_b, w_a, b_a, w_x, b_x, lam, h0, n_ctx, seq_ctx, tm):
    t = z.shape[0]
    r = conv_w.shape[1]
    heads, hw = w_a.shape[1], w_a.shape[2]
    hpg = 4
    while (hpg * hw) % LANES:
        hpg *= 2
    ng = heads // hpg
    cg = hpg * hw
    nseg = tm // seq_ctx
    ntiles = t // tm
    assert tm % SUBLANES == 0 and seq_ctx % (tm // SUBLANES) == 0

    def blockdiag(w):
        w5 = w.reshape(2, ng, hpg, hw, hw)
        eye = jnp.eye(hpg, dtype=w.dtype)
        return jnp.einsum("dghio,hk->dghiko", w5, eye).reshape(2, ng, cg, cg)

    wa, wx = blockdiag(w_a), blockdiag(w_x)
    wg = jnp.concatenate([wa[0], wx[0], wa[1], wx[1]], axis=-1).astype(BF16)
    ba, bx = b_a.reshape(2, ng, 1, cg), b_x.reshape(2, ng, 1, cg)
    bg = jnp.concatenate([ba[0], bx[0], ba[1], bx[1]], axis=-1)

    return pl.pallas_call(
        functools.partial(_rg_core_kernel, n_ctx_tiles=n_ctx // tm, seg=seq_ctx, tm=tm, cg=cg),
        grid=(ng, ntiles),
        in_specs=[pl.BlockSpec((tm, cg), lambda g, i: (i, g)),
                  pl.BlockSpec((tm, cg), lambda g, i: (i, ng + g)),
                  pl.BlockSpec((conv_w.shape[0], cg), lambda g, i: (0, g)),
                  pl.BlockSpec((1, cg), lambda g, i: (0, g)),
                  pl.BlockSpec((None, cg, 4 * cg), lambda g, i: (g, 0, 0)),
                  pl.BlockSpec((None, 1, 4 * cg), lambda g, i: (g, 0, 0)),
                  pl.BlockSpec((2, cg), lambda g, i: (0, g)),
                  pl.BlockSpec((None, 2, cg), lambda g, i: (i, 0, g))],
        out_specs=[pl.BlockSpec((tm, cg), lambda g, i: (i, g)),
                   pl.BlockSpec((None, 2 * nseg, cg), lambda g, i: (i, 0, g))],
        out_shape=[jax.ShapeDtypeStruct((t, r), BF16),
                   jax.ShapeDtypeStruct((ntiles, 2 * nseg, r), F32)],
        scratch_shapes=[pltpu.VMEM((cg // LANES, tm, LANES), F32)] * 4,
        compiler_params=_cparams(2), name="rg_core",
    )(z, z, conv_w, conv_b.reshape(1, r), wg, bg, lam, h0)


def _sgu_core_kernel(u_ref, v_ref, g_ref, ws_ref, bs_ref, y_ref, vn_s, *, chunk, gw, ngroups, tm):
    v = v_ref[...].astype(F32)
    ms = jnp.mean(v * v, axis=-1, keepdims=True)
    vn_s[...] = ((v * lax.rsqrt(ms + EPS)) * g_ref[...]).astype(BF16)
    for c in range(tm // chunk):
        rows = pl.ds(c * chunk, chunk)
        for g in range(ngroups):
            cols = pl.ds(g * gw, gw)
            vm = jnp.dot(ws_ref[g], vn_s[rows, cols], preferred_element_type=F32) + bs_ref[:, g:g + 1]
            y_ref[rows, cols] = (u_ref[rows, cols].astype(F32) * vm).astype(y_ref.dtype)


def _sgu_core(z, norm_g, w_s, b_s, tm):
    t = z.shape[0]
    w = norm_g.shape[0]
    ngroups, chunk = w_s.shape[0], w_s.shape[1]
    gw = w // ngroups
    return pl.pallas_call(
        functools.partial(_sgu_core_kernel, chunk=chunk, gw=gw, ngroups=ngroups, tm=tm),
        grid=(t // tm,),
        in_specs=[pl.BlockSpec((tm, w), lambda i: (i, 0)),
                  pl.BlockSpec((tm, w), lambda i: (i, 1)),
                  pl.BlockSpec((1, w), lambda i: (0, 0)),
                  pl.BlockSpec((ngroups, chunk, chunk), lambda i: (0, 0, 0)),
                  pl.BlockSpec((chunk, ngroups), lambda i: (0, 0))],
        out_specs=pl.BlockSpec((tm, w), lambda i: (i, 0)),
        out_shape=jax.ShapeDtypeStruct((t, w), BF16),
        scratch_shapes=[pltpu.VMEM((tm, w), BF16)],
        compiler_params=_cparams(1), name="sgu_core",
    )(z, z, norm_g.reshape(1, w), w_s.astype(BF16), b_s.T)


def _moe_plan(ids, n_exp, bm):
    t = ids.shape[0]
    e = ids[:, :TOP_K].reshape(-1)
    oh = (e[:, None] == jnp.arange(n_exp, dtype=jnp.int32)[None, :]).astype(jnp.int32)
    csum = jnp.cumsum(oh, axis=0)
    rank = jnp.sum((csum - oh) * oh, axis=1)
    counts = csum[-1]
    padded = ((counts + bm - 1) // bm) * bm
    ends = jnp.cumsum(padded)
    starts = ends - padded
    dest = (jnp.sum(starts[None, :] * oh, axis=1) + rank).astype(jnp.int32)
    p = TOP_K * t + n_exp * bm
    src = jnp.zeros((p,), jnp.int32).at[dest].set(jnp.arange(TOP_K * t, dtype=jnp.int32) // TOP_K)

    def block_table(blk):
        idx = jnp.arange(p // blk, dtype=jnp.int32)
        start = idx * blk
        exp = jnp.minimum(jnp.sum((start[:, None] >= ends[None, :]).astype(jnp.int32), axis=1), n_exp - 1)
        valid = jnp.clip(starts[exp] + counts[exp] - start, 0, blk)
        fetch = lax.cummax(jnp.where(valid > 0, idx, 0), axis=0)
        return exp.astype(jnp.int32), valid.astype(jnp.int32), fetch.astype(jnp.int32)

    return dest, src, block_table


def _gather_rows_kernel(src_ref, valid_ref, h_hbm, o_ref, buf, sem, *, rows, n_slab):
    b = pl.program_id(0)
    nb = pl.num_programs(0)
    slot = b % 2
    has_rows = valid_ref[b] > 0
    next_has_rows = jnp.logical_and(b + 1 < nb, valid_ref[jnp.minimum(b + 1, nb - 1)] > 0)

    def issue(blk, to_slot):
        base = blk * rows

        def body(q, carry):
            for prio in range(2):
                r = 2 * q + prio
                tok = pl.multiple_of(src_ref[base + r] * n_slab, n_slab)
                pltpu.make_async_copy(h_hbm.at[pl.ds(tok, n_slab), :],
                                      buf.at[to_slot, pl.ds(pl.multiple_of(r * n_slab, n_slab), n_slab), :],
                                      sem.at[to_slot]).start(priority=prio)
            return carry

        lax.fori_loop(0, rows // 2, body, 0, unroll=4)

    @pl.when(jnp.logical_and(b == 0, has_rows))
    def _():
        issue(0, 0)

    @pl.when(next_has_rows)
    def _():
        issue(b + 1, 1 - slot)

    @pl.when(has_rows)
    def _():
        pltpu.make_async_copy(h_hbm.at[pl.ds(0, rows * n_slab), :], buf.at[slot], sem.at[slot]).wait()
        for s in range(n_slab):
            o_ref[:, s * LANES:(s + 1) * LANES] = buf[slot, pl.ds(s, rows, stride=n_slab), :].astype(o_ref.dtype)

    @pl.when(jnp.logical_not(has_rows))
    def _():
        o_ref[...] = jnp.zeros_like(o_ref)


def _gather_rows(h_tok, src, valid, bm, d):
    n_slab = d // LANES
    p = src.shape[0]
    return pl.pallas_call(
        functools.partial(_gather_rows_kernel, rows=bm, n_slab=n_slab),
        grid_spec=pltpu.PrefetchScalarGridSpec(
            num_scalar_prefetch=2,
            grid=(p // bm,),
            in_specs=[pl.BlockSpec(memory_space=pl.ANY)],
            out_specs=pl.BlockSpec((bm, d), lambda b, src, va: (b, 0)),
            scratch_shapes=[pltpu.VMEM((2, bm * n_slab, LANES), F32), pltpu.SemaphoreType.DMA((2,))]),
        out_shape=jax.ShapeDtypeStruct((p, d), BF16),
        compiler_params=_cparams(1), name="moe_gather",
    )(src, valid, h_tok)


def _expert_changed(exp_ref, fetch_ref, b):
    prev = fetch_ref[jnp.maximum(b - 1, 0)]
    return jnp.logical_or(b == 0, exp_ref[b] != exp_ref[prev])


def _guarded_passes(valid, x_ref, o_ref, one_pass):
    bm = x_ref.shape[0]
    passes = _row_passes(bm)

    @pl.when(valid == bm)
    def _():
        for rows in passes:
            one_pass(rows)

    @pl.when(valid < bm)
    def _():
        for n, rows in enumerate(passes):
            first_row = n * (bm // len(passes))

            @pl.when(first_row < valid)
            def _():
                one_pass(rows)

            @pl.when(first_row >= valid)
            def _():
                o_ref[rows, :] = jnp.zeros((bm // len(passes), o_ref.shape[1]), o_ref.dtype)


def _grouped_swiglu_kernel(exp_ref, valid_ref, fetch_ref, x_ref, w1_ref, w3_ref, o_ref, wb1, wb3):
    b = pl.program_id(1)
    valid = valid_ref[b]

    @pl.when(jnp.logical_and(valid > 0, _expert_changed(exp_ref, fetch_ref, b)))
    def _():
        wb1[...] = w1_ref[...].astype(BF16)
        wb3[...] = w3_ref[...].astype(BF16)

    def one_pass(rows):
        x = x_ref[rows, :]
        a = jnp.dot(x, wb1[...], preferred_element_type=F32)
        c = jnp.dot(x, wb3[...], preferred_element_type=F32)
        o_ref[rows, :] = ((a * _sigmoid(a)) * c).astype(o_ref.dtype)

    _guarded_passes(valid, x_ref, o_ref, one_pass)


def _grouped_swiglu(xs, w1, w3, layer, table, bm, tn_pref=512):
    p, k = xs.shape
    n = w1.shape[3]
    tn = _pick_tile(n, tn_pref)
    return pl.pallas_call(
        _grouped_swiglu_kernel,
        grid_spec=pltpu.PrefetchScalarGridSpec(
            num_scalar_prefetch=3,
            grid=(n // tn, p // bm),
            in_specs=[pl.BlockSpec((bm, k), lambda j, b, ex, va, fe: (fe[b], 0)),
                      pl.BlockSpec((None, None, k, tn), lambda j, b, ex, va, fe: (layer, ex[fe[b]], 0, j)),
                      pl.BlockSpec((None, None, k, tn), lambda j, b, ex, va, fe: (layer, ex[fe[b]], 0, j))],
            out_specs=pl.BlockSpec((bm, tn), lambda j, b, ex, va, fe: (b, j)),
            scratch_shapes=[pltpu.VMEM((k, tn), BF16), pltpu.VMEM((k, tn), BF16)]),
        out_shape=jax.ShapeDtypeStruct((p, n), BF16),
        compiler_params=_cparams(2), name="moe_swiglu",
    )(*table, xs, w1, w3)


def _grouped_down_kernel(exp_ref, valid_ref, fetch_ref, x_ref, w_ref, o_ref, wb):
    b = pl.program_id(1)
    valid = valid_ref[b]

    @pl.when(jnp.logical_and(valid > 0, _expert_changed(exp_ref, fetch_ref, b)))
    def _():
        wb[...] = w_ref[...].astype(BF16)

    def one_pass(rows):
        o_ref[rows, :] = jnp.dot(x_ref[rows, :], wb[...], preferred_element_type=F32)

    _guarded_passes(valid, x_ref, o_ref, one_pass)


def _grouped_down(gs, w2, layer, table, bm, tn_pref=512):
    p, k = gs.shape
    n = w2.shape[3]
    tn = _pick_tile(n, tn_pref)
    return pl.pallas_call(
        _grouped_down_kernel,
        grid_spec=pltpu.PrefetchScalarGridSpec(
            num_scalar_prefetch=3,
            grid=(n // tn, p // bm),
            in_specs=[pl.BlockSpec((bm, k), lambda j, b, ex, va, fe: (fe[b], 0)),
                      pl.BlockSpec((None, None, k, tn), lambda j, b, ex, va, fe: (layer, ex[fe[b]], 0, j))],
            out_specs=pl.BlockSpec((bm, tn), lambda j, b, ex, va, fe: (b, j)),
            scratch_shapes=[pltpu.VMEM((k, tn), BF16)]),
        out_shape=jax.ShapeDtypeStruct((p, n), F32),
        compiler_params=_cparams(2), name="moe_down",
    )(*table, gs, w2)


def _combine_kernel(dest_ref, os_hbm, y_ref, gates_ref, mod_ref, g_ref, *rest,
                    rows, gate_row, n_ctx_tiles, final):
    if final:
        out_a, out_b, buf, sem = rest
    else:
        nmod_ref, out_a, out_b, buf, sem = rest
    i = pl.program_id(0)
    slot = i % 2

    def issue(tile, to_slot):
        base = tile * rows

        def body(r, carry):
            for k in range(TOP_K):
                row = dest_ref[TOP_K * (base + r) + k]
                pltpu.make_async_copy(os_hbm.at[pl.ds(row, 1), :], buf.at[to_slot, k, pl.ds(r, 1), :],
                                      sem.at[to_slot]).start(priority=k)
            return carry

        lax.fori_loop(0, rows, body, 0, unroll=4)

    @pl.when(i == 0)
    def _():
        issue(0, 0)

    @pl.when(i + 1 < pl.num_programs(0))
    def _():
        issue(i + 1, 1 - slot)

    for k in range(TOP_K):
        pltpu.make_async_copy(os_hbm.at[pl.ds(0, rows), :], buf.at[slot, k], sem.at[slot]).wait()
    gates = gates_ref[...]
    f = gates[:, 0:1] * buf[slot, 0] + gates[:, 1:2] * buf[slot, 1]
    y = y_ref[...] + mod_ref[gate_row:gate_row + 1, :] * f
    if final:
        ms = jnp.mean(y * y, axis=-1, keepdims=True)
        n = (y * lax.rsqrt(ms + EPS)) * g_ref[...]

        @pl.when(i < n_ctx_tiles)
        def _():
            out_a[...] = n

        @pl.when(i >= n_ctx_tiles)
        def _():
            out_b[...] = n
    else:
        out_a[...] = y
        out_b[...] = _norm_mod_value(y, g_ref[...], nmod_ref, 0).astype(out_b.dtype)


def _combine(os_, dest, gates, y, mod, layer, gate_row, next_g, final, n_ctx, dec_seq, tm):
    t, d = y.shape
    nct = n_ctx // tm

    def mod_spec(which):
        return pl.BlockSpec((None, None, 6, d),
                            lambda i, dst: (which, _mod_row(i, tm, n_ctx, dec_seq), 0, 0))

    in_specs = [pl.BlockSpec(memory_space=pl.ANY),
                pl.BlockSpec((tm, d), lambda i, dst: (i, 0)),
                pl.BlockSpec((tm, LANES), lambda i, dst: (i, 0)),
                mod_spec(layer),
                pl.BlockSpec((1, d), lambda i, dst: (0, 0))]
    args = [dest, os_, y, gates, mod, next_g.reshape(1, d)]
    if final:
        out_specs = [pl.BlockSpec((tm, d), lambda i, dst: (jnp.minimum(i, nct - 1), 0)),
                     pl.BlockSpec((tm, d), lambda i, dst: (jnp.maximum(i - nct, 0), 0))]
        out_shape = [jax.ShapeDtypeStruct((n_ctx, d), F32), jax.ShapeDtypeStruct((t - n_ctx, d), F32)]
    else:
        in_specs.append(mod_spec(layer + 1))
        args.append(mod)
        out_specs = [pl.BlockSpec((tm, d), lambda i, dst: (i, 0)),
                     pl.BlockSpec((tm, d), lambda i, dst: (i, 0))]
        out_shape = [jax.ShapeDtypeStruct((t, d), F32), jax.ShapeDtypeStruct((t, d), BF16)]
    return pl.pallas_call(
        functools.partial(_combine_kernel, rows=tm, gate_row=gate_row, n_ctx_tiles=nct, final=final),
        grid_spec=pltpu.PrefetchScalarGridSpec(
            num_scalar_prefetch=1,
            grid=(t // tm,),
            in_specs=in_specs,
            out_specs=out_specs,
            scratch_shapes=[pltpu.VMEM((2, TOP_K, tm, d), F32), pltpu.SemaphoreType.DMA((2,))]),
        out_shape=out_shape,
        compiler_params=_cparams(1), name="moe_combine_final" if final else "moe_combine",
    )(*args)


def kernel(x_prompt, x_sample, state_rglru, c, c_ctx, norm_mix_g, norm_ffn_g, w_mod, b_mod, final_norm_g, rg_w_in, rg_conv_w, rg_conv_b, rg_w_a, rg_b_a, rg_w_x, rg_b_x, rg_lam, rg_w_out, sg_w_in, sg_norm_g, sg_w_s, sg_b_s, sg_w_out, sc_w_in, sc_conv_w, sc_w_out, ff_w1, ff_w3, ff_w2, moe_router, moe_router_b, moe_w1, moe_w3, moe_w2):
    batch, seq, d = x_prompt.shape
    dec_batch, dec_seq, _ = x_sample.shape
    depth = w_mod.shape[0]
    n_ctx = batch * seq
    d_rnn = rg_w_out.shape[1]
    n_exp = moe_router.shape[2]
    chunk = sg_w_s.shape[2]

    tm = dec_seq
    assert dec_seq % seq == 0 and n_ctx % tm == 0 and seq % SUBLANES == 0
    tm_half = max(tm // 2, chunk)
    tm_small = max(tm // 4, chunk)
    assert tm % tm_half == 0 and tm % tm_small == 0 and tm_small % chunk == 0
    moe_bm, moe_sub = tm, tm_half

    n_cond = 1 + dec_batch
    cond = jnp.zeros((-(-n_cond // SUBLANES) * SUBLANES, d), F32)
    cond = cond.at[0].set(c_ctx).at[1:n_cond].set(c)
    mod = _adaln(cond, w_mod, b_mod)[:, :n_cond].reshape(depth, n_cond, 6, d)

    y, h = _embed_norm(x_prompt.reshape(n_ctx, d), x_sample.reshape(dec_batch * dec_seq, d),
                       _grid_pos_embed(dec_seq, d), norm_mix_g[0], mod, tm_small)

    states = []
    outs = None
    for i in range(depth):
        kind, j = i % 3, i // 3
        if h is None:
            h = _norm_mod(y, norm_mix_g[i], mod, i, 0, n_ctx, dec_seq, tm_half)
        if kind == 0:
            z = _mm_act(h, rg_w_in, j, None, F32, tm, tn_pref=1024)
            h0 = jnp.concatenate([jnp.zeros((n_ctx // tm, 2, d_rnn), F32),
                                  state_rglru[:, j].astype(F32)], axis=0)
            mix, st = _rg_core(z, rg_conv_w[j], rg_conv_b[j], rg_w_a[j], rg_b_a[j], rg_w_x[j], rg_b_x[j],
                               rg_lam[j], h0, n_ctx, seq, tm)
            nseg = tm // seq
            st = st[:n_ctx // tm].reshape(n_ctx // tm, 2, nseg, d_rnn)
            states.append(jnp.transpose(st, (0, 2, 1, 3)).reshape(batch, 2, d_rnn))
        elif kind == 1:
            z = _mm_act(h, sg_w_in, j, "gelu", BF16, tm, tn_pref=1024)
            mix = _sgu_core(z, sg_norm_g[j], sg_w_s[j], sg_b_s[j], tm_small)
        else:
            mix = _mm_sconv(h, sc_w_in, sc_conv_w, j, n_ctx, seq, tm)
        w_out = (rg_w_out, sg_w_out, sc_w_out)[kind]
        y = _mm_residual(mix, w_out, j, y, mod, i, 2, n_ctx, dec_seq, tm if mix.shape[1] <= 4096 else tm_half)

        f = i // 2
        last = i == depth - 1
        if i % 2 == 0:
            h = _norm_mod(y, norm_ffn_g[i], mod, i, 3, n_ctx, dec_seq, tm_half)
            g = _mm_swiglu(h, ff_w1, ff_w3, f, tm)
            y = _mm_residual(g, ff_w2, f, y, mod, i, 5, n_ctx, dec_seq, tm_half)
            h = None
        else:
            h_tok, ids, gates = _norm_router(y, norm_ffn_g[i], mod, i, 3, moe_router[f], moe_router_b[f],
                                             n_ctx, dec_seq, tm_half)
            dest, src, block_table = _moe_plan(ids, n_exp, moe_bm)
            big, small = block_table(moe_bm), block_table(moe_sub)
            xs = _gather_rows(h_tok, src, small[1], moe_sub, d)
            gs = _grouped_swiglu(xs, moe_w1, moe_w3, f, big, moe_bm)
            os_ = _grouped_down(gs, moe_w2, f, small, moe_sub)
            if last:
                outs = _combine(os_, dest, gates, y, mod, i, 5, final_norm_g, True, n_ctx, dec_seq, tm_small)
            else:
                y, h = _combine(os_, dest, gates, y, mod, i, 5, norm_mix_g[i + 1], False,
                                n_ctx, dec_seq, tm_small)

    y_p, y_s = outs if outs is not None else _final_norm(y, final_norm_g, n_ctx, tm_half)
    new_state = jnp.stack(states, axis=1).astype(x_prompt.dtype)
    return (y_p.reshape(batch, seq, d), y_s.reshape(dec_batch, dec_seq, d), new_state)
```

```python
import functools

import jax
import jax.numpy as jnp
from jax import lax
from jax.experimental import pallas as pl
from jax.experimental.pallas import tpu as pltpu

F32 = jnp.float32
BF16 = jnp.bfloat16

GRID_W = 64
EPS = 1e-6
RG_C = 8.0
TOP_K = 2

LANES = 128
SUBLANES = 8
VMEM_LIMIT_BYTES = 58 * 1024 * 1024
NEG_BIG = -1e30
MXU_ROWS_PER_PASS = 256
DMA_RING = 3


def _cparams(n_axes):
    return pltpu.CompilerParams(dimension_semantics=("arbitrary",) * n_axes,
                                vmem_limit_bytes=VMEM_LIMIT_BYTES)


def _pick_tile(n, pref):
    if n <= pref:
        return n
    t = (pref // LANES) * LANES
    while t > LANES and n % t:
        t -= LANES
    assert n % t == 0, (n, pref)
    return t


def _row_passes(rows):
    step = MXU_ROWS_PER_PASS if rows % MXU_ROWS_PER_PASS == 0 else rows
    return [pl.ds(r, step) for r in range(0, rows, step)]


def _sigmoid(x):
    return 1.0 / (1.0 + jnp.exp(-x))


def _gelu_tanh(x):
    c = 0.7978845608028654
    return 0.5 * x * (1.0 + jnp.tanh(c * (x + 0.044715 * (x * x * x))))


def _mod_row(i, tm, n_ctx, dec_seq):
    start = i * tm
    return jnp.where(start < n_ctx, 0, 1 + (start - n_ctx) // dec_seq)


def _adaln_kernel(c_ref, w_ref, b_ref, o_ref):
    c = c_ref[...]
    s = (c * _sigmoid(c)).astype(BF16)
    o_ref[...] = jnp.dot(s, w_ref[...].astype(BF16), preferred_element_type=F32) + b_ref[...]


def _adaln(cond, w_mod, b_mod):
    depth, d, n = w_mod.shape
    mc = cond.shape[0]
    tn = _pick_tile(n, 1024)
    return pl.pallas_call(
        _adaln_kernel,
        grid=(depth, n // tn),
        in_specs=[pl.BlockSpec((mc, d), lambda l, j: (0, 0)),
                  pl.BlockSpec((None, d, tn), lambda l, j: (l, 0, j)),
                  pl.BlockSpec((None, 1, tn), lambda l, j: (l, 0, j))],
        out_specs=pl.BlockSpec((None, mc, tn), lambda l, j: (l, 0, j)),
        out_shape=jax.ShapeDtypeStruct((depth, mc, n), F32),
        compiler_params=_cparams(2), name="adaln",
    )(cond, w_mod, b_mod.reshape(depth, 1, n))


def _embed_norm_kernel(xp_ref, xs_ref, pos_ref, g_ref, mod_ref, y_ref, h_ref, *, n_ctx_tiles):
    i = pl.program_id(0)

    @pl.when(i < n_ctx_tiles)
    def _():
        y_ref[...] = xp_ref[...]

    @pl.when(i >= n_ctx_tiles)
    def _():
        y_ref[...] = xs_ref[...] + pos_ref[...]

    h_ref[...] = _norm_mod_value(y_ref[...], g_ref[...], mod_ref, 0).astype(h_ref.dtype)


def _embed_norm(xp, xs, pos, g, mod, tm):
    n_ctx, d = xp.shape
    n_dec = xs.shape[0]
    dec_seq = pos.shape[0]
    nct = n_ctx // tm
    ppt = dec_seq // tm
    t = n_ctx + n_dec
    return pl.pallas_call(
        functools.partial(_embed_norm_kernel, n_ctx_tiles=nct),
        grid=(t // tm,),
        in_specs=[pl.BlockSpec((tm, d), lambda i: (jnp.minimum(i, nct - 1), 0)),
                  pl.BlockSpec((tm, d), lambda i: (jnp.maximum(i - nct, 0), 0)),
                  pl.BlockSpec((tm, d), lambda i: (jnp.maximum(i - nct, 0) % ppt, 0)),
                  pl.BlockSpec((1, d), lambda i: (0, 0)),
                  pl.BlockSpec((None, None, 6, d), lambda i: (0, _mod_row(i, tm, n_ctx, dec_seq), 0, 0))],
        out_specs=[pl.BlockSpec((tm, d), lambda i: (i, 0)),
                   pl.BlockSpec((tm, d), lambda i: (i, 0))],
        out_shape=[jax.ShapeDtypeStruct((t, d), F32), jax.ShapeDtypeStruct((t, d), BF16)],
        compiler_params=_cparams(1), name="embed_norm",
    )(xp, xs, pos, g.reshape(1, d), mod)


def _grid_pos_embed(length, d):
    rows = length // GRID_W
    r = jnp.repeat(jnp.arange(rows), GRID_W)
    col = jnp.tile(jnp.arange(GRID_W), rows)
    quarter = d // 4
    omega = 1.0 / (10000.0 ** (jnp.arange(quarter, dtype=F32) / quarter))

    def emb(p):
        ang = p[:, None].astype(F32) * omega[None, :]
        return jnp.concatenate([jnp.sin(ang), jnp.cos(ang)], axis=-1)

    return jnp.concatenate([emb(r), emb(col)], axis=-1).astype(F32)


def _norm_mod_value(y, g, mod_ref, shift_row):
    ms = jnp.mean(y * y, axis=-1, keepdims=True)
    n = (y * lax.rsqrt(ms + EPS)) * g
    return n * (1.0 + mod_ref[shift_row + 1:shift_row + 2, :]) + mod_ref[shift_row:shift_row + 1, :]


def _norm_mod_kernel(y_ref, g_ref, mod_ref, h_ref, *, shift_row):
    h_ref[...] = _norm_mod_value(y_ref[...], g_ref[...], mod_ref, shift_row).astype(h_ref.dtype)


def _norm_mod(y, g, mod, layer, shift_row, n_ctx, dec_seq, tm):
    t, d = y.shape
    return pl.pallas_call(
        functools.partial(_norm_mod_kernel, shift_row=shift_row),
        grid=(t // tm,),
        in_specs=[pl.BlockSpec((tm, d), lambda i: (i, 0)),
                  pl.BlockSpec((1, d), lambda i: (0, 0)),
                  pl.BlockSpec((None, None, 6, d),
                               lambda i: (layer, _mod_row(i, tm, n_ctx, dec_seq), 0, 0))],
        out_specs=pl.BlockSpec((tm, d), lambda i: (i, 0)),
        out_shape=jax.ShapeDtypeStruct((t, d), BF16),
        compiler_params=_cparams(1), name="norm_mod",
    )(y, g.reshape(1, d), mod)


def _norm_router_kernel(y_ref, g_ref, mod_ref, r_ref, rb_ref, h_ref, ids_ref, gates_ref, *,
                        shift_row, n_slab):
    h = _norm_mod_value(y_ref[...], g_ref[...], mod_ref, shift_row)
    tm = h.shape[0]
    for s in range(n_slab):
        h_ref[pl.ds(s, tm, stride=n_slab), :] = h[:, s * LANES:(s + 1) * LANES]
    h_hi = h.astype(BF16)
    h_lo = (h - h_hi.astype(F32)).astype(BF16)
    p_hi = jnp.dot(h_hi, r_ref[...], preferred_element_type=F32)
    p_lo = jnp.dot(h_lo, r_ref[:, :LANES], preferred_element_type=F32)
    logits = ((p_hi[:, :LANES] + p_hi[:, LANES:]) + p_lo) + rb_ref[...]
    lane = lax.broadcasted_iota(jnp.int32, logits.shape, 1).astype(F32)
    big = float(LANES)
    m1 = jnp.max(logits, axis=-1, keepdims=True)
    i1 = jnp.min(jnp.where(logits == m1, lane, big), axis=-1, keepdims=True)
    l2 = jnp.where(lane == i1, 2.0 * NEG_BIG, logits)
    m2 = jnp.max(l2, axis=-1, keepdims=True)
    i2 = jnp.min(jnp.where(l2 == m2, lane, big), axis=-1, keepdims=True)
    e = jnp.exp(m2 - m1)
    g1 = 1.0 / (1.0 + e)
    g2 = e / (1.0 + e)
    ids_ref[...] = jnp.where(lane == 0.0, i1, jnp.where(lane == 1.0, i2, 0.0)).astype(jnp.int32)
    gates_ref[...] = jnp.where(lane == 0.0, g1, jnp.where(lane == 1.0, g2, 0.0))


def _norm_router(y, g, mod, layer, shift_row, router, router_b, n_ctx, dec_seq, tm):
    t, d = y.shape
    n_exp = router.shape[1]
    n_slab = d // LANES
    rp = jnp.zeros((d, LANES), F32).at[:, :n_exp].set(router)
    r_hi = rp.astype(BF16)
    r_lo = (rp - r_hi.astype(F32)).astype(BF16)
    rp = jnp.concatenate([r_hi, r_lo], axis=1)
    rbp = jnp.full((1, LANES), NEG_BIG, F32).at[0, :n_exp].set(router_b)
    return pl.pallas_call(
        functools.partial(_norm_router_kernel, shift_row=shift_row, n_slab=n_slab),
        grid=(t // tm,),
        in_specs=[pl.BlockSpec((tm, d), lambda i: (i, 0)),
                  pl.BlockSpec((1, d), lambda i: (0, 0)),
                  pl.BlockSpec((None, None, 6, d),
                               lambda i: (layer, _mod_row(i, tm, n_ctx, dec_seq), 0, 0)),
                  pl.BlockSpec((d, 2 * LANES), lambda i: (0, 0)),
                  pl.BlockSpec((1, LANES), lambda i: (0, 0))],
        out_specs=[pl.BlockSpec((tm * n_slab, LANES), lambda i: (i, 0)),
                   pl.BlockSpec((tm, LANES), lambda i: (i, 0)),
                   pl.BlockSpec((tm, LANES), lambda i: (i, 0))],
        out_shape=[jax.ShapeDtypeStruct((t * n_slab, LANES), F32),
                   jax.ShapeDtypeStruct((t, LANES), jnp.int32),
                   jax.ShapeDtypeStruct((t, LANES), F32)],
        compiler_params=_cparams(1), name="norm_router",
    )(y, g.reshape(1, d), mod, rp, rbp)


def _final_norm_kernel(y_ref, g_ref, op_ref, os_ref, *, n_ctx_tiles):
    i = pl.program_id(0)
    y = y_ref[...]
    ms = jnp.mean(y * y, axis=-1, keepdims=True)
    n = (y * lax.rsqrt(ms + EPS)) * g_ref[...]

    @pl.when(i < n_ctx_tiles)
    def _():
        op_ref[...] = n

    @pl.when(i >= n_ctx_tiles)
    def _():
        os_ref[...] = n


def _final_norm(y, g, n_ctx, tm):
    t, d = y.shape
    nct = n_ctx // tm
    return pl.pallas_call(
        functools.partial(_final_norm_kernel, n_ctx_tiles=nct),
        grid=(t // tm,),
        in_specs=[pl.BlockSpec((tm, d), lambda i: (i, 0)),
                  pl.BlockSpec((1, d), lambda i: (0, 0))],
        out_specs=[pl.BlockSpec((tm, d), lambda i: (jnp.minimum(i, nct - 1), 0)),
                   pl.BlockSpec((tm, d), lambda i: (jnp.maximum(i - nct, 0), 0))],
        out_shape=[jax.ShapeDtypeStruct((n_ctx, d), F32),
                   jax.ShapeDtypeStruct((t - n_ctx, d), F32)],
        compiler_params=_cparams(1), name="final_norm",
    )(y, g.reshape(1, d))


def _cast_at_first_row_tile(w_refs, wb_refs):
    @pl.when(pl.program_id(1) == 0)
    def _():
        for w, wb in zip(w_refs, wb_refs):
            wb[...] = w[...].astype(BF16)


def _mm_act_kernel(x_ref, w_ref, o_ref, wb, *, act):
    _cast_at_first_row_tile((w_ref,), (wb,))
    for rows in _row_passes(x_ref.shape[0]):
        acc = jnp.dot(x_ref[rows, :], wb[...], preferred_element_type=F32)
        if act == "gelu":
            acc = _gelu_tanh(acc)
        o_ref[rows, :] = acc.astype(o_ref.dtype)


def _mm_act(x, w, layer, act, out_dtype, tm, tn_pref=512):
    t, k = x.shape
    n = w.shape[2]
    tn = _pick_tile(n, tn_pref)
    return pl.pallas_call(
        functools.partial(_mm_act_kernel, act=act),
        grid=(n // tn, t // tm),
        in_specs=[pl.BlockSpec((tm, k), lambda j, i: (i, 0)),
                  pl.BlockSpec((None, k, tn), lambda j, i: (layer, 0, j))],
        out_specs=pl.BlockSpec((tm, tn), lambda j, i: (i, j)),
        out_shape=jax.ShapeDtypeStruct((t, n), out_dtype),
        scratch_shapes=[pltpu.VMEM((k, tn), BF16)],
        compiler_params=_cparams(2), name="mm_" + str(act),
    )(x, w)


def _swiglu_passes(x_ref, wb1, wb3, o_ref):
    for rows in _row_passes(x_ref.shape[0]):
        x = x_ref[rows, :]
        a = jnp.dot(x, wb1[...], preferred_element_type=F32)
        b = jnp.dot(x, wb3[...], preferred_element_type=F32)
        o_ref[rows, :] = ((a * _sigmoid(a)) * b).astype(o_ref.dtype)


def _mm_swiglu_kernel(x_ref, w1_ref, w3_ref, o_ref, wb1, wb3):
    _cast_at_first_row_tile((w1_ref, w3_ref), (wb1, wb3))
    _swiglu_passes(x_ref, wb1, wb3, o_ref)


def _mm_swiglu(x, w1, w3, layer, tm, tn_pref=512):
    t, k = x.shape
    n = w1.shape[2]
    tn = _pick_tile(n, tn_pref)
    return pl.pallas_call(
        _mm_swiglu_kernel,
        grid=(n // tn, t // tm),
        in_specs=[pl.BlockSpec((tm, k), lambda j, i: (i, 0)),
                  pl.BlockSpec((None, k, tn), lambda j, i: (layer, 0, j)),
                  pl.BlockSpec((None, k, tn), lambda j, i: (layer, 0, j))],
        out_specs=pl.BlockSpec((tm, tn), lambda j, i: (i, j)),
        out_shape=jax.ShapeDtypeStruct((t, n), BF16),
        scratch_shapes=[pltpu.VMEM((k, tn), BF16), pltpu.VMEM((k, tn), BF16)],
        compiler_params=_cparams(2), name="mm_swiglu",
    )(x, w1, w3)


def _mm_sconv_kernel(x_ref, wb_ref, wc_ref, wx_ref, cw_ref, o_ref, sb, sc, sx, *, n_ctx_tiles, seq_ctx):
    _cast_at_first_row_tile((wb_ref, wc_ref, wx_ref), (sb, sc, sx))
    i = pl.program_id(1)
    is_ctx = i < n_ctx_tiles
    x = x_ref[...]
    bg = jnp.dot(x, sb[...], preferred_element_type=F32)
    p = jnp.dot(x, sc[...], preferred_element_type=F32) * jnp.dot(x, sx[...], preferred_element_type=F32)
    tm = p.shape[0]
    row = lax.broadcasted_iota(jnp.int32, p.shape, 0)
    pos = jnp.where(is_ctx, row % seq_ctx, row)
    last_pos = jnp.where(is_ctx, seq_ctx - 1, tm - 1)
    p_prev = jnp.where(pos == 0, 0.0, pltpu.roll(p, 1, axis=0))
    p_next = jnp.where(pos == last_pos, 0.0, pltpu.roll(p, tm - 1, axis=0))
    cw = cw_ref[...]
    conv = (cw[0:1] * p_prev + cw[1:2] * p) + cw[2:3] * p_next
    o_ref[...] = (bg * conv).astype(o_ref.dtype)


def _mm_sconv(x, w_in, conv_w, layer, n_ctx, seq_ctx, tm, tn_pref=512):
    t, k = x.shape
    d = conv_w.shape[2]
    tn = _pick_tile(d, tn_pref)
    nj = d // tn
    return pl.pallas_call(
        functools.partial(_mm_sconv_kernel, n_ctx_tiles=n_ctx // tm, seq_ctx=seq_ctx),
        grid=(nj, t // tm),
        in_specs=[pl.BlockSpec((tm, k), lambda j, i: (i, 0)),
                  pl.BlockSpec((None, k, tn), lambda j, i: (layer, 0, j)),
                  pl.BlockSpec((None, k, tn), lambda j, i: (layer, 0, nj + j)),
                  pl.BlockSpec((None, k, tn), lambda j, i: (layer, 0, 2 * nj + j)),
                  pl.BlockSpec((None, conv_w.shape[1], tn), lambda j, i: (layer, 0, j))],
        out_specs=pl.BlockSpec((tm, tn), lambda j, i: (i, j)),
        out_shape=jax.ShapeDtypeStruct((t, d), BF16),
        scratch_shapes=[pltpu.VMEM((k, tn), BF16)] * 3,
        compiler_params=_cparams(2), name="mm_sconv",
    )(x, w_in, w_in, w_in, conv_w)


def _mm_residual_kernel(x_ref, w_ref, y_ref, mod_ref, o_ref, wb, *, gate_row):
    _cast_at_first_row_tile((w_ref,), (wb,))
    gate = mod_ref[gate_row:gate_row + 1, :]
    for rows in _row_passes(x_ref.shape[0]):
        acc = jnp.dot(x_ref[rows, :], wb[...], preferred_element_type=F32)
        o_ref[rows, :] = y_ref[rows, :] + gate * acc


def _mm_residual(x, w, w_layer, y, mod, layer, gate_row, n_ctx, dec_seq, tm, tn_pref=512):
    t, k = x.shape
    n = w.shape[2]
    tn = _pick_tile(n, tn_pref)
    return pl.pallas_call(
        functools.partial(_mm_residual_kernel, gate_row=gate_row),
        grid=(n // tn, t // tm),
        in_specs=[pl.BlockSpec((tm, k), lambda j, i: (i, 0)),
                  pl.BlockSpec((None, k, tn), lambda j, i: (w_layer, 0, j)),
                  pl.BlockSpec((tm, tn), lambda j, i: (i, j)),
                  pl.BlockSpec((None, None, 6, tn),
                               lambda j, i: (layer, _mod_row(i, tm, n_ctx, dec_seq), 0, j))],
        out_specs=pl.BlockSpec((tm, tn), lambda j, i: (i, j)),
        out_shape=jax.ShapeDtypeStruct((t, n), F32),
        scratch_shapes=[pltpu.VMEM((k, tn), BF16)],
        compiler_params=_cparams(2), name="mm_residual",
    )(x, w, y, mod)


def _rg_core_kernel(gate_ref, x_ref, cw_ref, cb_ref, wg_ref, bg_ref, lam_ref, h0_ref,
                    y_ref, st_ref, af_s, bf_s, ab_s, bb_s, *, n_ctx_tiles, seg, tm, cg):
    i = pl.program_id(1)
    is_ctx = i < n_ctx_tiles
    nseg = tm // seg
    nslab = cg // LANES
    lsub = tm // SUBLANES
    sub_per_seq = seg // lsub
    cw = cw_ref[...]
    cb = cb_ref[...]
    lam = lam_ref[...]
    softplus_neg_lam = jnp.maximum(-lam, 0.0) + jnp.log1p(jnp.exp(-jnp.abs(lam)))
    neg_c_sp = (-RG_C) * softplus_neg_lam
    zeros8 = jnp.zeros((SUBLANES, cg), F32)

    for s in range(nseg):
        c0 = s * seg
        cur = x_ref[pl.ds(c0, seg), :]
        prev8 = zeros8 if s == 0 else jnp.where(is_ctx, 0.0, x_ref[pl.ds(c0 - SUBLANES, SUBLANES), :])
        next8 = zeros8 if s == nseg - 1 else jnp.where(is_ctx, 0.0, x_ref[pl.ds(c0 + seg, SUBLANES), :])
        head = jnp.concatenate([prev8, cur[:2 * SUBLANES]], axis=0)
        tail = jnp.concatenate([cur[seg - 2 * SUBLANES:], next8], axis=0)

        def shifted(delta):
            lo = SUBLANES + delta
            mid = x_ref[pl.ds(c0 + lo, seg - 2 * SUBLANES), :]
            return jnp.concatenate([head[lo:lo + SUBLANES], mid, tail[lo:lo + SUBLANES]], axis=0)

        xc = (((cw[0:1] * shifted(-2) + cw[1:2] * shifted(-1)) + cw[2:3] * cur)
              + cw[3:4] * shifted(1)) + cb
        g = jnp.dot(xc.astype(BF16), wg_ref[...], preferred_element_type=F32) + bg_ref[...]
        for d, (a_s, b_s) in enumerate(((af_s, bf_s), (ab_s, bb_s))):
            r = _sigmoid(g[:, (2 * d) * cg:(2 * d + 1) * cg])
            ig = _sigmoid(g[:, (2 * d + 1) * cg:(2 * d + 2) * cg])
            log_a = neg_c_sp[d:d + 1] * r
            a = jnp.exp(log_a)
            one_minus_a2 = -jnp.tanh(log_a) * (a * a + 1.0)
            root = jnp.where(one_minus_a2 > 0.0, one_minus_a2 * lax.rsqrt(one_minus_a2), 0.0)
            bt = root * (ig * xc)
            for kk in range(sub_per_seq):
                k = s * sub_per_seq + kk
                dst = pl.ds(k, lsub, stride=SUBLANES)
                for l in range(nslab):
                    a_s[l, dst, :] = a[kk * lsub:(kk + 1) * lsub, l * LANES:(l + 1) * LANES]
                    b_s[l, dst, :] = bt[kk * lsub:(kk + 1) * lsub, l * LANES:(l + 1) * LANES]

    def local_scan(j, carry):
        hf, pf, hb, pb = carry
        rf = pl.multiple_of(j * SUBLANES, SUBLANES)
        rb = pl.multiple_of((lsub - 1 - j) * SUBLANES, SUBLANES)
        nhf, npf, nhb, npb = [], [], [], []
        for l in range(nslab):
            a = af_s[l, pl.ds(rf, SUBLANES), :]
            h = a * hf[l] + bf_s[l, pl.ds(rf, SUBLANES), :]
            p = a * pf[l]
            bf_s[l, pl.ds(rf, SUBLANES), :] = h
            af_s[l, pl.ds(rf, SUBLANES), :] = p
            nhf.append(h)
            npf.append(p)
            a = ab_s[l, pl.ds(rb, SUBLANES), :]
            h = a * hb[l] + bb_s[l, pl.ds(rb, SUBLANES), :]
            p = a * pb[l]
            bb_s[l, pl.ds(rb, SUBLANES), :] = h
            ab_s[l, pl.ds(rb, SUBLANES), :] = p
            nhb.append(h)
            npb.append(p)
        return tuple(nhf), tuple(npf), tuple(nhb), tuple(npb)

    zero = tuple(jnp.zeros((SUBLANES, LANES), F32) for _ in range(nslab))
    one = tuple(jnp.ones((SUBLANES, LANES), F32) for _ in range(nslab))
    hf_end, pf_end, hb_end, pb_end = lax.fori_loop(0, lsub, local_scan, (zero, one, zero, one))

    row8 = lax.broadcasted_iota(jnp.int32, (SUBLANES, LANES), 0)
    h0 = h0_ref[...]
    for l in range(nslab):
        lanes = slice(l * LANES, (l + 1) * LANES)
        ent_f = jnp.zeros((SUBLANES, LANES), F32)
        h_in = h0[0:1, lanes]
        for k in range(SUBLANES):
            if k > 0 and k % sub_per_seq == 0:
                h_in = jnp.where(is_ctx, 0.0, h_in)
            ent_f = jnp.where(row8 == k, h_in, ent_f)
            h_in = hf_end[l][k:k + 1] + pf_end[l][k:k + 1] * h_in
            if (k + 1) % sub_per_seq == 0:
                q = k // sub_per_seq
                st_ref[q:q + 1, lanes] = h_in
        ent_b = jnp.zeros((SUBLANES, LANES), F32)
        h_in = h0[1:2, lanes]
        for k in reversed(range(SUBLANES)):
            if k < SUBLANES - 1 and (k + 1) % sub_per_seq == 0:
                h_in = jnp.where(is_ctx, 0.0, h_in)
            ent_b = jnp.where(row8 == k, h_in, ent_b)
            h_in = hb_end[l][k:k + 1] + pb_end[l][k:k + 1] * h_in
            if k % sub_per_seq == 0:
                q = k // sub_per_seq
                st_ref[nseg + q:nseg + q + 1, lanes] = h_in
        ch = min(tm, 256)
        ef = jnp.concatenate([ent_f] * (ch // SUBLANES), axis=0)
        eb = jnp.concatenate([ent_b] * (ch // SUBLANES), axis=0)
        for r0 in range(0, tm, ch):
            rows = pl.ds(r0, ch)
            bf_s[l, rows, :] = ((bf_s[l, rows, :] + af_s[l, rows, :] * ef)
                                + (bb_s[l, rows, :] + ab_s[l, rows, :] * eb))

    for k in range(SUBLANES):
        rows = pl.ds(k * lsub, lsub)
        hsum = jnp.concatenate([bf_s[l, pl.ds(k, lsub, stride=SUBLANES), :] for l in range(nslab)], axis=1)
        y_ref[rows, :] = (hsum * _gelu_tanh(gate_ref[rows, :])).astype(y_ref.dtype)


def _rg_core(z, conv_w, conv_b, w_a, b_a, w_x, b_x, lam, h0, n_ctx, seq_ctx, tm):
    t = z.shape[0]
    r = conv_w.shape[1]
    heads, hw = w_a.shape[1], w_a.shape[2]
    hpg = 4
    while (hpg * hw) % LANES:
        hpg *= 2
    ng = heads // hpg
    cg = hpg * hw
    nseg = tm // seq_ctx
    ntiles = t // tm
    assert tm % SUBLANES == 0 and seq_ctx % (tm // SUBLANES) == 0

    def blockdiag(w):
        rows = jnp.tile(w.reshape(2, ng, cg, hw), (1, 1, 1, hpg))
        head = jnp.arange(cg, dtype=jnp.int32) // hw
        return jnp.where(head[:, None] == head[None, :], rows, 0.0)

    wa, wx = blockdiag(w_a), blockdiag(w_x)
    wg = jnp.concatenate([wa[0], wx[0], wa[1], wx[1]], axis=-1).astype(BF16)
    ba, bx = b_a.reshape(2, ng, 1, cg), b_x.reshape(2, ng, 1, cg)
    bg = jnp.concatenate([ba[0], bx[0], ba[1], bx[1]], axis=-1)

    return pl.pallas_call(
        functools.partial(_rg_core_kernel, n_ctx_tiles=n_ctx // tm, seg=seq_ctx, tm=tm, cg=cg),
        grid=(ng, ntiles),
        in_specs=[pl.BlockSpec((tm, cg), lambda g, i: (i, g)),
                  pl.BlockSpec((tm, cg), lambda g, i: (i, ng + g)),
                  pl.BlockSpec((conv_w.shape[0], cg), lambda g, i: (0, g)),
                  pl.BlockSpec((1, cg), lambda g, i: (0, g)),
                  pl.BlockSpec((None, cg, 4 * cg), lambda g, i: (g, 0, 0)),
                  pl.BlockSpec((None, 1, 4 * cg), lambda g, i: (g, 0, 0)),
                  pl.BlockSpec((2, cg), lambda g, i: (0, g)),
                  pl.BlockSpec((None, 2, cg), lambda g, i: (i, 0, g))],
        out_specs=[pl.BlockSpec((tm, cg), lambda g, i: (i, g)),
                   pl.BlockSpec((None, 2 * nseg, cg), lambda g, i: (i, 0, g))],
        out_shape=[jax.ShapeDtypeStruct((t, r), BF16),
                   jax.ShapeDtypeStruct((ntiles, 2 * nseg, r), F32)],
        scratch_shapes=[pltpu.VMEM((cg // LANES, tm, LANES), F32)] * 4,
        compiler_params=_cparams(2), name="rg_core",
    )(z, z, conv_w, conv_b.reshape(1, r), wg, bg, lam, h0)


def _sgu_core_kernel(u_ref, v_ref, g_ref, ws_ref, bs_ref, y_ref, vn_s, *, chunk, gw, ngroups, tm):
    v = v_ref[...].astype(F32)
    ms = jnp.mean(v * v, axis=-1, keepdims=True)
    vn_s[...] = ((v * lax.rsqrt(ms + EPS)) * g_ref[...]).astype(BF16)
    for c in range(tm // chunk):
        rows = pl.ds(c * chunk, chunk)
        for g in range(ngroups):
            cols = pl.ds(g * gw, gw)
            vm = jnp.dot(ws_ref[g], vn_s[rows, cols], preferred_element_type=F32) + bs_ref[:, g:g + 1]
            y_ref[rows, cols] = (u_ref[rows, cols].astype(F32) * vm).astype(y_ref.dtype)


def _sgu_core(z, norm_g, w_s, b_s, tm):
    t = z.shape[0]
    w = norm_g.shape[0]
    ngroups, chunk = w_s.shape[0], w_s.shape[1]
    gw = w // ngroups
    return pl.pallas_call(
        functools.partial(_sgu_core_kernel, chunk=chunk, gw=gw, ngroups=ngroups, tm=tm),
        grid=(t // tm,),
        in_specs=[pl.BlockSpec((tm, w), lambda i: (i, 0)),
                  pl.BlockSpec((tm, w), lambda i: (i, 1)),
                  pl.BlockSpec((1, w), lambda i: (0, 0)),
                  pl.BlockSpec((ngroups, chunk, chunk), lambda i: (0, 0, 0)),
                  pl.BlockSpec((chunk, ngroups), lambda i: (0, 0))],
        out_specs=pl.BlockSpec((tm, w), lambda i: (i, 0)),
        out_shape=jax.ShapeDtypeStruct((t, w), BF16),
        scratch_shapes=[pltpu.VMEM((tm, w), BF16)],
        compiler_params=_cparams(1), name="sgu_core",
    )(z, z, norm_g.reshape(1, w), w_s.astype(BF16), b_s.T)


def _moe_plan(ids, n_exp, bm):
    t = ids.shape[0]
    e = ids[:, :TOP_K].reshape(-1)
    oh = (e[:, None] == jnp.arange(n_exp, dtype=jnp.int32)[None, :]).astype(jnp.int32)
    csum = jnp.cumsum(oh, axis=0)
    rank = jnp.sum((csum - oh) * oh, axis=1)
    counts = csum[-1]
    padded = ((counts + bm - 1) // bm) * bm
    ends = jnp.cumsum(padded)
    starts = ends - padded
    dest = (jnp.sum(starts[None, :] * oh, axis=1) + rank).astype(jnp.int32)
    p = TOP_K * t + n_exp * bm
    src = jnp.zeros((p,), jnp.int32).at[dest].set(jnp.arange(TOP_K * t, dtype=jnp.int32) // TOP_K)

    def block_table(blk):
        idx = jnp.arange(p // blk, dtype=jnp.int32)
        start = idx * blk
        exp = jnp.minimum(jnp.sum((start[:, None] >= ends[None, :]).astype(jnp.int32), axis=1), n_exp - 1)
        valid = jnp.clip(starts[exp] + counts[exp] - start, 0, blk)
        fetch = lax.cummax(jnp.where(valid > 0, idx, 0), axis=0)
        return exp.astype(jnp.int32), valid.astype(jnp.int32), fetch.astype(jnp.int32)

    return dest, src, block_table


def _gather_rows_kernel(src_ref, valid_ref, h_hbm, o_ref, buf, sem, *, rows, n_slab):
    b = pl.program_id(0)
    nb = pl.num_programs(0)
    slot = b % DMA_RING
    ahead = DMA_RING - 1

    def block_has_rows(blk):
        return jnp.logical_and(blk < nb, valid_ref[jnp.minimum(blk, nb - 1)] > 0)

    has_rows = block_has_rows(b)

    def issue(blk, to_slot):
        base = blk * rows

        def body(q, carry):
            for prio in range(2):
                r = 2 * q + prio
                tok = pl.multiple_of(src_ref[base + r] * n_slab, n_slab)
                pltpu.make_async_copy(h_hbm.at[pl.ds(tok, n_slab), :],
                                      buf.at[to_slot, pl.ds(pl.multiple_of(r * n_slab, n_slab), n_slab), :],
                                      sem.at[to_slot]).start(priority=prio)
            return carry

        lax.fori_loop(0, rows // 2, body, 0, unroll=4)

    @pl.when(b == 0)
    def _():
        for first in range(ahead):
            @pl.when(block_has_rows(first))
            def _():
                issue(first, first % DMA_RING)

    @pl.when(block_has_rows(b + ahead))
    def _():
        issue(b + ahead, (b + ahead) % DMA_RING)

    @pl.when(has_rows)
    def _():
        pltpu.make_async_copy(h_hbm.at[pl.ds(0, rows * n_slab), :], buf.at[slot], sem.at[slot]).wait()
        for s in range(n_slab):
            o_ref[:, s * LANES:(s + 1) * LANES] = buf[slot, pl.ds(s, rows, stride=n_slab), :].astype(o_ref.dtype)

    @pl.when(jnp.logical_not(has_rows))
    def _():
        o_ref[...] = jnp.zeros_like(o_ref)


def _gather_rows(h_tok, src, valid, bm, d):
    n_slab = d // LANES
    p = src.shape[0]
    return pl.pallas_call(
        functools.partial(_gather_rows_kernel, rows=bm, n_slab=n_slab),
        grid_spec=pltpu.PrefetchScalarGridSpec(
            num_scalar_prefetch=2,
            grid=(p // bm,),
            in_specs=[pl.BlockSpec(memory_space=pl.ANY)],
            out_specs=pl.BlockSpec((bm, d), lambda b, src, va: (b, 0)),
            scratch_shapes=[pltpu.VMEM((DMA_RING, bm * n_slab, LANES), F32),
                            pltpu.SemaphoreType.DMA((DMA_RING,))]),
        out_shape=jax.ShapeDtypeStruct((p, d), BF16),
        compiler_params=_cparams(1), name="moe_gather",
    )(src, valid, h_tok)


def _expert_changed(exp_ref, fetch_ref, b):
    prev = fetch_ref[jnp.maximum(b - 1, 0)]
    return jnp.logical_or(b == 0, exp_ref[b] != exp_ref[prev])


def _guarded_passes(valid, x_ref, o_ref, one_pass):
    bm = x_ref.shape[0]
    passes = _row_passes(bm)

    @pl.when(valid == bm)
    def _():
        for rows in passes:
            one_pass(rows)

    @pl.when(valid < bm)
    def _():
        for n, rows in enumerate(passes):
            first_row = n * (bm // len(passes))

            @pl.when(first_row < valid)
            def _():
                one_pass(rows)

            @pl.when(first_row >= valid)
            def _():
                o_ref[rows, :] = jnp.zeros((bm // len(passes), o_ref.shape[1]), o_ref.dtype)


def _grouped_swiglu_kernel(exp_ref, valid_ref, fetch_ref, x_ref, w1_ref, w3_ref, o_ref, wb1, wb3):
    b = pl.program_id(1)
    valid = valid_ref[b]

    @pl.when(jnp.logical_and(valid > 0, _expert_changed(exp_ref, fetch_ref, b)))
    def _():
        wb1[...] = w1_ref[...].astype(BF16)
        wb3[...] = w3_ref[...].astype(BF16)

    def one_pass(rows):
        x = x_ref[rows, :]
        a = jnp.dot(x, wb1[...], preferred_element_type=F32)
        c = jnp.dot(x, wb3[...], preferred_element_type=F32)
        o_ref[rows, :] = ((a * _sigmoid(a)) * c).astype(o_ref.dtype)

    _guarded_passes(valid, x_ref, o_ref, one_pass)


def _grouped_swiglu(xs, w1, w3, layer, table, bm, tn_pref=512):
    p, k = xs.shape
    n = w1.shape[3]
    tn = _pick_tile(n, tn_pref)
    return pl.pallas_call(
        _grouped_swiglu_kernel,
        grid_spec=pltpu.PrefetchScalarGridSpec(
            num_scalar_prefetch=3,
            grid=(n // tn, p // bm),
            in_specs=[pl.BlockSpec((bm, k), lambda j, b, ex, va, fe: (fe[b], 0)),
                      pl.BlockSpec((None, None, k, tn), lambda j, b, ex, va, fe: (layer, ex[fe[b]], 0, j)),
                      pl.BlockSpec((None, None, k, tn), lambda j, b, ex, va, fe: (layer, ex[fe[b]], 0, j))],
            out_specs=pl.BlockSpec((bm, tn), lambda j, b, ex, va, fe: (b, j)),
            scratch_shapes=[pltpu.VMEM((k, tn), BF16), pltpu.VMEM((k, tn), BF16)]),
        out_shape=jax.ShapeDtypeStruct((p, n), BF16),
        compiler_params=_cparams(2), name="moe_swiglu",
    )(*table, xs, w1, w3)


def _grouped_down_kernel(exp_ref, valid_ref, fetch_ref, x_ref, w_ref, o_ref, wb):
    b = pl.program_id(1)
    valid = valid_ref[b]

    @pl.when(jnp.logical_and(valid > 0, _expert_changed(exp_ref, fetch_ref, b)))
    def _():
        wb[...] = w_ref[...].astype(BF16)

    def one_pass(rows):
        o_ref[rows, :] = jnp.dot(x_ref[rows, :], wb[...], preferred_element_type=F32)

    _guarded_passes(valid, x_ref, o_ref, one_pass)


def _grouped_down(gs, w2, layer, table, bm, tn_pref=512):
    p, k = gs.shape
    n = w2.shape[3]
    tn = _pick_tile(n, tn_pref)
    return pl.pallas_call(
        _grouped_down_kernel,
        grid_spec=pltpu.PrefetchScalarGridSpec(
            num_scalar_prefetch=3,
            grid=(n // tn, p // bm),
            in_specs=[pl.BlockSpec((bm, k), lambda j, b, ex, va, fe: (fe[b], 0)),
                      pl.BlockSpec((None, None, k, tn), lambda j, b, ex, va, fe: (layer, ex[fe[b]], 0, j))],
            out_specs=pl.BlockSpec((bm, tn), lambda j, b, ex, va, fe: (b, j)),
            scratch_shapes=[pltpu.VMEM((k, tn), BF16)]),
        out_shape=jax.ShapeDtypeStruct((p, n), F32),
        compiler_params=_cparams(2), name="moe_down",
    )(*table, gs, w2)


def _combine_kernel(dest_ref, os_hbm, y_ref, gates_ref, mod_ref, g_ref, *rest,
                    rows, gate_row, n_ctx_tiles, final):
    if final:
        out_a, out_b, buf, sem = rest
    else:
        nmod_ref, out_a, out_b, buf, sem = rest
    i = pl.program_id(0)
    n_tiles = pl.num_programs(0)
    slot = i % DMA_RING
    ahead = DMA_RING - 1

    def issue(tile, to_slot):
        base = tile * rows

        def body(r, carry):
            for k in range(TOP_K):
                row = dest_ref[TOP_K * (base + r) + k]
                pltpu.make_async_copy(os_hbm.at[pl.ds(row, 1), :], buf.at[to_slot, k, pl.ds(r, 1), :],
                                      sem.at[to_slot]).start(priority=k)
            return carry

        lax.fori_loop(0, rows, body, 0, unroll=4)

    @pl.when(i == 0)
    def _():
        for first in range(ahead):
            @pl.when(first < n_tiles)
            def _():
                issue(first, first % DMA_RING)

    @pl.when(i + ahead < n_tiles)
    def _():
        issue(i + ahead, (i + ahead) % DMA_RING)

    for k in range(TOP_K):
        pltpu.make_async_copy(os_hbm.at[pl.ds(0, rows), :], buf.at[slot, k], sem.at[slot]).wait()
    gates = gates_ref[...]
    f = gates[:, 0:1] * buf[slot, 0] + gates[:, 1:2] * buf[slot, 1]
    y = y_ref[...] + mod_ref[gate_row:gate_row + 1, :] * f
    if final:
        ms = jnp.mean(y * y, axis=-1, keepdims=True)
        n = (y * lax.rsqrt(ms + EPS)) * g_ref[...]

        @pl.when(i < n_ctx_tiles)
        def _():
            out_a[...] = n

        @pl.when(i >= n_ctx_tiles)
        def _():
            out_b[...] = n
    else:
        out_a[...] = y
        out_b[...] = _norm_mod_value(y, g_ref[...], nmod_ref, 0).astype(out_b.dtype)


def _combine(os_, dest, gates, y, mod, layer, gate_row, next_g, final, n_ctx, dec_seq, tm):
    t, d = y.shape
    nct = n_ctx // tm

    def mod_spec(which):
        return pl.BlockSpec((None, None, 6, d),
                            lambda i, dst: (which, _mod_row(i, tm, n_ctx, dec_seq), 0, 0))

    in_specs = [pl.BlockSpec(memory_space=pl.ANY),
                pl.BlockSpec((tm, d), lambda i, dst: (i, 0)),
                pl.BlockSpec((tm, LANES), lambda i, dst: (i, 0)),
                mod_spec(layer),
                pl.BlockSpec((1, d), lambda i, dst: (0, 0))]
    args = [dest, os_, y, gates, mod, next_g.reshape(1, d)]
    if final:
        out_specs = [pl.BlockSpec((tm, d), lambda i, dst: (jnp.minimum(i, nct - 1), 0)),
                     pl.BlockSpec((tm, d), lambda i, dst: (jnp.maximum(i - nct, 0), 0))]
        out_shape = [jax.ShapeDtypeStruct((n_ctx, d), F32), jax.ShapeDtypeStruct((t - n_ctx, d), F32)]
    else:
        in_specs.append(mod_spec(layer + 1))
        args.append(mod)
        out_specs = [pl.BlockSpec((tm, d), lambda i, dst: (i, 0)),
                     pl.BlockSpec((tm, d), lambda i, dst: (i, 0))]
        out_shape = [jax.ShapeDtypeStruct((t, d), F32), jax.ShapeDtypeStruct((t, d), BF16)]
    return pl.pallas_call(
        functools.partial(_combine_kernel, rows=tm, gate_row=gate_row, n_ctx_tiles=nct, final=final),
        grid_spec=pltpu.PrefetchScalarGridSpec(
            num_scalar_prefetch=1,
            grid=(t // tm,),
            in_specs=in_specs,
            out_specs=out_specs,
            scratch_shapes=[pltpu.VMEM((DMA_RING, TOP_K, tm, d), F32),
                            pltpu.SemaphoreType.DMA((DMA_RING,))]),
        out_shape=out_shape,
        compiler_params=_cparams(1), name="moe_combine_final" if final else "moe_combine",
    )(*args)


def kernel(x_prompt, x_sample, state_rglru, c, c_ctx, norm_mix_g, norm_ffn_g, w_mod, b_mod, final_norm_g, rg_w_in, rg_conv_w, rg_conv_b, rg_w_a, rg_b_a, rg_w_x, rg_b_x, rg_lam, rg_w_out, sg_w_in, sg_norm_g, sg_w_s, sg_b_s, sg_w_out, sc_w_in, sc_conv_w, sc_w_out, ff_w1, ff_w3, ff_w2, moe_router, moe_router_b, moe_w1, moe_w3, moe_w2):
    batch, seq, d = x_prompt.shape
    dec_batch, dec_seq, _ = x_sample.shape
    depth = w_mod.shape[0]
    n_ctx = batch * seq
    d_rnn = rg_w_out.shape[1]
    n_exp = moe_router.shape[2]
    chunk = sg_w_s.shape[2]

    tm = dec_seq
    assert dec_seq % seq == 0 and n_ctx % tm == 0 and seq % SUBLANES == 0
    tm_half = max(tm // 2, chunk)
    tm_small = max(tm // 4, chunk)
    assert tm % tm_half == 0 and tm % tm_small == 0 and tm_small % chunk == 0
    moe_bm, moe_sub = tm, tm_half

    n_cond = 1 + dec_batch
    cond = jnp.zeros((-(-n_cond // SUBLANES) * SUBLANES, d), F32)
    cond = cond.at[0].set(c_ctx).at[1:n_cond].set(c)
    mod = _adaln(cond, w_mod, b_mod)[:, :n_cond].reshape(depth, n_cond, 6, d)

    y, h = _embed_norm(x_prompt.reshape(n_ctx, d), x_sample.reshape(dec_batch * dec_seq, d),
                       _grid_pos_embed(dec_seq, d), norm_mix_g[0], mod, tm_small)

    states = []
    outs = None
    for i in range(depth):
        kind, j = i % 3, i // 3
        if h is None:
            h = _norm_mod(y, norm_mix_g[i], mod, i, 0, n_ctx, dec_seq, tm_half)
        if kind == 0:
            z = _mm_act(h, rg_w_in, j, None, F32, tm, tn_pref=1024)
            h0 = jnp.concatenate([jnp.zeros((n_ctx // tm, 2, d_rnn), F32),
                                  state_rglru[:, j].astype(F32)], axis=0)
            mix, st = _rg_core(z, rg_conv_w[j], rg_conv_b[j], rg_w_a[j], rg_b_a[j], rg_w_x[j], rg_b_x[j],
                               rg_lam[j], h0, n_ctx, seq, tm)
            nseg = tm // seq
            st = st[:n_ctx // tm].reshape(n_ctx // tm, 2, nseg, d_rnn)
            states.append(jnp.transpose(st, (0, 2, 1, 3)).reshape(batch, 2, d_rnn))
        elif kind == 1:
            z = _mm_act(h, sg_w_in, j, "gelu", BF16, tm, tn_pref=1024)
            mix = _sgu_core(z, sg_norm_g[j], sg_w_s[j], sg_b_s[j], tm_small)
        else:
            mix = _mm_sconv(h, sc_w_in, sc_conv_w, j, n_ctx, seq, tm)
        w_out = (rg_w_out, sg_w_out, sc_w_out)[kind]
        y = _mm_residual(mix, w_out, j, y, mod, i, 2, n_ctx, dec_seq, tm if mix.shape[1] <= 4096 else tm_half)

        f = i // 2
        last = i == depth - 1
        if i % 2 == 0:
            h = _norm_mod(y, norm_ffn_g[i], mod, i, 3, n_ctx, dec_seq, tm_half)
            g = _mm_swiglu(h, ff_w1, ff_w3, f, tm)
            y = _mm_residual(g, ff_w2, f, y, mod, i, 5, n_ctx, dec_seq, tm_half)
            h = None
        else:
            h_tok, ids, gates = _norm_router(y, norm_ffn_g[i], mod, i, 3, moe_router[f], moe_router_b[f],
                                             n_ctx, dec_seq, tm_half)
            dest, src, block_table = _moe_plan(ids, n_exp, moe_bm)
            big, small = block_table(moe_bm), block_table(moe_sub)
            xs = _gather_rows(h_tok, src, small[1], moe_sub, d)
            gs = _grouped_swiglu(xs, moe_w1, moe_w3, f, big, moe_bm)
            os_ = _grouped_down(gs, moe_w2, f, big, moe_bm)
            if last:
                outs = _combine(os_, dest, gates, y, mod, i, 5, final_norm_g, True, n_ctx, dec_seq, tm_small)
            else:
                y, h = _combine(os_, dest, gates, y, mod, i, 5, norm_mix_g[i + 1], False,
                                n_ctx, dec_seq, tm_small)

    y_p, y_s = outs if outs is not None else _final_norm(y, final_norm_g, n_ctx, tm_half)
    new_state = jnp.stack(states, axis=1).astype(x_prompt.dtype)
    return (y_p.reshape(batch, seq, d), y_s.reshape(dec_batch, dec_seq, d), new_state)
```

```python
import functools

import jax
import jax.numpy as jnp
from jax import lax
from jax.experimental import pallas as pl
from jax.experimental.pallas import tpu as pltpu

F32 = jnp.float32
BF16 = jnp.bfloat16

GRID_W = 64
EPS = 1e-6
RG_C = 8.0
TOP_K = 2

LANES = 128
SUBLANES = 8
VMEM_LIMIT_BYTES = 58 * 1024 * 1024
NEG_BIG = -1e30
MXU_ROWS_PER_PASS = 256
RG_CHUNK_ROWS = 128
DMA_RING = 3


def _cparams(n_axes):
    return pltpu.CompilerParams(dimension_semantics=("arbitrary",) * n_axes,
                                vmem_limit_bytes=VMEM_LIMIT_BYTES)


def _pick_tile(n, pref):
    if n <= pref:
        return n
    t = (pref // LANES) * LANES
    while t > LANES and n % t:
        t -= LANES
    assert n % t == 0, (n, pref)
    return t


def _row_passes(rows):
    step = MXU_ROWS_PER_PASS if rows % MXU_ROWS_PER_PASS == 0 else rows
    return [pl.ds(r, step) for r in range(0, rows, step)]


def _sigmoid(x):
    return 1.0 / (1.0 + jnp.exp(-x))


def _gelu_tanh(x):
    c = 0.7978845608028654
    return 0.5 * x * (1.0 + jnp.tanh(c * (x + 0.044715 * (x * x * x))))


def _mod_row(i, tm, n_ctx, dec_seq):
    start = i * tm
    return jnp.where(start < n_ctx, 0, 1 + (start - n_ctx) // dec_seq)


def _adaln_kernel(c_ref, w_ref, b_ref, o_ref):
    c = c_ref[...]
    s = (c * _sigmoid(c)).astype(BF16)
    o_ref[...] = jnp.dot(s, w_ref[...].astype(BF16), preferred_element_type=F32) + b_ref[...]


def _adaln(cond, w_mod, b_mod):
    depth, d, n = w_mod.shape
    mc = cond.shape[0]
    tn = _pick_tile(n, 1024)
    return pl.pallas_call(
        _adaln_kernel,
        grid=(depth, n // tn),
        in_specs=[pl.BlockSpec((mc, d), lambda l, j: (0, 0)),
                  pl.BlockSpec((None, d, tn), lambda l, j: (l, 0, j)),
                  pl.BlockSpec((None, 1, tn), lambda l, j: (l, 0, j))],
        out_specs=pl.BlockSpec((None, mc, tn), lambda l, j: (l, 0, j)),
        out_shape=jax.ShapeDtypeStruct((depth, mc, n), F32),
        compiler_params=_cparams(2), name="adaln",
    )(cond, w_mod, b_mod.reshape(depth, 1, n))


def _embed_norm_kernel(xp_ref, xs_ref, pos_ref, g_ref, mod_ref, y_ref, h_ref, *, n_ctx_tiles):
    i = pl.program_id(0)

    @pl.when(i < n_ctx_tiles)
    def _():
        y_ref[...] = xp_ref[...]

    @pl.when(i >= n_ctx_tiles)
    def _():
        y_ref[...] = xs_ref[...] + pos_ref[...]

    h_ref[...] = _norm_mod_value(y_ref[...], g_ref[...], mod_ref, 0).astype(h_ref.dtype)


def _embed_norm(xp, xs, pos, g, mod, tm):
    n_ctx, d = xp.shape
    n_dec = xs.shape[0]
    dec_seq = pos.shape[0]
    nct = n_ctx // tm
    ppt = dec_seq // tm
    t = n_ctx + n_dec
    return pl.pallas_call(
        functools.partial(_embed_norm_kernel, n_ctx_tiles=nct),
        grid=(t // tm,),
        in_specs=[pl.BlockSpec((tm, d), lambda i: (jnp.minimum(i, nct - 1), 0)),
                  pl.BlockSpec((tm, d), lambda i: (jnp.maximum(i - nct, 0), 0)),
                  pl.BlockSpec((tm, d), lambda i: (jnp.maximum(i - nct, 0) % ppt, 0)),
                  pl.BlockSpec((1, d), lambda i: (0, 0)),
                  pl.BlockSpec((None, None, 6, d), lambda i: (0, _mod_row(i, tm, n_ctx, dec_seq), 0, 0))],
        out_specs=[pl.BlockSpec((tm, d), lambda i: (i, 0)),
                   pl.BlockSpec((tm, d), lambda i: (i, 0))],
        out_shape=[jax.ShapeDtypeStruct((t, d), F32), jax.ShapeDtypeStruct((t, d), BF16)],
        compiler_params=_cparams(1), name="embed_norm",
    )(xp, xs, pos, g.reshape(1, d), mod)


def _grid_pos_embed(length, d):
    rows = length // GRID_W
    r = jnp.repeat(jnp.arange(rows), GRID_W)
    col = jnp.tile(jnp.arange(GRID_W), rows)
    quarter = d // 4
    omega = 1.0 / (10000.0 ** (jnp.arange(quarter, dtype=F32) / quarter))

    def emb(p):
        ang = p[:, None].astype(F32) * omega[None, :]
        return jnp.concatenate([jnp.sin(ang), jnp.cos(ang)], axis=-1)

    return jnp.concatenate([emb(r), emb(col)], axis=-1).astype(F32)


def _norm_mod_value(y, g, mod_ref, shift_row):
    ms = jnp.mean(y * y, axis=-1, keepdims=True)
    n = (y * lax.rsqrt(ms + EPS)) * g
    return n * (1.0 + mod_ref[shift_row + 1:shift_row + 2, :]) + mod_ref[shift_row:shift_row + 1, :]


def _norm_mod_kernel(y_ref, g_ref, mod_ref, h_ref, *, shift_row):
    h_ref[...] = _norm_mod_value(y_ref[...], g_ref[...], mod_ref, shift_row).astype(h_ref.dtype)


def _norm_mod(y, g, mod, layer, shift_row, n_ctx, dec_seq, tm):
    t, d = y.shape
    return pl.pallas_call(
        functools.partial(_norm_mod_kernel, shift_row=shift_row),
        grid=(t // tm,),
        in_specs=[pl.BlockSpec((tm, d), lambda i: (i, 0)),
                  pl.BlockSpec((1, d), lambda i: (0, 0)),
                  pl.BlockSpec((None, None, 6, d),
                               lambda i: (layer, _mod_row(i, tm, n_ctx, dec_seq), 0, 0))],
        out_specs=pl.BlockSpec((tm, d), lambda i: (i, 0)),
        out_shape=jax.ShapeDtypeStruct((t, d), BF16),
        compiler_params=_cparams(1), name="norm_mod",
    )(y, g.reshape(1, d), mod)


def _norm_router_kernel(y_ref, g_ref, mod_ref, r_ref, rb_ref, h_ref, ids_ref, gates_ref, *,
                        shift_row, n_slab):
    h = _norm_mod_value(y_ref[...], g_ref[...], mod_ref, shift_row)
    tm = h.shape[0]
    for s in range(n_slab):
        h_ref[pl.ds(s, tm, stride=n_slab), :] = h[:, s * LANES:(s + 1) * LANES]
    h_hi = h.astype(BF16)
    h_lo = (h - h_hi.astype(F32)).astype(BF16)
    p_hi = jnp.dot(h_hi, r_ref[...], preferred_element_type=F32)
    p_lo = jnp.dot(h_lo, r_ref[:, :LANES], preferred_element_type=F32)
    logits = ((p_hi[:, :LANES] + p_hi[:, LANES:]) + p_lo) + rb_ref[...]
    lane = lax.broadcasted_iota(jnp.int32, logits.shape, 1).astype(F32)
    big = float(LANES)
    m1 = jnp.max(logits, axis=-1, keepdims=True)
    i1 = jnp.min(jnp.where(logits == m1, lane, big), axis=-1, keepdims=True)
    l2 = jnp.where(lane == i1, 2.0 * NEG_BIG, logits)
    m2 = jnp.max(l2, axis=-1, keepdims=True)
    i2 = jnp.min(jnp.where(l2 == m2, lane, big), axis=-1, keepdims=True)
    e = jnp.exp(m2 - m1)
    g1 = 1.0 / (1.0 + e)
    g2 = e / (1.0 + e)
    ids_ref[...] = jnp.where(lane == 0.0, i1, jnp.where(lane == 1.0, i2, 0.0)).astype(jnp.int32)
    gates_ref[...] = jnp.where(lane == 0.0, g1, jnp.where(lane == 1.0, g2, 0.0))


def _norm_router(y, g, mod, layer, shift_row, router, router_b, n_ctx, dec_seq, tm):
    t, d = y.shape
    n_exp = router.shape[1]
    n_slab = d // LANES
    rp = jnp.zeros((d, LANES), F32).at[:, :n_exp].set(router)
    r_hi = rp.astype(BF16)
    r_lo = (rp - r_hi.astype(F32)).astype(BF16)
    rp = jnp.concatenate([r_hi, r_lo], axis=1)
    rbp = jnp.full((1, LANES), NEG_BIG, F32).at[0, :n_exp].set(router_b)
    return pl.pallas_call(
        functools.partial(_norm_router_kernel, shift_row=shift_row, n_slab=n_slab),
        grid=(t // tm,),
        in_specs=[pl.BlockSpec((tm, d), lambda i: (i, 0)),
                  pl.BlockSpec((1, d), lambda i: (0, 0)),
                  pl.BlockSpec((None, None, 6, d),
                               lambda i: (layer, _mod_row(i, tm, n_ctx, dec_seq), 0, 0)),
                  pl.BlockSpec((d, 2 * LANES), lambda i: (0, 0)),
                  pl.BlockSpec((1, LANES), lambda i: (0, 0))],
        out_specs=[pl.BlockSpec((tm * n_slab, LANES), lambda i: (i, 0)),
                   pl.BlockSpec((tm, LANES), lambda i: (i, 0)),
                   pl.BlockSpec((tm, LANES), lambda i: (i, 0))],
        out_shape=[jax.ShapeDtypeStruct((t * n_slab, LANES), F32),
                   jax.ShapeDtypeStruct((t, LANES), jnp.int32),
                   jax.ShapeDtypeStruct((t, LANES), F32)],
        compiler_params=_cparams(1), name="norm_router",
    )(y, g.reshape(1, d), mod, rp, rbp)


def _final_norm_kernel(y_ref, g_ref, op_ref, os_ref, *, n_ctx_tiles):
    i = pl.program_id(0)
    y = y_ref[...]
    ms = jnp.mean(y * y, axis=-1, keepdims=True)
    n = (y * lax.rsqrt(ms + EPS)) * g_ref[...]

    @pl.when(i < n_ctx_tiles)
    def _():
        op_ref[...] = n

    @pl.when(i >= n_ctx_tiles)
    def _():
        os_ref[...] = n


def _final_norm(y, g, n_ctx, tm):
    t, d = y.shape
    nct = n_ctx // tm
    return pl.pallas_call(
        functools.partial(_final_norm_kernel, n_ctx_tiles=nct),
        grid=(t // tm,),
        in_specs=[pl.BlockSpec((tm, d), lambda i: (i, 0)),
                  pl.BlockSpec((1, d), lambda i: (0, 0))],
        out_specs=[pl.BlockSpec((tm, d), lambda i: (jnp.minimum(i, nct - 1), 0)),
                   pl.BlockSpec((tm, d), lambda i: (jnp.maximum(i - nct, 0), 0))],
        out_shape=[jax.ShapeDtypeStruct((n_ctx, d), F32),
                   jax.ShapeDtypeStruct((t - n_ctx, d), F32)],
        compiler_params=_cparams(1), name="final_norm",
    )(y, g.reshape(1, d))


def _cast_at_first_row_tile(w_refs, wb_refs):
    @pl.when(pl.program_id(1) == 0)
    def _():
        for w, wb in zip(w_refs, wb_refs):
            wb[...] = w[...].astype(BF16)


def _mm_act_kernel(x_ref, w_ref, o_ref, wb, *, act):
    _cast_at_first_row_tile((w_ref,), (wb,))
    for rows in _row_passes(x_ref.shape[0]):
        acc = jnp.dot(x_ref[rows, :], wb[...], preferred_element_type=F32)
        if act == "gelu":
            acc = _gelu_tanh(acc)
        o_ref[rows, :] = acc.astype(o_ref.dtype)


def _mm_act(x, w, layer, act, out_dtype, tm, tn_pref=512):
    t, k = x.shape
    n = w.shape[2]
    tn = _pick_tile(n, tn_pref)
    return pl.pallas_call(
        functools.partial(_mm_act_kernel, act=act),
        grid=(n // tn, t // tm),
        in_specs=[pl.BlockSpec((tm, k), lambda j, i: (i, 0)),
                  pl.BlockSpec((None, k, tn), lambda j, i: (layer, 0, j))],
        out_specs=pl.BlockSpec((tm, tn), lambda j, i: (i, j)),
        out_shape=jax.ShapeDtypeStruct((t, n), out_dtype),
        scratch_shapes=[pltpu.VMEM((k, tn), BF16)],
        compiler_params=_cparams(2), name="mm_" + str(act),
    )(x, w)


def _swiglu_passes(x_ref, wb1, wb3, o_ref):
    for rows in _row_passes(x_ref.shape[0]):
        x = x_ref[rows, :]
        a = jnp.dot(x, wb1[...], preferred_element_type=F32)
        b = jnp.dot(x, wb3[...], preferred_element_type=F32)
        o_ref[rows, :] = ((a * _sigmoid(a)) * b).astype(o_ref.dtype)


def _mm_swiglu_kernel(x_ref, w1_ref, w3_ref, o_ref, wb1, wb3):
    _cast_at_first_row_tile((w1_ref, w3_ref), (wb1, wb3))
    _swiglu_passes(x_ref, wb1, wb3, o_ref)


def _mm_swiglu(x, w1, w3, layer, tm, tn_pref=512):
    t, k = x.shape
    n = w1.shape[2]
    tn = _pick_tile(n, tn_pref)
    return pl.pallas_call(
        _mm_swiglu_kernel,
        grid=(n // tn, t // tm),
        in_specs=[pl.BlockSpec((tm, k), lambda j, i: (i, 0)),
                  pl.BlockSpec((None, k, tn), lambda j, i: (layer, 0, j)),
                  pl.BlockSpec((None, k, tn), lambda j, i: (layer, 0, j))],
        out_specs=pl.BlockSpec((tm, tn), lambda j, i: (i, j)),
        out_shape=jax.ShapeDtypeStruct((t, n), BF16),
        scratch_shapes=[pltpu.VMEM((k, tn), BF16), pltpu.VMEM((k, tn), BF16)],
        compiler_params=_cparams(2), name="mm_swiglu",
    )(x, w1, w3)


def _mm_sconv_kernel(x_ref, wb_ref, wc_ref, wx_ref, cw_ref, o_ref, sb, sc, sx, *, n_ctx_tiles, seq_ctx):
    _cast_at_first_row_tile((wb_ref, wc_ref, wx_ref), (sb, sc, sx))
    i = pl.program_id(1)
    is_ctx = i < n_ctx_tiles
    x = x_ref[...]
    bg = jnp.dot(x, sb[...], preferred_element_type=F32)
    p = jnp.dot(x, sc[...], preferred_element_type=F32) * jnp.dot(x, sx[...], preferred_element_type=F32)
    tm = p.shape[0]
    row = lax.broadcasted_iota(jnp.int32, p.shape, 0)
    pos = jnp.where(is_ctx, row % seq_ctx, row)
    last_pos = jnp.where(is_ctx, seq_ctx - 1, tm - 1)
    p_prev = jnp.where(pos == 0, 0.0, pltpu.roll(p, 1, axis=0))
    p_next = jnp.where(pos == last_pos, 0.0, pltpu.roll(p, tm - 1, axis=0))
    cw = cw_ref[...]
    conv = (cw[0:1] * p_prev + cw[1:2] * p) + cw[2:3] * p_next
    o_ref[...] = (bg * conv).astype(o_ref.dtype)


def _mm_sconv(x, w_in, conv_w, layer, n_ctx, seq_ctx, tm, tn_pref=512):
    t, k = x.shape
    d = conv_w.shape[2]
    tn = _pick_tile(d, tn_pref)
    nj = d // tn
    return pl.pallas_call(
        functools.partial(_mm_sconv_kernel, n_ctx_tiles=n_ctx // tm, seq_ctx=seq_ctx),
        grid=(nj, t // tm),
        in_specs=[pl.BlockSpec((tm, k), lambda j, i: (i, 0)),
                  pl.BlockSpec((None, k, tn), lambda j, i: (layer, 0, j)),
                  pl.BlockSpec((None, k, tn), lambda j, i: (layer, 0, nj + j)),
                  pl.BlockSpec((None, k, tn), lambda j, i: (layer, 0, 2 * nj + j)),
                  pl.BlockSpec((None, conv_w.shape[1], tn), lambda j, i: (layer, 0, j))],
        out_specs=pl.BlockSpec((tm, tn), lambda j, i: (i, j)),
        out_shape=jax.ShapeDtypeStruct((t, d), BF16),
        scratch_shapes=[pltpu.VMEM((k, tn), BF16)] * 3,
        compiler_params=_cparams(2), name="mm_sconv",
    )(x, w_in, w_in, w_in, conv_w)


def _mm_residual_kernel(x_ref, w_ref, y_ref, mod_ref, o_ref, wb, *, gate_row):
    _cast_at_first_row_tile((w_ref,), (wb,))
    gate = mod_ref[gate_row:gate_row + 1, :]
    for rows in _row_passes(x_ref.shape[0]):
        acc = jnp.dot(x_ref[rows, :], wb[...], preferred_element_type=F32)
        o_ref[rows, :] = y_ref[rows, :] + gate * acc


def _mm_residual(x, w, w_layer, y, mod, layer, gate_row, n_ctx, dec_seq, tm, tn_pref=512):
    t, k = x.shape
    n = w.shape[2]
    tn = _pick_tile(n, tn_pref)
    return pl.pallas_call(
        functools.partial(_mm_residual_kernel, gate_row=gate_row),
        grid=(n // tn, t // tm),
        in_specs=[pl.BlockSpec((tm, k), lambda j, i: (i, 0)),
                  pl.BlockSpec((None, k, tn), lambda j, i: (w_layer, 0, j)),
                  pl.BlockSpec((tm, tn), lambda j, i: (i, j)),
                  pl.BlockSpec((None, None, 6, tn),
                               lambda j, i: (layer, _mod_row(i, tm, n_ctx, dec_seq), 0, j))],
        out_specs=pl.BlockSpec((tm, tn), lambda j, i: (i, j)),
        out_shape=jax.ShapeDtypeStruct((t, n), F32),
        scratch_shapes=[pltpu.VMEM((k, tn), BF16)],
        compiler_params=_cparams(2), name="mm_residual",
    )(x, w, y, mod)


def _rg_core_kernel(gate_ref, x_ref, cw_ref, cb_ref, wg_ref, bg_ref, lam_ref, h0_ref,
                    y_ref, st_ref, af_s, bf_s, ab_s, bb_s, *, n_ctx_tiles, seg, tm, cg):
    i = pl.program_id(1)
    is_ctx = i < n_ctx_tiles
    nseg = tm // seg
    nslab = cg // LANES
    lsub = tm // SUBLANES
    sub_per_seq = seg // lsub
    cw = cw_ref[...]
    cb = cb_ref[...]
    lam = lam_ref[...]
    softplus_neg_lam = jnp.maximum(-lam, 0.0) + jnp.log1p(jnp.exp(-jnp.abs(lam)))
    neg_c_sp = (-RG_C) * softplus_neg_lam
    zeros8 = jnp.zeros((SUBLANES, cg), F32)

    cr = min(RG_CHUNK_ROWS, lsub)
    for c0 in range(0, tm, cr):
        cur = x_ref[pl.ds(c0, cr), :]
        if c0 == 0:
            prev8 = zeros8
        else:
            prev8 = x_ref[pl.ds(c0 - SUBLANES, SUBLANES), :]
            if c0 % seg == 0:
                prev8 = jnp.where(is_ctx, 0.0, prev8)
        if c0 + cr == tm:
            next8 = zeros8
        else:
            next8 = x_ref[pl.ds(c0 + cr, SUBLANES), :]
            if (c0 + cr) % seg == 0:
                next8 = jnp.where(is_ctx, 0.0, next8)
        head = jnp.concatenate([prev8, cur[:2 * SUBLANES]], axis=0)
        tail = jnp.concatenate([cur[cr - 2 * SUBLANES:], next8], axis=0)

        def shifted(delta):
            lo = SUBLANES + delta
            mid = x_ref[pl.ds(c0 + lo, cr - 2 * SUBLANES), :]
            return jnp.concatenate([head[lo:lo + SUBLANES], mid, tail[lo:lo + SUBLANES]], axis=0)

        xc = (((cw[0:1] * shifted(-2) + cw[1:2] * shifted(-1)) + cw[2:3] * cur)
              + cw[3:4] * shifted(1)) + cb
        g = jnp.dot(xc.astype(BF16), wg_ref[...], preferred_element_type=F32) + bg_ref[...]
        for d, (a_s, b_s) in enumerate(((af_s, bf_s), (ab_s, bb_s))):
            r = _sigmoid(g[:, (2 * d) * cg:(2 * d + 1) * cg])
            ig = _sigmoid(g[:, (2 * d + 1) * cg:(2 * d + 2) * cg])
            log_a = neg_c_sp[d:d + 1] * r
            a = jnp.exp(log_a)
            one_minus_a2 = -jnp.tanh(log_a) * (a * a + 1.0)
            root = jnp.where(one_minus_a2 > 0.0, one_minus_a2 * lax.rsqrt(one_minus_a2), 0.0)
            bt = root * (ig * xc)
            k, j0 = c0 // lsub, c0 % lsub
            dst = pl.ds(k + SUBLANES * j0, cr, stride=SUBLANES)
            for l in range(nslab):
                a_s[l, dst, :] = a[:, l * LANES:(l + 1) * LANES]
                b_s[l, dst, :] = bt[:, l * LANES:(l + 1) * LANES]

    def local_scan(j, carry):
        hf, pf, hb, pb = carry
        rf = pl.multiple_of(j * SUBLANES, SUBLANES)
        rb = pl.multiple_of((lsub - 1 - j) * SUBLANES, SUBLANES)
        nhf, npf, nhb, npb = [], [], [], []
        for l in range(nslab):
            a = af_s[l, pl.ds(rf, SUBLANES), :]
            h = a * hf[l] + bf_s[l, pl.ds(rf, SUBLANES), :]
            p = a * pf[l]
            bf_s[l, pl.ds(rf, SUBLANES), :] = h
            af_s[l, pl.ds(rf, SUBLANES), :] = p
            nhf.append(h)
            npf.append(p)
            a = ab_s[l, pl.ds(rb, SUBLANES), :]
            h = a * hb[l] + bb_s[l, pl.ds(rb, SUBLANES), :]
            p = a * pb[l]
            bb_s[l, pl.ds(rb, SUBLANES), :] = h
            ab_s[l, pl.ds(rb, SUBLANES), :] = p
            nhb.append(h)
            npb.append(p)
        return tuple(nhf), tuple(npf), tuple(nhb), tuple(npb)

    zero = tuple(jnp.zeros((SUBLANES, LANES), F32) for _ in range(nslab))
    one = tuple(jnp.ones((SUBLANES, LANES), F32) for _ in range(nslab))
    hf_end, pf_end, hb_end, pb_end = lax.fori_loop(0, lsub, local_scan, (zero, one, zero, one))

    row8 = lax.broadcasted_iota(jnp.int32, (SUBLANES, LANES), 0)
    h0 = h0_ref[...]
    for l in range(nslab):
        lanes = slice(l * LANES, (l + 1) * LANES)
        ent_f = jnp.zeros((SUBLANES, LANES), F32)
        h_in = h0[0:1, lanes]
        for k in range(SUBLANES):
            if k > 0 and k % sub_per_seq == 0:
                h_in = jnp.where(is_ctx, 0.0, h_in)
            ent_f = jnp.where(row8 == k, h_in, ent_f)
            h_in = hf_end[l][k:k + 1] + pf_end[l][k:k + 1] * h_in
            if (k + 1) % sub_per_seq == 0:
                q = k // sub_per_seq
                st_ref[q:q + 1, lanes] = h_in
        ent_b = jnp.zeros((SUBLANES, LANES), F32)
        h_in = h0[1:2, lanes]
        for k in reversed(range(SUBLANES)):
            if k < SUBLANES - 1 and (k + 1) % sub_per_seq == 0:
                h_in = jnp.where(is_ctx, 0.0, h_in)
            ent_b = jnp.where(row8 == k, h_in, ent_b)
            h_in = hb_end[l][k:k + 1] + pb_end[l][k:k + 1] * h_in
            if k % sub_per_seq == 0:
                q = k // sub_per_seq
                st_ref[nseg + q:nseg + q + 1, lanes] = h_in
        ch = min(tm, 256)
        ef = jnp.concatenate([ent_f] * (ch // SUBLANES), axis=0)
        eb = jnp.concatenate([ent_b] * (ch // SUBLANES), axis=0)
        for r0 in range(0, tm, ch):
            rows = pl.ds(r0, ch)
            bf_s[l, rows, :] = ((bf_s[l, rows, :] + af_s[l, rows, :] * ef)
                                + (bb_s[l, rows, :] + ab_s[l, rows, :] * eb))

    for k in range(SUBLANES):
        rows = pl.ds(k * lsub, lsub)
        hsum = jnp.concatenate([bf_s[l, pl.ds(k, lsub, stride=SUBLANES), :] for l in range(nslab)], axis=1)
        y_ref[rows, :] = (hsum * _gelu_tanh(gate_ref[rows, :])).astype(y_ref.dtype)


def _rg_core(z, conv_w, conv_b, w_a, b_a, w_x, b_x, lam, h0, n_ctx, seq_ctx, tm):
    t = z.shape[0]
    r = conv_w.shape[1]
    heads, hw = w_a.shape[1], w_a.shape[2]
    hpg = 4
    while (hpg * hw) % LANES:
        hpg *= 2
    ng = heads // hpg
    cg = hpg * hw
    nseg = tm // seq_ctx
    ntiles = t // tm
    assert tm % SUBLANES == 0 and seq_ctx % (tm // SUBLANES) == 0

    def blockdiag(w):
        rows = jnp.tile(w.reshape(2, ng, cg, hw), (1, 1, 1, hpg))
        head = jnp.arange(cg, dtype=jnp.int32) // hw
        return jnp.where(head[:, None] == head[None, :], rows, 0.0)

    wa, wx = blockdiag(w_a), blockdiag(w_x)
    wg = jnp.concatenate([wa[0], wx[0], wa[1], wx[1]], axis=-1).astype(BF16)
    ba, bx = b_a.reshape(2, ng, 1, cg), b_x.reshape(2, ng, 1, cg)
    bg = jnp.concatenate([ba[0], bx[0], ba[1], bx[1]], axis=-1)

    return pl.pallas_call(
        functools.partial(_rg_core_kernel, n_ctx_tiles=n_ctx // tm, seg=seq_ctx, tm=tm, cg=cg),
        grid=(ng, ntiles),
        in_specs=[pl.BlockSpec((tm, cg), lambda g, i: (i, g)),
                  pl.BlockSpec((tm, cg), lambda g, i: (i, ng + g)),
                  pl.BlockSpec((conv_w.shape[0], cg), lambda g, i: (0, g)),
                  pl.BlockSpec((1, cg), lambda g, i: (0, g)),
                  pl.BlockSpec((None, cg, 4 * cg), lambda g, i: (g, 0, 0)),
                  pl.BlockSpec((None, 1, 4 * cg), lambda g, i: (g, 0, 0)),
                  pl.BlockSpec((2, cg), lambda g, i: (0, g)),
                  pl.BlockSpec((None, 2, cg), lambda g, i: (i, 0, g))],
        out_specs=[pl.BlockSpec((tm, cg), lambda g, i: (i, g)),
                   pl.BlockSpec((None, 2 * nseg, cg), lambda g, i: (i, 0, g))],
        out_shape=[jax.ShapeDtypeStruct((t, r), BF16),
                   jax.ShapeDtypeStruct((ntiles, 2 * nseg, r), F32)],
        scratch_shapes=[pltpu.VMEM((cg // LANES, tm, LANES), F32)] * 4,
        compiler_params=_cparams(2), name="rg_core",
    )(z, z, conv_w, conv_b.reshape(1, r), wg, bg, lam, h0)


def _sgu_core_kernel(u_ref, v_ref, g_ref, ws_ref, bs_ref, y_ref, vn_s, *, chunk, gw, ngroups, tm):
    v = v_ref[...].astype(F32)
    ms = jnp.mean(v * v, axis=-1, keepdims=True)
    vn_s[...] = ((v * lax.rsqrt(ms + EPS)) * g_ref[...]).astype(BF16)
    for c in range(tm // chunk):
        rows = pl.ds(c * chunk, chunk)
        for g in range(ngroups):
            cols = pl.ds(g * gw, gw)
            vm = jnp.dot(ws_ref[g], vn_s[rows, cols], preferred_element_type=F32) + bs_ref[:, g:g + 1]
            y_ref[rows, cols] = (u_ref[rows, cols].astype(F32) * vm).astype(y_ref.dtype)


def _sgu_core(z, norm_g, w_s, b_s, tm):
    t = z.shape[0]
    w = norm_g.shape[0]
    ngroups, chunk = w_s.shape[0], w_s.shape[1]
    gw = w // ngroups
    return pl.pallas_call(
        functools.partial(_sgu_core_kernel, chunk=chunk, gw=gw, ngroups=ngroups, tm=tm),
        grid=(t // tm,),
        in_specs=[pl.BlockSpec((tm, w), lambda i: (i, 0)),
                  pl.BlockSpec((tm, w), lambda i: (i, 1)),
                  pl.BlockSpec((1, w), lambda i: (0, 0)),
                  pl.BlockSpec((ngroups, chunk, chunk), lambda i: (0, 0, 0)),
                  pl.BlockSpec((chunk, ngroups), lambda i: (0, 0))],
        out_specs=pl.BlockSpec((tm, w), lambda i: (i, 0)),
        out_shape=jax.ShapeDtypeStruct((t, w), BF16),
        scratch_shapes=[pltpu.VMEM((tm, w), BF16)],
        compiler_params=_cparams(1), name="sgu_core",
    )(z, z, norm_g.reshape(1, w), w_s.astype(BF16), b_s.T)


def _moe_plan(ids, n_exp, bm):
    t = ids.shape[0]
    e = ids[:, :TOP_K].reshape(-1)
    oh = (e[:, None] == jnp.arange(n_exp, dtype=jnp.int32)[None, :]).astype(jnp.int32)
    csum = jnp.cumsum(oh, axis=0)
    rank = jnp.sum((csum - oh) * oh, axis=1)
    counts = csum[-1]
    padded = ((counts + bm - 1) // bm) * bm
    ends = jnp.cumsum(padded)
    starts = ends - padded
    first_cnt = counts - jnp.maximum(padded - bm, 0)
    start_e = jnp.sum(starts[None, :] * oh, axis=1)
    first_e = jnp.sum(first_cnt[None, :] * oh, axis=1)
    dest = (start_e + rank + jnp.where(rank >= first_e, bm - first_e, 0)).astype(jnp.int32)
    p = TOP_K * t + n_exp * bm
    src = jnp.zeros((p,), jnp.int32).at[dest].set(jnp.arange(TOP_K * t, dtype=jnp.int32) // TOP_K)

    def block_table(blk):
        idx = jnp.arange(p // blk, dtype=jnp.int32)
        start = idx * blk
        exp = jnp.minimum(jnp.sum((start[:, None] >= ends[None, :]).astype(jnp.int32), axis=1), n_exp - 1)
        offset = start - starts[exp]
        valid = jnp.where(offset < bm, jnp.clip(first_cnt[exp] - offset, 0, blk), blk)
        valid = jnp.where(start < ends[-1], valid, 0)
        fetch = lax.cummax(jnp.where(valid > 0, idx, 0), axis=0)
        return exp.astype(jnp.int32), valid.astype(jnp.int32), fetch.astype(jnp.int32)

    return dest, src, block_table


def _gather_rows_kernel(src_ref, valid_ref, h_hbm, o_ref, buf, sem, *, rows, n_slab):
    b = pl.program_id(0)
    nb = pl.num_programs(0)
    slot = b % DMA_RING
    ahead = DMA_RING - 1

    def block_has_rows(blk):
        return jnp.logical_and(blk < nb, valid_ref[jnp.minimum(blk, nb - 1)] > 0)

    has_rows = block_has_rows(b)

    def issue(blk, to_slot):
        base = blk * rows

        def body(q, carry):
            for prio in range(2):
                r = 2 * q + prio
                tok = pl.multiple_of(src_ref[base + r] * n_slab, n_slab)
                pltpu.make_async_copy(h_hbm.at[pl.ds(tok, n_slab), :],
                                      buf.at[to_slot, pl.ds(pl.multiple_of(r * n_slab, n_slab), n_slab), :],
                                      sem.at[to_slot]).start(priority=prio)
            return carry

        lax.fori_loop(0, rows // 2, body, 0, unroll=4)

    @pl.when(b == 0)
    def _():
        for first in range(ahead):
            @pl.when(block_has_rows(first))
            def _():
                issue(first, first % DMA_RING)

    @pl.when(block_has_rows(b + ahead))
    def _():
        issue(b + ahead, (b + ahead) % DMA_RING)

    @pl.when(has_rows)
    def _():
        pltpu.make_async_copy(h_hbm.at[pl.ds(0, rows * n_slab), :], buf.at[slot], sem.at[slot]).wait()
        for s in range(n_slab):
            o_ref[:, s * LANES:(s + 1) * LANES] = buf[slot, pl.ds(s, rows, stride=n_slab), :].astype(o_ref.dtype)

    @pl.when(jnp.logical_not(has_rows))
    def _():
        o_ref[...] = jnp.zeros_like(o_ref)


def _gather_rows(h_tok, src, valid, bm, d):
    n_slab = d // LANES
    p = src.shape[0]
    return pl.pallas_call(
        functools.partial(_gather_rows_kernel, rows=bm, n_slab=n_slab),
        grid_spec=pltpu.PrefetchScalarGridSpec(
            num_scalar_prefetch=2,
            grid=(p // bm,),
            in_specs=[pl.BlockSpec(memory_space=pl.ANY)],
            out_specs=pl.BlockSpec((bm, d), lambda b, src, va: (b, 0)),
            scratch_shapes=[pltpu.VMEM((DMA_RING, bm * n_slab, LANES), F32),
                            pltpu.SemaphoreType.DMA((DMA_RING,))]),
        out_shape=jax.ShapeDtypeStruct((p, d), BF16),
        compiler_params=_cparams(1), name="moe_gather",
    )(src, valid, h_tok)


def _expert_changed(exp_ref, fetch_ref, b):
    prev = fetch_ref[jnp.maximum(b - 1, 0)]
    return jnp.logical_or(b == 0, exp_ref[b] != exp_ref[prev])


def _guarded_passes(valid, x_ref, o_ref, one_pass):
    bm = x_ref.shape[0]
    passes = _row_passes(bm)

    @pl.when(valid == bm)
    def _():
        for rows in passes:
            one_pass(rows)

    @pl.when(valid < bm)
    def _():
        for n, rows in enumerate(passes):
            first_row = n * (bm // len(passes))

            @pl.when(first_row < valid)
            def _():
                one_pass(rows)

            @pl.when(first_row >= valid)
            def _():
                o_ref[rows, :] = jnp.zeros((bm // len(passes), o_ref.shape[1]), o_ref.dtype)


def _grouped_swiglu_kernel(exp_ref, valid_ref, fetch_ref, x_ref, w1_ref, w3_ref, o_ref, wb1, wb3):
    b = pl.program_id(1)
    valid = valid_ref[b]

    @pl.when(jnp.logical_and(valid > 0, _expert_changed(exp_ref, fetch_ref, b)))
    def _():
        wb1[...] = w1_ref[...].astype(BF16)
        wb3[...] = w3_ref[...].astype(BF16)

    def one_pass(rows):
        x = x_ref[rows, :]
        a = jnp.dot(x, wb1[...], preferred_element_type=F32)
        c = jnp.dot(x, wb3[...], preferred_element_type=F32)
        o_ref[rows, :] = ((a * _sigmoid(a)) * c).astype(o_ref.dtype)

    _guarded_passes(valid, x_ref, o_ref, one_pass)


def _grouped_swiglu(xs, w1, w3, layer, table, bm, tn_pref=512):
    p, k = xs.shape
    n = w1.shape[3]
    tn = _pick_tile(n, tn_pref)
    return pl.pallas_call(
        _grouped_swiglu_kernel,
        grid_spec=pltpu.PrefetchScalarGridSpec(
            num_scalar_prefetch=3,
            grid=(n // tn, p // bm),
            in_specs=[pl.BlockSpec((bm, k), lambda j, b, ex, va, fe: (fe[b], 0)),
                      pl.BlockSpec((None, None, k, tn), lambda j, b, ex, va, fe: (layer, ex[fe[b]], 0, j)),
                      pl.BlockSpec((None, None, k, tn), lambda j, b, ex, va, fe: (layer, ex[fe[b]], 0, j))],
            out_specs=pl.BlockSpec((bm, tn), lambda j, b, ex, va, fe: (b, j)),
            scratch_shapes=[pltpu.VMEM((k, tn), BF16), pltpu.VMEM((k, tn), BF16)]),
        out_shape=jax.ShapeDtypeStruct((p, n), BF16),
        compiler_params=_cparams(2), name="moe_swiglu",
    )(*table, xs, w1, w3)


def _grouped_down_kernel(exp_ref, valid_ref, fetch_ref, x_ref, w_ref, o_ref, wb):
    b = pl.program_id(1)
    valid = valid_ref[b]

    @pl.when(jnp.logical_and(valid > 0, _expert_changed(exp_ref, fetch_ref, b)))
    def _():
        wb[...] = w_ref[...].astype(BF16)

    def one_pass(rows):
        o_ref[rows, :] = jnp.dot(x_ref[rows, :], wb[...], preferred_element_type=F32)

    _guarded_passes(valid, x_ref, o_ref, one_pass)


def _grouped_down(gs, w2, layer, table, bm, tn_pref=512):
    p, k = gs.shape
    n = w2.shape[3]
    tn = _pick_tile(n, tn_pref)
    return pl.pallas_call(
        _grouped_down_kernel,
        grid_spec=pltpu.PrefetchScalarGridSpec(
            num_scalar_prefetch=3,
            grid=(n // tn, p // bm),
            in_specs=[pl.BlockSpec((bm, k), lambda j, b, ex, va, fe: (fe[b], 0)),
                      pl.BlockSpec((None, None, k, tn), lambda j, b, ex, va, fe: (layer, ex[fe[b]], 0, j))],
            out_specs=pl.BlockSpec((bm, tn), lambda j, b, ex, va, fe: (b, j)),
            scratch_shapes=[pltpu.VMEM((k, tn), BF16)]),
        out_shape=jax.ShapeDtypeStruct((p, n), F32),
        compiler_params=_cparams(2), name="moe_down",
    )(*table, gs, w2)


def _combine_kernel(dest_ref, os_hbm, y_ref, gates_ref, mod_ref, g_ref, *rest,
                    rows, gate_row, n_ctx_tiles, final):
    if final:
        out_a, out_b, buf, sem = rest
    else:
        nmod_ref, out_a, out_b, buf, sem = rest
    i = pl.program_id(0)
    n_tiles = pl.num_programs(0)
    slot = i % DMA_RING
    ahead = DMA_RING - 1

    def issue(tile, to_slot):
        base = tile * rows

        def body(r, carry):
            for k in range(TOP_K):
                row = dest_ref[TOP_K * (base + r) + k]
                pltpu.make_async_copy(os_hbm.at[pl.ds(row, 1), :], buf.at[to_slot, k, pl.ds(r, 1), :],
                                      sem.at[to_slot]).start(priority=k)
            return carry

        lax.fori_loop(0, rows, body, 0, unroll=4)

    @pl.when(i == 0)
    def _():
        for first in range(ahead):
            @pl.when(first < n_tiles)
            def _():
                issue(first, first % DMA_RING)

    @pl.when(i + ahead < n_tiles)
    def _():
        issue(i + ahead, (i + ahead) % DMA_RING)

    for k in range(TOP_K):
        pltpu.make_async_copy(os_hbm.at[pl.ds(0, rows), :], buf.at[slot, k], sem.at[slot]).wait()
    gates = gates_ref[...]
    f = gates[:, 0:1] * buf[slot, 0] + gates[:, 1:2] * buf[slot, 1]
    y = y_ref[...] + mod_ref[gate_row:gate_row + 1, :] * f
    if final:
        ms = jnp.mean(y * y, axis=-1, keepdims=True)
        n = (y * lax.rsqrt(ms + EPS)) * g_ref[...]

        @pl.when(i < n_ctx_tiles)
        def _():
            out_a[...] = n

        @pl.when(i >= n_ctx_tiles)
        def _():
            out_b[...] = n
    else:
        out_a[...] = y
        out_b[...] = _norm_mod_value(y, g_ref[...], nmod_ref, 0).astype(out_b.dtype)


def _combine(os_, dest, gates, y, mod, layer, gate_row, next_g, final, n_ctx, dec_seq, tm):
    t, d = y.shape
    nct = n_ctx // tm

    def mod_spec(which):
        return pl.BlockSpec((None, None, 6, d),
                            lambda i, dst: (which, _mod_row(i, tm, n_ctx, dec_seq), 0, 0))

    in_specs = [pl.BlockSpec(memory_space=pl.ANY),
                pl.BlockSpec((tm, d), lambda i, dst: (i, 0)),
                pl.BlockSpec((tm, LANES), lambda i, dst: (i, 0)),
                mod_spec(layer),
                pl.BlockSpec((1, d), lambda i, dst: (0, 0))]
    args = [dest, os_, y, gates, mod, next_g.reshape(1, d)]
    if final:
        out_specs = [pl.BlockSpec((tm, d), lambda i, dst: (jnp.minimum(i, nct - 1), 0)),
                     pl.BlockSpec((tm, d), lambda i, dst: (jnp.maximum(i - nct, 0), 0))]
        out_shape = [jax.ShapeDtypeStruct((n_ctx, d), F32), jax.ShapeDtypeStruct((t - n_ctx, d), F32)]
    else:
        in_specs.append(mod_spec(layer + 1))
        args.append(mod)
        out_specs = [pl.BlockSpec((tm, d), lambda i, dst: (i, 0)),
                     pl.BlockSpec((tm, d), lambda i, dst: (i, 0))]
        out_shape = [jax.ShapeDtypeStruct((t, d), F32), jax.ShapeDtypeStruct((t, d), BF16)]
    return pl.pallas_call(
        functools.partial(_combine_kernel, rows=tm, gate_row=gate_row, n_ctx_tiles=nct, final=final),
        grid_spec=pltpu.PrefetchScalarGridSpec(
            num_scalar_prefetch=1,
            grid=(t // tm,),
            in_specs=in_specs,
            out_specs=out_specs,
            scratch_shapes=[pltpu.VMEM((DMA_RING, TOP_K, tm, d), F32),
                            pltpu.SemaphoreType.DMA((DMA_RING,))]),
        out_shape=out_shape,
        compiler_params=_cparams(1), name="moe_combine_final" if final else "moe_combine",
    )(*args)


def kernel(x_prompt, x_sample, state_rglru, c, c_ctx, norm_mix_g, norm_ffn_g, w_mod, b_mod, final_norm_g, rg_w_in, rg_conv_w, rg_conv_b, rg_w_a, rg_b_a, rg_w_x, rg_b_x, rg_lam, rg_w_out, sg_w_in, sg_norm_g, sg_w_s, sg_b_s, sg_w_out, sc_w_in, sc_conv_w, sc_w_out, ff_w1, ff_w3, ff_w2, moe_router, moe_router_b, moe_w1, moe_w3, moe_w2):
    batch, seq, d = x_prompt.shape
    dec_batch, dec_seq, _ = x_sample.shape
    depth = w_mod.shape[0]
    n_ctx = batch * seq
    d_rnn = rg_w_out.shape[1]
    n_exp = moe_router.shape[2]
    chunk = sg_w_s.shape[2]

    tm = dec_seq
    assert dec_seq % seq == 0 and n_ctx % tm == 0 and seq % SUBLANES == 0
    tm_half = max(tm // 2, chunk)
    tm_small = max(tm // 4, chunk)
    assert tm % tm_half == 0 and tm % tm_small == 0 and tm_small % chunk == 0
    moe_bm, moe_sub = tm, tm_half

    n_cond = 1 + dec_batch
    cond = jnp.zeros((-(-n_cond // SUBLANES) * SUBLANES, d), F32)
    cond = cond.at[0].set(c_ctx).at[1:n_cond].set(c)
    mod = _adaln(cond, w_mod, b_mod)[:, :n_cond].reshape(depth, n_cond, 6, d)

    y, h = _embed_norm(x_prompt.reshape(n_ctx, d), x_sample.reshape(dec_batch * dec_seq, d),
                       _grid_pos_embed(dec_seq, d), norm_mix_g[0], mod, tm_small)

    states = []
    outs = None
    for i in range(depth):
        kind, j = i % 3, i // 3
        if h is None:
            h = _norm_mod(y, norm_mix_g[i], mod, i, 0, n_ctx, dec_seq, tm_half)
        if kind == 0:
            z = _mm_act(h, rg_w_in, j, None, F32, tm, tn_pref=1024)
            h0 = jnp.concatenate([jnp.zeros((n_ctx // tm, 2, d_rnn), F32),
                                  state_rglru[:, j].astype(F32)], axis=0)
            mix, st = _rg_core(z, rg_conv_w[j], rg_conv_b[j], rg_w_a[j], rg_b_a[j], rg_w_x[j], rg_b_x[j],
                               rg_lam[j], h0, n_ctx, seq, tm)
            nseg = tm // seq
            st = st[:n_ctx // tm].reshape(n_ctx // tm, 2, nseg, d_rnn)
            states.append(jnp.transpose(st, (0, 2, 1, 3)).reshape(batch, 2, d_rnn))
        elif kind == 1:
            z = _mm_act(h, sg_w_in, j, "gelu", BF16, tm, tn_pref=1024)
            mix = _sgu_core(z, sg_norm_g[j], sg_w_s[j], sg_b_s[j], tm_small)
        else:
            mix = _mm_sconv(h, sc_w_in, sc_conv_w, j, n_ctx, seq, tm)
        w_out = (rg_w_out, sg_w_out, sc_w_out)[kind]
        y = _mm_residual(mix, w_out, j, y, mod, i, 2, n_ctx, dec_seq, tm if mix.shape[1] <= 4096 else tm_half)

        f = i // 2
        last = i == depth - 1
        if i % 2 == 0:
            h = _norm_mod(y, norm_ffn_g[i], mod, i, 3, n_ctx, dec_seq, tm_half)
            g = _mm_swiglu(h, ff_w1, ff_w3, f, tm)
            y = _mm_residual(g, ff_w2, f, y, mod, i, 5, n_ctx, dec_seq, tm_half)
            h = None
        else:
            h_tok, ids, gates = _norm_router(y, norm_ffn_g[i], mod, i, 3, moe_router[f], moe_router_b[f],
                                             n_ctx, dec_seq, tm_half)
            dest, src, block_table = _moe_plan(ids, n_exp, moe_bm)
            big, small = block_table(moe_bm), block_table(moe_sub)
            xs = _gather_rows(h_tok, src, small[1], moe_sub, d)
            gs = _grouped_swiglu(xs, moe_w1, moe_w3, f, big, moe_bm)
            os_ = _grouped_down(gs, moe_w2, f, big, moe_bm)
            if last:
                outs = _combine(os_, dest, gates, y, mod, i, 5, final_norm_g, True, n_ctx, dec_seq, tm_small)
            else:
                y, h = _combine(os_, dest, gates, y, mod, i, 5, norm_mix_g[i + 1], False,
                                n_ctx, dec_seq, tm_small)

    y_p, y_s = outs if outs is not None else _final_norm(y, final_norm_g, n_ctx, tm_half)
    new_state = jnp.stack(states, axis=1).astype(x_prompt.dtype)
    return (y_p.reshape(batch, seq, d), y_s.reshape(dec_batch, dec_seq, d), new_state)
```

```python
import functools

import jax
import jax.numpy as jnp
from jax import lax
from jax.experimental import pallas as pl
from jax.experimental.pallas import tpu as pltpu

F32 = jnp.float32
BF16 = jnp.bfloat16

GRID_W = 64
EPS = 1e-6
RG_C = 8.0
TOP_K = 2

LANES = 128
SUBLANES = 8
VMEM_LIMIT_BYTES = 58 * 1024 * 1024
NEG_BIG = -1e30
MXU_ROWS_PER_PASS = 256
RESIDENT_WEIGHT_BYTES = 12 * 1024 * 1024
RG_CHUNK_ROWS = 128
DMA_RING = 3


def _cparams(n_axes):
    return pltpu.CompilerParams(dimension_semantics=("arbitrary",) * n_axes,
                                vmem_limit_bytes=VMEM_LIMIT_BYTES)


def _pick_tile(n, pref):
    if n <= pref:
        return n
    t = (pref // LANES) * LANES
    while t > LANES and n % t:
        t -= LANES
    assert n % t == 0, (n, pref)
    return t


def _row_passes(rows):
    step = MXU_ROWS_PER_PASS if rows % MXU_ROWS_PER_PASS == 0 else rows
    return [pl.ds(r, step) for r in range(0, rows, step)]


def _sigmoid(x):
    return 1.0 / (1.0 + jnp.exp(-x))


def _gelu_tanh(x):
    c = 0.7978845608028654
    return 0.5 * x * (1.0 + jnp.tanh(c * (x + 0.044715 * (x * x * x))))


def _mod_row(i, tm, n_ctx, dec_seq):
    start = i * tm
    return jnp.where(start < n_ctx, 0, 1 + (start - n_ctx) // dec_seq)


def _adaln_kernel(c_ref, w_ref, b_ref, o_ref):
    c = c_ref[...]
    s = (c * _sigmoid(c)).astype(BF16)
    o_ref[...] = jnp.dot(s, w_ref[...].astype(BF16), preferred_element_type=F32) + b_ref[...]


def _adaln(cond, w_mod, b_mod):
    depth, d, n = w_mod.shape
    mc = cond.shape[0]
    tn = _pick_tile(n, 1024)
    return pl.pallas_call(
        _adaln_kernel,
        grid=(depth, n // tn),
        in_specs=[pl.BlockSpec((mc, d), lambda l, j: (0, 0)),
                  pl.BlockSpec((None, d, tn), lambda l, j: (l, 0, j)),
                  pl.BlockSpec((None, 1, tn), lambda l, j: (l, 0, j))],
        out_specs=pl.BlockSpec((None, mc, tn), lambda l, j: (l, 0, j)),
        out_shape=jax.ShapeDtypeStruct((depth, mc, n), F32),
        compiler_params=_cparams(2), name="adaln",
    )(cond, w_mod, b_mod.reshape(depth, 1, n))


def _embed_norm_kernel(xp_ref, xs_ref, pos_ref, g_ref, mod_ref, y_ref, h_ref, *, n_ctx_tiles):
    i = pl.program_id(0)

    @pl.when(i < n_ctx_tiles)
    def _():
        y_ref[...] = xp_ref[...]

    @pl.when(i >= n_ctx_tiles)
    def _():
        y_ref[...] = xs_ref[...] + pos_ref[...]

    h_ref[...] = _norm_mod_value(y_ref[...], g_ref[...], mod_ref, 0).astype(h_ref.dtype)


def _embed_norm(xp, xs, pos, g, mod, tm):
    n_ctx, d = xp.shape
    n_dec = xs.shape[0]
    dec_seq = pos.shape[0]
    nct = n_ctx // tm
    ppt = dec_seq // tm
    t = n_ctx + n_dec
    return pl.pallas_call(
        functools.partial(_embed_norm_kernel, n_ctx_tiles=nct),
        grid=(t // tm,),
        in_specs=[pl.BlockSpec((tm, d), lambda i: (jnp.minimum(i, nct - 1), 0)),
                  pl.BlockSpec((tm, d), lambda i: (jnp.maximum(i - nct, 0), 0)),
                  pl.BlockSpec((tm, d), lambda i: (jnp.maximum(i - nct, 0) % ppt, 0)),
                  pl.BlockSpec((1, d), lambda i: (0, 0)),
                  pl.BlockSpec((None, None, 6, d), lambda i: (0, _mod_row(i, tm, n_ctx, dec_seq), 0, 0))],
        out_specs=[pl.BlockSpec((tm, d), lambda i: (i, 0)),
                   pl.BlockSpec((tm, d), lambda i: (i, 0))],
        out_shape=[jax.ShapeDtypeStruct((t, d), F32), jax.ShapeDtypeStruct((t, d), BF16)],
        compiler_params=_cparams(1), name="embed_norm",
    )(xp, xs, pos, g.reshape(1, d), mod)


def _grid_pos_embed(length, d):
    rows = length // GRID_W
    r = jnp.repeat(jnp.arange(rows), GRID_W)
    col = jnp.tile(jnp.arange(GRID_W), rows)
    quarter = d // 4
    omega = 1.0 / (10000.0 ** (jnp.arange(quarter, dtype=F32) / quarter))

    def emb(p):
        ang = p[:, None].astype(F32) * omega[None, :]
        return jnp.concatenate([jnp.sin(ang), jnp.cos(ang)], axis=-1)

    return jnp.concatenate([emb(r), emb(col)], axis=-1).astype(F32)


def _norm_mod_value(y, g, mod_ref, shift_row):
    ms = jnp.mean(y * y, axis=-1, keepdims=True)
    n = (y * lax.rsqrt(ms + EPS)) * g
    return n * (1.0 + mod_ref[shift_row + 1:shift_row + 2, :]) + mod_ref[shift_row:shift_row + 1, :]


def _norm_mod_kernel(y_ref, g_ref, mod_ref, h_ref, *, shift_row):
    h_ref[...] = _norm_mod_value(y_ref[...], g_ref[...], mod_ref, shift_row).astype(h_ref.dtype)


def _norm_mod(y, g, mod, layer, shift_row, n_ctx, dec_seq, tm):
    t, d = y.shape
    return pl.pallas_call(
        functools.partial(_norm_mod_kernel, shift_row=shift_row),
        grid=(t // tm,),
        in_specs=[pl.BlockSpec((tm, d), lambda i: (i, 0)),
                  pl.BlockSpec((1, d), lambda i: (0, 0)),
                  pl.BlockSpec((None, None, 6, d),
                               lambda i: (layer, _mod_row(i, tm, n_ctx, dec_seq), 0, 0))],
        out_specs=pl.BlockSpec((tm, d), lambda i: (i, 0)),
        out_shape=jax.ShapeDtypeStruct((t, d), BF16),
        compiler_params=_cparams(1), name="norm_mod",
    )(y, g.reshape(1, d), mod)


def _norm_router_kernel(y_ref, g_ref, mod_ref, r_ref, rb_ref, h_ref, ids_ref, gates_ref, *,
                        shift_row, n_slab):
    h = _norm_mod_value(y_ref[...], g_ref[...], mod_ref, shift_row)
    tm = h.shape[0]
    for s in range(n_slab):
        h_ref[pl.ds(s, tm, stride=n_slab), :] = h[:, s * LANES:(s + 1) * LANES]
    h_hi = h.astype(BF16)
    h_lo = (h - h_hi.astype(F32)).astype(BF16)
    p_hi = jnp.dot(h_hi, r_ref[...], preferred_element_type=F32)
    p_lo = jnp.dot(h_lo, r_ref[:, :LANES], preferred_element_type=F32)
    logits = ((p_hi[:, :LANES] + p_hi[:, LANES:]) + p_lo) + rb_ref[...]
    lane = lax.broadcasted_iota(jnp.int32, logits.shape, 1).astype(F32)
    big = float(LANES)
    m1 = jnp.max(logits, axis=-1, keepdims=True)
    i1 = jnp.min(jnp.where(logits == m1, lane, big), axis=-1, keepdims=True)
    l2 = jnp.where(lane == i1, 2.0 * NEG_BIG, logits)
    m2 = jnp.max(l2, axis=-1, keepdims=True)
    i2 = jnp.min(jnp.where(l2 == m2, lane, big), axis=-1, keepdims=True)
    e = jnp.exp(m2 - m1)
    g1 = 1.0 / (1.0 + e)
    g2 = e / (1.0 + e)
    ids_ref[...] = jnp.where(lane == 0.0, i1, jnp.where(lane == 1.0, i2, 0.0)).astype(jnp.int32)
    gates_ref[...] = jnp.where(lane == 0.0, g1, jnp.where(lane == 1.0, g2, 0.0))


def _norm_router(y, g, mod, layer, shift_row, router, router_b, n_ctx, dec_seq, tm):
    t, d = y.shape
    n_exp = router.shape[1]
    n_slab = d // LANES
    rp = jnp.zeros((d, LANES), F32).at[:, :n_exp].set(router)
    r_hi = rp.astype(BF16)
    r_lo = (rp - r_hi.astype(F32)).astype(BF16)
    rp = jnp.concatenate([r_hi, r_lo], axis=1)
    rbp = jnp.full((1, LANES), NEG_BIG, F32).at[0, :n_exp].set(router_b)
    return pl.pallas_call(
        functools.partial(_norm_router_kernel, shift_row=shift_row, n_slab=n_slab),
        grid=(t // tm,),
        in_specs=[pl.BlockSpec((tm, d), lambda i: (i, 0)),
                  pl.BlockSpec((1, d), lambda i: (0, 0)),
                  pl.BlockSpec((None, None, 6, d),
                               lambda i: (layer, _mod_row(i, tm, n_ctx, dec_seq), 0, 0)),
                  pl.BlockSpec((d, 2 * LANES), lambda i: (0, 0)),
                  pl.BlockSpec((1, LANES), lambda i: (0, 0))],
        out_specs=[pl.BlockSpec((tm * n_slab, LANES), lambda i: (i, 0)),
                   pl.BlockSpec((tm, LANES), lambda i: (i, 0)),
                   pl.BlockSpec((tm, LANES), lambda i: (i, 0))],
        out_shape=[jax.ShapeDtypeStruct((t * n_slab, LANES), F32),
                   jax.ShapeDtypeStruct((t, LANES), jnp.int32),
                   jax.ShapeDtypeStruct((t, LANES), F32)],
        compiler_params=_cparams(1), name="norm_router",
    )(y, g.reshape(1, d), mod, rp, rbp)


def _final_norm_kernel(y_ref, g_ref, op_ref, os_ref, *, n_ctx_tiles):
    i = pl.program_id(0)
    y = y_ref[...]
    ms = jnp.mean(y * y, axis=-1, keepdims=True)
    n = (y * lax.rsqrt(ms + EPS)) * g_ref[...]

    @pl.when(i < n_ctx_tiles)
    def _():
        op_ref[...] = n

    @pl.when(i >= n_ctx_tiles)
    def _():
        os_ref[...] = n


def _final_norm(y, g, n_ctx, tm):
    t, d = y.shape
    nct = n_ctx // tm
    return pl.pallas_call(
        functools.partial(_final_norm_kernel, n_ctx_tiles=nct),
        grid=(t // tm,),
        in_specs=[pl.BlockSpec((tm, d), lambda i: (i, 0)),
                  pl.BlockSpec((1, d), lambda i: (0, 0))],
        out_specs=[pl.BlockSpec((tm, d), lambda i: (jnp.minimum(i, nct - 1), 0)),
                   pl.BlockSpec((tm, d), lambda i: (jnp.maximum(i - nct, 0), 0))],
        out_shape=[jax.ShapeDtypeStruct((n_ctx, d), F32),
                   jax.ShapeDtypeStruct((t - n_ctx, d), F32)],
        compiler_params=_cparams(1), name="final_norm",
    )(y, g.reshape(1, d))


def _cast_at_first_row_tile(w_refs, wb_refs):
    @pl.when(pl.program_id(1) == 0)
    def _():
        for w, wb in zip(w_refs, wb_refs):
            wb[...] = w[...].astype(BF16)


def _mm_act_kernel(x_ref, w_ref, o_ref, wb, *, act):
    _cast_at_first_row_tile((w_ref,), (wb,))
    for rows in _row_passes(x_ref.shape[0]):
        acc = jnp.dot(x_ref[rows, :], wb[...], preferred_element_type=F32)
        if act == "gelu":
            acc = _gelu_tanh(acc)
        o_ref[rows, :] = acc.astype(o_ref.dtype)


def _mm_act(x, w, layer, act, out_dtype, tm, tn_pref=512):
    t, k = x.shape
    n = w.shape[2]
    tn = _pick_tile(n, tn_pref)
    return pl.pallas_call(
        functools.partial(_mm_act_kernel, act=act),
        grid=(n // tn, t // tm),
        in_specs=[pl.BlockSpec((tm, k), lambda j, i: (i, 0)),
                  pl.BlockSpec((None, k, tn), lambda j, i: (layer, 0, j))],
        out_specs=pl.BlockSpec((tm, tn), lambda j, i: (i, j)),
        out_shape=jax.ShapeDtypeStruct((t, n), out_dtype),
        scratch_shapes=[pltpu.VMEM((k, tn), BF16)],
        compiler_params=_cparams(2), name="mm_" + str(act),
    )(x, w)


def _swiglu_passes(x_ref, wb1, wb3, o_ref):
    for rows in _row_passes(x_ref.shape[0]):
        x = x_ref[rows, :]
        a = jnp.dot(x, wb1[...], preferred_element_type=F32)
        b = jnp.dot(x, wb3[...], preferred_element_type=F32)
        o_ref[rows, :] = ((a * _sigmoid(a)) * b).astype(o_ref.dtype)


def _mm_swiglu_kernel(x_ref, w1_ref, w3_ref, o_ref, wb1, wb3):
    _cast_at_first_row_tile((w1_ref, w3_ref), (wb1, wb3))
    _swiglu_passes(x_ref, wb1, wb3, o_ref)


def _mm_swiglu(x, w1, w3, layer, tm, tn_pref=512):
    t, k = x.shape
    n = w1.shape[2]
    tn = _pick_tile(n, tn_pref)
    return pl.pallas_call(
        _mm_swiglu_kernel,
        grid=(n // tn, t // tm),
        in_specs=[pl.BlockSpec((tm, k), lambda j, i: (i, 0)),
                  pl.BlockSpec((None, k, tn), lambda j, i: (layer, 0, j)),
                  pl.BlockSpec((None, k, tn), lambda j, i: (layer, 0, j))],
        out_specs=pl.BlockSpec((tm, tn), lambda j, i: (i, j)),
        out_shape=jax.ShapeDtypeStruct((t, n), BF16),
        scratch_shapes=[pltpu.VMEM((k, tn), BF16), pltpu.VMEM((k, tn), BF16)],
        compiler_params=_cparams(2), name="mm_swiglu",
    )(x, w1, w3)


def _mm_sconv_kernel(x_ref, wb_ref, wc_ref, wx_ref, cw_ref, o_ref, sb, sc, sx, *, n_ctx_tiles, seq_ctx):
    _cast_at_first_row_tile((wb_ref, wc_ref, wx_ref), (sb, sc, sx))
    i = pl.program_id(1)
    is_ctx = i < n_ctx_tiles
    x = x_ref[...]
    bg = jnp.dot(x, sb[...], preferred_element_type=F32)
    p = jnp.dot(x, sc[...], preferred_element_type=F32) * jnp.dot(x, sx[...], preferred_element_type=F32)
    tm = p.shape[0]
    row = lax.broadcasted_iota(jnp.int32, p.shape, 0)
    pos = jnp.where(is_ctx, row % seq_ctx, row)
    last_pos = jnp.where(is_ctx, seq_ctx - 1, tm - 1)
    p_prev = jnp.where(pos == 0, 0.0, pltpu.roll(p, 1, axis=0))
    p_next = jnp.where(pos == last_pos, 0.0, pltpu.roll(p, tm - 1, axis=0))
    cw = cw_ref[...]
    conv = (cw[0:1] * p_prev + cw[1:2] * p) + cw[2:3] * p_next
    o_ref[...] = (bg * conv).astype(o_ref.dtype)


def _mm_sconv(x, w_in, conv_w, layer, n_ctx, seq_ctx, tm, tn_pref=512):
    t, k = x.shape
    d = conv_w.shape[2]
    tn = _pick_tile(d, tn_pref)
    nj = d // tn
    return pl.pallas_call(
        functools.partial(_mm_sconv_kernel, n_ctx_tiles=n_ctx // tm, seq_ctx=seq_ctx),
        grid=(nj, t // tm),
        in_specs=[pl.BlockSpec((tm, k), lambda j, i: (i, 0)),
                  pl.BlockSpec((None, k, tn), lambda j, i: (layer, 0, j)),
                  pl.BlockSpec((None, k, tn), lambda j, i: (layer, 0, nj + j)),
                  pl.BlockSpec((None, k, tn), lambda j, i: (layer, 0, 2 * nj + j)),
                  pl.BlockSpec((None, conv_w.shape[1], tn), lambda j, i: (layer, 0, j))],
        out_specs=pl.BlockSpec((tm, tn), lambda j, i: (i, j)),
        out_shape=jax.ShapeDtypeStruct((t, d), BF16),
        scratch_shapes=[pltpu.VMEM((k, tn), BF16)] * 3,
        compiler_params=_cparams(2), name="mm_sconv",
    )(x, w_in, w_in, w_in, conv_w)


def _mm_residual_kernel(x_ref, w_ref, y_ref, mod_ref, o_ref, wb, *, gate_row):
    _cast_at_first_row_tile((w_ref,), (wb,))
    gate = mod_ref[gate_row:gate_row + 1, :]
    for rows in _row_passes(x_ref.shape[0]):
        acc = jnp.dot(x_ref[rows, :], wb[...], preferred_element_type=F32)
        o_ref[rows, :] = y_ref[rows, :] + gate * acc


def _mm_residual(x, w, w_layer, y, mod, layer, gate_row, n_ctx, dec_seq, tm, tn_pref=512, w_buffers=2):
    t, k = x.shape
    n = w.shape[2]
    tn = _pick_tile(n, tn_pref)
    return pl.pallas_call(
        functools.partial(_mm_residual_kernel, gate_row=gate_row),
        grid=(n // tn, t // tm),
        in_specs=[pl.BlockSpec((tm, k), lambda j, i: (i, 0)),
                  pl.BlockSpec((None, k, tn), lambda j, i: (w_layer, 0, j),
                               pipeline_mode=pl.Buffered(w_buffers)),
                  pl.BlockSpec((tm, tn), lambda j, i: (i, j)),
                  pl.BlockSpec((None, None, 6, tn),
                               lambda j, i: (layer, _mod_row(i, tm, n_ctx, dec_seq), 0, j))],
        out_specs=pl.BlockSpec((tm, tn), lambda j, i: (i, j)),
        out_shape=jax.ShapeDtypeStruct((t, n), F32),
        scratch_shapes=[pltpu.VMEM((k, tn), BF16)],
        compiler_params=_cparams(2), name="mm_residual",
    )(x, w, y, mod)


def _residual_norm_kernel(x_ref, w_hbm, y_ref, mod_ref, g_ref, yo_ref, ho_ref, wb, stage, sem, *,
                          w_layer, gate_row, shift_row, chunk):
    k = wb.shape[0]
    nchunk = k // chunk

    @pl.when(pl.program_id(0) == 0)
    def _():
        def copy(c):
            return pltpu.make_async_copy(w_hbm.at[w_layer, pl.ds(c * chunk, chunk), :],
                                         stage.at[c % 2], sem.at[c % 2])

        copy(0).start()
        for c in range(nchunk):
            if c + 1 < nchunk:
                copy(c + 1).start()
            copy(c).wait()
            wb[pl.ds(c * chunk, chunk), :] = stage[c % 2].astype(BF16)

    gate = mod_ref[gate_row:gate_row + 1, :]
    g = g_ref[...]
    for rows in _row_passes(x_ref.shape[0]):
        acc = jnp.dot(x_ref[rows, :], wb[...], preferred_element_type=F32)
        y = y_ref[rows, :] + gate * acc
        yo_ref[rows, :] = y
        ho_ref[rows, :] = _norm_mod_value(y, g, mod_ref, shift_row).astype(ho_ref.dtype)


def _residual_norm(x, w, w_layer, y, mod, layer, gate_row, next_g, shift_row, n_ctx, dec_seq, tm):
    t, k = x.shape
    d = w.shape[2]
    chunk = _pick_tile(k, 512)
    return pl.pallas_call(
        functools.partial(_residual_norm_kernel, w_layer=w_layer, gate_row=gate_row,
                          shift_row=shift_row, chunk=chunk),
        grid=(t // tm,),
        in_specs=[pl.BlockSpec((tm, k), lambda i: (i, 0)),
                  pl.BlockSpec(memory_space=pl.ANY),
                  pl.BlockSpec((tm, d), lambda i: (i, 0)),
                  pl.BlockSpec((None, None, 6, d),
                               lambda i: (layer, _mod_row(i, tm, n_ctx, dec_seq), 0, 0)),
                  pl.BlockSpec((1, d), lambda i: (0, 0))],
        out_specs=[pl.BlockSpec((tm, d), lambda i: (i, 0)),
                   pl.BlockSpec((tm, d), lambda i: (i, 0))],
        out_shape=[jax.ShapeDtypeStruct((t, d), F32), jax.ShapeDtypeStruct((t, d), BF16)],
        scratch_shapes=[pltpu.VMEM((k, d), BF16), pltpu.VMEM((2, chunk, d), F32),
                        pltpu.SemaphoreType.DMA((2,))],
        compiler_params=_cparams(1), name="residual_norm",
    )(x, w, y, mod, next_g.reshape(1, d))


def _rg_core_kernel(gate_ref, x_ref, cw_ref, cb_ref, wg_ref, bg_ref, lam_ref, h0_ref,
                    y_ref, st_ref, af_s, bf_s, ab_s, bb_s, *, n_ctx_tiles, seg, tm, cg):
    i = pl.program_id(1)
    is_ctx = i < n_ctx_tiles
    nseg = tm // seg
    nslab = cg // LANES
    lsub = tm // SUBLANES
    sub_per_seq = seg // lsub
    cw = cw_ref[...]
    cb = cb_ref[...]
    lam = lam_ref[...]
    softplus_neg_lam = jnp.maximum(-lam, 0.0) + jnp.log1p(jnp.exp(-jnp.abs(lam)))
    neg_c_sp = (-RG_C) * softplus_neg_lam
    zeros8 = jnp.zeros((SUBLANES, cg), F32)

    cr = min(RG_CHUNK_ROWS, lsub)
    for c0 in range(0, tm, cr):
        cur = x_ref[pl.ds(c0, cr), :]
        if c0 == 0:
            prev8 = zeros8
        else:
            prev8 = x_ref[pl.ds(c0 - SUBLANES, SUBLANES), :]
            if c0 % seg == 0:
                prev8 = jnp.where(is_ctx, 0.0, prev8)
        if c0 + cr == tm:
            next8 = zeros8
        else:
            next8 = x_ref[pl.ds(c0 + cr, SUBLANES), :]
            if (c0 + cr) % seg == 0:
                next8 = jnp.where(is_ctx, 0.0, next8)
        head = jnp.concatenate([prev8, cur[:2 * SUBLANES]], axis=0)
        tail = jnp.concatenate([cur[cr - 2 * SUBLANES:], next8], axis=0)

        def shifted(delta):
            lo = SUBLANES + delta
            mid = x_ref[pl.ds(c0 + lo, cr - 2 * SUBLANES), :]
            return jnp.concatenate([head[lo:lo + SUBLANES], mid, tail[lo:lo + SUBLANES]], axis=0)

        xc = (((cw[0:1] * shifted(-2) + cw[1:2] * shifted(-1)) + cw[2:3] * cur)
              + cw[3:4] * shifted(1)) + cb
        g = jnp.dot(xc.astype(BF16), wg_ref[...], preferred_element_type=F32) + bg_ref[...]
        for d, (a_s, b_s) in enumerate(((af_s, bf_s), (ab_s, bb_s))):
            r = _sigmoid(g[:, (2 * d) * cg:(2 * d + 1) * cg])
            ig = _sigmoid(g[:, (2 * d + 1) * cg:(2 * d + 2) * cg])
            log_a = neg_c_sp[d:d + 1] * r
            a = jnp.exp(log_a)
            one_minus_a2 = -jnp.tanh(log_a) * (a * a + 1.0)
            root = jnp.where(one_minus_a2 > 0.0, one_minus_a2 * lax.rsqrt(one_minus_a2), 0.0)
            bt = root * (ig * xc)
            k, j0 = c0 // lsub, c0 % lsub
            dst = pl.ds(k + SUBLANES * j0, cr, stride=SUBLANES)
            for l in range(nslab):
                a_s[l, dst, :] = a[:, l * LANES:(l + 1) * LANES]
                b_s[l, dst, :] = bt[:, l * LANES:(l + 1) * LANES]

    def local_scan(j, carry):
        hf, pf, hb, pb = carry
        rf = pl.multiple_of(j * SUBLANES, SUBLANES)
        rb = pl.multiple_of((lsub - 1 - j) * SUBLANES, SUBLANES)
        nhf, npf, nhb, npb = [], [], [], []
        for l in range(nslab):
            a = af_s[l, pl.ds(rf, SUBLANES), :]
            h = a * hf[l] + bf_s[l, pl.ds(rf, SUBLANES), :]
            p = a * pf[l]
            bf_s[l, pl.ds(rf, SUBLANES), :] = h
            af_s[l, pl.ds(rf, SUBLANES), :] = p
            nhf.append(h)
            npf.append(p)
            a = ab_s[l, pl.ds(rb, SUBLANES), :]
            h = a * hb[l] + bb_s[l, pl.ds(rb, SUBLANES), :]
            p = a * pb[l]
            bb_s[l, pl.ds(rb, SUBLANES), :] = h
            ab_s[l, pl.ds(rb, SUBLANES), :] = p
            nhb.append(h)
            npb.append(p)
        return tuple(nhf), tuple(npf), tuple(nhb), tuple(npb)

    zero = tuple(jnp.zeros((SUBLANES, LANES), F32) for _ in range(nslab))
    one = tuple(jnp.ones((SUBLANES, LANES), F32) for _ in range(nslab))
    hf_end, pf_end, hb_end, pb_end = lax.fori_loop(0, lsub, local_scan, (zero, one, zero, one))

    row8 = lax.broadcasted_iota(jnp.int32, (SUBLANES, LANES), 0)
    h0 = h0_ref[...]
    for l in range(nslab):
        lanes = slice(l * LANES, (l + 1) * LANES)
        ent_f = jnp.zeros((SUBLANES, LANES), F32)
        h_in = h0[0:1, lanes]
        for k in range(SUBLANES):
            if k > 0 and k % sub_per_seq == 0:
                h_in = jnp.where(is_ctx, 0.0, h_in)
            ent_f = jnp.where(row8 == k, h_in, ent_f)
            h_in = hf_end[l][k:k + 1] + pf_end[l][k:k + 1] * h_in
            if (k + 1) % sub_per_seq == 0:
                q = k // sub_per_seq
                st_ref[q:q + 1, lanes] = h_in
        ent_b = jnp.zeros((SUBLANES, LANES), F32)
        h_in = h0[1:2, lanes]
        for k in reversed(range(SUBLANES)):
            if k < SUBLANES - 1 and (k + 1) % sub_per_seq == 0:
                h_in = jnp.where(is_ctx, 0.0, h_in)
            ent_b = jnp.where(row8 == k, h_in, ent_b)
            h_in = hb_end[l][k:k + 1] + pb_end[l][k:k + 1] * h_in
            if k % sub_per_seq == 0:
                q = k // sub_per_seq
                st_ref[nseg + q:nseg + q + 1, lanes] = h_in
        ch = min(tm, 256)
        ef = jnp.concatenate([ent_f] * (ch // SUBLANES), axis=0)
        eb = jnp.concatenate([ent_b] * (ch // SUBLANES), axis=0)
        for r0 in range(0, tm, ch):
            rows = pl.ds(r0, ch)
            bf_s[l, rows, :] = ((bf_s[l, rows, :] + af_s[l, rows, :] * ef)
                                + (bb_s[l, rows, :] + ab_s[l, rows, :] * eb))

    for k in range(SUBLANES):
        rows = pl.ds(k * lsub, lsub)
        hsum = jnp.concatenate([bf_s[l, pl.ds(k, lsub, stride=SUBLANES), :] for l in range(nslab)], axis=1)
        y_ref[rows, :] = (hsum * _gelu_tanh(gate_ref[rows, :])).astype(y_ref.dtype)


def _rg_core(z, conv_w, conv_b, w_a, b_a, w_x, b_x, lam, h0, n_ctx, seq_ctx, tm):
    t = z.shape[0]
    r = conv_w.shape[1]
    heads, hw = w_a.shape[1], w_a.shape[2]
    hpg = 4
    while (hpg * hw) % LANES:
        hpg *= 2
    ng = heads // hpg
    cg = hpg * hw
    nseg = tm // seq_ctx
    ntiles = t // tm
    assert tm % SUBLANES == 0 and seq_ctx % (tm // SUBLANES) == 0

    def blockdiag(w):
        rows = jnp.tile(w.reshape(2, ng, cg, hw), (1, 1, 1, hpg))
        head = jnp.arange(cg, dtype=jnp.int32) // hw
        return jnp.where(head[:, None] == head[None, :], rows, 0.0)

    wa, wx = blockdiag(w_a), blockdiag(w_x)
    wg = jnp.concatenate([wa[0], wx[0], wa[1], wx[1]], axis=-1).astype(BF16)
    ba, bx = b_a.reshape(2, ng, 1, cg), b_x.reshape(2, ng, 1, cg)
    bg = jnp.concatenate([ba[0], bx[0], ba[1], bx[1]], axis=-1)

    return pl.pallas_call(
        functools.partial(_rg_core_kernel, n_ctx_tiles=n_ctx // tm, seg=seq_ctx, tm=tm, cg=cg),
        grid=(ng, ntiles),
        in_specs=[pl.BlockSpec((tm, cg), lambda g, i: (i, g)),
                  pl.BlockSpec((tm, cg), lambda g, i: (i, ng + g)),
                  pl.BlockSpec((conv_w.shape[0], cg), lambda g, i: (0, g)),
                  pl.BlockSpec((1, cg), lambda g, i: (0, g)),
                  pl.BlockSpec((None, cg, 4 * cg), lambda g, i: (g, 0, 0)),
                  pl.BlockSpec((None, 1, 4 * cg), lambda g, i: (g, 0, 0)),
                  pl.BlockSpec((2, cg), lambda g, i: (0, g)),
                  pl.BlockSpec((None, 2, cg), lambda g, i: (i, 0, g))],
        out_specs=[pl.BlockSpec((tm, cg), lambda g, i: (i, g)),
                   pl.BlockSpec((None, 2 * nseg, cg), lambda g, i: (i, 0, g))],
        out_shape=[jax.ShapeDtypeStruct((t, r), BF16),
                   jax.ShapeDtypeStruct((ntiles, 2 * nseg, r), F32)],
        scratch_shapes=[pltpu.VMEM((cg // LANES, tm, LANES), F32)] * 4,
        compiler_params=_cparams(2), name="rg_core",
    )(z, z, conv_w, conv_b.reshape(1, r), wg, bg, lam, h0)


def _sgu_core_kernel(u_ref, v_ref, g_ref, ws_ref, bs_ref, y_ref, vn_s, *, chunk, gw, ngroups, tm):
    v = v_ref[...].astype(F32)
    ms = jnp.mean(v * v, axis=-1, keepdims=True)
    vn_s[...] = ((v * lax.rsqrt(ms + EPS)) * g_ref[...]).astype(BF16)
    for c in range(tm // chunk):
        rows = pl.ds(c * chunk, chunk)
        for g in range(ngroups):
            cols = pl.ds(g * gw, gw)
            vm = jnp.dot(ws_ref[g], vn_s[rows, cols], preferred_element_type=F32) + bs_ref[:, g:g + 1]
            y_ref[rows, cols] = (u_ref[rows, cols].astype(F32) * vm).astype(y_ref.dtype)


def _sgu_core(z, norm_g, w_s, b_s, tm):
    t = z.shape[0]
    w = norm_g.shape[0]
    ngroups, chunk = w_s.shape[0], w_s.shape[1]
    gw = w // ngroups
    return pl.pallas_call(
        functools.partial(_sgu_core_kernel, chunk=chunk, gw=gw, ngroups=ngroups, tm=tm),
        grid=(t // tm,),
        in_specs=[pl.BlockSpec((tm, w), lambda i: (i, 0)),
                  pl.BlockSpec((tm, w), lambda i: (i, 1)),
                  pl.BlockSpec((1, w), lambda i: (0, 0)),
                  pl.BlockSpec((ngroups, chunk, chunk), lambda i: (0, 0, 0)),
                  pl.BlockSpec((chunk, ngroups), lambda i: (0, 0))],
        out_specs=pl.BlockSpec((tm, w), lambda i: (i, 0)),
        out_shape=jax.ShapeDtypeStruct((t, w), BF16),
        scratch_shapes=[pltpu.VMEM((tm, w), BF16)],
        compiler_params=_cparams(1), name="sgu_core",
    )(z, z, norm_g.reshape(1, w), w_s.astype(BF16), b_s.T)


def _moe_plan(ids, n_exp, bm):
    t = ids.shape[0]
    e = ids[:, :TOP_K].reshape(-1)
    oh = (e[:, None] == jnp.arange(n_exp, dtype=jnp.int32)[None, :]).astype(jnp.int32)
    csum = jnp.cumsum(oh, axis=0)
    rank = jnp.sum((csum - oh) * oh, axis=1)
    counts = csum[-1]
    padded = ((counts + bm - 1) // bm) * bm
    ends = jnp.cumsum(padded)
    starts = ends - padded
    first_cnt = counts - jnp.maximum(padded - bm, 0)
    start_e = jnp.sum(starts[None, :] * oh, axis=1)
    first_e = jnp.sum(first_cnt[None, :] * oh, axis=1)
    dest = (start_e + rank + jnp.where(rank >= first_e, bm - first_e, 0)).astype(jnp.int32)
    p = TOP_K * t + n_exp * bm
    src = jnp.zeros((p,), jnp.int32).at[dest].set(jnp.arange(TOP_K * t, dtype=jnp.int32) // TOP_K)

    def block_table(blk):
        idx = jnp.arange(p // blk, dtype=jnp.int32)
        start = idx * blk
        exp = jnp.minimum(jnp.sum((start[:, None] >= ends[None, :]).astype(jnp.int32), axis=1), n_exp - 1)
        offset = start - starts[exp]
        valid = jnp.where(offset < bm, jnp.clip(first_cnt[exp] - offset, 0, blk), blk)
        valid = jnp.where(start < ends[-1], valid, 0)
        fetch = lax.cummax(jnp.where(valid > 0, idx, 0), axis=0)
        return exp.astype(jnp.int32), valid.astype(jnp.int32), fetch.astype(jnp.int32)

    return dest, src, block_table


def _gather_rows_kernel(src_ref, valid_ref, h_hbm, o_ref, buf, sem, *, rows, n_slab):
    b = pl.program_id(0)
    nb = pl.num_programs(0)
    slot = b % DMA_RING
    ahead = DMA_RING - 1

    def block_has_rows(blk):
        return jnp.logical_and(blk < nb, valid_ref[jnp.minimum(blk, nb - 1)] > 0)

    has_rows = block_has_rows(b)

    def issue(blk, to_slot):
        base = blk * rows

        def body(q, carry):
            for prio in range(2):
                r = 2 * q + prio
                tok = pl.multiple_of(src_ref[base + r] * n_slab, n_slab)
                pltpu.make_async_copy(h_hbm.at[pl.ds(tok, n_slab), :],
                                      buf.at[to_slot, pl.ds(pl.multiple_of(r * n_slab, n_slab), n_slab), :],
                                      sem.at[to_slot]).start(priority=prio)
            return carry

        lax.fori_loop(0, rows // 2, body, 0, unroll=4)

    @pl.when(b == 0)
    def _():
        for first in range(ahead):
            @pl.when(block_has_rows(first))
            def _():
                issue(first, first % DMA_RING)

    @pl.when(block_has_rows(b + ahead))
    def _():
        issue(b + ahead, (b + ahead) % DMA_RING)

    @pl.when(has_rows)
    def _():
        pltpu.make_async_copy(h_hbm.at[pl.ds(0, rows * n_slab), :], buf.at[slot], sem.at[slot]).wait()
        for s in range(n_slab):
            o_ref[:, s * LANES:(s + 1) * LANES] = buf[slot, pl.ds(s, rows, stride=n_slab), :].astype(o_ref.dtype)

    @pl.when(jnp.logical_not(has_rows))
    def _():
        o_ref[...] = jnp.zeros_like(o_ref)


def _gather_rows(h_tok, src, valid, bm, d):
    n_slab = d // LANES
    p = src.shape[0]
    return pl.pallas_call(
        functools.partial(_gather_rows_kernel, rows=bm, n_slab=n_slab),
        grid_spec=pltpu.PrefetchScalarGridSpec(
            num_scalar_prefetch=2,
            grid=(p // bm,),
            in_specs=[pl.BlockSpec(memory_space=pl.ANY)],
            out_specs=pl.BlockSpec((bm, d), lambda b, src, va: (b, 0)),
            scratch_shapes=[pltpu.VMEM((DMA_RING, bm * n_slab, LANES), F32),
                            pltpu.SemaphoreType.DMA((DMA_RING,))]),
        out_shape=jax.ShapeDtypeStruct((p, d), BF16),
        compiler_params=_cparams(1), name="moe_gather",
    )(src, valid, h_tok)


def _expert_changed(exp_ref, fetch_ref, b):
    prev = fetch_ref[jnp.maximum(b - 1, 0)]
    return jnp.logical_or(b == 0, exp_ref[b] != exp_ref[prev])


def _guarded_passes(valid, x_ref, o_ref, one_pass):
    bm = x_ref.shape[0]
    passes = _row_passes(bm)

    @pl.when(valid == bm)
    def _():
        for rows in passes:
            one_pass(rows)

    @pl.when(valid < bm)
    def _():
        for n, rows in enumerate(passes):
            first_row = n * (bm // len(passes))

            @pl.when(first_row < valid)
            def _():
                one_pass(rows)

            @pl.when(first_row >= valid)
            def _():
                o_ref[rows, :] = jnp.zeros((bm // len(passes), o_ref.shape[1]), o_ref.dtype)


def _grouped_swiglu_kernel(exp_ref, valid_ref, fetch_ref, x_ref, w1_ref, w3_ref, o_ref, wb1, wb3):
    b = pl.program_id(1)
    valid = valid_ref[b]

    @pl.when(jnp.logical_and(valid > 0, _expert_changed(exp_ref, fetch_ref, b)))
    def _():
        wb1[...] = w1_ref[...].astype(BF16)
        wb3[...] = w3_ref[...].astype(BF16)

    def one_pass(rows):
        x = x_ref[rows, :]
        a = jnp.dot(x, wb1[...], preferred_element_type=F32)
        c = jnp.dot(x, wb3[...], preferred_element_type=F32)
        o_ref[rows, :] = ((a * _sigmoid(a)) * c).astype(o_ref.dtype)

    _guarded_passes(valid, x_ref, o_ref, one_pass)


def _grouped_swiglu(xs, w1, w3, layer, table, bm, tn_pref=512):
    p, k = xs.shape
    n = w1.shape[3]
    tn = _pick_tile(n, tn_pref)
    return pl.pallas_call(
        _grouped_swiglu_kernel,
        grid_spec=pltpu.PrefetchScalarGridSpec(
            num_scalar_prefetch=3,
            grid=(n // tn, p // bm),
            in_specs=[pl.BlockSpec((bm, k), lambda j, b, ex, va, fe: (fe[b], 0)),
                      pl.BlockSpec((None, None, k, tn), lambda j, b, ex, va, fe: (layer, ex[fe[b]], 0, j)),
                      pl.BlockSpec((None, None, k, tn), lambda j, b, ex, va, fe: (layer, ex[fe[b]], 0, j))],
            out_specs=pl.BlockSpec((bm, tn), lambda j, b, ex, va, fe: (b, j)),
            scratch_shapes=[pltpu.VMEM((k, tn), BF16), pltpu.VMEM((k, tn), BF16)]),
        out_shape=jax.ShapeDtypeStruct((p, n), BF16),
        compiler_params=_cparams(2), name="moe_swiglu",
    )(*table, xs, w1, w3)


def _grouped_down_kernel(exp_ref, valid_ref, fetch_ref, x_ref, w_ref, o_ref, wb):
    b = pl.program_id(1)
    valid = valid_ref[b]

    @pl.when(jnp.logical_and(valid > 0, _expert_changed(exp_ref, fetch_ref, b)))
    def _():
        wb[...] = w_ref[...].astype(BF16)

    def one_pass(rows):
        o_ref[rows, :] = jnp.dot(x_ref[rows, :], wb[...], preferred_element_type=F32)

    _guarded_passes(valid, x_ref, o_ref, one_pass)


def _grouped_down(gs, w2, layer, table, bm, tn_pref=512):
    p, k = gs.shape
    n = w2.shape[3]
    tn = _pick_tile(n, tn_pref)
    return pl.pallas_call(
        _grouped_down_kernel,
        grid_spec=pltpu.PrefetchScalarGridSpec(
            num_scalar_prefetch=3,
            grid=(n // tn, p // bm),
            in_specs=[pl.BlockSpec((bm, k), lambda j, b, ex, va, fe: (fe[b], 0)),
                      pl.BlockSpec((None, None, k, tn), lambda j, b, ex, va, fe: (layer, ex[fe[b]], 0, j))],
            out_specs=pl.BlockSpec((bm, tn), lambda j, b, ex, va, fe: (b, j)),
            scratch_shapes=[pltpu.VMEM((k, tn), BF16)]),
        out_shape=jax.ShapeDtypeStruct((p, n), F32),
        compiler_params=_cparams(2), name="moe_down",
    )(*table, gs, w2)


def _combine_kernel(dest_ref, os_hbm, y_ref, gates_ref, mod_ref, g_ref, *rest,
                    rows, gate_row, n_ctx_tiles, final):
    if final:
        out_a, out_b, buf, sem = rest
    else:
        nmod_ref, out_a, out_b, buf, sem = rest
    i = pl.program_id(0)
    n_tiles = pl.num_programs(0)
    slot = i % DMA_RING
    ahead = DMA_RING - 1

    def issue(tile, to_slot):
        base = tile * rows

        def body(r, carry):
            for k in range(TOP_K):
                row = dest_ref[TOP_K * (base + r) + k]
                pltpu.make_async_copy(os_hbm.at[pl.ds(row, 1), :], buf.at[to_slot, k, pl.ds(r, 1), :],
                                      sem.at[to_slot]).start(priority=k)
            return carry

        lax.fori_loop(0, rows, body, 0, unroll=4)

    @pl.when(i == 0)
    def _():
        for first in range(ahead):
            @pl.when(first < n_tiles)
            def _():
                issue(first, first % DMA_RING)

    @pl.when(i + ahead < n_tiles)
    def _():
        issue(i + ahead, (i + ahead) % DMA_RING)

    for k in range(TOP_K):
        pltpu.make_async_copy(os_hbm.at[pl.ds(0, rows), :], buf.at[slot, k], sem.at[slot]).wait()
    gates = gates_ref[...]
    f = gates[:, 0:1] * buf[slot, 0] + gates[:, 1:2] * buf[slot, 1]
    y = y_ref[...] + mod_ref[gate_row:gate_row + 1, :] * f
    if final:
        ms = jnp.mean(y * y, axis=-1, keepdims=True)
        n = (y * lax.rsqrt(ms + EPS)) * g_ref[...]

        @pl.when(i < n_ctx_tiles)
        def _():
            out_a[...] = n

        @pl.when(i >= n_ctx_tiles)
        def _():
            out_b[...] = n
    else:
        out_a[...] = y
        out_b[...] = _norm_mod_value(y, g_ref[...], nmod_ref, 0).astype(out_b.dtype)


def _combine(os_, dest, gates, y, mod, layer, gate_row, next_g, final, n_ctx, dec_seq, tm):
    t, d = y.shape
    nct = n_ctx // tm

    def mod_spec(which):
        return pl.BlockSpec((None, None, 6, d),
                            lambda i, dst: (which, _mod_row(i, tm, n_ctx, dec_seq), 0, 0))

    in_specs = [pl.BlockSpec(memory_space=pl.ANY),
                pl.BlockSpec((tm, d), lambda i, dst: (i, 0)),
                pl.BlockSpec((tm, LANES), lambda i, dst: (i, 0)),
                mod_spec(layer),
                pl.BlockSpec((1, d), lambda i, dst: (0, 0))]
    args = [dest, os_, y, gates, mod, next_g.reshape(1, d)]
    if final:
        out_specs = [pl.BlockSpec((tm, d), lambda i, dst: (jnp.minimum(i, nct - 1), 0)),
                     pl.BlockSpec((tm, d), lambda i, dst: (jnp.maximum(i - nct, 0), 0))]
        out_shape = [jax.ShapeDtypeStruct((n_ctx, d), F32), jax.ShapeDtypeStruct((t - n_ctx, d), F32)]
    else:
        in_specs.append(mod_spec(layer + 1))
        args.append(mod)
        out_specs = [pl.BlockSpec((tm, d), lambda i, dst: (i, 0)),
                     pl.BlockSpec((tm, d), lambda i, dst: (i, 0))]
        out_shape = [jax.ShapeDtypeStruct((t, d), F32), jax.ShapeDtypeStruct((t, d), BF16)]
    return pl.pallas_call(
        functools.partial(_combine_kernel, rows=tm, gate_row=gate_row, n_ctx_tiles=nct, final=final),
        grid_spec=pltpu.PrefetchScalarGridSpec(
            num_scalar_prefetch=1,
            grid=(t // tm,),
            in_specs=in_specs,
            out_specs=out_specs,
            scratch_shapes=[pltpu.VMEM((DMA_RING, TOP_K, tm, d), F32),
                            pltpu.SemaphoreType.DMA((DMA_RING,))]),
        out_shape=out_shape,
        compiler_params=_cparams(1), name="moe_combine_final" if final else "moe_combine",
    )(*args)


def kernel(x_prompt, x_sample, state_rglru, c, c_ctx, norm_mix_g, norm_ffn_g, w_mod, b_mod, final_norm_g, rg_w_in, rg_conv_w, rg_conv_b, rg_w_a, rg_b_a, rg_w_x, rg_b_x, rg_lam, rg_w_out, sg_w_in, sg_norm_g, sg_w_s, sg_b_s, sg_w_out, sc_w_in, sc_conv_w, sc_w_out, ff_w1, ff_w3, ff_w2, moe_router, moe_router_b, moe_w1, moe_w3, moe_w2):
    batch, seq, d = x_prompt.shape
    dec_batch, dec_seq, _ = x_sample.shape
    depth = w_mod.shape[0]
    n_ctx = batch * seq
    d_rnn = rg_w_out.shape[1]
    n_exp = moe_router.shape[2]
    chunk = sg_w_s.shape[2]

    tm = dec_seq
    assert dec_seq % seq == 0 and n_ctx % tm == 0 and seq % SUBLANES == 0
    tm_half = max(tm // 2, chunk)
    tm_small = max(tm // 4, chunk)
    assert tm % tm_half == 0 and tm % tm_small == 0 and tm_small % chunk == 0
    moe_bm, moe_sub = tm, tm_half

    n_cond = 1 + dec_batch
    cond = jnp.zeros((-(-n_cond // SUBLANES) * SUBLANES, d), F32)
    cond = cond.at[0].set(c_ctx).at[1:n_cond].set(c)
    mod = _adaln(cond, w_mod, b_mod)[:, :n_cond].reshape(depth, n_cond, 6, d)

    y, h = _embed_norm(x_prompt.reshape(n_ctx, d), x_sample.reshape(dec_batch * dec_seq, d),
                       _grid_pos_embed(dec_seq, d), norm_mix_g[0], mod, tm_small)

    states = []
    outs = None
    for i in range(depth):
        kind, j = i % 3, i // 3
        if h is None:
            h = _norm_mod(y, norm_mix_g[i], mod, i, 0, n_ctx, dec_seq, tm_half)
        if kind == 0:
            z = _mm_act(h, rg_w_in, j, None, F32, tm, tn_pref=1024)
            h0 = jnp.concatenate([jnp.zeros((n_ctx // tm, 2, d_rnn), F32),
                                  state_rglru[:, j].astype(F32)], axis=0)
            mix, st = _rg_core(z, rg_conv_w[j], rg_conv_b[j], rg_w_a[j], rg_b_a[j], rg_w_x[j], rg_b_x[j],
                               rg_lam[j], h0, n_ctx, seq, tm)
            nseg = tm // seq
            st = st[:n_ctx // tm].reshape(n_ctx // tm, 2, nseg, d_rnn)
            states.append(jnp.transpose(st, (0, 2, 1, 3)).reshape(batch, 2, d_rnn))
        elif kind == 1:
            z = _mm_act(h, sg_w_in, j, "gelu", BF16, tm, tn_pref=1024)
            mix = _sgu_core(z, sg_norm_g[j], sg_w_s[j], sg_b_s[j], tm_small)
        else:
            mix = _mm_sconv(h, sc_w_in, sc_conv_w, j, n_ctx, seq, tm)
        w_out = (rg_w_out, sg_w_out, sc_w_out)[kind]
        h = None
        if i % 2 == 0 and mix.shape[1] * d * 2 <= RESIDENT_WEIGHT_BYTES:
            y, h = _residual_norm(mix, w_out, j, y, mod, i, 2, norm_ffn_g[i], 3, n_ctx, dec_seq, tm_half)
        else:
            y = _mm_residual(mix, w_out, j, y, mod, i, 2, n_ctx, dec_seq,
                             tm if mix.shape[1] <= 4096 else tm_half)

        f = i // 2
        last = i == depth - 1
        if i % 2 == 0:
            if h is None:
                h = _norm_mod(y, norm_ffn_g[i], mod, i, 3, n_ctx, dec_seq, tm_half)
            g = _mm_swiglu(h, ff_w1, ff_w3, f, tm)
            y = _mm_residual(g, ff_w2, f, y, mod, i, 5, n_ctx, dec_seq, tm, w_buffers=1)
            h = None
        else:
            h_tok, ids, gates = _norm_router(y, norm_ffn_g[i], mod, i, 3, moe_router[f], moe_router_b[f],
                                             n_ctx, dec_seq, tm_half)
            dest, src, block_table = _moe_plan(ids, n_exp, moe_bm)
            big, small = block_table(moe_bm), block_table(moe_sub)
            xs = _gather_rows(h_tok, src, small[1], moe_sub, d)
            gs = _grouped_swiglu(xs, moe_w1, moe_w3, f, big, moe_bm)
            os_ = _grouped_down(gs, moe_w2, f, big, moe_bm)
            if last:
                outs = _combine(os_, dest, gates, y, mod, i, 5, final_norm_g, True, n_ctx, dec_seq, tm_small)
            else:
                y, h = _combine(os_, dest, gates, y, mod, i, 5, norm_mix_g[i + 1], False,
                                n_ctx, dec_seq, tm_small)

    y_p, y_s = outs if outs is not None else _final_norm(y, final_norm_g, n_ctx, tm_half)
    new_state = jnp.stack(states, axis=1).astype(x_prompt.dtype)
    return (y_p.reshape(batch, seq, d), y_s.reshape(dec_batch, dec_seq, d), new_state)
```

```python
import functools

import jax
import jax.numpy as jnp
from jax import lax
from jax.experimental import pallas as pl
from jax.experimental.pallas import tpu as pltpu

F32 = jnp.float32
BF16 = jnp.bfloat16

GRID_W = 64
EPS = 1e-6
RG_C = 8.0
TOP_K = 2

LANES = 128
SUBLANES = 8
VMEM_LIMIT_BYTES = 58 * 1024 * 1024
NEG_BIG = -1e30
MXU_ROWS_PER_PASS = 256
RESIDENT_WEIGHT_BYTES = 12 * 1024 * 1024
RG_CHUNK_ROWS = 128
DMA_RING = 3


def _cparams(n_axes):
    return pltpu.CompilerParams(dimension_semantics=("arbitrary",) * n_axes,
                                vmem_limit_bytes=VMEM_LIMIT_BYTES)


def _pick_tile(n, pref):
    if n <= pref:
        return n
    t = (pref // LANES) * LANES
    while t > LANES and n % t:
        t -= LANES
    assert n % t == 0, (n, pref)
    return t


def _row_passes(rows):
    step = MXU_ROWS_PER_PASS if rows % MXU_ROWS_PER_PASS == 0 else rows
    return [pl.ds(r, step) for r in range(0, rows, step)]


def _sigmoid(x):
    return 1.0 / (1.0 + jnp.exp(-x))


def _gelu_tanh(x):
    c = 0.7978845608028654
    return 0.5 * x * (1.0 + jnp.tanh(c * (x + 0.044715 * (x * x * x))))


def _mod_row(i, tm, n_ctx, dec_seq):
    start = i * tm
    return jnp.where(start < n_ctx, 0, 1 + (start - n_ctx) // dec_seq)


def _adaln_kernel(c_ref, w_ref, b_ref, o_ref):
    c = c_ref[...]
    s = (c * _sigmoid(c)).astype(BF16)
    o_ref[...] = jnp.dot(s, w_ref[...].astype(BF16), preferred_element_type=F32) + b_ref[...]


def _adaln(cond, w_mod, b_mod):
    depth, d, n = w_mod.shape
    mc = cond.shape[0]
    tn = _pick_tile(n, 1024)
    return pl.pallas_call(
        _adaln_kernel,
        grid=(depth, n // tn),
        in_specs=[pl.BlockSpec((mc, d), lambda l, j: (0, 0)),
                  pl.BlockSpec((None, d, tn), lambda l, j: (l, 0, j)),
                  pl.BlockSpec((None, 1, tn), lambda l, j: (l, 0, j))],
        out_specs=pl.BlockSpec((None, mc, tn), lambda l, j: (l, 0, j)),
        out_shape=jax.ShapeDtypeStruct((depth, mc, n), F32),
        compiler_params=_cparams(2), name="adaln",
    )(cond, w_mod, b_mod.reshape(depth, 1, n))


def _embed_norm_kernel(xp_ref, xs_ref, pos_ref, g_ref, mod_ref, y_ref, h_ref, *, n_ctx_tiles):
    i = pl.program_id(0)

    @pl.when(i < n_ctx_tiles)
    def _():
        y_ref[...] = xp_ref[...]

    @pl.when(i >= n_ctx_tiles)
    def _():
        y_ref[...] = xs_ref[...] + pos_ref[...]

    h_ref[...] = _norm_mod_value(y_ref[...], g_ref[...], mod_ref, 0).astype(h_ref.dtype)


def _embed_norm(xp, xs, pos, g, mod, tm):
    n_ctx, d = xp.shape
    n_dec = xs.shape[0]
    dec_seq = pos.shape[0]
    nct = n_ctx // tm
    ppt = dec_seq // tm
    t = n_ctx + n_dec
    return pl.pallas_call(
        functools.partial(_embed_norm_kernel, n_ctx_tiles=nct),
        grid=(t // tm,),
        in_specs=[pl.BlockSpec((tm, d), lambda i: (jnp.minimum(i, nct - 1), 0)),
                  pl.BlockSpec((tm, d), lambda i: (jnp.maximum(i - nct, 0), 0)),
                  pl.BlockSpec((tm, d), lambda i: (jnp.maximum(i - nct, 0) % ppt, 0)),
                  pl.BlockSpec((1, d), lambda i: (0, 0)),
                  pl.BlockSpec((None, None, 6, d), lambda i: (0, _mod_row(i, tm, n_ctx, dec_seq), 0, 0))],
        out_specs=[pl.BlockSpec((tm, d), lambda i: (i, 0)),
                   pl.BlockSpec((tm, d), lambda i: (i, 0))],
        out_shape=[jax.ShapeDtypeStruct((t, d), F32), jax.ShapeDtypeStruct((t, d), BF16)],
        compiler_params=_cparams(1), name="embed_norm",
    )(xp, xs, pos, g.reshape(1, d), mod)


def _grid_pos_embed(length, d):
    rows = length // GRID_W
    r = jnp.repeat(jnp.arange(rows), GRID_W)
    col = jnp.tile(jnp.arange(GRID_W), rows)
    quarter = d // 4
    omega = 1.0 / (10000.0 ** (jnp.arange(quarter, dtype=F32) / quarter))

    def emb(p):
        ang = p[:, None].astype(F32) * omega[None, :]
        return jnp.concatenate([jnp.sin(ang), jnp.cos(ang)], axis=-1)

    return jnp.concatenate([emb(r), emb(col)], axis=-1).astype(F32)


def _norm_mod_value(y, g, mod_ref, shift_row):
    ms = jnp.mean(y * y, axis=-1, keepdims=True)
    n = (y * lax.rsqrt(ms + EPS)) * g
    return n * (1.0 + mod_ref[shift_row + 1:shift_row + 2, :]) + mod_ref[shift_row:shift_row + 1, :]


def _norm_mod_kernel(y_ref, g_ref, mod_ref, h_ref, *, shift_row):
    h_ref[...] = _norm_mod_value(y_ref[...], g_ref[...], mod_ref, shift_row).astype(h_ref.dtype)


def _norm_mod(y, g, mod, layer, shift_row, n_ctx, dec_seq, tm):
    t, d = y.shape
    return pl.pallas_call(
        functools.partial(_norm_mod_kernel, shift_row=shift_row),
        grid=(t // tm,),
        in_specs=[pl.BlockSpec((tm, d), lambda i: (i, 0)),
                  pl.BlockSpec((1, d), lambda i: (0, 0)),
                  pl.BlockSpec((None, None, 6, d),
                               lambda i: (layer, _mod_row(i, tm, n_ctx, dec_seq), 0, 0))],
        out_specs=pl.BlockSpec((tm, d), lambda i: (i, 0)),
        out_shape=jax.ShapeDtypeStruct((t, d), BF16),
        compiler_params=_cparams(1), name="norm_mod",
    )(y, g.reshape(1, d), mod)


def _norm_router_kernel(y_ref, g_ref, mod_ref, r_ref, rb_ref, h_ref, ids_ref, gates_ref, *,
                        shift_row, n_slab):
    h = _norm_mod_value(y_ref[...], g_ref[...], mod_ref, shift_row)
    tm = h.shape[0]
    half = n_slab * LANES
    bits = lax.bitcast_convert_type(h.astype(jnp.bfloat16).astype(F32), jnp.uint32)
    word = bits[:, half:] | (bits[:, :half] >> 16)
    for s in range(n_slab):
        h_ref[pl.ds(s, tm, stride=n_slab), :] = word[:, s * LANES:(s + 1) * LANES]
    h_hi = h.astype(BF16)
    h_lo = (h - h_hi.astype(F32)).astype(BF16)
    p_hi = jnp.dot(h_hi, r_ref[...], preferred_element_type=F32)
    p_lo = jnp.dot(h_lo, r_ref[:, :LANES], preferred_element_type=F32)
    logits = ((p_hi[:, :LANES] + p_hi[:, LANES:]) + p_lo) + rb_ref[...]
    lane = lax.broadcasted_iota(jnp.int32, logits.shape, 1).astype(F32)
    big = float(LANES)
    m1 = jnp.max(logits, axis=-1, keepdims=True)
    i1 = jnp.min(jnp.where(logits == m1, lane, big), axis=-1, keepdims=True)
    l2 = jnp.where(lane == i1, 2.0 * NEG_BIG, logits)
    m2 = jnp.max(l2, axis=-1, keepdims=True)
    i2 = jnp.min(jnp.where(l2 == m2, lane, big), axis=-1, keepdims=True)
    e = jnp.exp(m2 - m1)
    g1 = 1.0 / (1.0 + e)
    g2 = e / (1.0 + e)
    ids_ref[...] = jnp.where(lane == 0.0, i1, jnp.where(lane == 1.0, i2, 0.0)).astype(jnp.int32)
    gates_ref[...] = jnp.where(lane == 0.0, g1, jnp.where(lane == 1.0, g2, 0.0))


def _norm_router(y, g, mod, layer, shift_row, router, router_b, n_ctx, dec_seq, tm):
    t, d = y.shape
    n_exp = router.shape[1]
    n_slab = d // (2 * LANES)
    rp = jnp.zeros((d, LANES), F32).at[:, :n_exp].set(router)
    r_hi = rp.astype(BF16)
    r_lo = (rp - r_hi.astype(F32)).astype(BF16)
    rp = jnp.concatenate([r_hi, r_lo], axis=1)
    rbp = jnp.full((1, LANES), NEG_BIG, F32).at[0, :n_exp].set(router_b)
    return pl.pallas_call(
        functools.partial(_norm_router_kernel, shift_row=shift_row, n_slab=n_slab),
        grid=(t // tm,),
        in_specs=[pl.BlockSpec((tm, d), lambda i: (i, 0)),
                  pl.BlockSpec((1, d), lambda i: (0, 0)),
                  pl.BlockSpec((None, None, 6, d),
                               lambda i: (layer, _mod_row(i, tm, n_ctx, dec_seq), 0, 0)),
                  pl.BlockSpec((d, 2 * LANES), lambda i: (0, 0)),
                  pl.BlockSpec((1, LANES), lambda i: (0, 0))],
        out_specs=[pl.BlockSpec((tm * n_slab, LANES), lambda i: (i, 0)),
                   pl.BlockSpec((tm, LANES), lambda i: (i, 0)),
                   pl.BlockSpec((tm, LANES), lambda i: (i, 0))],
        out_shape=[jax.ShapeDtypeStruct((t * n_slab, LANES), jnp.uint32),
                   jax.ShapeDtypeStruct((t, LANES), jnp.int32),
                   jax.ShapeDtypeStruct((t, LANES), F32)],
        compiler_params=_cparams(1), name="norm_router",
    )(y, g.reshape(1, d), mod, rp, rbp)


def _final_norm_kernel(y_ref, g_ref, op_ref, os_ref, *, n_ctx_tiles):
    i = pl.program_id(0)
    y = y_ref[...]
    ms = jnp.mean(y * y, axis=-1, keepdims=True)
    n = (y * lax.rsqrt(ms + EPS)) * g_ref[...]

    @pl.when(i < n_ctx_tiles)
    def _():
        op_ref[...] = n

    @pl.when(i >= n_ctx_tiles)
    def _():
        os_ref[...] = n


def _final_norm(y, g, n_ctx, tm):
    t, d = y.shape
    nct = n_ctx // tm
    return pl.pallas_call(
        functools.partial(_final_norm_kernel, n_ctx_tiles=nct),
        grid=(t // tm,),
        in_specs=[pl.BlockSpec((tm, d), lambda i: (i, 0)),
                  pl.BlockSpec((1, d), lambda i: (0, 0))],
        out_specs=[pl.BlockSpec((tm, d), lambda i: (jnp.minimum(i, nct - 1), 0)),
                   pl.BlockSpec((tm, d), lambda i: (jnp.maximum(i - nct, 0), 0))],
        out_shape=[jax.ShapeDtypeStruct((n_ctx, d), F32),
                   jax.ShapeDtypeStruct((t - n_ctx, d), F32)],
        compiler_params=_cparams(1), name="final_norm",
    )(y, g.reshape(1, d))


def _cast_at_first_row_tile(w_refs, wb_refs):
    @pl.when(pl.program_id(1) == 0)
    def _():
        for w, wb in zip(w_refs, wb_refs):
            wb[...] = w[...].astype(BF16)


def _mm_act_kernel(x_ref, w_ref, o_ref, wb, *, act):
    _cast_at_first_row_tile((w_ref,), (wb,))
    for rows in _row_passes(x_ref.shape[0]):
        acc = jnp.dot(x_ref[rows, :], wb[...], preferred_element_type=F32)
        if act == "gelu":
            acc = _gelu_tanh(acc)
        o_ref[rows, :] = acc.astype(o_ref.dtype)


def _mm_act(x, w, layer, act, out_dtype, tm, tn_pref=512):
    t, k = x.shape
    n = w.shape[2]
    tn = _pick_tile(n, tn_pref)
    return pl.pallas_call(
        functools.partial(_mm_act_kernel, act=act),
        grid=(n // tn, t // tm),
        in_specs=[pl.BlockSpec((tm, k), lambda j, i: (i, 0)),
                  pl.BlockSpec((None, k, tn), lambda j, i: (layer, 0, j))],
        out_specs=pl.BlockSpec((tm, tn), lambda j, i: (i, j)),
        out_shape=jax.ShapeDtypeStruct((t, n), out_dtype),
        scratch_shapes=[pltpu.VMEM((k, tn), BF16)],
        compiler_params=_cparams(2), name="mm_" + str(act),
    )(x, w)


def _swiglu_passes(x_ref, wb1, wb3, o_ref):
    for rows in _row_passes(x_ref.shape[0]):
        x = x_ref[rows, :]
        a = jnp.dot(x, wb1[...], preferred_element_type=F32)
        b = jnp.dot(x, wb3[...], preferred_element_type=F32)
        o_ref[rows, :] = ((a * _sigmoid(a)) * b).astype(o_ref.dtype)


def _mm_swiglu_kernel(x_ref, w1_ref, w3_ref, o_ref, wb1, wb3):
    _cast_at_first_row_tile((w1_ref, w3_ref), (wb1, wb3))
    _swiglu_passes(x_ref, wb1, wb3, o_ref)


def _mm_swiglu(x, w1, w3, layer, tm, tn_pref=512):
    t, k = x.shape
    n = w1.shape[2]
    tn = _pick_tile(n, tn_pref)
    return pl.pallas_call(
        _mm_swiglu_kernel,
        grid=(n // tn, t // tm),
        in_specs=[pl.BlockSpec((tm, k), lambda j, i: (i, 0)),
                  pl.BlockSpec((None, k, tn), lambda j, i: (layer, 0, j)),
                  pl.BlockSpec((None, k, tn), lambda j, i: (layer, 0, j))],
        out_specs=pl.BlockSpec((tm, tn), lambda j, i: (i, j)),
        out_shape=jax.ShapeDtypeStruct((t, n), BF16),
        scratch_shapes=[pltpu.VMEM((k, tn), BF16), pltpu.VMEM((k, tn), BF16)],
        compiler_params=_cparams(2), name="mm_swiglu",
    )(x, w1, w3)


def _mm_sconv_kernel(x_ref, wb_ref, wc_ref, wx_ref, cw_ref, o_ref, sb, sc, sx, *, n_ctx_tiles, seq_ctx):
    _cast_at_first_row_tile((wb_ref, wc_ref, wx_ref), (sb, sc, sx))
    i = pl.program_id(1)
    is_ctx = i < n_ctx_tiles
    x = x_ref[...]
    bg = jnp.dot(x, sb[...], preferred_element_type=F32)
    p = jnp.dot(x, sc[...], preferred_element_type=F32) * jnp.dot(x, sx[...], preferred_element_type=F32)
    tm = p.shape[0]
    row = lax.broadcasted_iota(jnp.int32, p.shape, 0)
    pos = jnp.where(is_ctx, row % seq_ctx, row)
    last_pos = jnp.where(is_ctx, seq_ctx - 1, tm - 1)
    p_prev = jnp.where(pos == 0, 0.0, pltpu.roll(p, 1, axis=0))
    p_next = jnp.where(pos == last_pos, 0.0, pltpu.roll(p, tm - 1, axis=0))
    cw = cw_ref[...]
    conv = (cw[0:1] * p_prev + cw[1:2] * p) + cw[2:3] * p_next
    o_ref[...] = (bg * conv).astype(o_ref.dtype)


def _mm_sconv(x, w_in, conv_w, layer, n_ctx, seq_ctx, tm, tn_pref=512):
    t, k = x.shape
    d = conv_w.shape[2]
    tn = _pick_tile(d, tn_pref)
    nj = d // tn
    return pl.pallas_call(
        functools.partial(_mm_sconv_kernel, n_ctx_tiles=n_ctx // tm, seq_ctx=seq_ctx),
        grid=(nj, t // tm),
        in_specs=[pl.BlockSpec((tm, k), lambda j, i: (i, 0)),
                  pl.BlockSpec((None, k, tn), lambda j, i: (layer, 0, j)),
                  pl.BlockSpec((None, k, tn), lambda j, i: (layer, 0, nj + j)),
                  pl.BlockSpec((None, k, tn), lambda j, i: (layer, 0, 2 * nj + j)),
                  pl.BlockSpec((None, conv_w.shape[1], tn), lambda j, i: (layer, 0, j))],
        out_specs=pl.BlockSpec((tm, tn), lambda j, i: (i, j)),
        out_shape=jax.ShapeDtypeStruct((t, d), BF16),
        scratch_shapes=[pltpu.VMEM((k, tn), BF16)] * 3,
        compiler_params=_cparams(2), name="mm_sconv",
    )(x, w_in, w_in, w_in, conv_w)


def _mm_residual_kernel(x_ref, w_ref, y_ref, mod_ref, o_ref, wb, *, gate_row):
    _cast_at_first_row_tile((w_ref,), (wb,))
    gate = mod_ref[gate_row:gate_row + 1, :]
    for rows in _row_passes(x_ref.shape[0]):
        acc = jnp.dot(x_ref[rows, :], wb[...], preferred_element_type=F32)
        o_ref[rows, :] = y_ref[rows, :] + gate * acc


def _mm_residual(x, w, w_layer, y, mod, layer, gate_row, n_ctx, dec_seq, tm, tn_pref=512, w_buffers=2):
    t, k = x.shape
    n = w.shape[2]
    tn = _pick_tile(n, tn_pref)
    return pl.pallas_call(
        functools.partial(_mm_residual_kernel, gate_row=gate_row),
        grid=(n // tn, t // tm),
        in_specs=[pl.BlockSpec((tm, k), lambda j, i: (i, 0)),
                  pl.BlockSpec((None, k, tn), lambda j, i: (w_layer, 0, j),
                               pipeline_mode=pl.Buffered(w_buffers)),
                  pl.BlockSpec((tm, tn), lambda j, i: (i, j)),
                  pl.BlockSpec((None, None, 6, tn),
                               lambda j, i: (layer, _mod_row(i, tm, n_ctx, dec_seq), 0, j))],
        out_specs=pl.BlockSpec((tm, tn), lambda j, i: (i, j)),
        out_shape=jax.ShapeDtypeStruct((t, n), F32),
        scratch_shapes=[pltpu.VMEM((k, tn), BF16)],
        compiler_params=_cparams(2), name="mm_residual",
    )(x, w, y, mod)


def _residual_norm_kernel(x_ref, w_hbm, y_ref, mod_ref, g_ref, yo_ref, ho_ref, wb, stage, sem, *,
                          w_layer, gate_row, shift_row, chunk):
    k = wb.shape[0]
    nchunk = k // chunk

    @pl.when(pl.program_id(0) == 0)
    def _():
        def copy(c):
            return pltpu.make_async_copy(w_hbm.at[w_layer, pl.ds(c * chunk, chunk), :],
                                         stage.at[c % 2], sem.at[c % 2])

        copy(0).start()
        for c in range(nchunk):
            if c + 1 < nchunk:
                copy(c + 1).start()
            copy(c).wait()
            wb[pl.ds(c * chunk, chunk), :] = stage[c % 2].astype(BF16)

    gate = mod_ref[gate_row:gate_row + 1, :]
    g = g_ref[...]
    for rows in _row_passes(x_ref.shape[0]):
        acc = jnp.dot(x_ref[rows, :], wb[...], preferred_element_type=F32)
        y = y_ref[rows, :] + gate * acc
        yo_ref[rows, :] = y
        ho_ref[rows, :] = _norm_mod_value(y, g, mod_ref, shift_row).astype(ho_ref.dtype)


def _residual_norm(x, w, w_layer, y, mod, layer, gate_row, next_g, shift_row, n_ctx, dec_seq, tm):
    t, k = x.shape
    d = w.shape[2]
    chunk = _pick_tile(k, 512)
    return pl.pallas_call(
        functools.partial(_residual_norm_kernel, w_layer=w_layer, gate_row=gate_row,
                          shift_row=shift_row, chunk=chunk),
        grid=(t // tm,),
        in_specs=[pl.BlockSpec((tm, k), lambda i: (i, 0)),
                  pl.BlockSpec(memory_space=pl.ANY),
                  pl.BlockSpec((tm, d), lambda i: (i, 0)),
                  pl.BlockSpec((None, None, 6, d),
                               lambda i: (layer, _mod_row(i, tm, n_ctx, dec_seq), 0, 0)),
                  pl.BlockSpec((1, d), lambda i: (0, 0))],
        out_specs=[pl.BlockSpec((tm, d), lambda i: (i, 0)),
                   pl.BlockSpec((tm, d), lambda i: (i, 0))],
        out_shape=[jax.ShapeDtypeStruct((t, d), F32), jax.ShapeDtypeStruct((t, d), BF16)],
        scratch_shapes=[pltpu.VMEM((k, d), BF16), pltpu.VMEM((2, chunk, d), F32),
                        pltpu.SemaphoreType.DMA((2,))],
        compiler_params=_cparams(1), name="residual_norm",
    )(x, w, y, mod, next_g.reshape(1, d))


def _rg_core_kernel(gate_ref, x_ref, cw_ref, cb_ref, wg_ref, bg_ref, lam_ref, h0_ref,
                    y_ref, st_ref, af_s, bf_s, ab_s, bb_s, *, n_ctx_tiles, seg, tm, cg):
    i = pl.program_id(1)
    is_ctx = i < n_ctx_tiles
    nseg = tm // seg
    nslab = cg // LANES
    lsub = tm // SUBLANES
    sub_per_seq = seg // lsub
    cw = cw_ref[...]
    cb = cb_ref[...]
    lam = lam_ref[...]
    softplus_neg_lam = jnp.maximum(-lam, 0.0) + jnp.log1p(jnp.exp(-jnp.abs(lam)))
    neg_c_sp = (-RG_C) * softplus_neg_lam
    zeros8 = jnp.zeros((SUBLANES, cg), F32)

    cr = min(RG_CHUNK_ROWS, lsub)
    for c0 in range(0, tm, cr):
        cur = x_ref[pl.ds(c0, cr), :]
        if c0 == 0:
            prev8 = zeros8
        else:
            prev8 = x_ref[pl.ds(c0 - SUBLANES, SUBLANES), :]
            if c0 % seg == 0:
                prev8 = jnp.where(is_ctx, 0.0, prev8)
        if c0 + cr == tm:
            next8 = zeros8
        else:
            next8 = x_ref[pl.ds(c0 + cr, SUBLANES), :]
            if (c0 + cr) % seg == 0:
                next8 = jnp.where(is_ctx, 0.0, next8)
        head = jnp.concatenate([prev8, cur[:2 * SUBLANES]], axis=0)
        tail = jnp.concatenate([cur[cr - 2 * SUBLANES:], next8], axis=0)

        def shifted(delta):
            lo = SUBLANES + delta
            mid = x_ref[pl.ds(c0 + lo, cr - 2 * SUBLANES), :]
            return jnp.concatenate([head[lo:lo + SUBLANES], mid, tail[lo:lo + SUBLANES]], axis=0)

        xc = (((cw[0:1] * shifted(-2) + cw[1:2] * shifted(-1)) + cw[2:3] * cur)
              + cw[3:4] * shifted(1)) + cb
        g = jnp.dot(xc.astype(BF16), wg_ref[...], preferred_element_type=F32) + bg_ref[...]
        for d, (a_s, b_s) in enumerate(((af_s, bf_s), (ab_s, bb_s))):
            r = _sigmoid(g[:, (2 * d) * cg:(2 * d + 1) * cg])
            ig = _sigmoid(g[:, (2 * d + 1) * cg:(2 * d + 2) * cg])
            log_a = neg_c_sp[d:d + 1] * r
            a = jnp.exp(log_a)
            one_minus_a2 = -jnp.tanh(log_a) * (a * a + 1.0)
            root = jnp.where(one_minus_a2 > 0.0, one_minus_a2 * lax.rsqrt(one_minus_a2), 0.0)
            bt = root * (ig * xc)
            k, j0 = c0 // lsub, c0 % lsub
            dst = pl.ds(k + SUBLANES * j0, cr, stride=SUBLANES)
            for l in range(nslab):
                a_s[l, dst, :] = a[:, l * LANES:(l + 1) * LANES]
                b_s[l, dst, :] = bt[:, l * LANES:(l + 1) * LANES]

    def local_scan(j, carry):
        hf, pf, hb, pb = carry
        rf = pl.multiple_of(j * SUBLANES, SUBLANES)
        rb = pl.multiple_of((lsub - 1 - j) * SUBLANES, SUBLANES)
        nhf, npf, nhb, npb = [], [], [], []
        for l in range(nslab):
            a = af_s[l, pl.ds(rf, SUBLANES), :]
            h = a * hf[l] + bf_s[l, pl.ds(rf, SUBLANES), :]
            p = a * pf[l]
            bf_s[l, pl.ds(rf, SUBLANES), :] = h
            af_s[l, pl.ds(rf, SUBLANES), :] = p
            nhf.append(h)
            npf.append(p)
            a = ab_s[l, pl.ds(rb, SUBLANES), :]
            h = a * hb[l] + bb_s[l, pl.ds(rb, SUBLANES), :]
            p = a * pb[l]
            bb_s[l, pl.ds(rb, SUBLANES), :] = h
            ab_s[l, pl.ds(rb, SUBLANES), :] = p
            nhb.append(h)
            npb.append(p)
        return tuple(nhf), tuple(npf), tuple(nhb), tuple(npb)

    zero = tuple(jnp.zeros((SUBLANES, LANES), F32) for _ in range(nslab))
    one = tuple(jnp.ones((SUBLANES, LANES), F32) for _ in range(nslab))
    hf_end, pf_end, hb_end, pb_end = lax.fori_loop(0, lsub, local_scan, (zero, one, zero, one))

    row8 = lax.broadcasted_iota(jnp.int32, (SUBLANES, LANES), 0)
    h0 = h0_ref[...]
    for l in range(nslab):
        lanes = slice(l * LANES, (l + 1) * LANES)
        ent_f = jnp.zeros((SUBLANES, LANES), F32)
        h_in = h0[0:1, lanes]
        for k in range(SUBLANES):
            if k > 0 and k % sub_per_seq == 0:
                h_in = jnp.where(is_ctx, 0.0, h_in)
            ent_f = jnp.where(row8 == k, h_in, ent_f)
            h_in = hf_end[l][k:k + 1] + pf_end[l][k:k + 1] * h_in
            if (k + 1) % sub_per_seq == 0:
                q = k // sub_per_seq
                st_ref[q:q + 1, lanes] = h_in
        ent_b = jnp.zeros((SUBLANES, LANES), F32)
        h_in = h0[1:2, lanes]
        for k in reversed(range(SUBLANES)):
            if k < SUBLANES - 1 and (k + 1) % sub_per_seq == 0:
                h_in = jnp.where(is_ctx, 0.0, h_in)
            ent_b = jnp.where(row8 == k, h_in, ent_b)
            h_in = hb_end[l][k:k + 1] + pb_end[l][k:k + 1] * h_in
            if k % sub_per_seq == 0:
                q = k // sub_per_seq
                st_ref[nseg + q:nseg + q + 1, lanes] = h_in
        ch = min(tm, 256)
        ef = jnp.concatenate([ent_f] * (ch // SUBLANES), axis=0)
        eb = jnp.concatenate([ent_b] * (ch // SUBLANES), axis=0)
        for r0 in range(0, tm, ch):
            rows = pl.ds(r0, ch)
            bf_s[l, rows, :] = ((bf_s[l, rows, :] + af_s[l, rows, :] * ef)
                                + (bb_s[l, rows, :] + ab_s[l, rows, :] * eb))

    for k in range(SUBLANES):
        rows = pl.ds(k * lsub, lsub)
        hsum = jnp.concatenate([bf_s[l, pl.ds(k, lsub, stride=SUBLANES), :] for l in range(nslab)], axis=1)
        y_ref[rows, :] = (hsum * _gelu_tanh(gate_ref[rows, :])).astype(y_ref.dtype)


def _rg_core(z, conv_w, conv_b, w_a, b_a, w_x, b_x, lam, h0, n_ctx, seq_ctx, tm):
    t = z.shape[0]
    r = conv_w.shape[1]
    heads, hw = w_a.shape[1], w_a.shape[2]
    hpg = 4
    while (hpg * hw) % LANES:
        hpg *= 2
    ng = heads // hpg
    cg = hpg * hw
    nseg = tm // seq_ctx
    ntiles = t // tm
    assert tm % SUBLANES == 0 and seq_ctx % (tm // SUBLANES) == 0

    def blockdiag(w):
        rows = jnp.tile(w.reshape(2, ng, cg, hw), (1, 1, 1, hpg))
        head = jnp.arange(cg, dtype=jnp.int32) // hw
        return jnp.where(head[:, None] == head[None, :], rows, 0.0)

    wa, wx = blockdiag(w_a), blockdiag(w_x)
    wg = jnp.concatenate([wa[0], wx[0], wa[1], wx[1]], axis=-1).astype(BF16)
    ba, bx = b_a.reshape(2, ng, 1, cg), b_x.reshape(2, ng, 1, cg)
    bg = jnp.concatenate([ba[0], bx[0], ba[1], bx[1]], axis=-1)

    return pl.pallas_call(
        functools.partial(_rg_core_kernel, n_ctx_tiles=n_ctx // tm, seg=seq_ctx, tm=tm, cg=cg),
        grid=(ng, ntiles),
        in_specs=[pl.BlockSpec((tm, cg), lambda g, i: (i, g)),
                  pl.BlockSpec((tm, cg), lambda g, i: (i, ng + g)),
                  pl.BlockSpec((conv_w.shape[0], cg), lambda g, i: (0, g)),
                  pl.BlockSpec((1, cg), lambda g, i: (0, g)),
                  pl.BlockSpec((None, cg, 4 * cg), lambda g, i: (g, 0, 0)),
                  pl.BlockSpec((None, 1, 4 * cg), lambda g, i: (g, 0, 0)),
                  pl.BlockSpec((2, cg), lambda g, i: (0, g)),
                  pl.BlockSpec((None, 2, cg), lambda g, i: (i, 0, g))],
        out_specs=[pl.BlockSpec((tm, cg), lambda g, i: (i, g)),
                   pl.BlockSpec((None, 2 * nseg, cg), lambda g, i: (i, 0, g))],
        out_shape=[jax.ShapeDtypeStruct((t, r), BF16),
                   jax.ShapeDtypeStruct((ntiles, 2 * nseg, r), F32)],
        scratch_shapes=[pltpu.VMEM((cg // LANES, tm, LANES), F32)] * 4,
        compiler_params=_cparams(2), name="rg_core",
    )(z, z, conv_w, conv_b.reshape(1, r), wg, bg, lam, h0)


def _sgu_core_kernel(u_ref, v_ref, g_ref, ws_ref, bs_ref, y_ref, vn_s, *, chunk, gw, ngroups, tm):
    v = v_ref[...].astype(F32)
    ms = jnp.mean(v * v, axis=-1, keepdims=True)
    vn_s[...] = ((v * lax.rsqrt(ms + EPS)) * g_ref[...]).astype(BF16)
    for c in range(tm // chunk):
        rows = pl.ds(c * chunk, chunk)
        for g in range(ngroups):
            cols = pl.ds(g * gw, gw)
            vm = jnp.dot(ws_ref[g], vn_s[rows, cols], preferred_element_type=F32) + bs_ref[:, g:g + 1]
            y_ref[rows, cols] = (u_ref[rows, cols].astype(F32) * vm).astype(y_ref.dtype)


def _sgu_core(z, norm_g, w_s, b_s, tm):
    t = z.shape[0]
    w = norm_g.shape[0]
    ngroups, chunk = w_s.shape[0], w_s.shape[1]
    gw = w // ngroups
    return pl.pallas_call(
        functools.partial(_sgu_core_kernel, chunk=chunk, gw=gw, ngroups=ngroups, tm=tm),
        grid=(t // tm,),
        in_specs=[pl.BlockSpec((tm, w), lambda i: (i, 0)),
                  pl.BlockSpec((tm, w), lambda i: (i, 1)),
                  pl.BlockSpec((1, w), lambda i: (0, 0)),
                  pl.BlockSpec((ngroups, chunk, chunk), lambda i: (0, 0, 0)),
                  pl.BlockSpec((chunk, ngroups), lambda i: (0, 0))],
        out_specs=pl.BlockSpec((tm, w), lambda i: (i, 0)),
        out_shape=jax.ShapeDtypeStruct((t, w), BF16),
        scratch_shapes=[pltpu.VMEM((tm, w), BF16)],
        compiler_params=_cparams(1), name="sgu_core",
    )(z, z, norm_g.reshape(1, w), w_s.astype(BF16), b_s.T)


def _moe_plan(ids, n_exp, bm):
    t = ids.shape[0]
    e = ids[:, :TOP_K].reshape(-1)
    oh = (e[:, None] == jnp.arange(n_exp, dtype=jnp.int32)[None, :]).astype(jnp.int32)
    csum = jnp.cumsum(oh, axis=0)
    rank = jnp.sum((csum - oh) * oh, axis=1)
    counts = csum[-1]
    padded = ((counts + bm - 1) // bm) * bm
    ends = jnp.cumsum(padded)
    starts = ends - padded
    first_cnt = counts - jnp.maximum(padded - bm, 0)
    start_e = jnp.sum(starts[None, :] * oh, axis=1)
    first_e = jnp.sum(first_cnt[None, :] * oh, axis=1)
    dest = (start_e + rank + jnp.where(rank >= first_e, bm - first_e, 0)).astype(jnp.int32)
    p = TOP_K * t + n_exp * bm
    src = jnp.zeros((p,), jnp.int32).at[dest].set(jnp.arange(TOP_K * t, dtype=jnp.int32) // TOP_K)

    def block_table(blk):
        idx = jnp.arange(p // blk, dtype=jnp.int32)
        start = idx * blk
        exp = jnp.minimum(jnp.sum((start[:, None] >= ends[None, :]).astype(jnp.int32), axis=1), n_exp - 1)
        offset = start - starts[exp]
        valid = jnp.where(offset < bm, jnp.clip(first_cnt[exp] - offset, 0, blk), blk)
        valid = jnp.where(start < ends[-1], valid, 0)
        fetch = lax.cummax(jnp.where(valid > 0, idx, 0), axis=0)
        return exp.astype(jnp.int32), valid.astype(jnp.int32), fetch.astype(jnp.int32)

    return dest, src, block_table


def _gather_rows_kernel(src_ref, valid_ref, h_hbm, o_ref, buf, sem, *, rows, n_slab):
    b = pl.program_id(0)
    nb = pl.num_programs(0)
    slot = b % DMA_RING
    ahead = DMA_RING - 1

    def block_has_rows(blk):
        return jnp.logical_and(blk < nb, valid_ref[jnp.minimum(blk, nb - 1)] > 0)

    has_rows = block_has_rows(b)

    def issue(blk, to_slot):
        base = blk * rows

        def body(q, carry):
            for prio in range(2):
                r = 2 * q + prio
                tok = pl.multiple_of(src_ref[base + r] * n_slab, n_slab)
                pltpu.make_async_copy(h_hbm.at[pl.ds(tok, n_slab), :],
                                      buf.at[to_slot, pl.ds(pl.multiple_of(r * n_slab, n_slab), n_slab), :],
                                      sem.at[to_slot]).start(priority=prio)
            return carry

        lax.fori_loop(0, rows // 2, body, 0, unroll=4)

    @pl.when(b == 0)
    def _():
        for first in range(ahead):
            @pl.when(block_has_rows(first))
            def _():
                issue(first, first % DMA_RING)

    @pl.when(block_has_rows(b + ahead))
    def _():
        issue(b + ahead, (b + ahead) % DMA_RING)

    @pl.when(has_rows)
    def _():
        pltpu.make_async_copy(h_hbm.at[pl.ds(0, rows * n_slab), :], buf.at[slot], sem.at[slot]).wait()
        half = n_slab * LANES
        for s in range(n_slab):
            word = buf[slot, pl.ds(s, rows, stride=n_slab), :]
            lo = lax.bitcast_convert_type(word << 16, F32)
            hi = lax.bitcast_convert_type(word & jnp.uint32(0xFFFF0000), F32)
            o_ref[:, s * LANES:(s + 1) * LANES] = lo.astype(o_ref.dtype)
            o_ref[:, half + s * LANES:half + (s + 1) * LANES] = hi.astype(o_ref.dtype)

    @pl.when(jnp.logical_not(has_rows))
    def _():
        o_ref[...] = jnp.zeros_like(o_ref)


def _gather_rows(h_tok, src, valid, bm, d):
    n_slab = d // (2 * LANES)
    p = src.shape[0]
    return pl.pallas_call(
        functools.partial(_gather_rows_kernel, rows=bm, n_slab=n_slab),
        grid_spec=pltpu.PrefetchScalarGridSpec(
            num_scalar_prefetch=2,
            grid=(p // bm,),
            in_specs=[pl.BlockSpec(memory_space=pl.ANY)],
            out_specs=pl.BlockSpec((bm, d), lambda b, src, va: (b, 0)),
            scratch_shapes=[pltpu.VMEM((DMA_RING, bm * n_slab, LANES), jnp.uint32),
                            pltpu.SemaphoreType.DMA((DMA_RING,))]),
        out_shape=jax.ShapeDtypeStruct((p, d), BF16),
        compiler_params=_cparams(1), name="moe_gather",
    )(src, valid, h_tok)


def _expert_changed(exp_ref, fetch_ref, b):
    prev = fetch_ref[jnp.maximum(b - 1, 0)]
    return jnp.logical_or(b == 0, exp_ref[b] != exp_ref[prev])


def _guarded_passes(valid, x_ref, o_ref, one_pass):
    bm = x_ref.shape[0]
    passes = _row_passes(bm)

    @pl.when(valid == bm)
    def _():
        for rows in passes:
            one_pass(rows)

    @pl.when(valid < bm)
    def _():
        for n, rows in enumerate(passes):
            first_row = n * (bm // len(passes))

            @pl.when(first_row < valid)
            def _():
                one_pass(rows)

            @pl.when(first_row >= valid)
            def _():
                o_ref[rows, :] = jnp.zeros((bm // len(passes), o_ref.shape[1]), o_ref.dtype)


def _grouped_swiglu_kernel(exp_ref, valid_ref, fetch_ref, x_ref, w1_ref, w3_ref, o_ref, wb1, wb3):
    b = pl.program_id(1)
    valid = valid_ref[b]

    @pl.when(jnp.logical_and(valid > 0, _expert_changed(exp_ref, fetch_ref, b)))
    def _():
        wb1[...] = w1_ref[...].astype(BF16)
        wb3[...] = w3_ref[...].astype(BF16)

    def one_pass(rows):
        x = x_ref[rows, :]
        a = jnp.dot(x, wb1[...], preferred_element_type=F32)
        c = jnp.dot(x, wb3[...], preferred_element_type=F32)
        o_ref[rows, :] = ((a * _sigmoid(a)) * c).astype(o_ref.dtype)

    _guarded_passes(valid, x_ref, o_ref, one_pass)


def _grouped_swiglu(xs, w1, w3, layer, table, bm, tn_pref=512):
    p, k = xs.shape
    n = w1.shape[3]
    tn = _pick_tile(n, tn_pref)
    return pl.pallas_call(
        _grouped_swiglu_kernel,
        grid_spec=pltpu.PrefetchScalarGridSpec(
            num_scalar_prefetch=3,
            grid=(n // tn, p // bm),
            in_specs=[pl.BlockSpec((bm, k), lambda j, b, ex, va, fe: (fe[b], 0)),
                      pl.BlockSpec((None, None, k, tn), lambda j, b, ex, va, fe: (layer, ex[fe[b]], 0, j)),
                      pl.BlockSpec((None, None, k, tn), lambda j, b, ex, va, fe: (layer, ex[fe[b]], 0, j))],
            out_specs=pl.BlockSpec((bm, tn), lambda j, b, ex, va, fe: (b, j)),
            scratch_shapes=[pltpu.VMEM((k, tn), BF16), pltpu.VMEM((k, tn), BF16)]),
        out_shape=jax.ShapeDtypeStruct((p, n), BF16),
        compiler_params=_cparams(2), name="moe_swiglu",
    )(*table, xs, w1, w3)


def _grouped_down_kernel(exp_ref, valid_ref, fetch_ref, x_ref, w_ref, o_ref, wb):
    b = pl.program_id(1)
    valid = valid_ref[b]

    @pl.when(jnp.logical_and(valid > 0, _expert_changed(exp_ref, fetch_ref, b)))
    def _():
        wb[...] = w_ref[...].astype(BF16)

    def one_pass(rows):
        o_ref[rows, :] = jnp.dot(x_ref[rows, :], wb[...], preferred_element_type=F32)

    _guarded_passes(valid, x_ref, o_ref, one_pass)


def _grouped_down(gs, w2, layer, table, bm, tn_pref=512):
    p, k = gs.shape
    n = w2.shape[3]
    tn = _pick_tile(n, tn_pref)
    return pl.pallas_call(
        _grouped_down_kernel,
        grid_spec=pltpu.PrefetchScalarGridSpec(
            num_scalar_prefetch=3,
            grid=(n // tn, p // bm),
            in_specs=[pl.BlockSpec((bm, k), lambda j, b, ex, va, fe: (fe[b], 0)),
                      pl.BlockSpec((None, None, k, tn), lambda j, b, ex, va, fe: (layer, ex[fe[b]], 0, j))],
            out_specs=pl.BlockSpec((bm, tn), lambda j, b, ex, va, fe: (b, j)),
            scratch_shapes=[pltpu.VMEM((k, tn), BF16)]),
        out_shape=jax.ShapeDtypeStruct((p, n), F32),
        compiler_params=_cparams(2), name="moe_down",
    )(*table, gs, w2)


def _combine_kernel(dest_ref, os_hbm, y_ref, gates_ref, mod_ref, g_ref, *rest,
                    rows, gate_row, n_ctx_tiles, final):
    if final:
        out_a, out_b, buf, sem = rest
    else:
        nmod_ref, out_a, out_b, buf, sem = rest
    i = pl.program_id(0)
    n_tiles = pl.num_programs(0)
    slot = i % DMA_RING
    ahead = DMA_RING - 1

    def issue(tile, to_slot):
        base = tile * rows

        def body(r, carry):
            for k in range(TOP_K):
                row = dest_ref[TOP_K * (base + r) + k]
                pltpu.make_async_copy(os_hbm.at[pl.ds(row, 1), :], buf.at[to_slot, k, pl.ds(r, 1), :],
                                      sem.at[to_slot]).start(priority=k)
            return carry

        lax.fori_loop(0, rows, body, 0, unroll=4)

    @pl.when(i == 0)
    def _():
        for first in range(ahead):
            @pl.when(first < n_tiles)
            def _():
                issue(first, first % DMA_RING)

    @pl.when(i + ahead < n_tiles)
    def _():
        issue(i + ahead, (i + ahead) % DMA_RING)

    for k in range(TOP_K):
        pltpu.make_async_copy(os_hbm.at[pl.ds(0, rows), :], buf.at[slot, k], sem.at[slot]).wait()
    gates = gates_ref[...]
    f = gates[:, 0:1] * buf[slot, 0] + gates[:, 1:2] * buf[slot, 1]
    y = y_ref[...] + mod_ref[gate_row:gate_row + 1, :] * f
    if final:
        ms = jnp.mean(y * y, axis=-1, keepdims=True)
        n = (y * lax.rsqrt(ms + EPS)) * g_ref[...]

        @pl.when(i < n_ctx_tiles)
        def _():
            out_a[...] = n

        @pl.when(i >= n_ctx_tiles)
        def _():
            out_b[...] = n
    else:
        out_a[...] = y
        out_b[...] = _norm_mod_value(y, g_ref[...], nmod_ref, 0).astype(out_b.dtype)


def _combine(os_, dest, gates, y, mod, layer, gate_row, next_g, final, n_ctx, dec_seq, tm):
    t, d = y.shape
    nct = n_ctx // tm

    def mod_spec(which):
        return pl.BlockSpec((None, None, 6, d),
                            lambda i, dst: (which, _mod_row(i, tm, n_ctx, dec_seq), 0, 0))

    in_specs = [pl.BlockSpec(memory_space=pl.ANY),
                pl.BlockSpec((tm, d), lambda i, dst: (i, 0)),
                pl.BlockSpec((tm, LANES), lambda i, dst: (i, 0)),
                mod_spec(layer),
                pl.BlockSpec((1, d), lambda i, dst: (0, 0))]
    args = [dest, os_, y, gates, mod, next_g.reshape(1, d)]
    if final:
        out_specs = [pl.BlockSpec((tm, d), lambda i, dst: (jnp.minimum(i, nct - 1), 0)),
                     pl.BlockSpec((tm, d), lambda i, dst: (jnp.maximum(i - nct, 0), 0))]
        out_shape = [jax.ShapeDtypeStruct((n_ctx, d), F32), jax.ShapeDtypeStruct((t - n_ctx, d), F32)]
    else:
        in_specs.append(mod_spec(layer + 1))
        args.append(mod)
        out_specs = [pl.BlockSpec((tm, d), lambda i, dst: (i, 0)),
                     pl.BlockSpec((tm, d), lambda i, dst: (i, 0))]
        out_shape = [jax.ShapeDtypeStruct((t, d), F32), jax.ShapeDtypeStruct((t, d), BF16)]
    return pl.pallas_call(
        functools.partial(_combine_kernel, rows=tm, gate_row=gate_row, n_ctx_tiles=nct, final=final),
        grid_spec=pltpu.PrefetchScalarGridSpec(
            num_scalar_prefetch=1,
            grid=(t // tm,),
            in_specs=in_specs,
            out_specs=out_specs,
            scratch_shapes=[pltpu.VMEM((DMA_RING, TOP_K, tm, d), F32),
                            pltpu.SemaphoreType.DMA((DMA_RING,))]),
        out_shape=out_shape,
        compiler_params=_cparams(1), name="moe_combine_final" if final else "moe_combine",
    )(*args)


def kernel(x_prompt, x_sample, state_rglru, c, c_ctx, norm_mix_g, norm_ffn_g, w_mod, b_mod, final_norm_g, rg_w_in, rg_conv_w, rg_conv_b, rg_w_a, rg_b_a, rg_w_x, rg_b_x, rg_lam, rg_w_out, sg_w_in, sg_norm_g, sg_w_s, sg_b_s, sg_w_out, sc_w_in, sc_conv_w, sc_w_out, ff_w1, ff_w3, ff_w2, moe_router, moe_router_b, moe_w1, moe_w3, moe_w2):
    batch, seq, d = x_prompt.shape
    dec_batch, dec_seq, _ = x_sample.shape
    depth = w_mod.shape[0]
    n_ctx = batch * seq
    d_rnn = rg_w_out.shape[1]
    n_exp = moe_router.shape[2]
    chunk = sg_w_s.shape[2]

    tm = dec_seq
    assert dec_seq % seq == 0 and n_ctx % tm == 0 and seq % SUBLANES == 0
    tm_half = max(tm // 2, chunk)
    tm_small = max(tm // 4, chunk)
    assert tm % tm_half == 0 and tm % tm_small == 0 and tm_small % chunk == 0
    moe_bm, moe_sub = tm, tm_half

    n_cond = 1 + dec_batch
    cond = jnp.zeros((-(-n_cond // SUBLANES) * SUBLANES, d), F32)
    cond = cond.at[0].set(c_ctx).at[1:n_cond].set(c)
    mod = _adaln(cond, w_mod, b_mod)[:, :n_cond].reshape(depth, n_cond, 6, d)

    y, h = _embed_norm(x_prompt.reshape(n_ctx, d), x_sample.reshape(dec_batch * dec_seq, d),
                       _grid_pos_embed(dec_seq, d), norm_mix_g[0], mod, tm_small)

    states = []
    outs = None
    for i in range(depth):
        kind, j = i % 3, i // 3
        if h is None:
            h = _norm_mod(y, norm_mix_g[i], mod, i, 0, n_ctx, dec_seq, tm_half)
        if kind == 0:
            z = _mm_act(h, rg_w_in, j, None, F32, tm, tn_pref=1024)
            h0 = jnp.concatenate([jnp.zeros((n_ctx // tm, 2, d_rnn), F32),
                                  state_rglru[:, j].astype(F32)], axis=0)
            mix, st = _rg_core(z, rg_conv_w[j], rg_conv_b[j], rg_w_a[j], rg_b_a[j], rg_w_x[j], rg_b_x[j],
                               rg_lam[j], h0, n_ctx, seq, tm)
            nseg = tm // seq
            st = st[:n_ctx // tm].reshape(n_ctx // tm, 2, nseg, d_rnn)
            states.append(jnp.transpose(st, (0, 2, 1, 3)).reshape(batch, 2, d_rnn))
        elif kind == 1:
            z = _mm_act(h, sg_w_in, j, "gelu", BF16, tm, tn_pref=1024)
            mix = _sgu_core(z, sg_norm_g[j], sg_w_s[j], sg_b_s[j], tm_small)
        else:
            mix = _mm_sconv(h, sc_w_in, sc_conv_w, j, n_ctx, seq, tm)
        w_out = (rg_w_out, sg_w_out, sc_w_out)[kind]
        h = None
        if i % 2 == 0 and mix.shape[1] * d * 2 <= RESIDENT_WEIGHT_BYTES:
            y, h = _residual_norm(mix, w_out, j, y, mod, i, 2, norm_ffn_g[i], 3, n_ctx, dec_seq, tm_half)
        else:
            y = _mm_residual(mix, w_out, j, y, mod, i, 2, n_ctx, dec_seq,
                             tm if mix.shape[1] <= 4096 else tm_half)

        f = i // 2
        last = i == depth - 1
        if i % 2 == 0:
            if h is None:
                h = _norm_mod(y, norm_ffn_g[i], mod, i, 3, n_ctx, dec_seq, tm_half)
            g = _mm_swiglu(h, ff_w1, ff_w3, f, tm)
            y = _mm_residual(g, ff_w2, f, y, mod, i, 5, n_ctx, dec_seq, tm, w_buffers=1)
            h = None
        else:
            h_tok, ids, gates = _norm_router(y, norm_ffn_g[i], mod, i, 3, moe_router[f], moe_router_b[f],
                                             n_ctx, dec_seq, tm_half)
            dest, src, block_table = _moe_plan(ids, n_exp, moe_bm)
            big, small = block_table(moe_bm), block_table(moe_sub)
            xs = _gather_rows(h_tok, src, small[1], moe_sub, d)
            gs = _grouped_swiglu(xs, moe_w1, moe_w3, f, big, moe_bm)
            os_ = _grouped_down(gs, moe_w2, f, big, moe_bm)
            if last:
                outs = _combine(os_, dest, gates, y, mod, i, 5, final_norm_g, True, n_ctx, dec_seq, tm_small)
            else:
                y, h = _combine(os_, dest, gates, y, mod, i, 5, norm_mix_g[i + 1], False,
                                n_ctx, dec_seq, tm_small)

    y_p, y_s = outs if outs is not None else _final_norm(y, final_norm_g, n_ctx, tm_half)
    new_state = jnp.stack(states, axis=1).astype(x_prompt.dtype)
    return (y_p.reshape(batch, seq, d), y_s.reshape(dec_batch, dec_seq, d), new_state)
```

```python
import functools

import jax
import jax.numpy as jnp
from jax import lax
from jax.experimental import pallas as pl
from jax.experimental.pallas import tpu as pltpu

F32 = jnp.float32
BF16 = jnp.bfloat16

GRID_W = 64
EPS = 1e-6
RG_C = 8.0
TOP_K = 2

LANES = 128
SUBLANES = 8
VMEM_LIMIT_BYTES = 58 * 1024 * 1024
NEG_BIG = -1e30
MXU_ROWS_PER_PASS = 256
RESIDENT_WEIGHT_BYTES = 12 * 1024 * 1024
RG_CHUNK_ROWS = 128
DMA_RING = 3


def _cparams(n_axes):
    return pltpu.CompilerParams(dimension_semantics=("arbitrary",) * n_axes,
                                vmem_limit_bytes=VMEM_LIMIT_BYTES)


def _pick_tile(n, pref):
    if n <= pref:
        return n
    t = (pref // LANES) * LANES
    while t > LANES and n % t:
        t -= LANES
    assert n % t == 0, (n, pref)
    return t


def _row_passes(rows):
    step = MXU_ROWS_PER_PASS if rows % MXU_ROWS_PER_PASS == 0 else rows
    return [pl.ds(r, step) for r in range(0, rows, step)]


def _sigmoid(x):
    return 1.0 / (1.0 + jnp.exp(-x))


def _gelu_tanh(x):
    c = 0.7978845608028654
    return 0.5 * x * (1.0 + jnp.tanh(c * (x + 0.044715 * (x * x * x))))


def _mod_row(i, tm, n_ctx, dec_seq):
    start = i * tm
    return jnp.where(start < n_ctx, 0, 1 + (start - n_ctx) // dec_seq)


def _adaln_kernel(c_ref, w_ref, b_ref, o_ref):
    c = c_ref[...]
    s = (c * _sigmoid(c)).astype(BF16)
    o_ref[...] = jnp.dot(s, w_ref[...].astype(BF16), preferred_element_type=F32) + b_ref[...]


def _adaln(cond, w_mod, b_mod):
    depth, d, n = w_mod.shape
    mc = cond.shape[0]
    tn = _pick_tile(n, 1024)
    return pl.pallas_call(
        _adaln_kernel,
        grid=(depth, n // tn),
        in_specs=[pl.BlockSpec((mc, d), lambda l, j: (0, 0)),
                  pl.BlockSpec((None, d, tn), lambda l, j: (l, 0, j)),
                  pl.BlockSpec((None, 1, tn), lambda l, j: (l, 0, j))],
        out_specs=pl.BlockSpec((None, mc, tn), lambda l, j: (l, 0, j)),
        out_shape=jax.ShapeDtypeStruct((depth, mc, n), F32),
        compiler_params=_cparams(2), name="adaln",
    )(cond, w_mod, b_mod.reshape(depth, 1, n))


def _embed_norm_kernel(xp_ref, xs_ref, pos_ref, g_ref, mod_ref, y_ref, h_ref, *, n_ctx_tiles):
    i = pl.program_id(0)

    @pl.when(i < n_ctx_tiles)
    def _():
        y_ref[...] = xp_ref[...]

    @pl.when(i >= n_ctx_tiles)
    def _():
        y_ref[...] = xs_ref[...] + pos_ref[...]

    h_ref[...] = _norm_mod_value(y_ref[...], g_ref[...], mod_ref, 0).astype(h_ref.dtype)


def _embed_norm(xp, xs, pos, g, mod, tm):
    n_ctx, d = xp.shape
    n_dec = xs.shape[0]
    dec_seq = pos.shape[0]
    nct = n_ctx // tm
    ppt = dec_seq // tm
    t = n_ctx + n_dec
    return pl.pallas_call(
        functools.partial(_embed_norm_kernel, n_ctx_tiles=nct),
        grid=(t // tm,),
        in_specs=[pl.BlockSpec((tm, d), lambda i: (jnp.minimum(i, nct - 1), 0)),
                  pl.BlockSpec((tm, d), lambda i: (jnp.maximum(i - nct, 0), 0)),
                  pl.BlockSpec((tm, d), lambda i: (jnp.maximum(i - nct, 0) % ppt, 0)),
                  pl.BlockSpec((1, d), lambda i: (0, 0)),
                  pl.BlockSpec((None, None, 6, d), lambda i: (0, _mod_row(i, tm, n_ctx, dec_seq), 0, 0))],
        out_specs=[pl.BlockSpec((tm, d), lambda i: (i, 0)),
                   pl.BlockSpec((tm, d), lambda i: (i, 0))],
        out_shape=[jax.ShapeDtypeStruct((t, d), F32), jax.ShapeDtypeStruct((t, d), BF16)],
        compiler_params=_cparams(1), name="embed_norm",
    )(xp, xs, pos, g.reshape(1, d), mod)


def _grid_pos_embed(length, d):
    rows = length // GRID_W
    r = jnp.repeat(jnp.arange(rows), GRID_W)
    col = jnp.tile(jnp.arange(GRID_W), rows)
    quarter = d // 4
    omega = 1.0 / (10000.0 ** (jnp.arange(quarter, dtype=F32) / quarter))

    def emb(p):
        ang = p[:, None].astype(F32) * omega[None, :]
        return jnp.concatenate([jnp.sin(ang), jnp.cos(ang)], axis=-1)

    return jnp.concatenate([emb(r), emb(col)], axis=-1).astype(F32)


def _norm_mod_value(y, g, mod_ref, shift_row):
    ms = jnp.mean(y * y, axis=-1, keepdims=True)
    n = (y * lax.rsqrt(ms + EPS)) * g
    return n * (1.0 + mod_ref[shift_row + 1:shift_row + 2, :]) + mod_ref[shift_row:shift_row + 1, :]


def _norm_mod_kernel(y_ref, g_ref, mod_ref, h_ref, *, shift_row):
    h_ref[...] = _norm_mod_value(y_ref[...], g_ref[...], mod_ref, shift_row).astype(h_ref.dtype)


def _norm_mod(y, g, mod, layer, shift_row, n_ctx, dec_seq, tm):
    t, d = y.shape
    return pl.pallas_call(
        functools.partial(_norm_mod_kernel, shift_row=shift_row),
        grid=(t // tm,),
        in_specs=[pl.BlockSpec((tm, d), lambda i: (i, 0)),
                  pl.BlockSpec((1, d), lambda i: (0, 0)),
                  pl.BlockSpec((None, None, 6, d),
                               lambda i: (layer, _mod_row(i, tm, n_ctx, dec_seq), 0, 0))],
        out_specs=pl.BlockSpec((tm, d), lambda i: (i, 0)),
        out_shape=jax.ShapeDtypeStruct((t, d), BF16),
        compiler_params=_cparams(1), name="norm_mod",
    )(y, g.reshape(1, d), mod)


def _norm_router_kernel(y_ref, g_ref, mod_ref, r_ref, rb_ref, h_ref, ids_ref, gates_ref, *,
                        shift_row, n_slab):
    h = _norm_mod_value(y_ref[...], g_ref[...], mod_ref, shift_row)
    tm = h.shape[0]
    half = n_slab * LANES
    bits = lax.bitcast_convert_type(h.astype(jnp.bfloat16).astype(F32), jnp.uint32)
    word = bits[:, half:] | (bits[:, :half] >> 16)
    for s in range(n_slab):
        h_ref[pl.ds(s, tm, stride=n_slab), :] = word[:, s * LANES:(s + 1) * LANES]
    h_hi = h.astype(BF16)
    h_lo = (h - h_hi.astype(F32)).astype(BF16)
    p_hi = jnp.dot(h_hi, r_ref[...], preferred_element_type=F32)
    p_lo = jnp.dot(h_lo, r_ref[:, :LANES], preferred_element_type=F32)
    logits = ((p_hi[:, :LANES] + p_hi[:, LANES:]) + p_lo) + rb_ref[...]
    lane = lax.broadcasted_iota(jnp.int32, logits.shape, 1).astype(F32)
    big = float(LANES)
    m1 = jnp.max(logits, axis=-1, keepdims=True)
    i1 = jnp.min(jnp.where(logits == m1, lane, big), axis=-1, keepdims=True)
    l2 = jnp.where(lane == i1, 2.0 * NEG_BIG, logits)
    m2 = jnp.max(l2, axis=-1, keepdims=True)
    i2 = jnp.min(jnp.where(l2 == m2, lane, big), axis=-1, keepdims=True)
    e = jnp.exp(m2 - m1)
    g1 = 1.0 / (1.0 + e)
    g2 = e / (1.0 + e)
    ids_ref[...] = jnp.where(lane == 0.0, i1, jnp.where(lane == 1.0, i2, 0.0)).astype(jnp.int32)
    gates_ref[...] = jnp.where(lane == 0.0, g1, jnp.where(lane == 1.0, g2, 0.0))


def _norm_router(y, g, mod, layer, shift_row, router, router_b, n_ctx, dec_seq, tm):
    t, d = y.shape
    n_exp = router.shape[1]
    n_slab = d // (2 * LANES)
    rp = jnp.zeros((d, LANES), F32).at[:, :n_exp].set(router)
    r_hi = rp.astype(BF16)
    r_lo = (rp - r_hi.astype(F32)).astype(BF16)
    rp = jnp.concatenate([r_hi, r_lo], axis=1)
    rbp = jnp.full((1, LANES), NEG_BIG, F32).at[0, :n_exp].set(router_b)
    return pl.pallas_call(
        functools.partial(_norm_router_kernel, shift_row=shift_row, n_slab=n_slab),
        grid=(t // tm,),
        in_specs=[pl.BlockSpec((tm, d), lambda i: (i, 0)),
                  pl.BlockSpec((1, d), lambda i: (0, 0)),
                  pl.BlockSpec((None, None, 6, d),
                               lambda i: (layer, _mod_row(i, tm, n_ctx, dec_seq), 0, 0)),
                  pl.BlockSpec((d, 2 * LANES), lambda i: (0, 0)),
                  pl.BlockSpec((1, LANES), lambda i: (0, 0))],
        out_specs=[pl.BlockSpec((tm * n_slab, LANES), lambda i: (i, 0)),
                   pl.BlockSpec((tm, LANES), lambda i: (i, 0)),
                   pl.BlockSpec((tm, LANES), lambda i: (i, 0))],
        out_shape=[jax.ShapeDtypeStruct((t * n_slab, LANES), jnp.uint32),
                   jax.ShapeDtypeStruct((t, LANES), jnp.int32),
                   jax.ShapeDtypeStruct((t, LANES), F32)],
        compiler_params=_cparams(1), name="norm_router",
    )(y, g.reshape(1, d), mod, rp, rbp)


def _final_norm_kernel(y_ref, g_ref, op_ref, os_ref, *, n_ctx_tiles):
    i = pl.program_id(0)
    y = y_ref[...]
    ms = jnp.mean(y * y, axis=-1, keepdims=True)
    n = (y * lax.rsqrt(ms + EPS)) * g_ref[...]

    @pl.when(i < n_ctx_tiles)
    def _():
        op_ref[...] = n

    @pl.when(i >= n_ctx_tiles)
    def _():
        os_ref[...] = n


def _final_norm(y, g, n_ctx, tm):
    t, d = y.shape
    nct = n_ctx // tm
    return pl.pallas_call(
        functools.partial(_final_norm_kernel, n_ctx_tiles=nct),
        grid=(t // tm,),
        in_specs=[pl.BlockSpec((tm, d), lambda i: (i, 0)),
                  pl.BlockSpec((1, d), lambda i: (0, 0))],
        out_specs=[pl.BlockSpec((tm, d), lambda i: (jnp.minimum(i, nct - 1), 0)),
                   pl.BlockSpec((tm, d), lambda i: (jnp.maximum(i - nct, 0), 0))],
        out_shape=[jax.ShapeDtypeStruct((n_ctx, d), F32),
                   jax.ShapeDtypeStruct((t - n_ctx, d), F32)],
        compiler_params=_cparams(1), name="final_norm",
    )(y, g.reshape(1, d))


def _cast_at_first_row_tile(w_refs, wb_refs):
    @pl.when(pl.program_id(1) == 0)
    def _():
        for w, wb in zip(w_refs, wb_refs):
            wb[...] = w[...].astype(BF16)


def _mm_act_kernel(x_ref, w_ref, o_ref, wb, *, act):
    _cast_at_first_row_tile((w_ref,), (wb,))
    for rows in _row_passes(x_ref.shape[0]):
        acc = jnp.dot(x_ref[rows, :], wb[...], preferred_element_type=F32)
        if act == "gelu":
            acc = _gelu_tanh(acc)
        o_ref[rows, :] = acc.astype(o_ref.dtype)


def _mm_act(x, w, layer, act, out_dtype, tm, tn_pref=512):
    t, k = x.shape
    n = w.shape[2]
    tn = _pick_tile(n, tn_pref)
    return pl.pallas_call(
        functools.partial(_mm_act_kernel, act=act),
        grid=(n // tn, t // tm),
        in_specs=[pl.BlockSpec((tm, k), lambda j, i: (i, 0)),
                  pl.BlockSpec((None, k, tn), lambda j, i: (layer, 0, j))],
        out_specs=pl.BlockSpec((tm, tn), lambda j, i: (i, j)),
        out_shape=jax.ShapeDtypeStruct((t, n), out_dtype),
        scratch_shapes=[pltpu.VMEM((k, tn), BF16)],
        compiler_params=_cparams(2), name="mm_" + str(act),
    )(x, w)


def _swiglu_passes(x_ref, wb1, wb3, o_ref):
    for rows in _row_passes(x_ref.shape[0]):
        x = x_ref[rows, :]
        a = jnp.dot(x, wb1[...], preferred_element_type=F32)
        b = jnp.dot(x, wb3[...], preferred_element_type=F32)
        o_ref[rows, :] = ((a * _sigmoid(a)) * b).astype(o_ref.dtype)


def _mm_swiglu_kernel(x_ref, w1_ref, w3_ref, o_ref, wb1, wb3):
    _cast_at_first_row_tile((w1_ref, w3_ref), (wb1, wb3))
    _swiglu_passes(x_ref, wb1, wb3, o_ref)


def _mm_swiglu(x, w1, w3, layer, tm, tn_pref=512):
    t, k = x.shape
    n = w1.shape[2]
    tn = _pick_tile(n, tn_pref)
    return pl.pallas_call(
        _mm_swiglu_kernel,
        grid=(n // tn, t // tm),
        in_specs=[pl.BlockSpec((tm, k), lambda j, i: (i, 0)),
                  pl.BlockSpec((None, k, tn), lambda j, i: (layer, 0, j)),
                  pl.BlockSpec((None, k, tn), lambda j, i: (layer, 0, j))],
        out_specs=pl.BlockSpec((tm, tn), lambda j, i: (i, j)),
        out_shape=jax.ShapeDtypeStruct((t, n), BF16),
        scratch_shapes=[pltpu.VMEM((k, tn), BF16), pltpu.VMEM((k, tn), BF16)],
        compiler_params=_cparams(2), name="mm_swiglu",
    )(x, w1, w3)


def _mm_sconv_kernel(x_ref, wb_ref, wc_ref, wx_ref, cw_ref, o_ref, sb, sc, sx, *, n_ctx_tiles, seq_ctx):
    _cast_at_first_row_tile((wb_ref, wc_ref, wx_ref), (sb, sc, sx))
    i = pl.program_id(1)
    is_ctx = i < n_ctx_tiles
    x = x_ref[...]
    bg = jnp.dot(x, sb[...], preferred_element_type=F32)
    p = jnp.dot(x, sc[...], preferred_element_type=F32) * jnp.dot(x, sx[...], preferred_element_type=F32)
    tm = p.shape[0]
    row = lax.broadcasted_iota(jnp.int32, p.shape, 0)
    pos = jnp.where(is_ctx, row % seq_ctx, row)
    last_pos = jnp.where(is_ctx, seq_ctx - 1, tm - 1)
    p_prev = jnp.where(pos == 0, 0.0, pltpu.roll(p, 1, axis=0))
    p_next = jnp.where(pos == last_pos, 0.0, pltpu.roll(p, tm - 1, axis=0))
    cw = cw_ref[...]
    conv = (cw[0:1] * p_prev + cw[1:2] * p) + cw[2:3] * p_next
    o_ref[...] = (bg * conv).astype(o_ref.dtype)


def _mm_sconv(x, w_in, conv_w, layer, n_ctx, seq_ctx, tm, tn_pref=512):
    t, k = x.shape
    d = conv_w.shape[2]
    tn = _pick_tile(d, tn_pref)
    nj = d // tn
    return pl.pallas_call(
        functools.partial(_mm_sconv_kernel, n_ctx_tiles=n_ctx // tm, seq_ctx=seq_ctx),
        grid=(nj, t // tm),
        in_specs=[pl.BlockSpec((tm, k), lambda j, i: (i, 0)),
                  pl.BlockSpec((None, k, tn), lambda j, i: (layer, 0, j)),
                  pl.BlockSpec((None, k, tn), lambda j, i: (layer, 0, nj + j)),
                  pl.BlockSpec((None, k, tn), lambda j, i: (layer, 0, 2 * nj + j)),
                  pl.BlockSpec((None, conv_w.shape[1], tn), lambda j, i: (layer, 0, j))],
        out_specs=pl.BlockSpec((tm, tn), lambda j, i: (i, j)),
        out_shape=jax.ShapeDtypeStruct((t, d), BF16),
        scratch_shapes=[pltpu.VMEM((k, tn), BF16)] * 3,
        compiler_params=_cparams(2), name="mm_sconv",
    )(x, w_in, w_in, w_in, conv_w)


def _mm_residual_kernel(x_ref, w_ref, y_ref, mod_ref, o_ref, wb, *, gate_row):
    _cast_at_first_row_tile((w_ref,), (wb,))
    gate = mod_ref[gate_row:gate_row + 1, :]
    for rows in _row_passes(x_ref.shape[0]):
        acc = jnp.dot(x_ref[rows, :], wb[...], preferred_element_type=F32)
        o_ref[rows, :] = y_ref[rows, :] + gate * acc


def _mm_residual(x, w, w_layer, y, mod, layer, gate_row, n_ctx, dec_seq, tm, tn_pref=512, w_buffers=2):
    t, k = x.shape
    n = w.shape[2]
    tn = _pick_tile(n, tn_pref)
    return pl.pallas_call(
        functools.partial(_mm_residual_kernel, gate_row=gate_row),
        grid=(n // tn, t // tm),
        in_specs=[pl.BlockSpec((tm, k), lambda j, i: (i, 0)),
                  pl.BlockSpec((None, k, tn), lambda j, i: (w_layer, 0, j),
                               pipeline_mode=pl.Buffered(w_buffers)),
                  pl.BlockSpec((tm, tn), lambda j, i: (i, j)),
                  pl.BlockSpec((None, None, 6, tn),
                               lambda j, i: (layer, _mod_row(i, tm, n_ctx, dec_seq), 0, j))],
        out_specs=pl.BlockSpec((tm, tn), lambda j, i: (i, j)),
        out_shape=jax.ShapeDtypeStruct((t, n), F32),
        scratch_shapes=[pltpu.VMEM((k, tn), BF16)],
        compiler_params=_cparams(2), name="mm_residual",
    )(x, w, y, mod)


def _residual_norm_kernel(x_ref, w_hbm, y_ref, mod_ref, g_ref, yo_ref, ho_ref, wb, stage, sem, *,
                          w_layer, gate_row, shift_row, chunk):
    k = wb.shape[0]
    nchunk = k // chunk

    @pl.when(pl.program_id(0) == 0)
    def _():
        def copy(c):
            return pltpu.make_async_copy(w_hbm.at[w_layer, pl.ds(c * chunk, chunk), :],
                                         stage.at[c % 2], sem.at[c % 2])

        copy(0).start()
        for c in range(nchunk):
            if c + 1 < nchunk:
                copy(c + 1).start()
            copy(c).wait()
            wb[pl.ds(c * chunk, chunk), :] = stage[c % 2].astype(BF16)

    gate = mod_ref[gate_row:gate_row + 1, :]
    g = g_ref[...]
    for rows in _row_passes(x_ref.shape[0]):
        acc = jnp.dot(x_ref[rows, :], wb[...], preferred_element_type=F32)
        y = y_ref[rows, :] + gate * acc
        yo_ref[rows, :] = y
        ho_ref[rows, :] = _norm_mod_value(y, g, mod_ref, shift_row).astype(ho_ref.dtype)


def _residual_norm(x, w, w_layer, y, mod, layer, gate_row, next_g, shift_row, n_ctx, dec_seq, tm):
    t, k = x.shape
    d = w.shape[2]
    chunk = _pick_tile(k, 512)
    return pl.pallas_call(
        functools.partial(_residual_norm_kernel, w_layer=w_layer, gate_row=gate_row,
                          shift_row=shift_row, chunk=chunk),
        grid=(t // tm,),
        in_specs=[pl.BlockSpec((tm, k), lambda i: (i, 0)),
                  pl.BlockSpec(memory_space=pl.ANY),
                  pl.BlockSpec((tm, d), lambda i: (i, 0)),
                  pl.BlockSpec((None, None, 6, d),
                               lambda i: (layer, _mod_row(i, tm, n_ctx, dec_seq), 0, 0)),
                  pl.BlockSpec((1, d), lambda i: (0, 0))],
        out_specs=[pl.BlockSpec((tm, d), lambda i: (i, 0)),
                   pl.BlockSpec((tm, d), lambda i: (i, 0))],
        out_shape=[jax.ShapeDtypeStruct((t, d), F32), jax.ShapeDtypeStruct((t, d), BF16)],
        scratch_shapes=[pltpu.VMEM((k, d), BF16), pltpu.VMEM((2, chunk, d), F32),
                        pltpu.SemaphoreType.DMA((2,))],
        compiler_params=_cparams(1), name="residual_norm",
    )(x, w, y, mod, next_g.reshape(1, d))


def _rg_core_kernel(gate_ref, x_ref, cw_ref, cb_ref, wg_ref, bg_ref, lam_ref, h0_ref,
                    y_ref, st_ref, af_s, bf_s, ab_s, bb_s, *, n_ctx_tiles, seg, tm, cg):
    i = pl.program_id(1)
    is_ctx = i < n_ctx_tiles
    nseg = tm // seg
    nslab = cg // LANES
    lsub = tm // SUBLANES
    sub_per_seq = seg // lsub
    cw = cw_ref[...]
    cb = cb_ref[...]
    lam = lam_ref[...]
    softplus_neg_lam = jnp.maximum(-lam, 0.0) + jnp.log1p(jnp.exp(-jnp.abs(lam)))
    neg_c_sp = (-RG_C) * softplus_neg_lam
    zeros8 = jnp.zeros((SUBLANES, cg), F32)

    cr = min(RG_CHUNK_ROWS, lsub)
    for c0 in range(0, tm, cr):
        cur = x_ref[pl.ds(c0, cr), :]
        if c0 == 0:
            prev8 = zeros8
        else:
            prev8 = x_ref[pl.ds(c0 - SUBLANES, SUBLANES), :]
            if c0 % seg == 0:
                prev8 = jnp.where(is_ctx, 0.0, prev8)
        if c0 + cr == tm:
            next8 = zeros8
        else:
            next8 = x_ref[pl.ds(c0 + cr, SUBLANES), :]
            if (c0 + cr) % seg == 0:
                next8 = jnp.where(is_ctx, 0.0, next8)
        head = jnp.concatenate([prev8, cur[:2 * SUBLANES]], axis=0)
        tail = jnp.concatenate([cur[cr - 2 * SUBLANES:], next8], axis=0)

        def shifted(delta):
            lo = SUBLANES + delta
            mid = x_ref[pl.ds(c0 + lo, cr - 2 * SUBLANES), :]
            return jnp.concatenate([head[lo:lo + SUBLANES], mid, tail[lo:lo + SUBLANES]], axis=0)

        xc = (((cw[0:1] * shifted(-2) + cw[1:2] * shifted(-1)) + cw[2:3] * cur)
              + cw[3:4] * shifted(1)) + cb
        g = jnp.dot(xc.astype(BF16), wg_ref[...], preferred_element_type=F32) + bg_ref[...]
        for d, (a_s, b_s) in enumerate(((af_s, bf_s), (ab_s, bb_s))):
            r = _sigmoid(g[:, (2 * d) * cg:(2 * d + 1) * cg])
            ig = _sigmoid(g[:, (2 * d + 1) * cg:(2 * d + 2) * cg])
            log_a = neg_c_sp[d:d + 1] * r
            a = jnp.exp(log_a)
            one_minus_a2 = -jnp.tanh(log_a) * (a * a + 1.0)
            root = jnp.where(one_minus_a2 > 0.0, one_minus_a2 * lax.rsqrt(one_minus_a2), 0.0)
            bt = root * (ig * xc)
            k, j0 = c0 // lsub, c0 % lsub
            dst = pl.ds(k + SUBLANES * j0, cr, stride=SUBLANES)
            for l in range(nslab):
                a_s[l, dst, :] = a[:, l * LANES:(l + 1) * LANES]
                b_s[l, dst, :] = bt[:, l * LANES:(l + 1) * LANES]

    def local_scan(j, carry):
        hf, pf, hb, pb = carry
        rf = pl.multiple_of(j * SUBLANES, SUBLANES)
        rb = pl.multiple_of((lsub - 1 - j) * SUBLANES, SUBLANES)
        nhf, npf, nhb, npb = [], [], [], []
        for l in range(nslab):
            a = af_s[l, pl.ds(rf, SUBLANES), :]
            h = a * hf[l] + bf_s[l, pl.ds(rf, SUBLANES), :]
            p = a * pf[l]
            bf_s[l, pl.ds(rf, SUBLANES), :] = h
            af_s[l, pl.ds(rf, SUBLANES), :] = p
            nhf.append(h)
            npf.append(p)
            a = ab_s[l, pl.ds(rb, SUBLANES), :]
            h = a * hb[l] + bb_s[l, pl.ds(rb, SUBLANES), :]
            p = a * pb[l]
            bb_s[l, pl.ds(rb, SUBLANES), :] = h
            ab_s[l, pl.ds(rb, SUBLANES), :] = p
            nhb.append(h)
            npb.append(p)
        return tuple(nhf), tuple(npf), tuple(nhb), tuple(npb)

    zero = tuple(jnp.zeros((SUBLANES, LANES), F32) for _ in range(nslab))
    one = tuple(jnp.ones((SUBLANES, LANES), F32) for _ in range(nslab))
    hf_end, pf_end, hb_end, pb_end = lax.fori_loop(0, lsub, local_scan, (zero, one, zero, one))

    row8 = lax.broadcasted_iota(jnp.int32, (SUBLANES, LANES), 0)
    h0 = h0_ref[...]
    for l in range(nslab):
        lanes = slice(l * LANES, (l + 1) * LANES)
        ent_f = jnp.zeros((SUBLANES, LANES), F32)
        h_in = h0[0:1, lanes]
        for k in range(SUBLANES):
            if k > 0 and k % sub_per_seq == 0:
                h_in = jnp.where(is_ctx, 0.0, h_in)
            ent_f = jnp.where(row8 == k, h_in, ent_f)
            h_in = hf_end[l][k:k + 1] + pf_end[l][k:k + 1] * h_in
            if (k + 1) % sub_per_seq == 0:
                q = k // sub_per_seq
                st_ref[q:q + 1, lanes] = h_in
        ent_b = jnp.zeros((SUBLANES, LANES), F32)
        h_in = h0[1:2, lanes]
        for k in reversed(range(SUBLANES)):
            if k < SUBLANES - 1 and (k + 1) % sub_per_seq == 0:
                h_in = jnp.where(is_ctx, 0.0, h_in)
            ent_b = jnp.where(row8 == k, h_in, ent_b)
            h_in = hb_end[l][k:k + 1] + pb_end[l][k:k + 1] * h_in
            if k % sub_per_seq == 0:
                q = k // sub_per_seq
                st_ref[nseg + q:nseg + q + 1, lanes] = h_in
        ch = min(tm, 256)
        ef = jnp.concatenate([ent_f] * (ch // SUBLANES), axis=0)
        eb = jnp.concatenate([ent_b] * (ch // SUBLANES), axis=0)
        for r0 in range(0, tm, ch):
            rows = pl.ds(r0, ch)
            bf_s[l, rows, :] = ((bf_s[l, rows, :] + af_s[l, rows, :] * ef)
                                + (bb_s[l, rows, :] + ab_s[l, rows, :] * eb))

    for k in range(SUBLANES):
        rows = pl.ds(k * lsub, lsub)
        hsum = jnp.concatenate([bf_s[l, pl.ds(k, lsub, stride=SUBLANES), :] for l in range(nslab)], axis=1)
        y_ref[rows, :] = (hsum * _gelu_tanh(gate_ref[rows, :])).astype(y_ref.dtype)


def _rg_core(z, conv_w, conv_b, w_a, b_a, w_x, b_x, lam, h0, n_ctx, seq_ctx, tm):
    t = z.shape[0]
    r = conv_w.shape[1]
    heads, hw = w_a.shape[1], w_a.shape[2]
    hpg = 4
    while (hpg * hw) % LANES:
        hpg *= 2
    ng = heads // hpg
    cg = hpg * hw
    nseg = tm // seq_ctx
    ntiles = t // tm
    assert tm % SUBLANES == 0 and seq_ctx % (tm // SUBLANES) == 0

    def blockdiag(w):
        rows = jnp.tile(w.reshape(2, ng, cg, hw), (1, 1, 1, hpg))
        head = jnp.arange(cg, dtype=jnp.int32) // hw
        return jnp.where(head[:, None] == head[None, :], rows, 0.0)

    wa, wx = blockdiag(w_a), blockdiag(w_x)
    wg = jnp.concatenate([wa[0], wx[0], wa[1], wx[1]], axis=-1).astype(BF16)
    ba, bx = b_a.reshape(2, ng, 1, cg), b_x.reshape(2, ng, 1, cg)
    bg = jnp.concatenate([ba[0], bx[0], ba[1], bx[1]], axis=-1)

    return pl.pallas_call(
        functools.partial(_rg_core_kernel, n_ctx_tiles=n_ctx // tm, seg=seq_ctx, tm=tm, cg=cg),
        grid=(ng, ntiles),
        in_specs=[pl.BlockSpec((tm, cg), lambda g, i: (i, g)),
                  pl.BlockSpec((tm, cg), lambda g, i: (i, ng + g)),
                  pl.BlockSpec((conv_w.shape[0], cg), lambda g, i: (0, g)),
                  pl.BlockSpec((1, cg), lambda g, i: (0, g)),
                  pl.BlockSpec((None, cg, 4 * cg), lambda g, i: (g, 0, 0)),
                  pl.BlockSpec((None, 1, 4 * cg), lambda g, i: (g, 0, 0)),
                  pl.BlockSpec((2, cg), lambda g, i: (0, g)),
                  pl.BlockSpec((None, 2, cg), lambda g, i: (i, 0, g))],
        out_specs=[pl.BlockSpec((tm, cg), lambda g, i: (i, g)),
                   pl.BlockSpec((None, 2 * nseg, cg), lambda g, i: (i, 0, g))],
        out_shape=[jax.ShapeDtypeStruct((t, r), BF16),
                   jax.ShapeDtypeStruct((ntiles, 2 * nseg, r), F32)],
        scratch_shapes=[pltpu.VMEM((cg // LANES, tm, LANES), F32)] * 4,
        compiler_params=_cparams(2), name="rg_core",
    )(z, z, conv_w, conv_b.reshape(1, r), wg, bg, lam, h0)


def _sgu_core_kernel(u_ref, v_ref, g_ref, ws_ref, bs_ref, y_ref, vn_s, *, chunk, gw, ngroups, tm):
    v = v_ref[...].astype(F32)
    ms = jnp.mean(v * v, axis=-1, keepdims=True)
    vn_s[...] = ((v * lax.rsqrt(ms + EPS)) * g_ref[...]).astype(BF16)
    for c in range(tm // chunk):
        rows = pl.ds(c * chunk, chunk)
        for g in range(ngroups):
            cols = pl.ds(g * gw, gw)
            vm = jnp.dot(ws_ref[g], vn_s[rows, cols], preferred_element_type=F32) + bs_ref[:, g:g + 1]
            y_ref[rows, cols] = (u_ref[rows, cols].astype(F32) * vm).astype(y_ref.dtype)


def _sgu_core(z, norm_g, w_s, b_s, tm):
    t = z.shape[0]
    w = norm_g.shape[0]
    ngroups, chunk = w_s.shape[0], w_s.shape[1]
    gw = w // ngroups
    return pl.pallas_call(
        functools.partial(_sgu_core_kernel, chunk=chunk, gw=gw, ngroups=ngroups, tm=tm),
        grid=(t // tm,),
        in_specs=[pl.BlockSpec((tm, w), lambda i: (i, 0)),
                  pl.BlockSpec((tm, w), lambda i: (i, 1)),
                  pl.BlockSpec((1, w), lambda i: (0, 0)),
                  pl.BlockSpec((ngroups, chunk, chunk), lambda i: (0, 0, 0)),
                  pl.BlockSpec((chunk, ngroups), lambda i: (0, 0))],
        out_specs=pl.BlockSpec((tm, w), lambda i: (i, 0)),
        out_shape=jax.ShapeDtypeStruct((t, w), BF16),
        scratch_shapes=[pltpu.VMEM((tm, w), BF16)],
        compiler_params=_cparams(1), name="sgu_core",
    )(z, z, norm_g.reshape(1, w), w_s.astype(BF16), b_s.T)


def _moe_plan(ids, n_exp, bm):
    t = ids.shape[0]
    e = ids[:, :TOP_K].reshape(-1)
    oh = (e[:, None] == jnp.arange(n_exp, dtype=jnp.int32)[None, :]).astype(jnp.int32)
    csum = jnp.cumsum(oh, axis=0)
    rank = jnp.sum((csum - oh) * oh, axis=1)
    counts = csum[-1]
    padded = ((counts + bm - 1) // bm) * bm
    ends = jnp.cumsum(padded)
    starts = ends - padded
    first_cnt = counts - jnp.maximum(padded - bm, 0)
    start_e = jnp.sum(starts[None, :] * oh, axis=1)
    first_e = jnp.sum(first_cnt[None, :] * oh, axis=1)
    dest = (start_e + rank + jnp.where(rank >= first_e, bm - first_e, 0)).astype(jnp.int32)
    p = TOP_K * t + n_exp * bm
    src = jnp.zeros((p,), jnp.int32).at[dest].set(jnp.arange(TOP_K * t, dtype=jnp.int32) // TOP_K)

    def block_table(blk):
        idx = jnp.arange(p // blk, dtype=jnp.int32)
        start = idx * blk
        exp = jnp.minimum(jnp.sum((start[:, None] >= ends[None, :]).astype(jnp.int32), axis=1), n_exp - 1)
        offset = start - starts[exp]
        valid = jnp.where(offset < bm, jnp.clip(first_cnt[exp] - offset, 0, blk), blk)
        valid = jnp.where(start < ends[-1], valid, 0)
        fetch = lax.cummax(jnp.where(valid > 0, idx, 0), axis=0)
        return exp.astype(jnp.int32), valid.astype(jnp.int32), fetch.astype(jnp.int32)

    return dest, src, block_table


def _gather_rows_kernel(src_ref, valid_ref, h_hbm, o_ref, buf, sem, *, rows, n_slab):
    b = pl.program_id(0)
    nb = pl.num_programs(0)
    slot = b % DMA_RING
    ahead = DMA_RING - 1

    def block_has_rows(blk):
        return jnp.logical_and(blk < nb, valid_ref[jnp.minimum(blk, nb - 1)] > 0)

    has_rows = block_has_rows(b)

    def issue(blk, to_slot):
        base = blk * rows

        def body(q, carry):
            for prio in range(2):
                r = 2 * q + prio
                tok = pl.multiple_of(src_ref[base + r] * n_slab, n_slab)
                pltpu.make_async_copy(h_hbm.at[pl.ds(tok, n_slab), :],
                                      buf.at[to_slot, pl.ds(pl.multiple_of(r * n_slab, n_slab), n_slab), :],
                                      sem.at[to_slot]).start(priority=prio)
            return carry

        lax.fori_loop(0, rows // 2, body, 0, unroll=4)

    @pl.when(b == 0)
    def _():
        for first in range(ahead):
            @pl.when(block_has_rows(first))
            def _():
                issue(first, first % DMA_RING)

    @pl.when(block_has_rows(b + ahead))
    def _():
        issue(b + ahead, (b + ahead) % DMA_RING)

    @pl.when(has_rows)
    def _():
        pltpu.make_async_copy(h_hbm.at[pl.ds(0, rows * n_slab), :], buf.at[slot], sem.at[slot]).wait()
        half = n_slab * LANES
        for s in range(n_slab):
            word = buf[slot, pl.ds(s, rows, stride=n_slab), :]
            lo = lax.bitcast_convert_type(word << 16, F32)
            hi = lax.bitcast_convert_type(word & jnp.uint32(0xFFFF0000), F32)
            o_ref[:, s * LANES:(s + 1) * LANES] = lo.astype(o_ref.dtype)
            o_ref[:, half + s * LANES:half + (s + 1) * LANES] = hi.astype(o_ref.dtype)

    @pl.when(jnp.logical_not(has_rows))
    def _():
        o_ref[...] = jnp.zeros_like(o_ref)


def _gather_rows(h_tok, src, valid, bm, d):
    n_slab = d // (2 * LANES)
    p = src.shape[0]
    return pl.pallas_call(
        functools.partial(_gather_rows_kernel, rows=bm, n_slab=n_slab),
        grid_spec=pltpu.PrefetchScalarGridSpec(
            num_scalar_prefetch=2,
            grid=(p // bm,),
            in_specs=[pl.BlockSpec(memory_space=pl.ANY)],
            out_specs=pl.BlockSpec((bm, d), lambda b, src, va: (b, 0)),
            scratch_shapes=[pltpu.VMEM((DMA_RING, bm * n_slab, LANES), jnp.uint32),
                            pltpu.SemaphoreType.DMA((DMA_RING,))]),
        out_shape=jax.ShapeDtypeStruct((p, d), BF16),
        compiler_params=_cparams(1), name="moe_gather",
    )(src, valid, h_tok)


def _expert_changed(exp_ref, fetch_ref, b):
    prev = fetch_ref[jnp.maximum(b - 1, 0)]
    return jnp.logical_or(b == 0, exp_ref[b] != exp_ref[prev])


def _guarded_passes(valid, x_ref, o_ref, one_pass):
    bm = x_ref.shape[0]
    passes = _row_passes(bm)
    pass_rows = bm // len(passes)
    needed = (valid + (pass_rows - 1)) // pass_rows

    for count in range(len(passes) + 1):
        @pl.when(needed == count)
        def _():
            for rows in passes[:count]:
                one_pass(rows)
            for rows in passes[count:]:
                o_ref[rows, :] = jnp.zeros((pass_rows, o_ref.shape[1]), o_ref.dtype)


def _grouped_swiglu_kernel(exp_ref, valid_ref, fetch_ref, x_ref, w1_ref, w3_ref, o_ref, wb1, wb3):
    b = pl.program_id(1)
    valid = valid_ref[b]

    @pl.when(jnp.logical_and(valid > 0, _expert_changed(exp_ref, fetch_ref, b)))
    def _():
        wb1[...] = w1_ref[...].astype(BF16)
        wb3[...] = w3_ref[...].astype(BF16)

    def one_pass(rows):
        x = x_ref[rows, :]
        a = jnp.dot(x, wb1[...], preferred_element_type=F32)
        c = jnp.dot(x, wb3[...], preferred_element_type=F32)
        o_ref[rows, :] = ((a * _sigmoid(a)) * c).astype(o_ref.dtype)

    _guarded_passes(valid, x_ref, o_ref, one_pass)


def _grouped_swiglu(xs, w1, w3, layer, table, bm, tn_pref=512):
    p, k = xs.shape
    n = w1.shape[3]
    tn = _pick_tile(n, tn_pref)
    return pl.pallas_call(
        _grouped_swiglu_kernel,
        grid_spec=pltpu.PrefetchScalarGridSpec(
            num_scalar_prefetch=3,
            grid=(n // tn, p // bm),
            in_specs=[pl.BlockSpec((bm, k), lambda j, b, ex, va, fe: (fe[b], 0)),
                      pl.BlockSpec((None, None, k, tn), lambda j, b, ex, va, fe: (layer, ex[fe[b]], 0, j)),
                      pl.BlockSpec((None, None, k, tn), lambda j, b, ex, va, fe: (layer, ex[fe[b]], 0, j))],
            out_specs=pl.BlockSpec((bm, tn), lambda j, b, ex, va, fe: (b, j)),
            scratch_shapes=[pltpu.VMEM((k, tn), BF16), pltpu.VMEM((k, tn), BF16)]),
        out_shape=jax.ShapeDtypeStruct((p, n), BF16),
        compiler_params=_cparams(2), name="moe_swiglu",
    )(*table, xs, w1, w3)


def _grouped_down_kernel(exp_ref, valid_ref, fetch_ref, x_ref, w_ref, o_ref, wb):
    b = pl.program_id(1)
    valid = valid_ref[b]

    @pl.when(jnp.logical_and(valid > 0, _expert_changed(exp_ref, fetch_ref, b)))
    def _():
        wb[...] = w_ref[...].astype(BF16)

    def one_pass(rows):
        o_ref[rows, :] = jnp.dot(x_ref[rows, :], wb[...], preferred_element_type=F32)

    _guarded_passes(valid, x_ref, o_ref, one_pass)


def _grouped_down(gs, w2, layer, table, bm, tn_pref=512):
    p, k = gs.shape
    n = w2.shape[3]
    tn = _pick_tile(n, tn_pref)
    return pl.pallas_call(
        _grouped_down_kernel,
        grid_spec=pltpu.PrefetchScalarGridSpec(
            num_scalar_prefetch=3,
            grid=(n // tn, p // bm),
            in_specs=[pl.BlockSpec((bm, k), lambda j, b, ex, va, fe: (fe[b], 0)),
                      pl.BlockSpec((None, None, k, tn), lambda j, b, ex, va, fe: (layer, ex[fe[b]], 0, j))],
            out_specs=pl.BlockSpec((bm, tn), lambda j, b, ex, va, fe: (b, j)),
            scratch_shapes=[pltpu.VMEM((k, tn), BF16)]),
        out_shape=jax.ShapeDtypeStruct((p, n), F32),
        compiler_params=_cparams(2), name="moe_down",
    )(*table, gs, w2)


def _combine_kernel(dest_ref, os_hbm, y_ref, gates_ref, mod_ref, g_ref, *rest,
                    rows, gate_row, n_ctx_tiles, final):
    if final:
        out_a, out_b, buf, sem = rest
    else:
        nmod_ref, out_a, out_b, buf, sem = rest
    i = pl.program_id(0)
    n_tiles = pl.num_programs(0)
    slot = i % DMA_RING
    ahead = DMA_RING - 1

    def issue(tile, to_slot):
        base = tile * rows

        def body(r, carry):
            for k in range(TOP_K):
                row = dest_ref[TOP_K * (base + r) + k]
                pltpu.make_async_copy(os_hbm.at[pl.ds(row, 1), :], buf.at[to_slot, k, pl.ds(r, 1), :],
                                      sem.at[to_slot]).start(priority=k)
            return carry

        lax.fori_loop(0, rows, body, 0, unroll=4)

    @pl.when(i == 0)
    def _():
        for first in range(ahead):
            @pl.when(first < n_tiles)
            def _():
                issue(first, first % DMA_RING)

    @pl.when(i + ahead < n_tiles)
    def _():
        issue(i + ahead, (i + ahead) % DMA_RING)

    for k in range(TOP_K):
        pltpu.make_async_copy(os_hbm.at[pl.ds(0, rows), :], buf.at[slot, k], sem.at[slot]).wait()
    gates = gates_ref[...]
    f = gates[:, 0:1] * buf[slot, 0] + gates[:, 1:2] * buf[slot, 1]
    y = y_ref[...] + mod_ref[gate_row:gate_row + 1, :] * f
    if final:
        ms = jnp.mean(y * y, axis=-1, keepdims=True)
        n = (y * lax.rsqrt(ms + EPS)) * g_ref[...]

        @pl.when(i < n_ctx_tiles)
        def _():
            out_a[...] = n

        @pl.when(i >= n_ctx_tiles)
        def _():
            out_b[...] = n
    else:
        out_a[...] = y
        out_b[...] = _norm_mod_value(y, g_ref[...], nmod_ref, 0).astype(out_b.dtype)


def _combine(os_, dest, gates, y, mod, layer, gate_row, next_g, final, n_ctx, dec_seq, tm):
    t, d = y.shape
    nct = n_ctx // tm

    def mod_spec(which):
        return pl.BlockSpec((None, None, 6, d),
                            lambda i, dst: (which, _mod_row(i, tm, n_ctx, dec_seq), 0, 0))

    in_specs = [pl.BlockSpec(memory_space=pl.ANY),
                pl.BlockSpec((tm, d), lambda i, dst: (i, 0)),
                pl.BlockSpec((tm, LANES), lambda i, dst: (i, 0)),
                mod_spec(layer),
                pl.BlockSpec((1, d), lambda i, dst: (0, 0))]
    args = [dest, os_, y, gates, mod, next_g.reshape(1, d)]
    if final:
        out_specs = [pl.BlockSpec((tm, d), lambda i, dst: (jnp.minimum(i, nct - 1), 0)),
                     pl.BlockSpec((tm, d), lambda i, dst: (jnp.maximum(i - nct, 0), 0))]
        out_shape = [jax.ShapeDtypeStruct((n_ctx, d), F32), jax.ShapeDtypeStruct((t - n_ctx, d), F32)]
    else:
        in_specs.append(mod_spec(layer + 1))
        args.append(mod)
        out_specs = [pl.BlockSpec((tm, d), lambda i, dst: (i, 0)),
                     pl.BlockSpec((tm, d), lambda i, dst: (i, 0))]
        out_shape = [jax.ShapeDtypeStruct((t, d), F32), jax.ShapeDtypeStruct((t, d), BF16)]
    return pl.pallas_call(
        functools.partial(_combine_kernel, rows=tm, gate_row=gate_row, n_ctx_tiles=nct, final=final),
        grid_spec=pltpu.PrefetchScalarGridSpec(
            num_scalar_prefetch=1,
            grid=(t // tm,),
            in_specs=in_specs,
            out_specs=out_specs,
            scratch_shapes=[pltpu.VMEM((DMA_RING, TOP_K, tm, d), F32),
                            pltpu.SemaphoreType.DMA((DMA_RING,))]),
        out_shape=out_shape,
        compiler_params=_cparams(1), name="moe_combine_final" if final else "moe_combine",
    )(*args)


def kernel(x_prompt, x_sample, state_rglru, c, c_ctx, norm_mix_g, norm_ffn_g, w_mod, b_mod, final_norm_g, rg_w_in, rg_conv_w, rg_conv_b, rg_w_a, rg_b_a, rg_w_x, rg_b_x, rg_lam, rg_w_out, sg_w_in, sg_norm_g, sg_w_s, sg_b_s, sg_w_out, sc_w_in, sc_conv_w, sc_w_out, ff_w1, ff_w3, ff_w2, moe_router, moe_router_b, moe_w1, moe_w3, moe_w2):
    batch, seq, d = x_prompt.shape
    dec_batch, dec_seq, _ = x_sample.shape
    depth = w_mod.shape[0]
    n_ctx = batch * seq
    d_rnn = rg_w_out.shape[1]
    n_exp = moe_router.shape[2]
    chunk = sg_w_s.shape[2]

    tm = dec_seq
    assert dec_seq % seq == 0 and n_ctx % tm == 0 and seq % SUBLANES == 0
    tm_half = max(tm // 2, chunk)
    tm_small = max(tm // 4, chunk)
    assert tm % tm_half == 0 and tm % tm_small == 0 and tm_small % chunk == 0
    moe_bm, moe_sub = tm, tm_half
    tm_wide = 2 * tm if (n_ctx + dec_batch * dec_seq) % (2 * tm) == 0 else tm

    n_cond = 1 + dec_batch
    cond = jnp.zeros((-(-n_cond // SUBLANES) * SUBLANES, d), F32)
    cond = cond.at[0].set(c_ctx).at[1:n_cond].set(c)
    mod = _adaln(cond, w_mod, b_mod)[:, :n_cond].reshape(depth, n_cond, 6, d)

    y, h = _embed_norm(x_prompt.reshape(n_ctx, d), x_sample.reshape(dec_batch * dec_seq, d),
                       _grid_pos_embed(dec_seq, d), norm_mix_g[0], mod, tm_small)

    states = []
    outs = None
    for i in range(depth):
        kind, j = i % 3, i // 3
        if h is None:
            h = _norm_mod(y, norm_mix_g[i], mod, i, 0, n_ctx, dec_seq, tm_half)
        if kind == 0:
            z = _mm_act(h, rg_w_in, j, None, F32, tm_wide, tn_pref=1024)
            h0 = jnp.concatenate([jnp.zeros((n_ctx // tm, 2, d_rnn), F32),
                                  state_rglru[:, j].astype(F32)], axis=0)
            mix, st = _rg_core(z, rg_conv_w[j], rg_conv_b[j], rg_w_a[j], rg_b_a[j], rg_w_x[j], rg_b_x[j],
                               rg_lam[j], h0, n_ctx, seq, tm)
            nseg = tm // seq
            st = st[:n_ctx // tm].reshape(n_ctx // tm, 2, nseg, d_rnn)
            states.append(jnp.transpose(st, (0, 2, 1, 3)).reshape(batch, 2, d_rnn))
        elif kind == 1:
            z = _mm_act(h, sg_w_in, j, "gelu", BF16, tm_wide, tn_pref=1024)
            mix = _sgu_core(z, sg_norm_g[j], sg_w_s[j], sg_b_s[j], tm_small)
        else:
            mix = _mm_sconv(h, sc_w_in, sc_conv_w, j, n_ctx, seq, tm)
        w_out = (rg_w_out, sg_w_out, sc_w_out)[kind]
        h = None
        if i % 2 == 0 and mix.shape[1] * d * 2 <= RESIDENT_WEIGHT_BYTES:
            y, h = _residual_norm(mix, w_out, j, y, mod, i, 2, norm_ffn_g[i], 3, n_ctx, dec_seq, tm_half)
        else:
            y = _mm_residual(mix, w_out, j, y, mod, i, 2, n_ctx, dec_seq,
                             tm if mix.shape[1] <= 4096 else tm_half)

        f = i // 2
        last = i == depth - 1
        if i % 2 == 0:
            if h is None:
                h = _norm_mod(y, norm_ffn_g[i], mod, i, 3, n_ctx, dec_seq, tm_half)
            g = _mm_swiglu(h, ff_w1, ff_w3, f, tm_wide)
            y = _mm_residual(g, ff_w2, f, y, mod, i, 5, n_ctx, dec_seq, tm, w_buffers=1)
            h = None
        else:
            h_tok, ids, gates = _norm_router(y, norm_ffn_g[i], mod, i, 3, moe_router[f], moe_router_b[f],
                                             n_ctx, dec_seq, tm_half)
            dest, src, block_table = _moe_plan(ids, n_exp, moe_bm)
            big, small = block_table(moe_bm), block_table(moe_sub)
            xs = _gather_rows(h_tok, src, small[1], moe_sub, d)
            gs = _grouped_swiglu(xs, moe_w1, moe_w3, f, big, moe_bm)
            os_ = _grouped_down(gs, moe_w2, f, big, moe_bm)
            if last:
                outs = _combine(os_, dest, gates, y, mod, i, 5, final_norm_g, True, n_ctx, dec_seq, tm_small)
            else:
                y, h = _combine(os_, dest, gates, y, mod, i, 5, norm_mix_g[i + 1], False,
                                n_ctx, dec_seq, tm_small)

    y_p, y_s = outs if outs is not None else _final_norm(y, final_norm_g, n_ctx, tm_half)
    new_state = jnp.stack(states, axis=1).astype(x_prompt.dtype)
    return (y_p.reshape(batch, seq, d), y_s.reshape(dec_batch, dec_seq, d), new_state)
```

```python
import functools

import jax
import jax.numpy as jnp
from jax import lax
from jax.experimental import pallas as pl
from jax.experimental.pallas import tpu as pltpu

F32 = jnp.float32
BF16 = jnp.bfloat16

GRID_W = 64
EPS = 1e-6
RG_C = 8.0
TOP_K = 2

LANES = 128
SUBLANES = 8
VMEM_LIMIT_BYTES = 58 * 1024 * 1024
NEG_BIG = -1e30
MXU_ROWS_PER_PASS = 256
RESIDENT_WEIGHT_BYTES = 12 * 1024 * 1024
RG_CHUNK_ROWS = 128
DMA_RING = 3


def _cparams(n_axes):
    return pltpu.CompilerParams(dimension_semantics=("arbitrary",) * n_axes,
                                vmem_limit_bytes=VMEM_LIMIT_BYTES)


def _pick_tile(n, pref):
    if n <= pref:
        return n
    t = (pref // LANES) * LANES
    while t > LANES and n % t:
        t -= LANES
    assert n % t == 0, (n, pref)
    return t


def _row_passes(rows):
    step = MXU_ROWS_PER_PASS if rows % MXU_ROWS_PER_PASS == 0 else rows
    return [pl.ds(r, step) for r in range(0, rows, step)]


def _sigmoid(x):
    return 1.0 / (1.0 + jnp.exp(-x))


def _gelu_tanh(x):
    c = 0.7978845608028654
    return 0.5 * x * (1.0 + jnp.tanh(c * (x + 0.044715 * (x * x * x))))


def _mod_row(i, tm, n_ctx, dec_seq):
    start = i * tm
    return jnp.where(start < n_ctx, 0, 1 + (start - n_ctx) // dec_seq)


def _adaln_kernel(c_ref, w_ref, b_ref, o_ref):
    c = c_ref[...]
    s = (c * _sigmoid(c)).astype(BF16)
    o_ref[...] = jnp.dot(s, w_ref[...].astype(BF16), preferred_element_type=F32) + b_ref[...]


def _adaln(cond, w_mod, b_mod):
    depth, d, n = w_mod.shape
    mc = cond.shape[0]
    tn = _pick_tile(n, 1024)
    return pl.pallas_call(
        _adaln_kernel,
        grid=(depth, n // tn),
        in_specs=[pl.BlockSpec((mc, d), lambda l, j: (0, 0)),
                  pl.BlockSpec((None, d, tn), lambda l, j: (l, 0, j)),
                  pl.BlockSpec((None, 1, tn), lambda l, j: (l, 0, j))],
        out_specs=pl.BlockSpec((None, mc, tn), lambda l, j: (l, 0, j)),
        out_shape=jax.ShapeDtypeStruct((depth, mc, n), F32),
        compiler_params=_cparams(2), name="adaln",
    )(cond, w_mod, b_mod.reshape(depth, 1, n))


def _embed_norm_kernel(xp_ref, xs_ref, pos_ref, g_ref, mod_ref, y_ref, h_ref, *, n_ctx_tiles):
    i = pl.program_id(0)

    @pl.when(i < n_ctx_tiles)
    def _():
        y_ref[...] = xp_ref[...]

    @pl.when(i >= n_ctx_tiles)
    def _():
        y_ref[...] = xs_ref[...] + pos_ref[...]

    h_ref[...] = _norm_mod_value(y_ref[...], g_ref[...], mod_ref, 0).astype(h_ref.dtype)


def _embed_norm(xp, xs, pos, g, mod, tm):
    n_ctx, d = xp.shape
    n_dec = xs.shape[0]
    dec_seq = pos.shape[0]
    nct = n_ctx // tm
    ppt = dec_seq // tm
    t = n_ctx + n_dec
    return pl.pallas_call(
        functools.partial(_embed_norm_kernel, n_ctx_tiles=nct),
        grid=(t // tm,),
        in_specs=[pl.BlockSpec((tm, d), lambda i: (jnp.minimum(i, nct - 1), 0)),
                  pl.BlockSpec((tm, d), lambda i: (jnp.maximum(i - nct, 0), 0)),
                  pl.BlockSpec((tm, d), lambda i: (jnp.maximum(i - nct, 0) % ppt, 0)),
                  pl.BlockSpec((1, d), lambda i: (0, 0)),
                  pl.BlockSpec((None, None, 6, d), lambda i: (0, _mod_row(i, tm, n_ctx, dec_seq), 0, 0))],
        out_specs=[pl.BlockSpec((tm, d), lambda i: (i, 0)),
                   pl.BlockSpec((tm, d), lambda i: (i, 0))],
        out_shape=[jax.ShapeDtypeStruct((t, d), F32), jax.ShapeDtypeStruct((t, d), BF16)],
        compiler_params=_cparams(1), name="embed_norm",
    )(xp, xs, pos, g.reshape(1, d), mod)


def _grid_pos_embed(length, d):
    rows = length // GRID_W
    r = jnp.repeat(jnp.arange(rows), GRID_W)
    col = jnp.tile(jnp.arange(GRID_W), rows)
    quarter = d // 4
    omega = 1.0 / (10000.0 ** (jnp.arange(quarter, dtype=F32) / quarter))

    def emb(p):
        ang = p[:, None].astype(F32) * omega[None, :]
        return jnp.concatenate([jnp.sin(ang), jnp.cos(ang)], axis=-1)

    return jnp.concatenate([emb(r), emb(col)], axis=-1).astype(F32)


def _norm_mod_value(y, g, mod_ref, shift_row):
    ms = jnp.mean(y * y, axis=-1, keepdims=True)
    n = (y * lax.rsqrt(ms + EPS)) * g
    return n * (1.0 + mod_ref[shift_row + 1:shift_row + 2, :]) + mod_ref[shift_row:shift_row + 1, :]


def _norm_mod_kernel(y_ref, g_ref, mod_ref, h_ref, *, shift_row):
    h_ref[...] = _norm_mod_value(y_ref[...], g_ref[...], mod_ref, shift_row).astype(h_ref.dtype)


def _norm_mod(y, g, mod, layer, shift_row, n_ctx, dec_seq, tm):
    t, d = y.shape
    return pl.pallas_call(
        functools.partial(_norm_mod_kernel, shift_row=shift_row),
        grid=(t // tm,),
        in_specs=[pl.BlockSpec((tm, d), lambda i: (i, 0)),
                  pl.BlockSpec((1, d), lambda i: (0, 0)),
                  pl.BlockSpec((None, None, 6, d),
                               lambda i: (layer, _mod_row(i, tm, n_ctx, dec_seq), 0, 0))],
        out_specs=pl.BlockSpec((tm, d), lambda i: (i, 0)),
        out_shape=jax.ShapeDtypeStruct((t, d), BF16),
        compiler_params=_cparams(1), name="norm_mod",
    )(y, g.reshape(1, d), mod)


def _route_rows(h, row0, r_ref, rb_ref, h_ref, ids_ref, gates_ref, n_slab):
    nr = h.shape[0]
    rows = pl.ds(row0, nr)
    half = n_slab * LANES
    bits = lax.bitcast_convert_type(h.astype(jnp.bfloat16).astype(F32), jnp.uint32)
    word = bits[:, half:] | (bits[:, :half] >> 16)
    for s in range(n_slab):
        h_ref[pl.ds(row0 * n_slab + s, nr, stride=n_slab), :] = word[:, s * LANES:(s + 1) * LANES]
    h_hi = h.astype(BF16)
    h_lo = (h - h_hi.astype(F32)).astype(BF16)
    p_hi = jnp.dot(h_hi, r_ref[...], preferred_element_type=F32)
    p_lo = jnp.dot(h_lo, r_ref[:, :LANES], preferred_element_type=F32)
    logits = ((p_hi[:, :LANES] + p_hi[:, LANES:]) + p_lo) + rb_ref[...]
    lane = lax.broadcasted_iota(jnp.int32, logits.shape, 1).astype(F32)
    big = float(LANES)
    m1 = jnp.max(logits, axis=-1, keepdims=True)
    i1 = jnp.min(jnp.where(logits == m1, lane, big), axis=-1, keepdims=True)
    l2 = jnp.where(lane == i1, 2.0 * NEG_BIG, logits)
    m2 = jnp.max(l2, axis=-1, keepdims=True)
    i2 = jnp.min(jnp.where(l2 == m2, lane, big), axis=-1, keepdims=True)
    e = jnp.exp(m2 - m1)
    g1 = 1.0 / (1.0 + e)
    g2 = e / (1.0 + e)
    ids_ref[rows, :] = jnp.where(lane == 0.0, i1, jnp.where(lane == 1.0, i2, 0.0)).astype(jnp.int32)
    gates_ref[rows, :] = jnp.where(lane == 0.0, g1, jnp.where(lane == 1.0, g2, 0.0))


def _norm_router_kernel(y_ref, g_ref, mod_ref, r_ref, rb_ref, h_ref, ids_ref, gates_ref, *,
                        shift_row, n_slab):
    h = _norm_mod_value(y_ref[...], g_ref[...], mod_ref, shift_row)
    _route_rows(h, 0, r_ref, rb_ref, h_ref, ids_ref, gates_ref, n_slab)


def _router_operands(router, router_b, d):
    n_exp = router.shape[1]
    rp = jnp.zeros((d, LANES), F32).at[:, :n_exp].set(router)
    r_hi = rp.astype(BF16)
    r_lo = (rp - r_hi.astype(F32)).astype(BF16)
    rbp = jnp.full((1, LANES), NEG_BIG, F32).at[0, :n_exp].set(router_b)
    return jnp.concatenate([r_hi, r_lo], axis=1), rbp


def _norm_router(y, g, mod, layer, shift_row, router, router_b, n_ctx, dec_seq, tm):
    t, d = y.shape
    n_slab = d // (2 * LANES)
    rp, rbp = _router_operands(router, router_b, d)
    return pl.pallas_call(
        functools.partial(_norm_router_kernel, shift_row=shift_row, n_slab=n_slab),
        grid=(t // tm,),
        in_specs=[pl.BlockSpec((tm, d), lambda i: (i, 0)),
                  pl.BlockSpec((1, d), lambda i: (0, 0)),
                  pl.BlockSpec((None, None, 6, d),
                               lambda i: (layer, _mod_row(i, tm, n_ctx, dec_seq), 0, 0)),
                  pl.BlockSpec((d, 2 * LANES), lambda i: (0, 0)),
                  pl.BlockSpec((1, LANES), lambda i: (0, 0))],
        out_specs=[pl.BlockSpec((tm * n_slab, LANES), lambda i: (i, 0)),
                   pl.BlockSpec((tm, LANES), lambda i: (i, 0)),
                   pl.BlockSpec((tm, LANES), lambda i: (i, 0))],
        out_shape=[jax.ShapeDtypeStruct((t * n_slab, LANES), jnp.uint32),
                   jax.ShapeDtypeStruct((t, LANES), jnp.int32),
                   jax.ShapeDtypeStruct((t, LANES), F32)],
        compiler_params=_cparams(1), name="norm_router",
    )(y, g.reshape(1, d), mod, rp, rbp)


def _final_norm_kernel(y_ref, g_ref, op_ref, os_ref, *, n_ctx_tiles):
    i = pl.program_id(0)
    y = y_ref[...]
    ms = jnp.mean(y * y, axis=-1, keepdims=True)
    n = (y * lax.rsqrt(ms + EPS)) * g_ref[...]

    @pl.when(i < n_ctx_tiles)
    def _():
        op_ref[...] = n

    @pl.when(i >= n_ctx_tiles)
    def _():
        os_ref[...] = n


def _final_norm(y, g, n_ctx, tm):
    t, d = y.shape
    nct = n_ctx // tm
    return pl.pallas_call(
        functools.partial(_final_norm_kernel, n_ctx_tiles=nct),
        grid=(t // tm,),
        in_specs=[pl.BlockSpec((tm, d), lambda i: (i, 0)),
                  pl.BlockSpec((1, d), lambda i: (0, 0))],
        out_specs=[pl.BlockSpec((tm, d), lambda i: (jnp.minimum(i, nct - 1), 0)),
                   pl.BlockSpec((tm, d), lambda i: (jnp.maximum(i - nct, 0), 0))],
        out_shape=[jax.ShapeDtypeStruct((n_ctx, d), F32),
                   jax.ShapeDtypeStruct((t - n_ctx, d), F32)],
        compiler_params=_cparams(1), name="final_norm",
    )(y, g.reshape(1, d))


def _cast_at_first_row_tile(w_refs, wb_refs):
    @pl.when(pl.program_id(1) == 0)
    def _():
        for w, wb in zip(w_refs, wb_refs):
            wb[...] = w[...].astype(BF16)


def _mm_act_kernel(x_ref, w_ref, o_ref, wb, *, act):
    _cast_at_first_row_tile((w_ref,), (wb,))
    for rows in _row_passes(x_ref.shape[0]):
        acc = jnp.dot(x_ref[rows, :], wb[...], preferred_element_type=F32)
        if act == "gelu":
            acc = _gelu_tanh(acc)
        o_ref[rows, :] = acc.astype(o_ref.dtype)


def _mm_act(x, w, layer, act, out_dtype, tm, tn_pref=512):
    t, k = x.shape
    n = w.shape[2]
    tn = _pick_tile(n, tn_pref)
    return pl.pallas_call(
        functools.partial(_mm_act_kernel, act=act),
        grid=(n // tn, t // tm),
        in_specs=[pl.BlockSpec((tm, k), lambda j, i: (i, 0)),
                  pl.BlockSpec((None, k, tn), lambda j, i: (layer, 0, j))],
        out_specs=pl.BlockSpec((tm, tn), lambda j, i: (i, j)),
        out_shape=jax.ShapeDtypeStruct((t, n), out_dtype),
        scratch_shapes=[pltpu.VMEM((k, tn), BF16)],
        compiler_params=_cparams(2), name="mm_" + str(act),
    )(x, w)


def _swiglu_passes(x_ref, wb1, wb3, o_ref):
    for rows in _row_passes(x_ref.shape[0]):
        x = x_ref[rows, :]
        a = jnp.dot(x, wb1[...], preferred_element_type=F32)
        b = jnp.dot(x, wb3[...], preferred_element_type=F32)
        o_ref[rows, :] = ((a * _sigmoid(a)) * b).astype(o_ref.dtype)


def _mm_swiglu_kernel(x_ref, w1_ref, w3_ref, o_ref, wb1, wb3):
    _cast_at_first_row_tile((w1_ref, w3_ref), (wb1, wb3))
    _swiglu_passes(x_ref, wb1, wb3, o_ref)


def _mm_swiglu(x, w1, w3, layer, tm, tn_pref=512):
    t, k = x.shape
    n = w1.shape[2]
    tn = _pick_tile(n, tn_pref)
    return pl.pallas_call(
        _mm_swiglu_kernel,
        grid=(n // tn, t // tm),
        in_specs=[pl.BlockSpec((tm, k), lambda j, i: (i, 0)),
                  pl.BlockSpec((None, k, tn), lambda j, i: (layer, 0, j)),
                  pl.BlockSpec((None, k, tn), lambda j, i: (layer, 0, j))],
        out_specs=pl.BlockSpec((tm, tn), lambda j, i: (i, j)),
        out_shape=jax.ShapeDtypeStruct((t, n), BF16),
        scratch_shapes=[pltpu.VMEM((k, tn), BF16), pltpu.VMEM((k, tn), BF16)],
        compiler_params=_cparams(2), name="mm_swiglu",
    )(x, w1, w3)


def _mm_sconv_kernel(x_ref, wb_ref, wc_ref, wx_ref, cw_ref, o_ref, sb, sc, sx, *, n_ctx_tiles, seq_ctx):
    _cast_at_first_row_tile((wb_ref, wc_ref, wx_ref), (sb, sc, sx))
    i = pl.program_id(1)
    is_ctx = i < n_ctx_tiles
    x = x_ref[...]
    bg = jnp.dot(x, sb[...], preferred_element_type=F32)
    p = jnp.dot(x, sc[...], preferred_element_type=F32) * jnp.dot(x, sx[...], preferred_element_type=F32)
    tm = p.shape[0]
    row = lax.broadcasted_iota(jnp.int32, p.shape, 0)
    pos = jnp.where(is_ctx, row % seq_ctx, row)
    last_pos = jnp.where(is_ctx, seq_ctx - 1, tm - 1)
    p_prev = jnp.where(pos == 0, 0.0, pltpu.roll(p, 1, axis=0))
    p_next = jnp.where(pos == last_pos, 0.0, pltpu.roll(p, tm - 1, axis=0))
    cw = cw_ref[...]
    conv = (cw[0:1] * p_prev + cw[1:2] * p) + cw[2:3] * p_next
    o_ref[...] = (bg * conv).astype(o_ref.dtype)


def _mm_sconv(x, w_in, conv_w, layer, n_ctx, seq_ctx, tm, tn_pref=512):
    t, k = x.shape
    d = conv_w.shape[2]
    tn = _pick_tile(d, tn_pref)
    nj = d // tn
    return pl.pallas_call(
        functools.partial(_mm_sconv_kernel, n_ctx_tiles=n_ctx // tm, seq_ctx=seq_ctx),
        grid=(nj, t // tm),
        in_specs=[pl.BlockSpec((tm, k), lambda j, i: (i, 0)),
                  pl.BlockSpec((None, k, tn), lambda j, i: (layer, 0, j)),
                  pl.BlockSpec((None, k, tn), lambda j, i: (layer, 0, nj + j)),
                  pl.BlockSpec((None, k, tn), lambda j, i: (layer, 0, 2 * nj + j)),
                  pl.BlockSpec((None, conv_w.shape[1], tn), lambda j, i: (layer, 0, j))],
        out_specs=pl.BlockSpec((tm, tn), lambda j, i: (i, j)),
        out_shape=jax.ShapeDtypeStruct((t, d), BF16),
        scratch_shapes=[pltpu.VMEM((k, tn), BF16)] * 3,
        compiler_params=_cparams(2), name="mm_sconv",
    )(x, w_in, w_in, w_in, conv_w)


def _mm_residual_kernel(x_ref, w_ref, y_ref, mod_ref, o_ref, wb, *, gate_row):
    _cast_at_first_row_tile((w_ref,), (wb,))
    gate = mod_ref[gate_row:gate_row + 1, :]
    for rows in _row_passes(x_ref.shape[0]):
        acc = jnp.dot(x_ref[rows, :], wb[...], preferred_element_type=F32)
        o_ref[rows, :] = y_ref[rows, :] + gate * acc


def _mm_residual(x, w, w_layer, y, mod, layer, gate_row, n_ctx, dec_seq, tm, tn_pref=512, w_buffers=2):
    t, k = x.shape
    n = w.shape[2]
    tn = _pick_tile(n, tn_pref)
    return pl.pallas_call(
        functools.partial(_mm_residual_kernel, gate_row=gate_row),
        grid=(n // tn, t // tm),
        in_specs=[pl.BlockSpec((tm, k), lambda j, i: (i, 0)),
                  pl.BlockSpec((None, k, tn), lambda j, i: (w_layer, 0, j),
                               pipeline_mode=pl.Buffered(w_buffers)),
                  pl.BlockSpec((tm, tn), lambda j, i: (i, j)),
                  pl.BlockSpec((None, None, 6, tn),
                               lambda j, i: (layer, _mod_row(i, tm, n_ctx, dec_seq), 0, j))],
        out_specs=pl.BlockSpec((tm, tn), lambda j, i: (i, j)),
        out_shape=jax.ShapeDtypeStruct((t, n), F32),
        scratch_shapes=[pltpu.VMEM((k, tn), BF16)],
        compiler_params=_cparams(2), name="mm_residual",
    )(x, w, y, mod)


def _residual_norm_kernel(x_ref, w_hbm, y_ref, mod_ref, g_ref, *rest,
                          w_layer, gate_row, shift_row, chunk, route_slabs):
    if route_slabs:
        r_ref, rb_ref, yo_ref, ho_ref, ids_ref, gates_ref, wb, stage, sem = rest
    else:
        yo_ref, ho_ref, wb, stage, sem = rest
    k = wb.shape[0]
    nchunk = k // chunk

    @pl.when(pl.program_id(0) == 0)
    def _():
        def copy(c):
            return pltpu.make_async_copy(w_hbm.at[w_layer, pl.ds(c * chunk, chunk), :],
                                         stage.at[c % 2], sem.at[c % 2])

        copy(0).start()
        for c in range(nchunk):
            if c + 1 < nchunk:
                copy(c + 1).start()
            copy(c).wait()
            wb[pl.ds(c * chunk, chunk), :] = stage[c % 2].astype(BF16)

    gate = mod_ref[gate_row:gate_row + 1, :]
    g = g_ref[...]
    for rows in _row_passes(x_ref.shape[0]):
        acc = jnp.dot(x_ref[rows, :], wb[...], preferred_element_type=F32)
        y = y_ref[rows, :] + gate * acc
        yo_ref[rows, :] = y
        h = _norm_mod_value(y, g, mod_ref, shift_row)
        if route_slabs:
            _route_rows(h, rows.start, r_ref, rb_ref, ho_ref, ids_ref, gates_ref, route_slabs)
        else:
            ho_ref[rows, :] = h.astype(ho_ref.dtype)


def _residual_norm(x, w, w_layer, y, mod, layer, gate_row, next_g, shift_row, n_ctx, dec_seq, tm,
                   router=None, router_b=None):
    t, k = x.shape
    d = w.shape[2]
    chunk = _pick_tile(k, 512)
    in_specs = [pl.BlockSpec((tm, k), lambda i: (i, 0)),
                pl.BlockSpec(memory_space=pl.ANY),
                pl.BlockSpec((tm, d), lambda i: (i, 0)),
                pl.BlockSpec((None, None, 6, d),
                             lambda i: (layer, _mod_row(i, tm, n_ctx, dec_seq), 0, 0)),
                pl.BlockSpec((1, d), lambda i: (0, 0))]
    args = [x, w, y, mod, next_g.reshape(1, d)]
    if router is None:
        route_slabs = 0
        out_specs = [pl.BlockSpec((tm, d), lambda i: (i, 0)),
                     pl.BlockSpec((tm, d), lambda i: (i, 0))]
        out_shape = [jax.ShapeDtypeStruct((t, d), F32), jax.ShapeDtypeStruct((t, d), BF16)]
    else:
        route_slabs = d // (2 * LANES)
        in_specs += [pl.BlockSpec((d, 2 * LANES), lambda i: (0, 0)),
                     pl.BlockSpec((1, LANES), lambda i: (0, 0))]
        args += list(_router_operands(router, router_b, d))
        out_specs = [pl.BlockSpec((tm, d), lambda i: (i, 0)),
                     pl.BlockSpec((tm * route_slabs, LANES), lambda i: (i, 0)),
                     pl.BlockSpec((tm, LANES), lambda i: (i, 0)),
                     pl.BlockSpec((tm, LANES), lambda i: (i, 0))]
        out_shape = [jax.ShapeDtypeStruct((t, d), F32),
                     jax.ShapeDtypeStruct((t * route_slabs, LANES), jnp.uint32),
                     jax.ShapeDtypeStruct((t, LANES), jnp.int32),
                     jax.ShapeDtypeStruct((t, LANES), F32)]
    return pl.pallas_call(
        functools.partial(_residual_norm_kernel, w_layer=w_layer, gate_row=gate_row,
                          shift_row=shift_row, chunk=chunk, route_slabs=route_slabs),
        grid=(t // tm,),
        in_specs=in_specs,
        out_specs=out_specs,
        out_shape=out_shape,
        scratch_shapes=[pltpu.VMEM((k, d), BF16), pltpu.VMEM((2, chunk, d), F32),
                        pltpu.SemaphoreType.DMA((2,))],
        compiler_params=_cparams(1), name="residual_route" if router is not None else "residual_norm",
    )(*args)


def _rg_core_kernel(gate_ref, x_ref, cw_ref, cb_ref, wg_ref, bg_ref, lam_ref, h0_ref,
                    y_ref, st_ref, af_s, bf_s, ab_s, bb_s, *, n_ctx_tiles, seg, tm, cg):
    i = pl.program_id(1)
    is_ctx = i < n_ctx_tiles
    nseg = tm // seg
    nslab = cg // LANES
    lsub = tm // SUBLANES
    sub_per_seq = seg // lsub
    cw = cw_ref[...]
    cb = cb_ref[...]
    lam = lam_ref[...]
    softplus_neg_lam = jnp.maximum(-lam, 0.0) + jnp.log1p(jnp.exp(-jnp.abs(lam)))
    neg_c_sp = (-RG_C) * softplus_neg_lam
    zeros8 = jnp.zeros((SUBLANES, cg), F32)

    cr = min(RG_CHUNK_ROWS, lsub)
    for c0 in range(0, tm, cr):
        cur = x_ref[pl.ds(c0, cr), :]
        if c0 == 0:
            prev8 = zeros8
        else:
            prev8 = x_ref[pl.ds(c0 - SUBLANES, SUBLANES), :]
            if c0 % seg == 0:
                prev8 = jnp.where(is_ctx, 0.0, prev8)
        if c0 + cr == tm:
            next8 = zeros8
        else:
            next8 = x_ref[pl.ds(c0 + cr, SUBLANES), :]
            if (c0 + cr) % seg == 0:
                next8 = jnp.where(is_ctx, 0.0, next8)
        head = jnp.concatenate([prev8, cur[:2 * SUBLANES]], axis=0)
        tail = jnp.concatenate([cur[cr - 2 * SUBLANES:], next8], axis=0)

        def shifted(delta):
            lo = SUBLANES + delta
            mid = x_ref[pl.ds(c0 + lo, cr - 2 * SUBLANES), :]
            return jnp.concatenate([head[lo:lo + SUBLANES], mid, tail[lo:lo + SUBLANES]], axis=0)

        xc = (((cw[0:1] * shifted(-2) + cw[1:2] * shifted(-1)) + cw[2:3] * cur)
              + cw[3:4] * shifted(1)) + cb
        g = jnp.dot(xc.astype(BF16), wg_ref[...], preferred_element_type=F32) + bg_ref[...]
        for d, (a_s, b_s) in enumerate(((af_s, bf_s), (ab_s, bb_s))):
            r = _sigmoid(g[:, (2 * d) * cg:(2 * d + 1) * cg])
            ig = _sigmoid(g[:, (2 * d + 1) * cg:(2 * d + 2) * cg])
            log_a = neg_c_sp[d:d + 1] * r
            a = jnp.exp(log_a)
            one_minus_a2 = -jnp.tanh(log_a) * (a * a + 1.0)
            root = jnp.where(one_minus_a2 > 0.0, one_minus_a2 * lax.rsqrt(one_minus_a2), 0.0)
            bt = root * (ig * xc)
            k, j0 = c0 // lsub, c0 % lsub
            dst = pl.ds(k + SUBLANES * j0, cr, stride=SUBLANES)
            for l in range(nslab):
                a_s[l, dst, :] = a[:, l * LANES:(l + 1) * LANES]
                b_s[l, dst, :] = bt[:, l * LANES:(l + 1) * LANES]

    def local_scan(j, carry):
        hf, pf, hb, pb = carry
        rf = pl.multiple_of(j * SUBLANES, SUBLANES)
        rb = pl.multiple_of((lsub - 1 - j) * SUBLANES, SUBLANES)
        nhf, npf, nhb, npb = [], [], [], []
        for l in range(nslab):
            a = af_s[l, pl.ds(rf, SUBLANES), :]
            h = a * hf[l] + bf_s[l, pl.ds(rf, SUBLANES), :]
            p = a * pf[l]
            bf_s[l, pl.ds(rf, SUBLANES), :] = h
            af_s[l, pl.ds(rf, SUBLANES), :] = p
            nhf.append(h)
            npf.append(p)
            a = ab_s[l, pl.ds(rb, SUBLANES), :]
            h = a * hb[l] + bb_s[l, pl.ds(rb, SUBLANES), :]
            p = a * pb[l]
            bb_s[l, pl.ds(rb, SUBLANES), :] = h
            ab_s[l, pl.ds(rb, SUBLANES), :] = p
            nhb.append(h)
            npb.append(p)
        return tuple(nhf), tuple(npf), tuple(nhb), tuple(npb)

    zero = tuple(jnp.zeros((SUBLANES, LANES), F32) for _ in range(nslab))
    one = tuple(jnp.ones((SUBLANES, LANES), F32) for _ in range(nslab))
    hf_end, pf_end, hb_end, pb_end = lax.fori_loop(0, lsub, local_scan, (zero, one, zero, one))

    row8 = lax.broadcasted_iota(jnp.int32, (SUBLANES, LANES), 0)
    h0 = h0_ref[...]
    for l in range(nslab):
        lanes = slice(l * LANES, (l + 1) * LANES)
        ent_f = jnp.zeros((SUBLANES, LANES), F32)
        h_in = h0[0:1, lanes]
        for k in range(SUBLANES):
            if k > 0 and k % sub_per_seq == 0:
                h_in = jnp.where(is_ctx, 0.0, h_in)
            ent_f = jnp.where(row8 == k, h_in, ent_f)
            h_in = hf_end[l][k:k + 1] + pf_end[l][k:k + 1] * h_in
            if (k + 1) % sub_per_seq == 0:
                q = k // sub_per_seq
                st_ref[q:q + 1, lanes] = h_in
        ent_b = jnp.zeros((SUBLANES, LANES), F32)
        h_in = h0[1:2, lanes]
        for k in reversed(range(SUBLANES)):
            if k < SUBLANES - 1 and (k + 1) % sub_per_seq == 0:
                h_in = jnp.where(is_ctx, 0.0, h_in)
            ent_b = jnp.where(row8 == k, h_in, ent_b)
            h_in = hb_end[l][k:k + 1] + pb_end[l][k:k + 1] * h_in
            if k % sub_per_seq == 0:
                q = k // sub_per_seq
                st_ref[nseg + q:nseg + q + 1, lanes] = h_in
        ch = min(tm, 256)
        ef = jnp.concatenate([ent_f] * (ch // SUBLANES), axis=0)
        eb = jnp.concatenate([ent_b] * (ch // SUBLANES), axis=0)
        for r0 in range(0, tm, ch):
            rows = pl.ds(r0, ch)
            bf_s[l, rows, :] = ((bf_s[l, rows, :] + af_s[l, rows, :] * ef)
                                + (bb_s[l, rows, :] + ab_s[l, rows, :] * eb))

    for k in range(SUBLANES):
        rows = pl.ds(k * lsub, lsub)
        hsum = jnp.concatenate([bf_s[l, pl.ds(k, lsub, stride=SUBLANES), :] for l in range(nslab)], axis=1)
        y_ref[rows, :] = (hsum * _gelu_tanh(gate_ref[rows, :])).astype(y_ref.dtype)


def _rg_core(z, conv_w, conv_b, w_a, b_a, w_x, b_x, lam, h0, n_ctx, seq_ctx, tm):
    t = z.shape[0]
    r = conv_w.shape[1]
    heads, hw = w_a.shape[1], w_a.shape[2]
    hpg = 4
    while (hpg * hw) % LANES:
        hpg *= 2
    ng = heads // hpg
    cg = hpg * hw
    nseg = tm // seq_ctx
    ntiles = t // tm
    assert tm % SUBLANES == 0 and seq_ctx % (tm // SUBLANES) == 0

    def blockdiag(w):
        rows = jnp.tile(w.reshape(2, ng, cg, hw), (1, 1, 1, hpg))
        head = jnp.arange(cg, dtype=jnp.int32) // hw
        return jnp.where(head[:, None] == head[None, :], rows, 0.0)

    wa, wx = blockdiag(w_a), blockdiag(w_x)
    wg = jnp.concatenate([wa[0], wx[0], wa[1], wx[1]], axis=-1).astype(BF16)
    ba, bx = b_a.reshape(2, ng, 1, cg), b_x.reshape(2, ng, 1, cg)
    bg = jnp.concatenate([ba[0], bx[0], ba[1], bx[1]], axis=-1)

    return pl.pallas_call(
        functools.partial(_rg_core_kernel, n_ctx_tiles=n_ctx // tm, seg=seq_ctx, tm=tm, cg=cg),
        grid=(ng, ntiles),
        in_specs=[pl.BlockSpec((tm, cg), lambda g, i: (i, g)),
                  pl.BlockSpec((tm, cg), lambda g, i: (i, ng + g)),
                  pl.BlockSpec((conv_w.shape[0], cg), lambda g, i: (0, g)),
                  pl.BlockSpec((1, cg), lambda g, i: (0, g)),
                  pl.BlockSpec((None, cg, 4 * cg), lambda g, i: (g, 0, 0)),
                  pl.BlockSpec((None, 1, 4 * cg), lambda g, i: (g, 0, 0)),
                  pl.BlockSpec((2, cg), lambda g, i: (0, g)),
                  pl.BlockSpec((None, 2, cg), lambda g, i: (i, 0, g))],
        out_specs=[pl.BlockSpec((tm, cg), lambda g, i: (i, g)),
                   pl.BlockSpec((None, 2 * nseg, cg), lambda g, i: (i, 0, g))],
        out_shape=[jax.ShapeDtypeStruct((t, r), BF16),
                   jax.ShapeDtypeStruct((ntiles, 2 * nseg, r), F32)],
        scratch_shapes=[pltpu.VMEM((cg // LANES, tm, LANES), F32)] * 4,
        compiler_params=_cparams(2), name="rg_core",
    )(z, z, conv_w, conv_b.reshape(1, r), wg, bg, lam, h0)


def _sgu_core_kernel(u_ref, v_ref, g_ref, ws_ref, bs_ref, y_ref, vn_s, *, chunk, gw, ngroups, tm):
    v = v_ref[...].astype(F32)
    ms = jnp.mean(v * v, axis=-1, keepdims=True)
    vn_s[...] = ((v * lax.rsqrt(ms + EPS)) * g_ref[...]).astype(BF16)
    for c in range(tm // chunk):
        rows = pl.ds(c * chunk, chunk)
        for g in range(ngroups):
            cols = pl.ds(g * gw, gw)
            vm = jnp.dot(ws_ref[g], vn_s[rows, cols], preferred_element_type=F32) + bs_ref[:, g:g + 1]
            y_ref[rows, cols] = (u_ref[rows, cols].astype(F32) * vm).astype(y_ref.dtype)


def _sgu_core(z, norm_g, w_s, b_s, tm):
    t = z.shape[0]
    w = norm_g.shape[0]
    ngroups, chunk = w_s.shape[0], w_s.shape[1]
    gw = w // ngroups
    return pl.pallas_call(
        functools.partial(_sgu_core_kernel, chunk=chunk, gw=gw, ngroups=ngroups, tm=tm),
        grid=(t // tm,),
        in_specs=[pl.BlockSpec((tm, w), lambda i: (i, 0)),
                  pl.BlockSpec((tm, w), lambda i: (i, 1)),
                  pl.BlockSpec((1, w), lambda i: (0, 0)),
                  pl.BlockSpec((ngroups, chunk, chunk), lambda i: (0, 0, 0)),
                  pl.BlockSpec((chunk, ngroups), lambda i: (0, 0))],
        out_specs=pl.BlockSpec((tm, w), lambda i: (i, 0)),
        out_shape=jax.ShapeDtypeStruct((t, w), BF16),
        scratch_shapes=[pltpu.VMEM((tm, w), BF16)],
        compiler_params=_cparams(1), name="sgu_core",
    )(z, z, norm_g.reshape(1, w), w_s.astype(BF16), b_s.T)


def _moe_plan(ids, n_exp, bm):
    t = ids.shape[0]
    e = ids[:, :TOP_K].reshape(-1)
    oh = (e[:, None] == jnp.arange(n_exp, dtype=jnp.int32)[None, :]).astype(jnp.int32)
    csum = jnp.cumsum(oh, axis=0)
    rank = jnp.sum((csum - oh) * oh, axis=1)
    counts = csum[-1]
    padded = ((counts + bm - 1) // bm) * bm
    ends = jnp.cumsum(padded)
    starts = ends - padded
    first_cnt = counts - jnp.maximum(padded - bm, 0)
    start_e = jnp.sum(starts[None, :] * oh, axis=1)
    first_e = jnp.sum(first_cnt[None, :] * oh, axis=1)
    dest = (start_e + rank + jnp.where(rank >= first_e, bm - first_e, 0)).astype(jnp.int32)
    p = TOP_K * t + n_exp * bm
    src = jnp.zeros((p,), jnp.int32).at[dest].set(jnp.arange(TOP_K * t, dtype=jnp.int32) // TOP_K)

    def block_table(blk):
        idx = jnp.arange(p // blk, dtype=jnp.int32)
        start = idx * blk
        exp = jnp.minimum(jnp.sum((start[:, None] >= ends[None, :]).astype(jnp.int32), axis=1), n_exp - 1)
        offset = start - starts[exp]
        valid = jnp.where(offset < bm, jnp.clip(first_cnt[exp] - offset, 0, blk), blk)
        valid = jnp.where(start < ends[-1], valid, 0)
        fetch = lax.cummax(jnp.where(valid > 0, idx, 0), axis=0)
        return exp.astype(jnp.int32), valid.astype(jnp.int32), fetch.astype(jnp.int32)

    return dest, src, block_table


def _gather_rows_kernel(src_ref, valid_ref, h_hbm, o_ref, buf, sem, *, rows, n_slab):
    b = pl.program_id(0)
    nb = pl.num_programs(0)
    slot = b % DMA_RING
    ahead = DMA_RING - 1

    def block_has_rows(blk):
        return jnp.logical_and(blk < nb, valid_ref[jnp.minimum(blk, nb - 1)] > 0)

    has_rows = block_has_rows(b)

    def issue(blk, to_slot):
        base = blk * rows

        def body(q, carry):
            for prio in range(2):
                r = 2 * q + prio
                tok = pl.multiple_of(src_ref[base + r] * n_slab, n_slab)
                pltpu.make_async_copy(h_hbm.at[pl.ds(tok, n_slab), :],
                                      buf.at[to_slot, pl.ds(pl.multiple_of(r * n_slab, n_slab), n_slab), :],
                                      sem.at[to_slot]).start(priority=prio)
            return carry

        lax.fori_loop(0, rows // 2, body, 0, unroll=4)

    @pl.when(b == 0)
    def _():
        for first in range(ahead):
            @pl.when(block_has_rows(first))
            def _():
                issue(first, first % DMA_RING)

    @pl.when(block_has_rows(b + ahead))
    def _():
        issue(b + ahead, (b + ahead) % DMA_RING)

    @pl.when(has_rows)
    def _():
        pltpu.make_async_copy(h_hbm.at[pl.ds(0, rows * n_slab), :], buf.at[slot], sem.at[slot]).wait()
        half = n_slab * LANES
        for s in range(n_slab):
            word = buf[slot, pl.ds(s, rows, stride=n_slab), :]
            lo = lax.bitcast_convert_type(word << 16, F32)
            hi = lax.bitcast_convert_type(word & jnp.uint32(0xFFFF0000), F32)
            o_ref[:, s * LANES:(s + 1) * LANES] = lo.astype(o_ref.dtype)
            o_ref[:, half + s * LANES:half + (s + 1) * LANES] = hi.astype(o_ref.dtype)

    @pl.when(jnp.logical_not(has_rows))
    def _():
        o_ref[...] = jnp.zeros_like(o_ref)


def _gather_rows(h_tok, src, valid, bm, d):
    n_slab = d // (2 * LANES)
    p = src.shape[0]
    return pl.pallas_call(
        functools.partial(_gather_rows_kernel, rows=bm, n_slab=n_slab),
        grid_spec=pltpu.PrefetchScalarGridSpec(
            num_scalar_prefetch=2,
            grid=(p // bm,),
            in_specs=[pl.BlockSpec(memory_space=pl.ANY)],
            out_specs=pl.BlockSpec((bm, d), lambda b, src, va: (b, 0)),
            scratch_shapes=[pltpu.VMEM((DMA_RING, bm * n_slab, LANES), jnp.uint32),
                            pltpu.SemaphoreType.DMA((DMA_RING,))]),
        out_shape=jax.ShapeDtypeStruct((p, d), BF16),
        compiler_params=_cparams(1), name="moe_gather",
    )(src, valid, h_tok)


def _expert_changed(exp_ref, fetch_ref, b):
    prev = fetch_ref[jnp.maximum(b - 1, 0)]
    return jnp.logical_or(b == 0, exp_ref[b] != exp_ref[prev])


def _guarded_passes(valid, x_ref, o_ref, one_pass):
    bm = x_ref.shape[0]
    passes = _row_passes(bm)
    pass_rows = bm // len(passes)
    needed = (valid + (pass_rows - 1)) // pass_rows

    for count in range(len(passes) + 1):
        @pl.when(needed == count)
        def _():
            for rows in passes[:count]:
                one_pass(rows)
            for rows in passes[count:]:
                o_ref[rows, :] = jnp.zeros((pass_rows, o_ref.shape[1]), o_ref.dtype)


def _grouped_swiglu_kernel(exp_ref, valid_ref, fetch_ref, x_ref, w1_ref, w3_ref, o_ref, wb1, wb3):
    b = pl.program_id(1)
    valid = valid_ref[b]

    @pl.when(jnp.logical_and(valid > 0, _expert_changed(exp_ref, fetch_ref, b)))
    def _():
        wb1[...] = w1_ref[...].astype(BF16)
        wb3[...] = w3_ref[...].astype(BF16)

    def one_pass(rows):
        x = x_ref[rows, :]
        a = jnp.dot(x, wb1[...], preferred_element_type=F32)
        c = jnp.dot(x, wb3[...], preferred_element_type=F32)
        o_ref[rows, :] = ((a * _sigmoid(a)) * c).astype(o_ref.dtype)

    _guarded_passes(valid, x_ref, o_ref, one_pass)


def _grouped_swiglu(xs, w1, w3, layer, table, bm, tn_pref=512):
    p, k = xs.shape
    n = w1.shape[3]
    tn = _pick_tile(n, tn_pref)
    return pl.pallas_call(
        _grouped_swiglu_kernel,
        grid_spec=pltpu.PrefetchScalarGridSpec(
            num_scalar_prefetch=3,
            grid=(n // tn, p // bm),
            in_specs=[pl.BlockSpec((bm, k), lambda j, b, ex, va, fe: (fe[b], 0)),
                      pl.BlockSpec((None, None, k, tn), lambda j, b, ex, va, fe: (layer, ex[fe[b]], 0, j)),
                      pl.BlockSpec((None, None, k, tn), lambda j, b, ex, va, fe: (layer, ex[fe[b]], 0, j))],
            out_specs=pl.BlockSpec((bm, tn), lambda j, b, ex, va, fe: (b, j)),
            scratch_shapes=[pltpu.VMEM((k, tn), BF16), pltpu.VMEM((k, tn), BF16)]),
        out_shape=jax.ShapeDtypeStruct((p, n), BF16),
        compiler_params=_cparams(2), name="moe_swiglu",
    )(*table, xs, w1, w3)


def _grouped_down_kernel(exp_ref, valid_ref, fetch_ref, x_ref, w_ref, o_ref, wb):
    b = pl.program_id(1)
    valid = valid_ref[b]

    @pl.when(jnp.logical_and(valid > 0, _expert_changed(exp_ref, fetch_ref, b)))
    def _():
        wb[...] = w_ref[...].astype(BF16)

    def one_pass(rows):
        o_ref[rows, :] = jnp.dot(x_ref[rows, :], wb[...], preferred_element_type=F32)

    _guarded_passes(valid, x_ref, o_ref, one_pass)


def _grouped_down(gs, w2, layer, table, bm, tn_pref=512):
    p, k = gs.shape
    n = w2.shape[3]
    tn = _pick_tile(n, tn_pref)
    return pl.pallas_call(
        _grouped_down_kernel,
        grid_spec=pltpu.PrefetchScalarGridSpec(
            num_scalar_prefetch=3,
            grid=(n // tn, p // bm),
            in_specs=[pl.BlockSpec((bm, k), lambda j, b, ex, va, fe: (fe[b], 0)),
                      pl.BlockSpec((None, None, k, tn), lambda j, b, ex, va, fe: (layer, ex[fe[b]], 0, j))],
            out_specs=pl.BlockSpec((bm, tn), lambda j, b, ex, va, fe: (b, j)),
            scratch_shapes=[pltpu.VMEM((k, tn), BF16)]),
        out_shape=jax.ShapeDtypeStruct((p, n), F32),
        compiler_params=_cparams(2), name="moe_down",
    )(*table, gs, w2)


def _combine_kernel(dest_ref, os_hbm, y_ref, gates_ref, mod_ref, g_ref, *rest,
                    rows, gate_row, n_ctx_tiles, final):
    if final:
        out_a, out_b, buf, sem = rest
    else:
        nmod_ref, out_a, out_b, buf, sem = rest
    i = pl.program_id(0)
    n_tiles = pl.num_programs(0)
    slot = i % DMA_RING
    ahead = DMA_RING - 1

    def issue(tile, to_slot):
        base = tile * rows

        def body(r, carry):
            for k in range(TOP_K):
                row = dest_ref[TOP_K * (base + r) + k]
                pltpu.make_async_copy(os_hbm.at[pl.ds(row, 1), :], buf.at[to_slot, k, pl.ds(r, 1), :],
                                      sem.at[to_slot]).start(priority=k)
            return carry

        lax.fori_loop(0, rows, body, 0, unroll=4)

    @pl.when(i == 0)
    def _():
        for first in range(ahead):
            @pl.when(first < n_tiles)
            def _():
                issue(first, first % DMA_RING)

    @pl.when(i + ahead < n_tiles)
    def _():
        issue(i + ahead, (i + ahead) % DMA_RING)

    for k in range(TOP_K):
        pltpu.make_async_copy(os_hbm.at[pl.ds(0, rows), :], buf.at[slot, k], sem.at[slot]).wait()
    gates = gates_ref[...]
    f = gates[:, 0:1] * buf[slot, 0] + gates[:, 1:2] * buf[slot, 1]
    y = y_ref[...] + mod_ref[gate_row:gate_row + 1, :] * f
    if final:
        ms = jnp.mean(y * y, axis=-1, keepdims=True)
        n = (y * lax.rsqrt(ms + EPS)) * g_ref[...]

        @pl.when(i < n_ctx_tiles)
        def _():
            out_a[...] = n

        @pl.when(i >= n_ctx_tiles)
        def _():
            out_b[...] = n
    else:
        out_a[...] = y
        out_b[...] = _norm_mod_value(y, g_ref[...], nmod_ref, 0).astype(out_b.dtype)


def _combine(os_, dest, gates, y, mod, layer, gate_row, next_g, final, n_ctx, dec_seq, tm):
    t, d = y.shape
    nct = n_ctx // tm

    def mod_spec(which):
        return pl.BlockSpec((None, None, 6, d),
                            lambda i, dst: (which, _mod_row(i, tm, n_ctx, dec_seq), 0, 0))

    in_specs = [pl.BlockSpec(memory_space=pl.ANY),
                pl.BlockSpec((tm, d), lambda i, dst: (i, 0)),
                pl.BlockSpec((tm, LANES), lambda i, dst: (i, 0)),
                mod_spec(layer),
                pl.BlockSpec((1, d), lambda i, dst: (0, 0))]
    args = [dest, os_, y, gates, mod, next_g.reshape(1, d)]
    if final:
        out_specs = [pl.BlockSpec((tm, d), lambda i, dst: (jnp.minimum(i, nct - 1), 0)),
                     pl.BlockSpec((tm, d), lambda i, dst: (jnp.maximum(i - nct, 0), 0))]
        out_shape = [jax.ShapeDtypeStruct((n_ctx, d), F32), jax.ShapeDtypeStruct((t - n_ctx, d), F32)]
    else:
        in_specs.append(mod_spec(layer + 1))
        args.append(mod)
        out_specs = [pl.BlockSpec((tm, d), lambda i, dst: (i, 0)),
                     pl.BlockSpec((tm, d), lambda i, dst: (i, 0))]
        out_shape = [jax.ShapeDtypeStruct((t, d), F32), jax.ShapeDtypeStruct((t, d), BF16)]
    return pl.pallas_call(
        functools.partial(_combine_kernel, rows=tm, gate_row=gate_row, n_ctx_tiles=nct, final=final),
        grid_spec=pltpu.PrefetchScalarGridSpec(
            num_scalar_prefetch=1,
            grid=(t // tm,),
            in_specs=in_specs,
            out_specs=out_specs,
            scratch_shapes=[pltpu.VMEM((DMA_RING, TOP_K, tm, d), F32),
                            pltpu.SemaphoreType.DMA((DMA_RING,))]),
        out_shape=out_shape,
        compiler_params=_cparams(1), name="moe_combine_final" if final else "moe_combine",
    )(*args)


def kernel(x_prompt, x_sample, state_rglru, c, c_ctx, norm_mix_g, norm_ffn_g, w_mod, b_mod, final_norm_g, rg_w_in, rg_conv_w, rg_conv_b, rg_w_a, rg_b_a, rg_w_x, rg_b_x, rg_lam, rg_w_out, sg_w_in, sg_norm_g, sg_w_s, sg_b_s, sg_w_out, sc_w_in, sc_conv_w, sc_w_out, ff_w1, ff_w3, ff_w2, moe_router, moe_router_b, moe_w1, moe_w3, moe_w2):
    batch, seq, d = x_prompt.shape
    dec_batch, dec_seq, _ = x_sample.shape
    depth = w_mod.shape[0]
    n_ctx = batch * seq
    d_rnn = rg_w_out.shape[1]
    n_exp = moe_router.shape[2]
    chunk = sg_w_s.shape[2]

    tm = dec_seq
    assert dec_seq % seq == 0 and n_ctx % tm == 0 and seq % SUBLANES == 0
    tm_half = max(tm // 2, chunk)
    tm_small = max(tm // 4, chunk)
    assert tm % tm_half == 0 and tm % tm_small == 0 and tm_small % chunk == 0
    moe_bm, moe_sub = tm, tm_half
    tm_wide = 2 * tm if (n_ctx + dec_batch * dec_seq) % (2 * tm) == 0 else tm

    n_cond = 1 + dec_batch
    cond = jnp.zeros((-(-n_cond // SUBLANES) * SUBLANES, d), F32)
    cond = cond.at[0].set(c_ctx).at[1:n_cond].set(c)
    mod = _adaln(cond, w_mod, b_mod)[:, :n_cond].reshape(depth, n_cond, 6, d)

    y, h = _embed_norm(x_prompt.reshape(n_ctx, d), x_sample.reshape(dec_batch * dec_seq, d),
                       _grid_pos_embed(dec_seq, d), norm_mix_g[0], mod, tm_small)

    states = []
    outs = None
    for i in range(depth):
        kind, j = i % 3, i // 3
        if h is None:
            h = _norm_mod(y, norm_mix_g[i], mod, i, 0, n_ctx, dec_seq, tm_half)
        if kind == 0:
            z = _mm_act(h, rg_w_in, j, None, F32, tm_wide, tn_pref=1024)
            h0 = jnp.concatenate([jnp.zeros((n_ctx // tm, 2, d_rnn), F32),
                                  state_rglru[:, j].astype(F32)], axis=0)
            mix, st = _rg_core(z, rg_conv_w[j], rg_conv_b[j], rg_w_a[j], rg_b_a[j], rg_w_x[j], rg_b_x[j],
                               rg_lam[j], h0, n_ctx, seq, tm)
            nseg = tm // seq
            st = st[:n_ctx // tm].reshape(n_ctx // tm, 2, nseg, d_rnn)
            states.append(jnp.transpose(st, (0, 2, 1, 3)).reshape(batch, 2, d_rnn))
        elif kind == 1:
            z = _mm_act(h, sg_w_in, j, "gelu", BF16, tm_wide, tn_pref=1024)
            mix = _sgu_core(z, sg_norm_g[j], sg_w_s[j], sg_b_s[j], tm_small)
        else:
            mix = _mm_sconv(h, sc_w_in, sc_conv_w, j, n_ctx, seq, tm)
        w_out = (rg_w_out, sg_w_out, sc_w_out)[kind]
        f = i // 2
        last = i == depth - 1
        dense_ffn = i % 2 == 0
        h = routed = None
        if mix.shape[1] * d * 2 > RESIDENT_WEIGHT_BYTES:
            y = _mm_residual(mix, w_out, j, y, mod, i, 2, n_ctx, dec_seq,
                             tm if mix.shape[1] <= 4096 else tm_half)
        elif dense_ffn:
            y, h = _residual_norm(mix, w_out, j, y, mod, i, 2, norm_ffn_g[i], 3, n_ctx, dec_seq, tm_half)
        else:
            y, *routed = _residual_norm(mix, w_out, j, y, mod, i, 2, norm_ffn_g[i], 3, n_ctx, dec_seq,
                                        tm_half, router=moe_router[f], router_b=moe_router_b[f])

        if dense_ffn:
            if h is None:
                h = _norm_mod(y, norm_ffn_g[i], mod, i, 3, n_ctx, dec_seq, tm_half)
            g = _mm_swiglu(h, ff_w1, ff_w3, f, tm_wide)
            y = _mm_residual(g, ff_w2, f, y, mod, i, 5, n_ctx, dec_seq, tm, w_buffers=1)
            h = None
        else:
            if routed is None:
                routed = _norm_router(y, norm_ffn_g[i], mod, i, 3, moe_router[f], moe_router_b[f],
                                      n_ctx, dec_seq, tm_half)
            h_tok, ids, gates = routed
            dest, src, block_table = _moe_plan(ids, n_exp, moe_bm)
            big, small = block_table(moe_bm), block_table(moe_sub)
            xs = _gather_rows(h_tok, src, small[1], moe_sub, d)
            gs = _grouped_swiglu(xs, moe_w1, moe_w3, f, big, moe_bm)
            os_ = _grouped_down(gs, moe_w2, f, big, moe_bm)
            if last:
                outs = _combine(os_, dest, gates, y, mod, i, 5, final_norm_g, True, n_ctx, dec_seq, tm_small)
            else:
                y, h = _combine(os_, dest, gates, y, mod, i, 5, norm_mix_g[i + 1], False,
                                n_ctx, dec_seq, tm_small)

    y_p, y_s = outs if outs is not None else _final_norm(y, final_norm_g, n_ctx, tm_half)
    new_state = jnp.stack(states, axis=1).astype(x_prompt.dtype)
    return (y_p.reshape(batch, seq, d), y_s.reshape(dec_batch, dec_seq, d), new_state)
```

```python
import functools

import jax
import jax.numpy as jnp
from jax import lax
from jax.experimental import pallas as pl
from jax.experimental.pallas import tpu as pltpu

F32 = jnp.float32
BF16 = jnp.bfloat16

GRID_W = 64
EPS = 1e-6
RG_C = 8.0
TOP_K = 2

LANES = 128
SUBLANES = 8
VMEM_LIMIT_BYTES = 58 * 1024 * 1024
NEG_BIG = -1e30
MXU_ROWS_PER_PASS = 256
RESIDENT_WEIGHT_BYTES = 16 * 1024 * 1024
RG_CHUNK_ROWS = 128
DMA_RING = 3


def _cparams(n_axes):
    return pltpu.CompilerParams(dimension_semantics=("arbitrary",) * n_axes,
                                vmem_limit_bytes=VMEM_LIMIT_BYTES)


def _pick_tile(n, pref):
    if n <= pref:
        return n
    t = (pref // LANES) * LANES
    while t > LANES and n % t:
        t -= LANES
    assert n % t == 0, (n, pref)
    return t


def _row_passes(rows):
    step = MXU_ROWS_PER_PASS if rows % MXU_ROWS_PER_PASS == 0 else rows
    return [pl.ds(r, step) for r in range(0, rows, step)]


def _sigmoid(x):
    return 1.0 / (1.0 + jnp.exp(-x))


def _gelu_tanh(x):
    c = 0.7978845608028654
    return 0.5 * x * (1.0 + jnp.tanh(c * (x + 0.044715 * (x * x * x))))


def _mod_row(i, tm, n_ctx, dec_seq):
    start = i * tm
    return jnp.where(start < n_ctx, 0, 1 + (start - n_ctx) // dec_seq)


def _adaln_kernel(c_ref, w_ref, b_ref, o_ref):
    c = c_ref[...]
    s = (c * _sigmoid(c)).astype(BF16)
    o_ref[...] = jnp.dot(s, w_ref[...].astype(BF16), preferred_element_type=F32) + b_ref[...]


def _adaln(cond, w_mod, b_mod):
    depth, d, n = w_mod.shape
    mc = cond.shape[0]
    tn = _pick_tile(n, 1024)
    return pl.pallas_call(
        _adaln_kernel,
        grid=(depth, n // tn),
        in_specs=[pl.BlockSpec((mc, d), lambda l, j: (0, 0)),
                  pl.BlockSpec((None, d, tn), lambda l, j: (l, 0, j)),
                  pl.BlockSpec((None, 1, tn), lambda l, j: (l, 0, j))],
        out_specs=pl.BlockSpec((None, mc, tn), lambda l, j: (l, 0, j)),
        out_shape=jax.ShapeDtypeStruct((depth, mc, n), F32),
        compiler_params=_cparams(2), name="adaln",
    )(cond, w_mod, b_mod.reshape(depth, 1, n))


def _embed_norm_kernel(xp_ref, xs_ref, pos_ref, g_ref, mod_ref, y_ref, h_ref, *, n_ctx_tiles):
    i = pl.program_id(0)

    @pl.when(i < n_ctx_tiles)
    def _():
        y_ref[...] = xp_ref[...]

    @pl.when(i >= n_ctx_tiles)
    def _():
        y_ref[...] = xs_ref[...] + pos_ref[...]

    h_ref[...] = _norm_mod_value(y_ref[...], g_ref[...], mod_ref, 0).astype(h_ref.dtype)


def _embed_norm(xp, xs, pos, g, mod, tm):
    n_ctx, d = xp.shape
    n_dec = xs.shape[0]
    dec_seq = pos.shape[0]
    nct = n_ctx // tm
    ppt = dec_seq // tm
    t = n_ctx + n_dec
    return pl.pallas_call(
        functools.partial(_embed_norm_kernel, n_ctx_tiles=nct),
        grid=(t // tm,),
        in_specs=[pl.BlockSpec((tm, d), lambda i: (jnp.minimum(i, nct - 1), 0)),
                  pl.BlockSpec((tm, d), lambda i: (jnp.maximum(i - nct, 0), 0)),
                  pl.BlockSpec((tm, d), lambda i: (jnp.maximum(i - nct, 0) % ppt, 0)),
                  pl.BlockSpec((1, d), lambda i: (0, 0)),
                  pl.BlockSpec((None, None, 6, d), lambda i: (0, _mod_row(i, tm, n_ctx, dec_seq), 0, 0))],
        out_specs=[pl.BlockSpec((tm, d), lambda i: (i, 0)),
                   pl.BlockSpec((tm, d), lambda i: (i, 0))],
        out_shape=[jax.ShapeDtypeStruct((t, d), F32), jax.ShapeDtypeStruct((t, d), BF16)],
        compiler_params=_cparams(1), name="embed_norm",
    )(xp, xs, pos, g.reshape(1, d), mod)


def _grid_pos_embed(length, d):
    rows = length // GRID_W
    r = jnp.repeat(jnp.arange(rows), GRID_W)
    col = jnp.tile(jnp.arange(GRID_W), rows)
    quarter = d // 4
    omega = 1.0 / (10000.0 ** (jnp.arange(quarter, dtype=F32) / quarter))

    def emb(p):
        ang = p[:, None].astype(F32) * omega[None, :]
        return jnp.concatenate([jnp.sin(ang), jnp.cos(ang)], axis=-1)

    return jnp.concatenate([emb(r), emb(col)], axis=-1).astype(F32)


def _norm_mod_value(y, g, mod_ref, shift_row):
    ms = jnp.mean(y * y, axis=-1, keepdims=True)
    n = (y * lax.rsqrt(ms + EPS)) * g
    return n * (1.0 + mod_ref[shift_row + 1:shift_row + 2, :]) + mod_ref[shift_row:shift_row + 1, :]


def _norm_mod_kernel(y_ref, g_ref, mod_ref, h_ref, *, shift_row):
    h_ref[...] = _norm_mod_value(y_ref[...], g_ref[...], mod_ref, shift_row).astype(h_ref.dtype)


def _norm_mod(y, g, mod, layer, shift_row, n_ctx, dec_seq, tm):
    t, d = y.shape
    return pl.pallas_call(
        functools.partial(_norm_mod_kernel, shift_row=shift_row),
        grid=(t // tm,),
        in_specs=[pl.BlockSpec((tm, d), lambda i: (i, 0)),
                  pl.BlockSpec((1, d), lambda i: (0, 0)),
                  pl.BlockSpec((None, None, 6, d),
                               lambda i: (layer, _mod_row(i, tm, n_ctx, dec_seq), 0, 0))],
        out_specs=pl.BlockSpec((tm, d), lambda i: (i, 0)),
        out_shape=jax.ShapeDtypeStruct((t, d), BF16),
        compiler_params=_cparams(1), name="norm_mod",
    )(y, g.reshape(1, d), mod)


def _route_rows(h, row0, r_ref, rb_ref, h_ref, ids_ref, gates_ref, n_slab):
    nr = h.shape[0]
    rows = pl.ds(row0, nr)
    half = n_slab * LANES
    bits = lax.bitcast_convert_type(h.astype(jnp.bfloat16).astype(F32), jnp.uint32)
    word = bits[:, half:] | (bits[:, :half] >> 16)
    for s in range(n_slab):
        h_ref[pl.ds(row0 * n_slab + s, nr, stride=n_slab), :] = word[:, s * LANES:(s + 1) * LANES]
    h_hi = h.astype(BF16)
    h_lo = (h - h_hi.astype(F32)).astype(BF16)
    p_hi = jnp.dot(h_hi, r_ref[...], preferred_element_type=F32)
    p_lo = jnp.dot(h_lo, r_ref[:, :LANES], preferred_element_type=F32)
    logits = ((p_hi[:, :LANES] + p_hi[:, LANES:]) + p_lo) + rb_ref[...]
    lane = lax.broadcasted_iota(jnp.int32, logits.shape, 1).astype(F32)
    big = float(LANES)
    m1 = jnp.max(logits, axis=-1, keepdims=True)
    i1 = jnp.min(jnp.where(logits == m1, lane, big), axis=-1, keepdims=True)
    l2 = jnp.where(lane == i1, 2.0 * NEG_BIG, logits)
    m2 = jnp.max(l2, axis=-1, keepdims=True)
    i2 = jnp.min(jnp.where(l2 == m2, lane, big), axis=-1, keepdims=True)
    e = jnp.exp(m2 - m1)
    g1 = 1.0 / (1.0 + e)
    g2 = e / (1.0 + e)
    ids_ref[rows, :] = jnp.where(lane == 0.0, i1, jnp.where(lane == 1.0, i2, 0.0)).astype(jnp.int32)
    gates_ref[rows, :] = jnp.where(lane == 0.0, g1, jnp.where(lane == 1.0, g2, 0.0))


def _norm_router_kernel(y_ref, g_ref, mod_ref, r_ref, rb_ref, h_ref, ids_ref, gates_ref, *,
                        shift_row, n_slab):
    h = _norm_mod_value(y_ref[...], g_ref[...], mod_ref, shift_row)
    _route_rows(h, 0, r_ref, rb_ref, h_ref, ids_ref, gates_ref, n_slab)


def _router_operands(router, router_b, d):
    n_exp = router.shape[1]
    rp = jnp.zeros((d, LANES), F32).at[:, :n_exp].set(router)
    r_hi = rp.astype(BF16)
    r_lo = (rp - r_hi.astype(F32)).astype(BF16)
    rbp = jnp.full((1, LANES), NEG_BIG, F32).at[0, :n_exp].set(router_b)
    return jnp.concatenate([r_hi, r_lo], axis=1), rbp


def _norm_router(y, g, mod, layer, shift_row, router, router_b, n_ctx, dec_seq, tm):
    t, d = y.shape
    n_slab = d // (2 * LANES)
    rp, rbp = _router_operands(router, router_b, d)
    return pl.pallas_call(
        functools.partial(_norm_router_kernel, shift_row=shift_row, n_slab=n_slab),
        grid=(t // tm,),
        in_specs=[pl.BlockSpec((tm, d), lambda i: (i, 0)),
                  pl.BlockSpec((1, d), lambda i: (0, 0)),
                  pl.BlockSpec((None, None, 6, d),
                               lambda i: (layer, _mod_row(i, tm, n_ctx, dec_seq), 0, 0)),
                  pl.BlockSpec((d, 2 * LANES), lambda i: (0, 0)),
                  pl.BlockSpec((1, LANES), lambda i: (0, 0))],
        out_specs=[pl.BlockSpec((tm * n_slab, LANES), lambda i: (i, 0)),
                   pl.BlockSpec((tm, LANES), lambda i: (i, 0)),
                   pl.BlockSpec((tm, LANES), lambda i: (i, 0))],
        out_shape=[jax.ShapeDtypeStruct((t * n_slab, LANES), jnp.uint32),
                   jax.ShapeDtypeStruct((t, LANES), jnp.int32),
                   jax.ShapeDtypeStruct((t, LANES), F32)],
        compiler_params=_cparams(1), name="norm_router",
    )(y, g.reshape(1, d), mod, rp, rbp)


def _final_norm_kernel(y_ref, g_ref, op_ref, os_ref, *, n_ctx_tiles):
    i = pl.program_id(0)
    y = y_ref[...]
    ms = jnp.mean(y * y, axis=-1, keepdims=True)
    n = (y * lax.rsqrt(ms + EPS)) * g_ref[...]

    @pl.when(i < n_ctx_tiles)
    def _():
        op_ref[...] = n

    @pl.when(i >= n_ctx_tiles)
    def _():
        os_ref[...] = n


def _final_norm(y, g, n_ctx, tm):
    t, d = y.shape
    nct = n_ctx // tm
    return pl.pallas_call(
        functools.partial(_final_norm_kernel, n_ctx_tiles=nct),
        grid=(t // tm,),
        in_specs=[pl.BlockSpec((tm, d), lambda i: (i, 0)),
                  pl.BlockSpec((1, d), lambda i: (0, 0))],
        out_specs=[pl.BlockSpec((tm, d), lambda i: (jnp.minimum(i, nct - 1), 0)),
                   pl.BlockSpec((tm, d), lambda i: (jnp.maximum(i - nct, 0), 0))],
        out_shape=[jax.ShapeDtypeStruct((n_ctx, d), F32),
                   jax.ShapeDtypeStruct((t - n_ctx, d), F32)],
        compiler_params=_cparams(1), name="final_norm",
    )(y, g.reshape(1, d))


def _cast_at_first_row_tile(w_refs, wb_refs):
    @pl.when(pl.program_id(1) == 0)
    def _():
        for w, wb in zip(w_refs, wb_refs):
            wb[...] = w[...].astype(BF16)


def _mm_act_kernel(x_ref, w_ref, o_ref, wb, *, act):
    _cast_at_first_row_tile((w_ref,), (wb,))
    for rows in _row_passes(x_ref.shape[0]):
        acc = jnp.dot(x_ref[rows, :], wb[...], preferred_element_type=F32)
        if act == "gelu":
            acc = _gelu_tanh(acc)
        o_ref[rows, :] = acc.astype(o_ref.dtype)


def _mm_act(x, w, layer, act, out_dtype, tm, tn_pref=512):
    t, k = x.shape
    n = w.shape[2]
    tn = _pick_tile(n, tn_pref)
    return pl.pallas_call(
        functools.partial(_mm_act_kernel, act=act),
        grid=(n // tn, t // tm),
        in_specs=[pl.BlockSpec((tm, k), lambda j, i: (i, 0)),
                  pl.BlockSpec((None, k, tn), lambda j, i: (layer, 0, j))],
        out_specs=pl.BlockSpec((tm, tn), lambda j, i: (i, j)),
        out_shape=jax.ShapeDtypeStruct((t, n), out_dtype),
        scratch_shapes=[pltpu.VMEM((k, tn), BF16)],
        compiler_params=_cparams(2), name="mm_" + str(act),
    )(x, w)


def _swiglu_passes(x_ref, wb1, wb3, o_ref):
    for rows in _row_passes(x_ref.shape[0]):
        x = x_ref[rows, :]
        a = jnp.dot(x, wb1[...], preferred_element_type=F32)
        b = jnp.dot(x, wb3[...], preferred_element_type=F32)
        o_ref[rows, :] = ((a * _sigmoid(a)) * b).astype(o_ref.dtype)


def _mm_swiglu_kernel(x_ref, w1_ref, w3_ref, o_ref, wb1, wb3):
    _cast_at_first_row_tile((w1_ref, w3_ref), (wb1, wb3))
    _swiglu_passes(x_ref, wb1, wb3, o_ref)


def _mm_swiglu(x, w1, w3, layer, tm, tn_pref=512):
    t, k = x.shape
    n = w1.shape[2]
    tn = _pick_tile(n, tn_pref)
    return pl.pallas_call(
        _mm_swiglu_kernel,
        grid=(n // tn, t // tm),
        in_specs=[pl.BlockSpec((tm, k), lambda j, i: (i, 0)),
                  pl.BlockSpec((None, k, tn), lambda j, i: (layer, 0, j)),
                  pl.BlockSpec((None, k, tn), lambda j, i: (layer, 0, j))],
        out_specs=pl.BlockSpec((tm, tn), lambda j, i: (i, j)),
        out_shape=jax.ShapeDtypeStruct((t, n), BF16),
        scratch_shapes=[pltpu.VMEM((k, tn), BF16), pltpu.VMEM((k, tn), BF16)],
        compiler_params=_cparams(2), name="mm_swiglu",
    )(x, w1, w3)


def _mm_sconv_kernel(x_ref, wb_ref, wc_ref, wx_ref, cw_ref, o_ref, sb, sc, sx, *, n_ctx_tiles, seq_ctx):
    _cast_at_first_row_tile((wb_ref, wc_ref, wx_ref), (sb, sc, sx))
    i = pl.program_id(1)
    is_ctx = i < n_ctx_tiles
    x = x_ref[...]
    bg = jnp.dot(x, sb[...], preferred_element_type=F32)
    p = jnp.dot(x, sc[...], preferred_element_type=F32) * jnp.dot(x, sx[...], preferred_element_type=F32)
    tm = p.shape[0]
    row = lax.broadcasted_iota(jnp.int32, p.shape, 0)
    pos = jnp.where(is_ctx, row % seq_ctx, row)
    last_pos = jnp.where(is_ctx, seq_ctx - 1, tm - 1)
    p_prev = jnp.where(pos == 0, 0.0, pltpu.roll(p, 1, axis=0))
    p_next = jnp.where(pos == last_pos, 0.0, pltpu.roll(p, tm - 1, axis=0))
    cw = cw_ref[...]
    conv = (cw[0:1] * p_prev + cw[1:2] * p) + cw[2:3] * p_next
    o_ref[...] = (bg * conv).astype(o_ref.dtype)


def _mm_sconv(x, w_in, conv_w, layer, n_ctx, seq_ctx, tm, tn_pref=512):
    t, k = x.shape
    d = conv_w.shape[2]
    tn = _pick_tile(d, tn_pref)
    nj = d // tn
    return pl.pallas_call(
        functools.partial(_mm_sconv_kernel, n_ctx_tiles=n_ctx // tm, seq_ctx=seq_ctx),
        grid=(nj, t // tm),
        in_specs=[pl.BlockSpec((tm, k), lambda j, i: (i, 0)),
                  pl.BlockSpec((None, k, tn), lambda j, i: (layer, 0, j)),
                  pl.BlockSpec((None, k, tn), lambda j, i: (layer, 0, nj + j)),
                  pl.BlockSpec((None, k, tn), lambda j, i: (layer, 0, 2 * nj + j)),
                  pl.BlockSpec((None, conv_w.shape[1], tn), lambda j, i: (layer, 0, j))],
        out_specs=pl.BlockSpec((tm, tn), lambda j, i: (i, j)),
        out_shape=jax.ShapeDtypeStruct((t, d), BF16),
        scratch_shapes=[pltpu.VMEM((k, tn), BF16)] * 3,
        compiler_params=_cparams(2), name="mm_sconv",
    )(x, w_in, w_in, w_in, conv_w)


def _mm_residual_kernel(x_ref, w_ref, y_ref, mod_ref, o_ref, wb, *, gate_row):
    _cast_at_first_row_tile((w_ref,), (wb,))
    gate = mod_ref[gate_row:gate_row + 1, :]
    for rows in _row_passes(x_ref.shape[0]):
        acc = jnp.dot(x_ref[rows, :], wb[...], preferred_element_type=F32)
        o_ref[rows, :] = y_ref[rows, :] + gate * acc


def _mm_residual(x, w, w_layer, y, mod, layer, gate_row, n_ctx, dec_seq, tm, tn_pref=512, w_buffers=2):
    t, k = x.shape
    n = w.shape[2]
    tn = _pick_tile(n, tn_pref)
    return pl.pallas_call(
        functools.partial(_mm_residual_kernel, gate_row=gate_row),
        grid=(n // tn, t // tm),
        in_specs=[pl.BlockSpec((tm, k), lambda j, i: (i, 0)),
                  pl.BlockSpec((None, k, tn), lambda j, i: (w_layer, 0, j),
                               pipeline_mode=pl.Buffered(w_buffers)),
                  pl.BlockSpec((tm, tn), lambda j, i: (i, j)),
                  pl.BlockSpec((None, None, 6, tn),
                               lambda j, i: (layer, _mod_row(i, tm, n_ctx, dec_seq), 0, j))],
        out_specs=pl.BlockSpec((tm, tn), lambda j, i: (i, j)),
        out_shape=jax.ShapeDtypeStruct((t, n), F32),
        scratch_shapes=[pltpu.VMEM((k, tn), BF16)],
        compiler_params=_cparams(2), name="mm_residual",
    )(x, w, y, mod)


def _residual_norm_kernel(x_ref, w_hbm, y_ref, mod_ref, g_ref, *rest,
                          w_layer, gate_row, shift_row, chunk, route_slabs):
    if route_slabs:
        r_ref, rb_ref, yo_ref, ho_ref, ids_ref, gates_ref, wb, stage, sem = rest
    else:
        yo_ref, ho_ref, wb, stage, sem = rest
    k = wb.shape[0]
    nchunk = k // chunk

    @pl.when(pl.program_id(0) == 0)
    def _():
        def copy(c):
            return pltpu.make_async_copy(w_hbm.at[w_layer, pl.ds(c * chunk, chunk), :],
                                         stage.at[c % 2], sem.at[c % 2])

        copy(0).start()
        for c in range(nchunk):
            if c + 1 < nchunk:
                copy(c + 1).start()
            copy(c).wait()
            wb[pl.ds(c * chunk, chunk), :] = stage[c % 2].astype(BF16)

    gate = mod_ref[gate_row:gate_row + 1, :]
    g = g_ref[...]
    for rows in _row_passes(x_ref.shape[0]):
        acc = jnp.dot(x_ref[rows, :], wb[...], preferred_element_type=F32)
        y = y_ref[rows, :] + gate * acc
        yo_ref[rows, :] = y
        h = _norm_mod_value(y, g, mod_ref, shift_row)
        if route_slabs:
            _route_rows(h, rows.start, r_ref, rb_ref, ho_ref, ids_ref, gates_ref, route_slabs)
        else:
            ho_ref[rows, :] = h.astype(ho_ref.dtype)


def _residual_norm(x, w, w_layer, y, mod, layer, gate_row, next_g, shift_row, n_ctx, dec_seq, tm,
                   router=None, router_b=None):
    t, k = x.shape
    d = w.shape[2]
    chunk = _pick_tile(k, 512)
    in_specs = [pl.BlockSpec((tm, k), lambda i: (i, 0)),
                pl.BlockSpec(memory_space=pl.ANY),
                pl.BlockSpec((tm, d), lambda i: (i, 0)),
                pl.BlockSpec((None, None, 6, d),
                             lambda i: (layer, _mod_row(i, tm, n_ctx, dec_seq), 0, 0)),
                pl.BlockSpec((1, d), lambda i: (0, 0))]
    args = [x, w, y, mod, next_g.reshape(1, d)]
    if router is None:
        route_slabs = 0
        out_specs = [pl.BlockSpec((tm, d), lambda i: (i, 0)),
                     pl.BlockSpec((tm, d), lambda i: (i, 0))]
        out_shape = [jax.ShapeDtypeStruct((t, d), F32), jax.ShapeDtypeStruct((t, d), BF16)]
    else:
        route_slabs = d // (2 * LANES)
        in_specs += [pl.BlockSpec((d, 2 * LANES), lambda i: (0, 0)),
                     pl.BlockSpec((1, LANES), lambda i: (0, 0))]
        args += list(_router_operands(router, router_b, d))
        out_specs = [pl.BlockSpec((tm, d), lambda i: (i, 0)),
                     pl.BlockSpec((tm * route_slabs, LANES), lambda i: (i, 0)),
                     pl.BlockSpec((tm, LANES), lambda i: (i, 0)),
                     pl.BlockSpec((tm, LANES), lambda i: (i, 0))]
        out_shape = [jax.ShapeDtypeStruct((t, d), F32),
                     jax.ShapeDtypeStruct((t * route_slabs, LANES), jnp.uint32),
                     jax.ShapeDtypeStruct((t, LANES), jnp.int32),
                     jax.ShapeDtypeStruct((t, LANES), F32)]
    return pl.pallas_call(
        functools.partial(_residual_norm_kernel, w_layer=w_layer, gate_row=gate_row,
                          shift_row=shift_row, chunk=chunk, route_slabs=route_slabs),
        grid=(t // tm,),
        in_specs=in_specs,
        out_specs=out_specs,
        out_shape=out_shape,
        scratch_shapes=[pltpu.VMEM((k, d), BF16), pltpu.VMEM((2, chunk, d), F32),
                        pltpu.SemaphoreType.DMA((2,))],
        compiler_params=_cparams(1), name="residual_route" if router is not None else "residual_norm",
    )(*args)


def _rg_core_kernel(gate_ref, x_ref, cw_ref, cb_ref, wg_ref, bg_ref, lam_ref, h0_ref,
                    y_ref, st_ref, af_s, bf_s, ab_s, bb_s, *, n_ctx_tiles, seg, tm, cg):
    i = pl.program_id(1)
    is_ctx = i < n_ctx_tiles
    nseg = tm // seg
    nslab = cg // LANES
    lsub = tm // SUBLANES
    sub_per_seq = seg // lsub
    cw = cw_ref[...]
    cb = cb_ref[...]
    lam = lam_ref[...]
    softplus_neg_lam = jnp.maximum(-lam, 0.0) + jnp.log1p(jnp.exp(-jnp.abs(lam)))
    neg_c_sp = (-RG_C) * softplus_neg_lam
    zeros8 = jnp.zeros((SUBLANES, cg), F32)

    cr = min(RG_CHUNK_ROWS, lsub)
    for c0 in range(0, tm, cr):
        cur = x_ref[pl.ds(c0, cr), :]
        if c0 == 0:
            prev8 = zeros8
        else:
            prev8 = x_ref[pl.ds(c0 - SUBLANES, SUBLANES), :]
            if c0 % seg == 0:
                prev8 = jnp.where(is_ctx, 0.0, prev8)
        if c0 + cr == tm:
            next8 = zeros8
        else:
            next8 = x_ref[pl.ds(c0 + cr, SUBLANES), :]
            if (c0 + cr) % seg == 0:
                next8 = jnp.where(is_ctx, 0.0, next8)
        head = jnp.concatenate([prev8, cur[:2 * SUBLANES]], axis=0)
        tail = jnp.concatenate([cur[cr - 2 * SUBLANES:], next8], axis=0)

        def shifted(delta):
            lo = SUBLANES + delta
            mid = x_ref[pl.ds(c0 + lo, cr - 2 * SUBLANES), :]
            return jnp.concatenate([head[lo:lo + SUBLANES], mid, tail[lo:lo + SUBLANES]], axis=0)

        xc = (((cw[0:1] * shifted(-2) + cw[1:2] * shifted(-1)) + cw[2:3] * cur)
              + cw[3:4] * shifted(1)) + cb
        g = jnp.dot(xc.astype(BF16), wg_ref[...], preferred_element_type=F32) + bg_ref[...]
        for d, (a_s, b_s) in enumerate(((af_s, bf_s), (ab_s, bb_s))):
            r = _sigmoid(g[:, (2 * d) * cg:(2 * d + 1) * cg])
            ig = _sigmoid(g[:, (2 * d + 1) * cg:(2 * d + 2) * cg])
            log_a = neg_c_sp[d:d + 1] * r
            a = jnp.exp(log_a)
            one_minus_a2 = -jnp.tanh(log_a) * (a * a + 1.0)
            root = jnp.where(one_minus_a2 > 0.0, one_minus_a2 * lax.rsqrt(one_minus_a2), 0.0)
            bt = root * (ig * xc)
            k, j0 = c0 // lsub, c0 % lsub
            dst = pl.ds(k + SUBLANES * j0, cr, stride=SUBLANES)
            for l in range(nslab):
                a_s[l, dst, :] = a[:, l * LANES:(l + 1) * LANES]
                b_s[l, dst, :] = bt[:, l * LANES:(l + 1) * LANES]

    def local_scan(j, carry):
        hf, pf, hb, pb = carry
        rf = pl.multiple_of(j * SUBLANES, SUBLANES)
        rb = pl.multiple_of((lsub - 1 - j) * SUBLANES, SUBLANES)
        nhf, npf, nhb, npb = [], [], [], []
        for l in range(nslab):
            a = af_s[l, pl.ds(rf, SUBLANES), :]
            h = a * hf[l] + bf_s[l, pl.ds(rf, SUBLANES), :]
            p = a * pf[l]
            bf_s[l, pl.ds(rf, SUBLANES), :] = h
            af_s[l, pl.ds(rf, SUBLANES), :] = p
            nhf.append(h)
            npf.append(p)
            a = ab_s[l, pl.ds(rb, SUBLANES), :]
            h = a * hb[l] + bb_s[l, pl.ds(rb, SUBLANES), :]
            p = a * pb[l]
            bb_s[l, pl.ds(rb, SUBLANES), :] = h
            ab_s[l, pl.ds(rb, SUBLANES), :] = p
            nhb.append(h)
            npb.append(p)
        return tuple(nhf), tuple(npf), tuple(nhb), tuple(npb)

    zero = tuple(jnp.zeros((SUBLANES, LANES), F32) for _ in range(nslab))
    one = tuple(jnp.ones((SUBLANES, LANES), F32) for _ in range(nslab))
    hf_end, pf_end, hb_end, pb_end = lax.fori_loop(0, lsub, local_scan, (zero, one, zero, one))

    row8 = lax.broadcasted_iota(jnp.int32, (SUBLANES, LANES), 0)
    h0 = h0_ref[...]
    for l in range(nslab):
        lanes = slice(l * LANES, (l + 1) * LANES)
        ent_f = jnp.zeros((SUBLANES, LANES), F32)
        h_in = h0[0:1, lanes]
        for k in range(SUBLANES):
            if k > 0 and k % sub_per_seq == 0:
                h_in = jnp.where(is_ctx, 0.0, h_in)
            ent_f = jnp.where(row8 == k, h_in, ent_f)
            h_in = hf_end[l][k:k + 1] + pf_end[l][k:k + 1] * h_in
            if (k + 1) % sub_per_seq == 0:
                q = k // sub_per_seq
                st_ref[q:q + 1, lanes] = h_in
        ent_b = jnp.zeros((SUBLANES, LANES), F32)
        h_in = h0[1:2, lanes]
        for k in reversed(range(SUBLANES)):
            if k < SUBLANES - 1 and (k + 1) % sub_per_seq == 0:
                h_in = jnp.where(is_ctx, 0.0, h_in)
            ent_b = jnp.where(row8 == k, h_in, ent_b)
            h_in = hb_end[l][k:k + 1] + pb_end[l][k:k + 1] * h_in
            if k % sub_per_seq == 0:
                q = k // sub_per_seq
                st_ref[nseg + q:nseg + q + 1, lanes] = h_in
        ch = min(tm, 256)
        ef = jnp.concatenate([ent_f] * (ch // SUBLANES), axis=0)
        eb = jnp.concatenate([ent_b] * (ch // SUBLANES), axis=0)
        for r0 in range(0, tm, ch):
            rows = pl.ds(r0, ch)
            bf_s[l, rows, :] = ((bf_s[l, rows, :] + af_s[l, rows, :] * ef)
                                + (bb_s[l, rows, :] + ab_s[l, rows, :] * eb))

    for k in range(SUBLANES):
        rows = pl.ds(k * lsub, lsub)
        hsum = jnp.concatenate([bf_s[l, pl.ds(k, lsub, stride=SUBLANES), :] for l in range(nslab)], axis=1)
        y_ref[rows, :] = (hsum * _gelu_tanh(gate_ref[rows, :])).astype(y_ref.dtype)


def _rg_core(z, conv_w, conv_b, w_a, b_a, w_x, b_x, lam, h0, n_ctx, seq_ctx, tm):
    t = z.shape[0]
    r = conv_w.shape[1]
    heads, hw = w_a.shape[1], w_a.shape[2]
    hpg = 4
    while (hpg * hw) % LANES:
        hpg *= 2
    ng = heads // hpg
    cg = hpg * hw
    nseg = tm // seq_ctx
    ntiles = t // tm
    assert tm % SUBLANES == 0 and seq_ctx % (tm // SUBLANES) == 0

    def blockdiag(w):
        rows = jnp.tile(w.reshape(2, ng, cg, hw), (1, 1, 1, hpg))
        head = jnp.arange(cg, dtype=jnp.int32) // hw
        return jnp.where(head[:, None] == head[None, :], rows, 0.0)

    wa, wx = blockdiag(w_a), blockdiag(w_x)
    wg = jnp.concatenate([wa[0], wx[0], wa[1], wx[1]], axis=-1).astype(BF16)
    ba, bx = b_a.reshape(2, ng, 1, cg), b_x.reshape(2, ng, 1, cg)
    bg = jnp.concatenate([ba[0], bx[0], ba[1], bx[1]], axis=-1)

    return pl.pallas_call(
        functools.partial(_rg_core_kernel, n_ctx_tiles=n_ctx // tm, seg=seq_ctx, tm=tm, cg=cg),
        grid=(ng, ntiles),
        in_specs=[pl.BlockSpec((tm, cg), lambda g, i: (i, g)),
                  pl.BlockSpec((tm, cg), lambda g, i: (i, ng + g)),
                  pl.BlockSpec((conv_w.shape[0], cg), lambda g, i: (0, g)),
                  pl.BlockSpec((1, cg), lambda g, i: (0, g)),
                  pl.BlockSpec((None, cg, 4 * cg), lambda g, i: (g, 0, 0)),
                  pl.BlockSpec((None, 1, 4 * cg), lambda g, i: (g, 0, 0)),
                  pl.BlockSpec((2, cg), lambda g, i: (0, g)),
                  pl.BlockSpec((None, 2, cg), lambda g, i: (i, 0, g))],
        out_specs=[pl.BlockSpec((tm, cg), lambda g, i: (i, g)),
                   pl.BlockSpec((None, 2 * nseg, cg), lambda g, i: (i, 0, g))],
        out_shape=[jax.ShapeDtypeStruct((t, r), BF16),
                   jax.ShapeDtypeStruct((ntiles, 2 * nseg, r), F32)],
        scratch_shapes=[pltpu.VMEM((cg // LANES, tm, LANES), F32)] * 4,
        compiler_params=_cparams(2), name="rg_core",
    )(z, z, conv_w, conv_b.reshape(1, r), wg, bg, lam, h0)


def _sgu_core_kernel(u_ref, v_ref, g_ref, ws_ref, bs_ref, y_ref, vn_s, *, chunk, gw, ngroups, tm):
    v = v_ref[...].astype(F32)
    ms = jnp.mean(v * v, axis=-1, keepdims=True)
    vn_s[...] = ((v * lax.rsqrt(ms + EPS)) * g_ref[...]).astype(BF16)
    for c in range(tm // chunk):
        rows = pl.ds(c * chunk, chunk)
        for g in range(ngroups):
            cols = pl.ds(g * gw, gw)
            vm = jnp.dot(ws_ref[g], vn_s[rows, cols], preferred_element_type=F32) + bs_ref[:, g:g + 1]
            y_ref[rows, cols] = (u_ref[rows, cols].astype(F32) * vm).astype(y_ref.dtype)


def _sgu_core(z, norm_g, w_s, b_s, tm):
    t = z.shape[0]
    w = norm_g.shape[0]
    ngroups, chunk = w_s.shape[0], w_s.shape[1]
    gw = w // ngroups
    return pl.pallas_call(
        functools.partial(_sgu_core_kernel, chunk=chunk, gw=gw, ngroups=ngroups, tm=tm),
        grid=(t // tm,),
        in_specs=[pl.BlockSpec((tm, w), lambda i: (i, 0)),
                  pl.BlockSpec((tm, w), lambda i: (i, 1)),
                  pl.BlockSpec((1, w), lambda i: (0, 0)),
                  pl.BlockSpec((ngroups, chunk, chunk), lambda i: (0, 0, 0)),
                  pl.BlockSpec((chunk, ngroups), lambda i: (0, 0))],
        out_specs=pl.BlockSpec((tm, w), lambda i: (i, 0)),
        out_shape=jax.ShapeDtypeStruct((t, w), BF16),
        scratch_shapes=[pltpu.VMEM((tm, w), BF16)],
        compiler_params=_cparams(1), name="sgu_core",
    )(z, z, norm_g.reshape(1, w), w_s.astype(BF16), b_s.T)


def _moe_plan(ids, n_exp, bm):
    t = ids.shape[0]
    e = ids[:, :TOP_K].reshape(-1)
    oh = (e[:, None] == jnp.arange(n_exp, dtype=jnp.int32)[None, :]).astype(jnp.int32)
    csum = jnp.cumsum(oh, axis=0)
    rank = jnp.sum((csum - oh) * oh, axis=1)
    counts = csum[-1]
    padded = ((counts + bm - 1) // bm) * bm
    ends = jnp.cumsum(padded)
    starts = ends - padded
    first_cnt = counts - jnp.maximum(padded - bm, 0)
    start_e = jnp.sum(starts[None, :] * oh, axis=1)
    first_e = jnp.sum(first_cnt[None, :] * oh, axis=1)
    dest = (start_e + rank + jnp.where(rank >= first_e, bm - first_e, 0)).astype(jnp.int32)
    p = TOP_K * t + n_exp * bm
    src = jnp.zeros((p,), jnp.int32).at[dest].set(jnp.arange(TOP_K * t, dtype=jnp.int32) // TOP_K)

    def block_table(blk):
        idx = jnp.arange(p // blk, dtype=jnp.int32)
        start = idx * blk
        exp = jnp.minimum(jnp.sum((start[:, None] >= ends[None, :]).astype(jnp.int32), axis=1), n_exp - 1)
        offset = start - starts[exp]
        valid = jnp.where(offset < bm, jnp.clip(first_cnt[exp] - offset, 0, blk), blk)
        valid = jnp.where(start < ends[-1], valid, 0)
        fetch = lax.cummax(jnp.where(valid > 0, idx, 0), axis=0)
        return exp.astype(jnp.int32), valid.astype(jnp.int32), fetch.astype(jnp.int32)

    return dest, src, block_table


def _gather_rows_kernel(src_ref, valid_ref, h_hbm, o_ref, buf, sem, *, rows, n_slab):
    b = pl.program_id(0)
    nb = pl.num_programs(0)
    slot = b % DMA_RING
    ahead = DMA_RING - 1

    def block_has_rows(blk):
        return jnp.logical_and(blk < nb, valid_ref[jnp.minimum(blk, nb - 1)] > 0)

    has_rows = block_has_rows(b)

    def issue(blk, to_slot):
        base = blk * rows

        def body(q, carry):
            for prio in range(2):
                r = 2 * q + prio
                tok = pl.multiple_of(src_ref[base + r] * n_slab, n_slab)
                pltpu.make_async_copy(h_hbm.at[pl.ds(tok, n_slab), :],
                                      buf.at[to_slot, pl.ds(pl.multiple_of(r * n_slab, n_slab), n_slab), :],
                                      sem.at[to_slot]).start(priority=prio)
            return carry

        lax.fori_loop(0, rows // 2, body, 0, unroll=4)

    @pl.when(b == 0)
    def _():
        for first in range(ahead):
            @pl.when(block_has_rows(first))
            def _():
                issue(first, first % DMA_RING)

    @pl.when(block_has_rows(b + ahead))
    def _():
        issue(b + ahead, (b + ahead) % DMA_RING)

    @pl.when(has_rows)
    def _():
        pltpu.make_async_copy(h_hbm.at[pl.ds(0, rows * n_slab), :], buf.at[slot], sem.at[slot]).wait()
        half = n_slab * LANES
        for s in range(n_slab):
            word = buf[slot, pl.ds(s, rows, stride=n_slab), :]
            lo = lax.bitcast_convert_type(word << 16, F32)
            hi = lax.bitcast_convert_type(word & jnp.uint32(0xFFFF0000), F32)
            o_ref[:, s * LANES:(s + 1) * LANES] = lo.astype(o_ref.dtype)
            o_ref[:, half + s * LANES:half + (s + 1) * LANES] = hi.astype(o_ref.dtype)

    @pl.when(jnp.logical_not(has_rows))
    def _():
        o_ref[...] = jnp.zeros_like(o_ref)


def _gather_rows(h_tok, src, valid, bm, d):
    n_slab = d // (2 * LANES)
    p = src.shape[0]
    return pl.pallas_call(
        functools.partial(_gather_rows_kernel, rows=bm, n_slab=n_slab),
        grid_spec=pltpu.PrefetchScalarGridSpec(
            num_scalar_prefetch=2,
            grid=(p // bm,),
            in_specs=[pl.BlockSpec(memory_space=pl.ANY)],
            out_specs=pl.BlockSpec((bm, d), lambda b, src, va: (b, 0)),
            scratch_shapes=[pltpu.VMEM((DMA_RING, bm * n_slab, LANES), jnp.uint32),
                            pltpu.SemaphoreType.DMA((DMA_RING,))]),
        out_shape=jax.ShapeDtypeStruct((p, d), BF16),
        compiler_params=_cparams(1), name="moe_gather",
    )(src, valid, h_tok)


def _expert_changed(exp_ref, fetch_ref, b):
    prev = fetch_ref[jnp.maximum(b - 1, 0)]
    return jnp.logical_or(b == 0, exp_ref[b] != exp_ref[prev])


def _guarded_passes(valid, x_ref, o_ref, one_pass):
    bm = x_ref.shape[0]
    passes = _row_passes(bm)
    pass_rows = bm // len(passes)
    needed = (valid + (pass_rows - 1)) // pass_rows

    for count in range(len(passes) + 1):
        @pl.when(needed == count)
        def _():
            for rows in passes[:count]:
                one_pass(rows)
            for rows in passes[count:]:
                o_ref[rows, :] = jnp.zeros((pass_rows, o_ref.shape[1]), o_ref.dtype)


def _grouped_swiglu_kernel(exp_ref, valid_ref, fetch_ref, x_ref, w1_ref, w3_ref, o_ref, wb1, wb3):
    b = pl.program_id(1)
    valid = valid_ref[b]

    @pl.when(jnp.logical_and(valid > 0, _expert_changed(exp_ref, fetch_ref, b)))
    def _():
        wb1[...] = w1_ref[...].astype(BF16)
        wb3[...] = w3_ref[...].astype(BF16)

    def one_pass(rows):
        x = x_ref[rows, :]
        a = jnp.dot(x, wb1[...], preferred_element_type=F32)
        c = jnp.dot(x, wb3[...], preferred_element_type=F32)
        o_ref[rows, :] = ((a * _sigmoid(a)) * c).astype(o_ref.dtype)

    _guarded_passes(valid, x_ref, o_ref, one_pass)


def _grouped_swiglu(xs, w1, w3, layer, table, bm, tn_pref=512):
    p, k = xs.shape
    n = w1.shape[3]
    tn = _pick_tile(n, tn_pref)
    return pl.pallas_call(
        _grouped_swiglu_kernel,
        grid_spec=pltpu.PrefetchScalarGridSpec(
            num_scalar_prefetch=3,
            grid=(n // tn, p // bm),
            in_specs=[pl.BlockSpec((bm, k), lambda j, b, ex, va, fe: (fe[b], 0)),
                      pl.BlockSpec((None, None, k, tn), lambda j, b, ex, va, fe: (layer, ex[fe[b]], 0, j)),
                      pl.BlockSpec((None, None, k, tn), lambda j, b, ex, va, fe: (layer, ex[fe[b]], 0, j))],
            out_specs=pl.BlockSpec((bm, tn), lambda j, b, ex, va, fe: (b, j)),
            scratch_shapes=[pltpu.VMEM((k, tn), BF16), pltpu.VMEM((k, tn), BF16)]),
        out_shape=jax.ShapeDtypeStruct((p, n), BF16),
        compiler_params=_cparams(2), name="moe_swiglu",
    )(*table, xs, w1, w3)


def _grouped_down_kernel(exp_ref, valid_ref, fetch_ref, x_ref, w_ref, o_ref, wb):
    b = pl.program_id(1)
    valid = valid_ref[b]

    @pl.when(jnp.logical_and(valid > 0, _expert_changed(exp_ref, fetch_ref, b)))
    def _():
        wb[...] = w_ref[...].astype(BF16)

    def one_pass(rows):
        o_ref[rows, :] = jnp.dot(x_ref[rows, :], wb[...], preferred_element_type=F32)

    _guarded_passes(valid, x_ref, o_ref, one_pass)


def _grouped_down(gs, w2, layer, table, bm, tn_pref=512):
    p, k = gs.shape
    n = w2.shape[3]
    tn = _pick_tile(n, tn_pref)
    return pl.pallas_call(
        _grouped_down_kernel,
        grid_spec=pltpu.PrefetchScalarGridSpec(
            num_scalar_prefetch=3,
            grid=(n // tn, p // bm),
            in_specs=[pl.BlockSpec((bm, k), lambda j, b, ex, va, fe: (fe[b], 0)),
                      pl.BlockSpec((None, None, k, tn), lambda j, b, ex, va, fe: (layer, ex[fe[b]], 0, j))],
            out_specs=pl.BlockSpec((bm, tn), lambda j, b, ex, va, fe: (b, j)),
            scratch_shapes=[pltpu.VMEM((k, tn), BF16)]),
        out_shape=jax.ShapeDtypeStruct((p, n), F32),
        compiler_params=_cparams(2), name="moe_down",
    )(*table, gs, w2)


def _combine_kernel(dest_ref, os_hbm, y_ref, gates_ref, mod_ref, g_ref, *rest,
                    rows, gate_row, n_ctx_tiles, final):
    if final:
        out_a, out_b, buf, sem = rest
    else:
        nmod_ref, out_a, out_b, buf, sem = rest
    i = pl.program_id(0)
    n_tiles = pl.num_programs(0)
    slot = i % DMA_RING
    ahead = DMA_RING - 1

    def issue(tile, to_slot):
        base = tile * rows

        def body(r, carry):
            for k in range(TOP_K):
                row = dest_ref[TOP_K * (base + r) + k]
                pltpu.make_async_copy(os_hbm.at[pl.ds(row, 1), :], buf.at[to_slot, k, pl.ds(r, 1), :],
                                      sem.at[to_slot]).start(priority=k)
            return carry

        lax.fori_loop(0, rows, body, 0, unroll=4)

    @pl.when(i == 0)
    def _():
        for first in range(ahead):
            @pl.when(first < n_tiles)
            def _():
                issue(first, first % DMA_RING)

    @pl.when(i + ahead < n_tiles)
    def _():
        issue(i + ahead, (i + ahead) % DMA_RING)

    for k in range(TOP_K):
        pltpu.make_async_copy(os_hbm.at[pl.ds(0, rows), :], buf.at[slot, k], sem.at[slot]).wait()
    gates = gates_ref[...]
    f = gates[:, 0:1] * buf[slot, 0] + gates[:, 1:2] * buf[slot, 1]
    y = y_ref[...] + mod_ref[gate_row:gate_row + 1, :] * f
    if final:
        ms = jnp.mean(y * y, axis=-1, keepdims=True)
        n = (y * lax.rsqrt(ms + EPS)) * g_ref[...]

        @pl.when(i < n_ctx_tiles)
        def _():
            out_a[...] = n

        @pl.when(i >= n_ctx_tiles)
        def _():
            out_b[...] = n
    else:
        out_a[...] = y
        out_b[...] = _norm_mod_value(y, g_ref[...], nmod_ref, 0).astype(out_b.dtype)


def _combine(os_, dest, gates, y, mod, layer, gate_row, next_g, final, n_ctx, dec_seq, tm):
    t, d = y.shape
    nct = n_ctx // tm

    def mod_spec(which):
        return pl.BlockSpec((None, None, 6, d),
                            lambda i, dst: (which, _mod_row(i, tm, n_ctx, dec_seq), 0, 0))

    in_specs = [pl.BlockSpec(memory_space=pl.ANY),
                pl.BlockSpec((tm, d), lambda i, dst: (i, 0)),
                pl.BlockSpec((tm, LANES), lambda i, dst: (i, 0)),
                mod_spec(layer),
                pl.BlockSpec((1, d), lambda i, dst: (0, 0))]
    args = [dest, os_, y, gates, mod, next_g.reshape(1, d)]
    if final:
        out_specs = [pl.BlockSpec((tm, d), lambda i, dst: (jnp.minimum(i, nct - 1), 0)),
                     pl.BlockSpec((tm, d), lambda i, dst: (jnp.maximum(i - nct, 0), 0))]
        out_shape = [jax.ShapeDtypeStruct((n_ctx, d), F32), jax.ShapeDtypeStruct((t - n_ctx, d), F32)]
    else:
        in_specs.append(mod_spec(layer + 1))
        args.append(mod)
        out_specs = [pl.BlockSpec((tm, d), lambda i, dst: (i, 0)),
                     pl.BlockSpec((tm, d), lambda i, dst: (i, 0))]
        out_shape = [jax.ShapeDtypeStruct((t, d), F32), jax.ShapeDtypeStruct((t, d), BF16)]
    return pl.pallas_call(
        functools.partial(_combine_kernel, rows=tm, gate_row=gate_row, n_ctx_tiles=nct, final=final),
        grid_spec=pltpu.PrefetchScalarGridSpec(
            num_scalar_prefetch=1,
            grid=(t // tm,),
            in_specs=in_specs,
            out_specs=out_specs,
            scratch_shapes=[pltpu.VMEM((DMA_RING, TOP_K, tm, d), F32),
                            pltpu.SemaphoreType.DMA((DMA_RING,))]),
        out_shape=out_shape,
        compiler_params=_cparams(1), name="moe_combine_final" if final else "moe_combine",
    )(*args)


def kernel(x_prompt, x_sample, state_rglru, c, c_ctx, norm_mix_g, norm_ffn_g, w_mod, b_mod, final_norm_g, rg_w_in, rg_conv_w, rg_conv_b, rg_w_a, rg_b_a, rg_w_x, rg_b_x, rg_lam, rg_w_out, sg_w_in, sg_norm_g, sg_w_s, sg_b_s, sg_w_out, sc_w_in, sc_conv_w, sc_w_out, ff_w1, ff_w3, ff_w2, moe_router, moe_router_b, moe_w1, moe_w3, moe_w2):
    batch, seq, d = x_prompt.shape
    dec_batch, dec_seq, _ = x_sample.shape
    depth = w_mod.shape[0]
    n_ctx = batch * seq
    d_rnn = rg_w_out.shape[1]
    n_exp = moe_router.shape[2]
    chunk = sg_w_s.shape[2]

    tm = dec_seq
    assert dec_seq % seq == 0 and n_ctx % tm == 0 and seq % SUBLANES == 0
    tm_half = max(tm // 2, chunk)
    tm_small = max(tm // 4, chunk)
    assert tm % tm_half == 0 and tm % tm_small == 0 and tm_small % chunk == 0
    moe_bm, moe_sub = tm, tm_half
    tm_wide = 2 * tm if (n_ctx + dec_batch * dec_seq) % (2 * tm) == 0 else tm

    n_cond = 1 + dec_batch
    cond = jnp.zeros((-(-n_cond // SUBLANES) * SUBLANES, d), F32)
    cond = cond.at[0].set(c_ctx).at[1:n_cond].set(c)
    mod = _adaln(cond, w_mod, b_mod)[:, :n_cond].reshape(depth, n_cond, 6, d)

    y, h = _embed_norm(x_prompt.reshape(n_ctx, d), x_sample.reshape(dec_batch * dec_seq, d),
                       _grid_pos_embed(dec_seq, d), norm_mix_g[0], mod, tm_small)

    states = []
    outs = None
    for i in range(depth):
        kind, j = i % 3, i // 3
        if h is None:
            h = _norm_mod(y, norm_mix_g[i], mod, i, 0, n_ctx, dec_seq, tm_half)
        if kind == 0:
            z = _mm_act(h, rg_w_in, j, None, F32, tm_wide, tn_pref=1024)
            h0 = jnp.concatenate([jnp.zeros((n_ctx // tm, 2, d_rnn), F32),
                                  state_rglru[:, j].astype(F32)], axis=0)
            mix, st = _rg_core(z, rg_conv_w[j], rg_conv_b[j], rg_w_a[j], rg_b_a[j], rg_w_x[j], rg_b_x[j],
                               rg_lam[j], h0, n_ctx, seq, tm)
            nseg = tm // seq
            st = st[:n_ctx // tm].reshape(n_ctx // tm, 2, nseg, d_rnn)
            states.append(jnp.transpose(st, (0, 2, 1, 3)).reshape(batch, 2, d_rnn))
        elif kind == 1:
            z = _mm_act(h, sg_w_in, j, "gelu", BF16, tm_wide, tn_pref=1024)
            mix = _sgu_core(z, sg_norm_g[j], sg_w_s[j], sg_b_s[j], tm_small)
        else:
            mix = _mm_sconv(h, sc_w_in, sc_conv_w, j, n_ctx, seq, tm)
        w_out = (rg_w_out, sg_w_out, sc_w_out)[kind]
        f = i // 2
        last = i == depth - 1
        dense_ffn = i % 2 == 0
        h = routed = None
        if mix.shape[1] * d * 2 > RESIDENT_WEIGHT_BYTES:
            y = _mm_residual(mix, w_out, j, y, mod, i, 2, n_ctx, dec_seq,
                             tm if mix.shape[1] <= 4096 else tm_half)
        elif dense_ffn:
            y, h = _residual_norm(mix, w_out, j, y, mod, i, 2, norm_ffn_g[i], 3, n_ctx, dec_seq, tm_half)
        else:
            y, *routed = _residual_norm(mix, w_out, j, y, mod, i, 2, norm_ffn_g[i], 3, n_ctx, dec_seq,
                                        tm_half, router=moe_router[f], router_b=moe_router_b[f])

        if dense_ffn:
            if h is None:
                h = _norm_mod(y, norm_ffn_g[i], mod, i, 3, n_ctx, dec_seq, tm_half)
            g = _mm_swiglu(h, ff_w1, ff_w3, f, tm_wide)
            y = _mm_residual(g, ff_w2, f, y, mod, i, 5, n_ctx, dec_seq, tm, w_buffers=1)
            h = None
        else:
            if routed is None:
                routed = _norm_router(y, norm_ffn_g[i], mod, i, 3, moe_router[f], moe_router_b[f],
                                      n_ctx, dec_seq, tm_half)
            h_tok, ids, gates = routed
            dest, src, block_table = _moe_plan(ids, n_exp, moe_bm)
            big, small = block_table(moe_bm), block_table(moe_sub)
            xs = _gather_rows(h_tok, src, small[1], moe_sub, d)
            gs = _grouped_swiglu(xs, moe_w1, moe_w3, f, big, moe_bm)
            os_ = _grouped_down(gs, moe_w2, f, big, moe_bm)
            if last:
                outs = _combine(os_, dest, gates, y, mod, i, 5, final_norm_g, True, n_ctx, dec_seq, tm_small)
            else:
                y, h = _combine(os_, dest, gates, y, mod, i, 5, norm_mix_g[i + 1], False,
                                n_ctx, dec_seq, tm_small)

    y_p, y_s = outs if outs is not None else _final_norm(y, final_norm_g, n_ctx, tm_half)
    new_state = jnp.stack(states, axis=1).astype(x_prompt.dtype)
    return (y_p.reshape(batch, seq, d), y_s.reshape(dec_batch, dec_seq, d), new_state)
```

```python
import functools

import jax
import jax.numpy as jnp
from jax import lax
from jax.experimental import pallas as pl
from jax.experimental.pallas import tpu as pltpu

F32 = jnp.float32
BF16 = jnp.bfloat16

GRID_W = 64
EPS = 1e-6
RG_C = 8.0
TOP_K = 2

LANES = 128
SUBLANES = 8
VMEM_LIMIT_BYTES = 58 * 1024 * 1024
NEG_BIG = -1e30
MXU_ROWS_PER_PASS = 256
RESIDENT_WEIGHT_BYTES = 16 * 1024 * 1024
RG_CHUNK_ROWS = 128
DMA_RING = 3


def _cparams(n_axes):
    return pltpu.CompilerParams(dimension_semantics=("arbitrary",) * n_axes,
                                vmem_limit_bytes=VMEM_LIMIT_BYTES)


def _pick_tile(n, pref):
    if n <= pref:
        return n
    t = (pref // LANES) * LANES
    while t > LANES and n % t:
        t -= LANES
    assert n % t == 0, (n, pref)
    return t


def _row_passes(rows):
    step = MXU_ROWS_PER_PASS if rows % MXU_ROWS_PER_PASS == 0 else rows
    return [pl.ds(r, step) for r in range(0, rows, step)]


def _sigmoid(x):
    return 1.0 / (1.0 + jnp.exp(-x))


def _gelu_tanh(x):
    c = 0.7978845608028654
    return 0.5 * x * (1.0 + jnp.tanh(c * (x + 0.044715 * (x * x * x))))


def _mod_row(i, tm, n_ctx, dec_seq):
    start = i * tm
    return jnp.where(start < n_ctx, 0, 1 + (start - n_ctx) // dec_seq)


def _adaln_kernel(c_ref, w_ref, b_ref, o_ref):
    c = c_ref[...]
    s = (c * _sigmoid(c)).astype(BF16)
    o_ref[...] = jnp.dot(s, w_ref[...].astype(BF16), preferred_element_type=F32) + b_ref[...]


def _adaln(cond, w_mod, b_mod):
    depth, d, n = w_mod.shape
    mc = cond.shape[0]
    tn = _pick_tile(n, 1024)
    return pl.pallas_call(
        _adaln_kernel,
        grid=(depth, n // tn),
        in_specs=[pl.BlockSpec((mc, d), lambda l, j: (0, 0)),
                  pl.BlockSpec((None, d, tn), lambda l, j: (l, 0, j)),
                  pl.BlockSpec((None, 1, tn), lambda l, j: (l, 0, j))],
        out_specs=pl.BlockSpec((None, mc, tn), lambda l, j: (l, 0, j)),
        out_shape=jax.ShapeDtypeStruct((depth, mc, n), F32),
        compiler_params=_cparams(2), name="adaln",
    )(cond, w_mod, b_mod.reshape(depth, 1, n))


def _embed_norm_kernel(xp_ref, xs_ref, pos_ref, g_ref, mod_ref, y_ref, h_ref, *, n_ctx_tiles):
    i = pl.program_id(0)

    @pl.when(i < n_ctx_tiles)
    def _():
        y_ref[...] = xp_ref[...]

    @pl.when(i >= n_ctx_tiles)
    def _():
        y_ref[...] = xs_ref[...] + pos_ref[...]

    h_ref[...] = _norm_mod_value(y_ref[...], g_ref[...], mod_ref, 0).astype(h_ref.dtype)


def _embed_norm(xp, xs, pos, g, mod, tm):
    n_ctx, d = xp.shape
    n_dec = xs.shape[0]
    dec_seq = pos.shape[0]
    nct = n_ctx // tm
    ppt = dec_seq // tm
    t = n_ctx + n_dec
    return pl.pallas_call(
        functools.partial(_embed_norm_kernel, n_ctx_tiles=nct),
        grid=(t // tm,),
        in_specs=[pl.BlockSpec((tm, d), lambda i: (jnp.minimum(i, nct - 1), 0)),
                  pl.BlockSpec((tm, d), lambda i: (jnp.maximum(i - nct, 0), 0)),
                  pl.BlockSpec((tm, d), lambda i: (jnp.maximum(i - nct, 0) % ppt, 0)),
                  pl.BlockSpec((1, d), lambda i: (0, 0)),
                  pl.BlockSpec((None, None, 6, d), lambda i: (0, _mod_row(i, tm, n_ctx, dec_seq), 0, 0))],
        out_specs=[pl.BlockSpec((tm, d), lambda i: (i, 0)),
                   pl.BlockSpec((tm, d), lambda i: (i, 0))],
        out_shape=[jax.ShapeDtypeStruct((t, d), F32), jax.ShapeDtypeStruct((t, d), BF16)],
        compiler_params=_cparams(1), name="embed_norm",
    )(xp, xs, pos, g.reshape(1, d), mod)


def _grid_pos_embed(length, d):
    rows = length // GRID_W
    r = jnp.repeat(jnp.arange(rows), GRID_W)
    col = jnp.tile(jnp.arange(GRID_W), rows)
    quarter = d // 4
    omega = 1.0 / (10000.0 ** (jnp.arange(quarter, dtype=F32) / quarter))

    def emb(p):
        ang = p[:, None].astype(F32) * omega[None, :]
        return jnp.concatenate([jnp.sin(ang), jnp.cos(ang)], axis=-1)

    return jnp.concatenate([emb(r), emb(col)], axis=-1).astype(F32)


def _norm_mod_value(y, g, mod_ref, shift_row):
    ms = jnp.mean(y * y, axis=-1, keepdims=True)
    n = (y * lax.rsqrt(ms + EPS)) * g
    return n * (1.0 + mod_ref[shift_row + 1:shift_row + 2, :]) + mod_ref[shift_row:shift_row + 1, :]


def _norm_mod_kernel(y_ref, g_ref, mod_ref, h_ref, *, shift_row):
    h_ref[...] = _norm_mod_value(y_ref[...], g_ref[...], mod_ref, shift_row).astype(h_ref.dtype)


def _norm_mod(y, g, mod, layer, shift_row, n_ctx, dec_seq, tm):
    t, d = y.shape
    return pl.pallas_call(
        functools.partial(_norm_mod_kernel, shift_row=shift_row),
        grid=(t // tm,),
        in_specs=[pl.BlockSpec((tm, d), lambda i: (i, 0)),
                  pl.BlockSpec((1, d), lambda i: (0, 0)),
                  pl.BlockSpec((None, None, 6, d),
                               lambda i: (layer, _mod_row(i, tm, n_ctx, dec_seq), 0, 0))],
        out_specs=pl.BlockSpec((tm, d), lambda i: (i, 0)),
        out_shape=jax.ShapeDtypeStruct((t, d), BF16),
        compiler_params=_cparams(1), name="norm_mod",
    )(y, g.reshape(1, d), mod)


def _route_rows(h, row0, r_ref, rb_ref, h_ref, ids_ref, gates_ref, n_slab):
    nr = h.shape[0]
    rows = pl.ds(row0, nr)
    half = n_slab * LANES
    bits = lax.bitcast_convert_type(h.astype(jnp.bfloat16).astype(F32), jnp.uint32)
    word = bits[:, half:] | (bits[:, :half] >> 16)
    for s in range(n_slab):
        h_ref[pl.ds(row0 * n_slab + s, nr, stride=n_slab), :] = word[:, s * LANES:(s + 1) * LANES]
    h_hi = h.astype(BF16)
    h_lo = (h - h_hi.astype(F32)).astype(BF16)
    p_hi = jnp.dot(h_hi, r_ref[...], preferred_element_type=F32)
    p_lo = jnp.dot(h_lo, r_ref[:, :LANES], preferred_element_type=F32)
    logits = ((p_hi[:, :LANES] + p_hi[:, LANES:]) + p_lo) + rb_ref[...]
    lane = lax.broadcasted_iota(jnp.int32, logits.shape, 1).astype(F32)
    big = float(LANES)
    m1 = jnp.max(logits, axis=-1, keepdims=True)
    i1 = jnp.min(jnp.where(logits == m1, lane, big), axis=-1, keepdims=True)
    l2 = jnp.where(lane == i1, 2.0 * NEG_BIG, logits)
    m2 = jnp.max(l2, axis=-1, keepdims=True)
    i2 = jnp.min(jnp.where(l2 == m2, lane, big), axis=-1, keepdims=True)
    e = jnp.exp(m2 - m1)
    g1 = 1.0 / (1.0 + e)
    g2 = e / (1.0 + e)
    ids_ref[rows, :] = jnp.where(lane == 0.0, i1, jnp.where(lane == 1.0, i2, 0.0)).astype(jnp.int32)
    gates_ref[rows, :] = jnp.where(lane == 0.0, g1, jnp.where(lane == 1.0, g2, 0.0))


def _norm_router_kernel(y_ref, g_ref, mod_ref, r_ref, rb_ref, h_ref, ids_ref, gates_ref, *,
                        shift_row, n_slab):
    h = _norm_mod_value(y_ref[...], g_ref[...], mod_ref, shift_row)
    _route_rows(h, 0, r_ref, rb_ref, h_ref, ids_ref, gates_ref, n_slab)


def _router_operands(router, router_b, d):
    n_exp = router.shape[1]
    rp = jnp.zeros((d, LANES), F32).at[:, :n_exp].set(router)
    r_hi = rp.astype(BF16)
    r_lo = (rp - r_hi.astype(F32)).astype(BF16)
    rbp = jnp.full((1, LANES), NEG_BIG, F32).at[0, :n_exp].set(router_b)
    return jnp.concatenate([r_hi, r_lo], axis=1), rbp


def _norm_router(y, g, mod, layer, shift_row, router, router_b, n_ctx, dec_seq, tm):
    t, d = y.shape
    n_slab = d // (2 * LANES)
    rp, rbp = _router_operands(router, router_b, d)
    return pl.pallas_call(
        functools.partial(_norm_router_kernel, shift_row=shift_row, n_slab=n_slab),
        grid=(t // tm,),
        in_specs=[pl.BlockSpec((tm, d), lambda i: (i, 0)),
                  pl.BlockSpec((1, d), lambda i: (0, 0)),
                  pl.BlockSpec((None, None, 6, d),
                               lambda i: (layer, _mod_row(i, tm, n_ctx, dec_seq), 0, 0)),
                  pl.BlockSpec((d, 2 * LANES), lambda i: (0, 0)),
                  pl.BlockSpec((1, LANES), lambda i: (0, 0))],
        out_specs=[pl.BlockSpec((tm * n_slab, LANES), lambda i: (i, 0)),
                   pl.BlockSpec((tm, LANES), lambda i: (i, 0)),
                   pl.BlockSpec((tm, LANES), lambda i: (i, 0))],
        out_shape=[jax.ShapeDtypeStruct((t * n_slab, LANES), jnp.uint32),
                   jax.ShapeDtypeStruct((t, LANES), jnp.int32),
                   jax.ShapeDtypeStruct((t, LANES), F32)],
        compiler_params=_cparams(1), name="norm_router",
    )(y, g.reshape(1, d), mod, rp, rbp)


def _final_norm_kernel(y_ref, g_ref, op_ref, os_ref, *, n_ctx_tiles):
    i = pl.program_id(0)
    y = y_ref[...]
    ms = jnp.mean(y * y, axis=-1, keepdims=True)
    n = (y * lax.rsqrt(ms + EPS)) * g_ref[...]

    @pl.when(i < n_ctx_tiles)
    def _():
        op_ref[...] = n

    @pl.when(i >= n_ctx_tiles)
    def _():
        os_ref[...] = n


def _final_norm(y, g, n_ctx, tm):
    t, d = y.shape
    nct = n_ctx // tm
    return pl.pallas_call(
        functools.partial(_final_norm_kernel, n_ctx_tiles=nct),
        grid=(t // tm,),
        in_specs=[pl.BlockSpec((tm, d), lambda i: (i, 0)),
                  pl.BlockSpec((1, d), lambda i: (0, 0))],
        out_specs=[pl.BlockSpec((tm, d), lambda i: (jnp.minimum(i, nct - 1), 0)),
                   pl.BlockSpec((tm, d), lambda i: (jnp.maximum(i - nct, 0), 0))],
        out_shape=[jax.ShapeDtypeStruct((n_ctx, d), F32),
                   jax.ShapeDtypeStruct((t - n_ctx, d), F32)],
        compiler_params=_cparams(1), name="final_norm",
    )(y, g.reshape(1, d))


def _cast_at_first_row_tile(w_refs, wb_refs):
    @pl.when(pl.program_id(1) == 0)
    def _():
        for w, wb in zip(w_refs, wb_refs):
            wb[...] = w[...].astype(BF16)


def _mm_act_kernel(x_ref, w_ref, o_ref, wb, *, act):
    _cast_at_first_row_tile((w_ref,), (wb,))
    for rows in _row_passes(x_ref.shape[0]):
        acc = jnp.dot(x_ref[rows, :], wb[...], preferred_element_type=F32)
        if act == "gelu":
            acc = _gelu_tanh(acc)
        o_ref[rows, :] = acc.astype(o_ref.dtype)


def _mm_act(x, w, layer, act, out_dtype, tm, tn_pref=512):
    t, k = x.shape
    n = w.shape[2]
    tn = _pick_tile(n, tn_pref)
    return pl.pallas_call(
        functools.partial(_mm_act_kernel, act=act),
        grid=(n // tn, t // tm),
        in_specs=[pl.BlockSpec((tm, k), lambda j, i: (i, 0)),
                  pl.BlockSpec((None, k, tn), lambda j, i: (layer, 0, j))],
        out_specs=pl.BlockSpec((tm, tn), lambda j, i: (i, j)),
        out_shape=jax.ShapeDtypeStruct((t, n), out_dtype),
        scratch_shapes=[pltpu.VMEM((k, tn), BF16)],
        compiler_params=_cparams(2), name="mm_" + str(act),
    )(x, w)


def _swiglu_passes(x_ref, wb1, wb3, o_ref):
    for rows in _row_passes(x_ref.shape[0]):
        x = x_ref[rows, :]
        a = jnp.dot(x, wb1[...], preferred_element_type=F32)
        b = jnp.dot(x, wb3[...], preferred_element_type=F32)
        o_ref[rows, :] = ((a * _sigmoid(a)) * b).astype(o_ref.dtype)


def _mm_swiglu_kernel(x_ref, w1_ref, w3_ref, o_ref, wb1, wb3):
    _cast_at_first_row_tile((w1_ref, w3_ref), (wb1, wb3))
    _swiglu_passes(x_ref, wb1, wb3, o_ref)


def _mm_swiglu(x, w1, w3, layer, tm, tn_pref=512):
    t, k = x.shape
    n = w1.shape[2]
    tn = _pick_tile(n, tn_pref)
    return pl.pallas_call(
        _mm_swiglu_kernel,
        grid=(n // tn, t // tm),
        in_specs=[pl.BlockSpec((tm, k), lambda j, i: (i, 0)),
                  pl.BlockSpec((None, k, tn), lambda j, i: (layer, 0, j)),
                  pl.BlockSpec((None, k, tn), lambda j, i: (layer, 0, j))],
        out_specs=pl.BlockSpec((tm, tn), lambda j, i: (i, j)),
        out_shape=jax.ShapeDtypeStruct((t, n), BF16),
        scratch_shapes=[pltpu.VMEM((k, tn), BF16), pltpu.VMEM((k, tn), BF16)],
        compiler_params=_cparams(2), name="mm_swiglu",
    )(x, w1, w3)


def _mm_sconv_kernel(x_ref, wb_ref, wc_ref, wx_ref, cw_ref, o_ref, sb, sc, sx, *, n_ctx_tiles, seq_ctx):
    _cast_at_first_row_tile((wb_ref, wc_ref, wx_ref), (sb, sc, sx))
    i = pl.program_id(1)
    is_ctx = i < n_ctx_tiles
    x = x_ref[...]
    bg = jnp.dot(x, sb[...], preferred_element_type=F32)
    p = jnp.dot(x, sc[...], preferred_element_type=F32) * jnp.dot(x, sx[...], preferred_element_type=F32)
    tm = p.shape[0]
    row = lax.broadcasted_iota(jnp.int32, p.shape, 0)
    pos = jnp.where(is_ctx, row % seq_ctx, row)
    last_pos = jnp.where(is_ctx, seq_ctx - 1, tm - 1)
    p_prev = jnp.where(pos == 0, 0.0, pltpu.roll(p, 1, axis=0))
    p_next = jnp.where(pos == last_pos, 0.0, pltpu.roll(p, tm - 1, axis=0))
    cw = cw_ref[...]
    conv = (cw[0:1] * p_prev + cw[1:2] * p) + cw[2:3] * p_next
    o_ref[...] = (bg * conv).astype(o_ref.dtype)


def _mm_sconv(x, w_in, conv_w, layer, n_ctx, seq_ctx, tm, tn_pref=512):
    t, k = x.shape
    d = conv_w.shape[2]
    tn = _pick_tile(d, tn_pref)
    nj = d // tn
    return pl.pallas_call(
        functools.partial(_mm_sconv_kernel, n_ctx_tiles=n_ctx // tm, seq_ctx=seq_ctx),
        grid=(nj, t // tm),
        in_specs=[pl.BlockSpec((tm, k), lambda j, i: (i, 0)),
                  pl.BlockSpec((None, k, tn), lambda j, i: (layer, 0, j)),
                  pl.BlockSpec((None, k, tn), lambda j, i: (layer, 0, nj + j)),
                  pl.BlockSpec((None, k, tn), lambda j, i: (layer, 0, 2 * nj + j)),
                  pl.BlockSpec((None, conv_w.shape[1], tn), lambda j, i: (layer, 0, j))],
        out_specs=pl.BlockSpec((tm, tn), lambda j, i: (i, j)),
        out_shape=jax.ShapeDtypeStruct((t, d), BF16),
        scratch_shapes=[pltpu.VMEM((k, tn), BF16)] * 3,
        compiler_params=_cparams(2), name="mm_sconv",
    )(x, w_in, w_in, w_in, conv_w)


def _mm_residual_kernel(x_ref, w_ref, y_ref, mod_ref, o_ref, wb, *, gate_row):
    _cast_at_first_row_tile((w_ref,), (wb,))
    gate = mod_ref[gate_row:gate_row + 1, :]
    for rows in _row_passes(x_ref.shape[0]):
        acc = jnp.dot(x_ref[rows, :], wb[...], preferred_element_type=F32)
        o_ref[rows, :] = y_ref[rows, :] + gate * acc


def _mm_residual(x, w, w_layer, y, mod, layer, gate_row, n_ctx, dec_seq, tm, tn_pref=512, w_buffers=2):
    t, k = x.shape
    n = w.shape[2]
    tn = _pick_tile(n, tn_pref)
    return pl.pallas_call(
        functools.partial(_mm_residual_kernel, gate_row=gate_row),
        grid=(n // tn, t // tm),
        in_specs=[pl.BlockSpec((tm, k), lambda j, i: (i, 0)),
                  pl.BlockSpec((None, k, tn), lambda j, i: (w_layer, 0, j),
                               pipeline_mode=pl.Buffered(w_buffers)),
                  pl.BlockSpec((tm, tn), lambda j, i: (i, j)),
                  pl.BlockSpec((None, None, 6, tn),
                               lambda j, i: (layer, _mod_row(i, tm, n_ctx, dec_seq), 0, j))],
        out_specs=pl.BlockSpec((tm, tn), lambda j, i: (i, j)),
        out_shape=jax.ShapeDtypeStruct((t, n), F32),
        scratch_shapes=[pltpu.VMEM((k, tn), BF16)],
        compiler_params=_cparams(2), name="mm_residual",
    )(x, w, y, mod)


def _residual_norm_kernel(x_ref, w_hbm, y_ref, mod_ref, g_ref, *rest,
                          w_layer, gate_row, shift_row, chunk, route_slabs):
    if route_slabs:
        r_ref, rb_ref, yo_ref, ho_ref, ids_ref, gates_ref, wb, stage, sem = rest
    else:
        yo_ref, ho_ref, wb, stage, sem = rest
    k = wb.shape[0]
    nchunk = k // chunk

    @pl.when(pl.program_id(0) == 0)
    def _():
        def copy(c):
            return pltpu.make_async_copy(w_hbm.at[w_layer, pl.ds(c * chunk, chunk), :],
                                         stage.at[c % 2], sem.at[c % 2])

        copy(0).start()
        for c in range(nchunk):
            if c + 1 < nchunk:
                copy(c + 1).start()
            copy(c).wait()
            wb[pl.ds(c * chunk, chunk), :] = stage[c % 2].astype(BF16)

    gate = mod_ref[gate_row:gate_row + 1, :]
    g = g_ref[...]
    for rows in _row_passes(x_ref.shape[0]):
        acc = jnp.dot(x_ref[rows, :], wb[...], preferred_element_type=F32)
        y = y_ref[rows, :] + gate * acc
        yo_ref[rows, :] = y
        h = _norm_mod_value(y, g, mod_ref, shift_row)
        if route_slabs:
            _route_rows(h, rows.start, r_ref, rb_ref, ho_ref, ids_ref, gates_ref, route_slabs)
        else:
            ho_ref[rows, :] = h.astype(ho_ref.dtype)


def _residual_norm(x, w, w_layer, y, mod, layer, gate_row, next_g, shift_row, n_ctx, dec_seq, tm,
                   router=None, router_b=None):
    t, k = x.shape
    d = w.shape[2]
    chunk = _pick_tile(k, 512)
    in_specs = [pl.BlockSpec((tm, k), lambda i: (i, 0)),
                pl.BlockSpec(memory_space=pl.ANY),
                pl.BlockSpec((tm, d), lambda i: (i, 0)),
                pl.BlockSpec((None, None, 6, d),
                             lambda i: (layer, _mod_row(i, tm, n_ctx, dec_seq), 0, 0)),
                pl.BlockSpec((1, d), lambda i: (0, 0))]
    args = [x, w, y, mod, next_g.reshape(1, d)]
    if router is None:
        route_slabs = 0
        out_specs = [pl.BlockSpec((tm, d), lambda i: (i, 0)),
                     pl.BlockSpec((tm, d), lambda i: (i, 0))]
        out_shape = [jax.ShapeDtypeStruct((t, d), F32), jax.ShapeDtypeStruct((t, d), BF16)]
    else:
        route_slabs = d // (2 * LANES)
        in_specs += [pl.BlockSpec((d, 2 * LANES), lambda i: (0, 0)),
                     pl.BlockSpec((1, LANES), lambda i: (0, 0))]
        args += list(_router_operands(router, router_b, d))
        out_specs = [pl.BlockSpec((tm, d), lambda i: (i, 0)),
                     pl.BlockSpec((tm * route_slabs, LANES), lambda i: (i, 0)),
                     pl.BlockSpec((tm, LANES), lambda i: (i, 0)),
                     pl.BlockSpec((tm, LANES), lambda i: (i, 0))]
        out_shape = [jax.ShapeDtypeStruct((t, d), F32),
                     jax.ShapeDtypeStruct((t * route_slabs, LANES), jnp.uint32),
                     jax.ShapeDtypeStruct((t, LANES), jnp.int32),
                     jax.ShapeDtypeStruct((t, LANES), F32)]
    return pl.pallas_call(
        functools.partial(_residual_norm_kernel, w_layer=w_layer, gate_row=gate_row,
                          shift_row=shift_row, chunk=chunk, route_slabs=route_slabs),
        grid=(t // tm,),
        in_specs=in_specs,
        out_specs=out_specs,
        out_shape=out_shape,
        scratch_shapes=[pltpu.VMEM((k, d), BF16), pltpu.VMEM((2, chunk, d), F32),
                        pltpu.SemaphoreType.DMA((2,))],
        compiler_params=_cparams(1), name="residual_route" if router is not None else "residual_norm",
    )(*args)


def _rg_core_kernel(gate_ref, x_ref, cw_ref, cb_ref, wg_ref, bg_ref, lam_ref, h0_ref,
                    y_ref, st_ref, af_s, bf_s, ab_s, bb_s, *, n_ctx_tiles, seg, tm, cg):
    i = pl.program_id(1)
    is_ctx = i < n_ctx_tiles
    nseg = tm // seg
    nslab = cg // LANES
    lsub = tm // SUBLANES
    sub_per_seq = seg // lsub
    cw = cw_ref[...]
    cb = cb_ref[...]
    lam = lam_ref[...]
    softplus_neg_lam = jnp.maximum(-lam, 0.0) + jnp.log1p(jnp.exp(-jnp.abs(lam)))
    neg_c_sp = (-RG_C) * softplus_neg_lam
    zeros8 = jnp.zeros((SUBLANES, cg), F32)

    cr = min(RG_CHUNK_ROWS, lsub)
    for c0 in range(0, tm, cr):
        cur = x_ref[pl.ds(c0, cr), :]
        if c0 == 0:
            prev8 = zeros8
        else:
            prev8 = x_ref[pl.ds(c0 - SUBLANES, SUBLANES), :]
            if c0 % seg == 0:
                prev8 = jnp.where(is_ctx, 0.0, prev8)
        if c0 + cr == tm:
            next8 = zeros8
        else:
            next8 = x_ref[pl.ds(c0 + cr, SUBLANES), :]
            if (c0 + cr) % seg == 0:
                next8 = jnp.where(is_ctx, 0.0, next8)
        head = jnp.concatenate([prev8, cur[:2 * SUBLANES]], axis=0)
        tail = jnp.concatenate([cur[cr - 2 * SUBLANES:], next8], axis=0)

        def shifted(delta):
            lo = SUBLANES + delta
            mid = x_ref[pl.ds(c0 + lo, cr - 2 * SUBLANES), :]
            return jnp.concatenate([head[lo:lo + SUBLANES], mid, tail[lo:lo + SUBLANES]], axis=0)

        xc = (((cw[0:1] * shifted(-2) + cw[1:2] * shifted(-1)) + cw[2:3] * cur)
              + cw[3:4] * shifted(1)) + cb
        g = jnp.dot(xc.astype(BF16), wg_ref[...], preferred_element_type=F32) + bg_ref[...]
        for d, (a_s, b_s) in enumerate(((af_s, bf_s), (ab_s, bb_s))):
            r = _sigmoid(g[:, (2 * d) * cg:(2 * d + 1) * cg])
            ig = _sigmoid(g[:, (2 * d + 1) * cg:(2 * d + 2) * cg])
            log_a = neg_c_sp[d:d + 1] * r
            a = jnp.exp(log_a)
            one_minus_a2 = -jnp.tanh(log_a) * (a * a + 1.0)
            root = jnp.where(one_minus_a2 > 0.0, one_minus_a2 * lax.rsqrt(one_minus_a2), 0.0)
            bt = root * (ig * xc)
            k, j0 = c0 // lsub, c0 % lsub
            dst = pl.ds(k + SUBLANES * j0, cr, stride=SUBLANES)
            for l in range(nslab):
                a_s[l, dst, :] = a[:, l * LANES:(l + 1) * LANES]
                b_s[l, dst, :] = bt[:, l * LANES:(l + 1) * LANES]

    def local_scan(j, carry):
        hf, pf, hb, pb = carry
        rf = pl.multiple_of(j * SUBLANES, SUBLANES)
        rb = pl.multiple_of((lsub - 1 - j) * SUBLANES, SUBLANES)
        nhf, npf, nhb, npb = [], [], [], []
        for l in range(nslab):
            a = af_s[l, pl.ds(rf, SUBLANES), :]
            h = a * hf[l] + bf_s[l, pl.ds(rf, SUBLANES), :]
            p = a * pf[l]
            bf_s[l, pl.ds(rf, SUBLANES), :] = h
            af_s[l, pl.ds(rf, SUBLANES), :] = p
            nhf.append(h)
            npf.append(p)
            a = ab_s[l, pl.ds(rb, SUBLANES), :]
            h = a * hb[l] + bb_s[l, pl.ds(rb, SUBLANES), :]
            p = a * pb[l]
            bb_s[l, pl.ds(rb, SUBLANES), :] = h
            ab_s[l, pl.ds(rb, SUBLANES), :] = p
            nhb.append(h)
            npb.append(p)
        return tuple(nhf), tuple(npf), tuple(nhb), tuple(npb)

    zero = tuple(jnp.zeros((SUBLANES, LANES), F32) for _ in range(nslab))
    one = tuple(jnp.ones((SUBLANES, LANES), F32) for _ in range(nslab))
    hf_end, pf_end, hb_end, pb_end = lax.fori_loop(0, lsub, local_scan, (zero, one, zero, one))

    row8 = lax.broadcasted_iota(jnp.int32, (SUBLANES, LANES), 0)
    h0 = h0_ref[...]
    for l in range(nslab):
        lanes = slice(l * LANES, (l + 1) * LANES)
        ent_f = jnp.zeros((SUBLANES, LANES), F32)
        h_in = h0[0:1, lanes]
        for k in range(SUBLANES):
            if k > 0 and k % sub_per_seq == 0:
                h_in = jnp.where(is_ctx, 0.0, h_in)
            ent_f = jnp.where(row8 == k, h_in, ent_f)
            h_in = hf_end[l][k:k + 1] + pf_end[l][k:k + 1] * h_in
            if (k + 1) % sub_per_seq == 0:
                q = k // sub_per_seq
                st_ref[q:q + 1, lanes] = h_in
        ent_b = jnp.zeros((SUBLANES, LANES), F32)
        h_in = h0[1:2, lanes]
        for k in reversed(range(SUBLANES)):
            if k < SUBLANES - 1 and (k + 1) % sub_per_seq == 0:
                h_in = jnp.where(is_ctx, 0.0, h_in)
            ent_b = jnp.where(row8 == k, h_in, ent_b)
            h_in = hb_end[l][k:k + 1] + pb_end[l][k:k + 1] * h_in
            if k % sub_per_seq == 0:
                q = k // sub_per_seq
                st_ref[nseg + q:nseg + q + 1, lanes] = h_in
        ch = min(tm, 256)
        ef = jnp.concatenate([ent_f] * (ch // SUBLANES), axis=0)
        eb = jnp.concatenate([ent_b] * (ch // SUBLANES), axis=0)
        for r0 in range(0, tm, ch):
            rows = pl.ds(r0, ch)
            bf_s[l, rows, :] = ((bf_s[l, rows, :] + af_s[l, rows, :] * ef)
                                + (bb_s[l, rows, :] + ab_s[l, rows, :] * eb))

    for k in range(SUBLANES):
        rows = pl.ds(k * lsub, lsub)
        hsum = jnp.concatenate([bf_s[l, pl.ds(k, lsub, stride=SUBLANES), :] for l in range(nslab)], axis=1)
        y_ref[rows, :] = (hsum * _gelu_tanh(gate_ref[rows, :])).astype(y_ref.dtype)


def _rg_core(z, conv_w, conv_b, w_a, b_a, w_x, b_x, lam, h0, n_ctx, seq_ctx, tm):
    t = z.shape[0]
    r = conv_w.shape[1]
    heads, hw = w_a.shape[1], w_a.shape[2]
    hpg = 4
    while (hpg * hw) % LANES:
        hpg *= 2
    ng = heads // hpg
    cg = hpg * hw
    nseg = tm // seq_ctx
    ntiles = t // tm
    assert tm % SUBLANES == 0 and seq_ctx % (tm // SUBLANES) == 0

    def blockdiag(w):
        rows = jnp.tile(w.reshape(2, ng, cg, hw), (1, 1, 1, hpg))
        head = jnp.arange(cg, dtype=jnp.int32) // hw
        return jnp.where(head[:, None] == head[None, :], rows, 0.0)

    wa, wx = blockdiag(w_a), blockdiag(w_x)
    wg = jnp.concatenate([wa[0], wx[0], wa[1], wx[1]], axis=-1).astype(BF16)
    ba, bx = b_a.reshape(2, ng, 1, cg), b_x.reshape(2, ng, 1, cg)
    bg = jnp.concatenate([ba[0], bx[0], ba[1], bx[1]], axis=-1)

    return pl.pallas_call(
        functools.partial(_rg_core_kernel, n_ctx_tiles=n_ctx // tm, seg=seq_ctx, tm=tm, cg=cg),
        grid=(ng, ntiles),
        in_specs=[pl.BlockSpec((tm, cg), lambda g, i: (i, g)),
                  pl.BlockSpec((tm, cg), lambda g, i: (i, ng + g)),
                  pl.BlockSpec((conv_w.shape[0], cg), lambda g, i: (0, g)),
                  pl.BlockSpec((1, cg), lambda g, i: (0, g)),
                  pl.BlockSpec((None, cg, 4 * cg), lambda g, i: (g, 0, 0)),
                  pl.BlockSpec((None, 1, 4 * cg), lambda g, i: (g, 0, 0)),
                  pl.BlockSpec((2, cg), lambda g, i: (0, g)),
                  pl.BlockSpec((None, 2, cg), lambda g, i: (i, 0, g))],
        out_specs=[pl.BlockSpec((tm, cg), lambda g, i: (i, g)),
                   pl.BlockSpec((None, 2 * nseg, cg), lambda g, i: (i, 0, g))],
        out_shape=[jax.ShapeDtypeStruct((t, r), BF16),
                   jax.ShapeDtypeStruct((ntiles, 2 * nseg, r), F32)],
        scratch_shapes=[pltpu.VMEM((cg // LANES, tm, LANES), F32)] * 4,
        compiler_params=_cparams(2), name="rg_core",
    )(z, z, conv_w, conv_b.reshape(1, r), wg, bg, lam, h0)


def _sgu_core_kernel(u_ref, v_ref, g_ref, ws_ref, bs_ref, y_ref, vn_s, *, chunk, gw, ngroups, tm):
    v = v_ref[...].astype(F32)
    ms = jnp.mean(v * v, axis=-1, keepdims=True)
    vn_s[...] = ((v * lax.rsqrt(ms + EPS)) * g_ref[...]).astype(BF16)
    for c in range(tm // chunk):
        rows = pl.ds(c * chunk, chunk)
        for g in range(ngroups):
            cols = pl.ds(g * gw, gw)
            vm = jnp.dot(ws_ref[g], vn_s[rows, cols], preferred_element_type=F32) + bs_ref[:, g:g + 1]
            y_ref[rows, cols] = (u_ref[rows, cols].astype(F32) * vm).astype(y_ref.dtype)


def _sgu_core(z, norm_g, w_s, b_s, tm):
    t = z.shape[0]
    w = norm_g.shape[0]
    ngroups, chunk = w_s.shape[0], w_s.shape[1]
    gw = w // ngroups
    return pl.pallas_call(
        functools.partial(_sgu_core_kernel, chunk=chunk, gw=gw, ngroups=ngroups, tm=tm),
        grid=(t // tm,),
        in_specs=[pl.BlockSpec((tm, w), lambda i: (i, 0)),
                  pl.BlockSpec((tm, w), lambda i: (i, 1)),
                  pl.BlockSpec((1, w), lambda i: (0, 0)),
                  pl.BlockSpec((ngroups, chunk, chunk), lambda i: (0, 0, 0)),
                  pl.BlockSpec((chunk, ngroups), lambda i: (0, 0))],
        out_specs=pl.BlockSpec((tm, w), lambda i: (i, 0)),
        out_shape=jax.ShapeDtypeStruct((t, w), BF16),
        scratch_shapes=[pltpu.VMEM((tm, w), BF16)],
        compiler_params=_cparams(1), name="sgu_core",
    )(z, z, norm_g.reshape(1, w), w_s.astype(BF16), b_s.T)


def _moe_plan(ids, n_exp, bm):
    t = ids.shape[0]
    e = ids[:, :TOP_K].reshape(-1)
    oh = (e[:, None] == jnp.arange(n_exp, dtype=jnp.int32)[None, :]).astype(jnp.int32)
    csum = jnp.cumsum(oh, axis=0)
    rank = jnp.sum((csum - oh) * oh, axis=1)
    counts = csum[-1]
    padded = ((counts + bm - 1) // bm) * bm
    ends = jnp.cumsum(padded)
    starts = ends - padded
    first_cnt = counts - jnp.maximum(padded - bm, 0)
    start_e = jnp.sum(starts[None, :] * oh, axis=1)
    first_e = jnp.sum(first_cnt[None, :] * oh, axis=1)
    dest = (start_e + rank + jnp.where(rank >= first_e, bm - first_e, 0)).astype(jnp.int32)
    p = TOP_K * t + n_exp * bm
    src = jnp.zeros((p,), jnp.int32).at[dest].set(jnp.arange(TOP_K * t, dtype=jnp.int32) // TOP_K)

    def block_table(blk):
        idx = jnp.arange(p // blk, dtype=jnp.int32)
        start = idx * blk
        exp = jnp.minimum(jnp.sum((start[:, None] >= ends[None, :]).astype(jnp.int32), axis=1), n_exp - 1)
        offset = start - starts[exp]
        valid = jnp.where(offset < bm, jnp.clip(first_cnt[exp] - offset, 0, blk), blk)
        valid = jnp.where(start < ends[-1], valid, 0)
        fetch = lax.cummax(jnp.where(valid > 0, idx, 0), axis=0)
        return exp.astype(jnp.int32), valid.astype(jnp.int32), fetch.astype(jnp.int32)

    return dest, src, block_table


def _gather_rows_kernel(src_ref, valid_ref, h_hbm, o_ref, buf, sem, *, rows, n_slab):
    b = pl.program_id(0)
    nb = pl.num_programs(0)
    slot = b % DMA_RING
    ahead = DMA_RING - 1

    def block_has_rows(blk):
        return jnp.logical_and(blk < nb, valid_ref[jnp.minimum(blk, nb - 1)] > 0)

    has_rows = block_has_rows(b)

    def issue(blk, to_slot):
        base = blk * rows

        def body(q, carry):
            for prio in range(2):
                r = 2 * q + prio
                tok = pl.multiple_of(src_ref[base + r] * n_slab, n_slab)
                pltpu.make_async_copy(h_hbm.at[pl.ds(tok, n_slab), :],
                                      buf.at[to_slot, pl.ds(pl.multiple_of(r * n_slab, n_slab), n_slab), :],
                                      sem.at[to_slot]).start(priority=prio)
            return carry

        lax.fori_loop(0, rows // 2, body, 0, unroll=4)

    @pl.when(b == 0)
    def _():
        for first in range(ahead):
            @pl.when(block_has_rows(first))
            def _():
                issue(first, first % DMA_RING)

    @pl.when(block_has_rows(b + ahead))
    def _():
        issue(b + ahead, (b + ahead) % DMA_RING)

    @pl.when(has_rows)
    def _():
        pltpu.make_async_copy(h_hbm.at[pl.ds(0, rows * n_slab), :], buf.at[slot], sem.at[slot]).wait()
        half = n_slab * LANES
        for s in range(n_slab):
            word = buf[slot, pl.ds(s, rows, stride=n_slab), :]
            lo = lax.bitcast_convert_type(word << 16, F32)
            hi = lax.bitcast_convert_type(word & jnp.uint32(0xFFFF0000), F32)
            o_ref[:, s * LANES:(s + 1) * LANES] = lo.astype(o_ref.dtype)
            o_ref[:, half + s * LANES:half + (s + 1) * LANES] = hi.astype(o_ref.dtype)

    @pl.when(jnp.logical_not(has_rows))
    def _():
        o_ref[...] = jnp.zeros_like(o_ref)


def _gather_rows(h_tok, src, valid, bm, d):
    n_slab = d // (2 * LANES)
    p = src.shape[0]
    return pl.pallas_call(
        functools.partial(_gather_rows_kernel, rows=bm, n_slab=n_slab),
        grid_spec=pltpu.PrefetchScalarGridSpec(
            num_scalar_prefetch=2,
            grid=(p // bm,),
            in_specs=[pl.BlockSpec(memory_space=pl.ANY)],
            out_specs=pl.BlockSpec((bm, d), lambda b, src, va: (b, 0)),
            scratch_shapes=[pltpu.VMEM((DMA_RING, bm * n_slab, LANES), jnp.uint32),
                            pltpu.SemaphoreType.DMA((DMA_RING,))]),
        out_shape=jax.ShapeDtypeStruct((p, d), BF16),
        compiler_params=_cparams(1), name="moe_gather",
    )(src, valid, h_tok)


def _expert_changed(exp_ref, fetch_ref, b):
    prev = fetch_ref[jnp.maximum(b - 1, 0)]
    return jnp.logical_or(b == 0, exp_ref[b] != exp_ref[prev])


def _guarded_passes(valid, x_ref, o_ref, one_pass):
    bm = x_ref.shape[0]
    passes = _row_passes(bm)
    pass_rows = bm // len(passes)
    needed = (valid + (pass_rows - 1)) // pass_rows

    for count in range(len(passes) + 1):
        @pl.when(needed == count)
        def _():
            for rows in passes[:count]:
                one_pass(rows)
            for rows in passes[count:]:
                o_ref[rows, :] = jnp.zeros((pass_rows, o_ref.shape[1]), o_ref.dtype)


def _grouped_swiglu_kernel(exp_ref, valid_ref, fetch_ref, x_ref, w1_ref, w3_ref, o_ref, wb1, wb3):
    b = pl.program_id(1)
    valid = valid_ref[b]

    @pl.when(jnp.logical_and(valid > 0, _expert_changed(exp_ref, fetch_ref, b)))
    def _():
        wb1[...] = w1_ref[...].astype(BF16)
        wb3[...] = w3_ref[...].astype(BF16)

    def one_pass(rows):
        x = x_ref[rows, :]
        a = jnp.dot(x, wb1[...], preferred_element_type=F32)
        c = jnp.dot(x, wb3[...], preferred_element_type=F32)
        o_ref[rows, :] = ((a * _sigmoid(a)) * c).astype(o_ref.dtype)

    _guarded_passes(valid, x_ref, o_ref, one_pass)


def _grouped_swiglu(xs, w1, w3, layer, table, bm, tn_pref=512):
    p, k = xs.shape
    n = w1.shape[3]
    tn = _pick_tile(n, tn_pref)
    return pl.pallas_call(
        _grouped_swiglu_kernel,
        grid_spec=pltpu.PrefetchScalarGridSpec(
            num_scalar_prefetch=3,
            grid=(n // tn, p // bm),
            in_specs=[pl.BlockSpec((bm, k), lambda j, b, ex, va, fe: (fe[b], 0)),
                      pl.BlockSpec((None, None, k, tn), lambda j, b, ex, va, fe: (layer, ex[fe[b]], 0, j)),
                      pl.BlockSpec((None, None, k, tn), lambda j, b, ex, va, fe: (layer, ex[fe[b]], 0, j))],
            out_specs=pl.BlockSpec((bm, tn), lambda j, b, ex, va, fe: (b, j)),
            scratch_shapes=[pltpu.VMEM((k, tn), BF16), pltpu.VMEM((k, tn), BF16)]),
        out_shape=jax.ShapeDtypeStruct((p, n), BF16),
        compiler_params=_cparams(2), name="moe_swiglu",
    )(*table, xs, w1, w3)


def _grouped_down_kernel(exp_ref, valid_ref, fetch_ref, x_ref, w_ref, o_ref, wb):
    b = pl.program_id(1)
    valid = valid_ref[b]

    @pl.when(jnp.logical_and(valid > 0, _expert_changed(exp_ref, fetch_ref, b)))
    def _():
        wb[...] = w_ref[...].astype(BF16)

    def one_pass(rows):
        o_ref[rows, :] = jnp.dot(x_ref[rows, :], wb[...], preferred_element_type=F32)

    _guarded_passes(valid, x_ref, o_ref, one_pass)


def _grouped_down(gs, w2, layer, table, bm, tn_pref=512):
    p, k = gs.shape
    n = w2.shape[3]
    tn = _pick_tile(n, tn_pref)
    return pl.pallas_call(
        _grouped_down_kernel,
        grid_spec=pltpu.PrefetchScalarGridSpec(
            num_scalar_prefetch=3,
            grid=(n // tn, p // bm),
            in_specs=[pl.BlockSpec((bm, k), lambda j, b, ex, va, fe: (fe[b], 0)),
                      pl.BlockSpec((None, None, k, tn), lambda j, b, ex, va, fe: (layer, ex[fe[b]], 0, j))],
            out_specs=pl.BlockSpec((bm, tn), lambda j, b, ex, va, fe: (b, j)),
            scratch_shapes=[pltpu.VMEM((k, tn), BF16)]),
        out_shape=jax.ShapeDtypeStruct((p, n), F32),
        compiler_params=_cparams(2), name="moe_down",
    )(*table, gs, w2)


def _combine_kernel(dest_ref, os_hbm, y_ref, gates_ref, mod_ref, g_ref, *rest,
                    rows, gate_row, n_ctx_tiles, final):
    if final:
        out_a, out_b, buf, sem = rest
    else:
        nmod_ref, out_a, out_b, buf, sem = rest
    i = pl.program_id(0)
    n_tiles = pl.num_programs(0)
    slot = i % DMA_RING
    ahead = DMA_RING - 1

    def issue(tile, to_slot):
        base = tile * rows

        def body(r, carry):
            for k in range(TOP_K):
                row = dest_ref[TOP_K * (base + r) + k]
                pltpu.make_async_copy(os_hbm.at[pl.ds(row, 1), :], buf.at[to_slot, k, pl.ds(r, 1), :],
                                      sem.at[to_slot]).start(priority=k)
            return carry

        lax.fori_loop(0, rows, body, 0, unroll=4)

    @pl.when(i == 0)
    def _():
        for first in range(ahead):
            @pl.when(first < n_tiles)
            def _():
                issue(first, first % DMA_RING)

    @pl.when(i + ahead < n_tiles)
    def _():
        issue(i + ahead, (i + ahead) % DMA_RING)

    for k in range(TOP_K):
        pltpu.make_async_copy(os_hbm.at[pl.ds(0, rows), :], buf.at[slot, k], sem.at[slot]).wait()
    gates = gates_ref[...]
    f = gates[:, 0:1] * buf[slot, 0] + gates[:, 1:2] * buf[slot, 1]
    y = y_ref[...] + mod_ref[gate_row:gate_row + 1, :] * f
    if final:
        ms = jnp.mean(y * y, axis=-1, keepdims=True)
        n = (y * lax.rsqrt(ms + EPS)) * g_ref[...]

        @pl.when(i < n_ctx_tiles)
        def _():
            out_a[...] = n

        @pl.when(i >= n_ctx_tiles)
        def _():
            out_b[...] = n
    else:
        out_a[...] = y
        out_b[...] = _norm_mod_value(y, g_ref[...], nmod_ref, 0).astype(out_b.dtype)


def _combine(os_, dest, gates, y, mod, layer, gate_row, next_g, final, n_ctx, dec_seq, tm):
    t, d = y.shape
    nct = n_ctx // tm

    def mod_spec(which):
        return pl.BlockSpec((None, None, 6, d),
                            lambda i, dst: (which, _mod_row(i, tm, n_ctx, dec_seq), 0, 0))

    in_specs = [pl.BlockSpec(memory_space=pl.ANY),
                pl.BlockSpec((tm, d), lambda i, dst: (i, 0)),
                pl.BlockSpec((tm, LANES), lambda i, dst: (i, 0)),
                mod_spec(layer),
                pl.BlockSpec((1, d), lambda i, dst: (0, 0))]
    args = [dest, os_, y, gates, mod, next_g.reshape(1, d)]
    if final:
        out_specs = [pl.BlockSpec((tm, d), lambda i, dst: (jnp.minimum(i, nct - 1), 0)),
                     pl.BlockSpec((tm, d), lambda i, dst: (jnp.maximum(i - nct, 0), 0))]
        out_shape = [jax.ShapeDtypeStruct((n_ctx, d), F32), jax.ShapeDtypeStruct((t - n_ctx, d), F32)]
    else:
        in_specs.append(mod_spec(layer + 1))
        args.append(mod)
        out_specs = [pl.BlockSpec((tm, d), lambda i, dst: (i, 0)),
                     pl.BlockSpec((tm, d), lambda i, dst: (i, 0))]
        out_shape = [jax.ShapeDtypeStruct((t, d), F32), jax.ShapeDtypeStruct((t, d), BF16)]
    return pl.pallas_call(
        functools.partial(_combine_kernel, rows=tm, gate_row=gate_row, n_ctx_tiles=nct, final=final),
        grid_spec=pltpu.PrefetchScalarGridSpec(
            num_scalar_prefetch=1,
            grid=(t // tm,),
            in_specs=in_specs,
            out_specs=out_specs,
            scratch_shapes=[pltpu.VMEM((DMA_RING, TOP_K, tm, d), F32),
                            pltpu.SemaphoreType.DMA((DMA_RING,))]),
        out_shape=out_shape,
        compiler_params=_cparams(1), name="moe_combine_final" if final else "moe_combine",
    )(*args)


def kernel(x_prompt, x_sample, state_rglru, c, c_ctx, norm_mix_g, norm_ffn_g, w_mod, b_mod, final_norm_g, rg_w_in, rg_conv_w, rg_conv_b, rg_w_a, rg_b_a, rg_w_x, rg_b_x, rg_lam, rg_w_out, sg_w_in, sg_norm_g, sg_w_s, sg_b_s, sg_w_out, sc_w_in, sc_conv_w, sc_w_out, ff_w1, ff_w3, ff_w2, moe_router, moe_router_b, moe_w1, moe_w3, moe_w2):
    batch, seq, d = x_prompt.shape
    dec_batch, dec_seq, _ = x_sample.shape
    depth = w_mod.shape[0]
    n_ctx = batch * seq
    d_rnn = rg_w_out.shape[1]
    n_exp = moe_router.shape[2]
    chunk = sg_w_s.shape[2]

    tm = dec_seq
    assert dec_seq % seq == 0 and n_ctx % tm == 0 and seq % SUBLANES == 0
    tm_half = max(tm // 2, chunk)
    tm_small = max(tm // 4, chunk)
    assert tm % tm_half == 0 and tm % tm_small == 0 and tm_small % chunk == 0
    moe_bm, moe_sub = tm, tm_half
    tm_wide = 2 * tm if (n_ctx + dec_batch * dec_seq) % (2 * tm) == 0 else tm

    n_cond = 1 + dec_batch
    cond = jnp.zeros((-(-n_cond // SUBLANES) * SUBLANES, d), F32)
    cond = cond.at[0].set(c_ctx).at[1:n_cond].set(c)
    mod = _adaln(cond, w_mod, b_mod)[:, :n_cond].reshape(depth, n_cond, 6, d)

    y, h = _embed_norm(x_prompt.reshape(n_ctx, d), x_sample.reshape(dec_batch * dec_seq, d),
                       _grid_pos_embed(dec_seq, d), norm_mix_g[0], mod, tm_half)

    states = []
    outs = None
    for i in range(depth):
        kind, j = i % 3, i // 3
        if h is None:
            h = _norm_mod(y, norm_mix_g[i], mod, i, 0, n_ctx, dec_seq, tm_half)
        if kind == 0:
            z = _mm_act(h, rg_w_in, j, None, F32, tm_wide, tn_pref=1024)
            h0 = jnp.concatenate([jnp.zeros((n_ctx // tm, 2, d_rnn), F32),
                                  state_rglru[:, j].astype(F32)], axis=0)
            mix, st = _rg_core(z, rg_conv_w[j], rg_conv_b[j], rg_w_a[j], rg_b_a[j], rg_w_x[j], rg_b_x[j],
                               rg_lam[j], h0, n_ctx, seq, tm)
            nseg = tm // seq
            st = st[:n_ctx // tm].reshape(n_ctx // tm, 2, nseg, d_rnn)
            states.append(jnp.transpose(st, (0, 2, 1, 3)).reshape(batch, 2, d_rnn))
        elif kind == 1:
            z = _mm_act(h, sg_w_in, j, "gelu", BF16, tm_wide, tn_pref=1024)
            mix = _sgu_core(z, sg_norm_g[j], sg_w_s[j], sg_b_s[j], tm_half)
        else:
            mix = _mm_sconv(h, sc_w_in, sc_conv_w, j, n_ctx, seq, tm)
        w_out = (rg_w_out, sg_w_out, sc_w_out)[kind]
        f = i // 2
        last = i == depth - 1
        dense_ffn = i % 2 == 0
        h = routed = None
        if mix.shape[1] * d * 2 > RESIDENT_WEIGHT_BYTES:
            y = _mm_residual(mix, w_out, j, y, mod, i, 2, n_ctx, dec_seq,
                             tm if mix.shape[1] <= 4096 else tm_half)
        elif dense_ffn:
            y, h = _residual_norm(mix, w_out, j, y, mod, i, 2, norm_ffn_g[i], 3, n_ctx, dec_seq, tm_half)
        else:
            y, *routed = _residual_norm(mix, w_out, j, y, mod, i, 2, norm_ffn_g[i], 3, n_ctx, dec_seq,
                                        tm_half, router=moe_router[f], router_b=moe_router_b[f])

        if dense_ffn:
            if h is None:
                h = _norm_mod(y, norm_ffn_g[i], mod, i, 3, n_ctx, dec_seq, tm_half)
            g = _mm_swiglu(h, ff_w1, ff_w3, f, tm_wide)
            y = _mm_residual(g, ff_w2, f, y, mod, i, 5, n_ctx, dec_seq, tm, w_buffers=1)
            h = None
        else:
            if routed is None:
                routed = _norm_router(y, norm_ffn_g[i], mod, i, 3, moe_router[f], moe_router_b[f],
                                      n_ctx, dec_seq, tm_half)
            h_tok, ids, gates = routed
            dest, src, block_table = _moe_plan(ids, n_exp, moe_bm)
            big, small = block_table(moe_bm), block_table(moe_sub)
            xs = _gather_rows(h_tok, src, small[1], moe_sub, d)
            gs = _grouped_swiglu(xs, moe_w1, moe_w3, f, big, moe_bm)
            os_ = _grouped_down(gs, moe_w2, f, big, moe_bm)
            if last:
                outs = _combine(os_, dest, gates, y, mod, i, 5, final_norm_g, True, n_ctx, dec_seq, tm_small)
            else:
                y, h = _combine(os_, dest, gates, y, mod, i, 5, norm_mix_g[i + 1], False,
                                n_ctx, dec_seq, tm_small)

    y_p, y_s = outs if outs is not None else _final_norm(y, final_norm_g, n_ctx, tm_half)
    new_state = jnp.stack(states, axis=1).astype(x_prompt.dtype)
    return (y_p.reshape(batch, seq, d), y_s.reshape(dec_batch, dec_seq, d), new_state)
```
